```python
import math
import jax, jax.numpy as jnp
from jax import lax
import numpy as np

D_MODEL = 1024
BATCH = 8
SEQ = 8192
DEPTH = 4

N_MIXERS = 3
N_POOL_LAYERS = (DEPTH + 2) // 3
N_SSD_LAYERS = (DEPTH + 1) // 3
N_SB_LAYERS = DEPTH // 3
NORM_EPS = 1e-6

POOL_GROUPS = 4
POOL_WINDOWS = (2, 4, 8, 16)
POOL_GROUP_DIM = D_MODEL // POOL_GROUPS

SSD_D_INNER = 2 * D_MODEL
SSD_HEAD_DIM = 64
SSD_HEADS = SSD_D_INNER // SSD_HEAD_DIM
SSD_GROUPS = 8
SSD_HEADS_PER_GROUP = SSD_HEADS // SSD_GROUPS
SSD_STATE = 128
SSD_CONV = 4
SSD_CHUNK = 256
SSD_GN = SSD_GROUPS * SSD_STATE
SSD_CONV_CH = SSD_D_INNER + 2 * SSD_GN
SSD_IN_DIM = SSD_D_INNER + SSD_CONV_CH + SSD_HEADS
SSD_NORM_GROUP = SSD_D_INNER // SSD_GROUPS

SB_HEADS = 16
SB_HEAD_DIM = D_MODEL // SB_HEADS
SB_BLOCK = 128

FFN_HIDDEN = ((8 * D_MODEL + 3 * 256 - 1) // (3 * 256)) * 256

kernel_name = "hybrid_pool_ssd_stickbreak_block"


def rms_norm(x, gain):
    xf = x.astype(jnp.float32)
    y = xf * lax.rsqrt(jnp.mean(xf * xf, axis=-1, keepdims=True) + NORM_EPS)
    return (y * gain.astype(jnp.float32)).astype(x.dtype)


def pool_mixer(h, w_in, w_group, scale):
    b, s, _ = h.shape
    u = (h @ w_in).reshape(b, s, POOL_GROUPS, POOL_GROUP_DIM).astype(jnp.float32)
    cs = jnp.cumsum(u, axis=1)
    pos = jnp.arange(s)
    outs = []
    for g, w in enumerate(POOL_WINDOWS):
        csg = cs[:, :, g]
        lagged = jnp.pad(csg[:, : s - w], ((0, 0), (w, 0), (0, 0)))
        count = jnp.minimum(pos + 1, w).astype(jnp.float32)[None, :, None]
        outs.append((csg - lagged) / count - u[:, :, g])
    p = jnp.stack(outs, axis=2)
    y = jnp.einsum('bsgc,gcd->bsgd', p, w_group.astype(jnp.float32))
    y = y.reshape(b, s, D_MODEL) * scale.astype(jnp.float32)
    return y.astype(h.dtype)


def ssd_chunked_scan(xdt, da, bmat, cmat):
    b, s = da.shape[:2]
    pad = (-s) % SSD_CHUNK

    def chunks(t):
        t = jnp.pad(t, [(0, 0), (0, pad)] + [(0, 0)] * (t.ndim - 2))
        return jnp.swapaxes(t.reshape(b, -1, SSD_CHUNK, *t.shape[2:]), 0, 1)

    causal = jnp.tril(jnp.ones((SSD_CHUNK, SSD_CHUNK), bool))[None, :, :, None, None]

    def step(state, inp):
        xc, ac, bc, cc = inp
        acum = jnp.cumsum(ac, axis=1)
        diff = acum[:, :, None] - acum[:, None, :]
        decay = jnp.exp(jnp.where(causal, diff, -jnp.inf))
        cb = jnp.einsum('btgn,bsgn->btsg', cc, bc)
        y = jnp.einsum('btsg,btsgh,bsghp->btghp', cb, decay, xc)
        y = y + jnp.einsum('btgn,bghpn,btgh->btghp', cc, state, jnp.exp(acum))
        a_last = acum[:, -1]
        w = jnp.exp(a_last[:, None] - acum)
        state = state * jnp.exp(a_last)[..., None, None] + jnp.einsum('bsgn,bsgh,bsghp->bghpn', bc, w, xc)
        return state, y

    state0 = jnp.zeros((b, SSD_GROUPS, SSD_HEADS_PER_GROUP, SSD_HEAD_DIM, SSD_STATE), jnp.float32)
    _, ys = lax.scan(step, state0, (chunks(xdt), chunks(da), chunks(bmat), chunks(cmat)))
    ys = jnp.swapaxes(ys, 0, 1).reshape(b, -1, *ys.shape[3:])
    return ys[:, :s]


def ssd_mixer(h, w_in, conv_w, conv_b, dt_bias, a_log, d_skip, out_norm, w_out):
    b, s, _ = h.shape
    f32 = jnp.float32
    proj = h @ w_in
    z = proj[..., :SSD_D_INNER]
    xbc = proj[..., SSD_D_INNER:SSD_D_INNER + SSD_CONV_CH]
    dt = proj[..., SSD_D_INNER + SSD_CONV_CH:]
    xbc = lax.conv_general_dilated(
        xbc, conv_w[:, None, :].astype(xbc.dtype), window_strides=(1,),
        padding=[(SSD_CONV - 1, 0)], dimension_numbers=('NWC', 'WIO', 'NWC'),
        feature_group_count=SSD_CONV_CH)
    xbc = jax.nn.silu(xbc.astype(f32) + conv_b.astype(f32))
    xs = xbc[..., :SSD_D_INNER].reshape(b, s, SSD_GROUPS, SSD_HEADS_PER_GROUP, SSD_HEAD_DIM)
    bm = xbc[..., SSD_D_INNER:SSD_D_INNER + SSD_GN].reshape(b, s, SSD_GROUPS, SSD_STATE)
    cm = xbc[..., SSD_D_INNER + SSD_GN:].reshape(b, s, SSD_GROUPS, SSD_STATE)
    dt = jax.nn.softplus(dt.astype(f32) + dt_bias.astype(f32)).reshape(b, s, SSD_GROUPS, SSD_HEADS_PER_GROUP)
    a = -jnp.exp(a_log.astype(f32)).reshape(SSD_GROUPS, SSD_HEADS_PER_GROUP)
    y = ssd_chunked_scan(xs * dt[..., None], dt * a, bm, cm)
    y = y + d_skip.astype(f32).reshape(SSD_GROUPS, SSD_HEADS_PER_GROUP, 1) * xs
    g = (y.reshape(b, s, SSD_D_INNER) * jax.nn.silu(z.astype(f32))).reshape(b, s, SSD_GROUPS, SSD_NORM_GROUP)
    g = g * lax.rsqrt(jnp.mean(g * g, axis=-1, keepdims=True) + NORM_EPS)
    g = g.reshape(b, s, SSD_D_INNER) * out_norm.astype(f32)
    return g.astype(h.dtype) @ w_out


def stick_breaking_mixer(h, w_qkv, q_norm, k_norm, w_out):
    b, s, _ = h.shape
    f32 = jnp.float32
    qkv = (h @ w_qkv).reshape(b, s, 3, SB_HEADS, SB_HEAD_DIM)
    q = rms_norm(qkv[:, :, 0], q_norm).astype(f32).transpose(0, 2, 1, 3)
    k = rms_norm(qkv[:, :, 1], k_norm).astype(f32).transpose(0, 2, 1, 3)
    v = qkv[:, :, 2].astype(f32).transpose(0, 2, 1, 3)
    n_blocks = s // SB_BLOCK
    qb = q.reshape(b, SB_HEADS, n_blocks, SB_BLOCK, SB_HEAD_DIM).transpose(2, 0, 1, 3, 4)
    inv_sqrt_d = 1.0 / math.sqrt(SB_HEAD_DIM)
    key_pos = jnp.arange(s)

    def block(args):
        q_blk, blk = args
        z = jnp.einsum('bhqd,bhkd->bhqk', q_blk, k) * inv_sqrt_d
        t = blk * SB_BLOCK + jnp.arange(SB_BLOCK)
        mask = key_pos[None, :] < t[:, None]
        log_1m = jnp.where(mask, jax.nn.log_sigmoid(-z), 0.0)
        after = lax.cumsum(log_1m, axis=3, reverse=True) - log_1m
        a = jnp.where(mask, jnp.exp(jax.nn.log_sigmoid(z) + after), 0.0)
        return jnp.einsum('bhqk,bhkd->bhqd', a, v)

    o = lax.map(block, (qb, jnp.arange(n_blocks)))
    o = o.transpose(1, 0, 3, 2, 4).reshape(b, s, D_MODEL)
    return o.astype(h.dtype) @ w_out


def swiglu(h, w_gate, w_up, w_down):
    return (jax.nn.silu(h @ w_gate) * (h @ w_up)) @ w_down


def _fwd_setup_inputs(seed: int = 0) -> dict:
    key = jax.random.key(seed)
    ks = jax.random.split(key, 24)
    f32 = jnp.float32

    def nrm(k, shape, scale):
        return jax.random.normal(k, shape, f32) * scale

    def gain(k, shape):
        return 1.0 + 0.02 * jax.random.normal(k, shape, f32)

    dt0 = jnp.exp(jax.random.uniform(ks[10], (N_SSD_LAYERS, SSD_HEADS), f32, math.log(1e-3), math.log(1e-1)))
    return {
        "x": nrm(ks[0], (BATCH, SEQ, D_MODEL), 1.0),
        "mix_norm": gain(ks[1], (DEPTH, D_MODEL)),
        "pool_in": nrm(ks[2], (N_POOL_LAYERS, D_MODEL, D_MODEL), D_MODEL ** -0.5),
        "pool_group": nrm(ks[3], (N_POOL_LAYERS, POOL_GROUPS, POOL_GROUP_DIM, POOL_GROUP_DIM), POOL_GROUP_DIM ** -0.5),
        "pool_scale": gain(ks[4], (N_POOL_LAYERS, D_MODEL)),
        "ssd_in": nrm(ks[5], (N_SSD_LAYERS, D_MODEL, SSD_IN_DIM), D_MODEL ** -0.5),
        "ssd_conv_w": nrm(ks[6], (N_SSD_LAYERS, SSD_CONV, SSD_CONV_CH), SSD_CONV ** -0.5),
        "ssd_conv_b": nrm(ks[7], (N_SSD_LAYERS, SSD_CONV_CH), 0.01),
        "ssd_dt_bias": dt0 + jnp.log(-jnp.expm1(-dt0)),
        "ssd_a_log": jnp.log(jax.random.uniform(ks[8], (N_SSD_LAYERS, SSD_HEADS), f32, 1.0, 16.0)),
        "ssd_d": gain(ks[9], (N_SSD_LAYERS, SSD_HEADS)),
        "ssd_out_norm": gain(ks[11], (N_SSD_LAYERS, SSD_D_INNER)),
        "ssd_out": nrm(ks[12], (N_SSD_LAYERS, SSD_D_INNER, D_MODEL), SSD_D_INNER ** -0.5),
        "sb_qkv": nrm(ks[13], (N_SB_LAYERS, D_MODEL, 3 * D_MODEL), D_MODEL ** -0.5),
        "sb_q_norm": gain(ks[14], (N_SB_LAYERS, SB_HEAD_DIM)),
        "sb_k_norm": gain(ks[15], (N_SB_LAYERS, SB_HEAD_DIM)),
        "sb_out": nrm(ks[16], (N_SB_LAYERS, D_MODEL, D_MODEL), D_MODEL ** -0.5),
        "ffn_norm": gain(ks[17], (DEPTH, D_MODEL)),
        "ffn_gate": nrm(ks[18], (DEPTH, D_MODEL, FFN_HIDDEN), D_MODEL ** -0.5),
        "ffn_up": nrm(ks[19], (DEPTH, D_MODEL, FFN_HIDDEN), D_MODEL ** -0.5),
        "ffn_down": nrm(ks[20], (DEPTH, FFN_HIDDEN, D_MODEL), FFN_HIDDEN ** -0.5),
    }


def _fwd_reference(x, mix_norm, pool_in, pool_group, pool_scale, ssd_in, ssd_conv_w, ssd_conv_b,
              ssd_dt_bias, ssd_a_log, ssd_d, ssd_out_norm, ssd_out, sb_qkv, sb_q_norm, sb_k_norm,
              sb_out, ffn_norm, ffn_gate, ffn_up, ffn_down):
    for i in range(DEPTH):
        kind, j = i % N_MIXERS, i // N_MIXERS
        h = rms_norm(x, mix_norm[i])
        if kind == 0:
            m = pool_mixer(h, pool_in[j], pool_group[j], pool_scale[j])
        elif kind == 1:
            m = ssd_mixer(h, ssd_in[j], ssd_conv_w[j], ssd_conv_b[j], ssd_dt_bias[j], ssd_a_log[j],
                          ssd_d[j], ssd_out_norm[j], ssd_out[j])
        else:
            m = stick_breaking_mixer(h, sb_qkv[j], sb_q_norm[j], sb_k_norm[j], sb_out[j])
        x = x + m
        h = rms_norm(x, ffn_norm[i])
        x = x + swiglu(h, ffn_gate[i], ffn_up[i], ffn_down[i])
    return x


import jax as _jax
import jax.numpy as _jnp

TWIN_FORMAT = 'train_step'
FWD_PARAMS = ['x', 'mix_norm', 'pool_in', 'pool_group', 'pool_scale', 'ssd_in', 'ssd_conv_w', 'ssd_conv_b', 'ssd_dt_bias', 'ssd_a_log', 'ssd_d', 'ssd_out_norm', 'ssd_out', 'sb_qkv', 'sb_q_norm', 'sb_k_norm', 'sb_out', 'ffn_norm', 'ffn_gate', 'ffn_up', 'ffn_down']
TWIN_WEIGHTS = ['mix_norm', 'pool_in', 'pool_group', 'pool_scale', 'ssd_in', 'ssd_conv_w', 'ssd_conv_b', 'ssd_dt_bias', 'ssd_a_log', 'ssd_d', 'ssd_out_norm', 'ssd_out', 'sb_qkv', 'sb_q_norm', 'sb_k_norm', 'sb_out', 'ffn_norm', 'ffn_gate', 'ffn_up', 'ffn_down']
TWIN_DIFF_INPUT = 'x'
TWIN_INPUTS = ['x', 'mix_norm', 'pool_in', 'pool_group', 'pool_scale', 'ssd_in', 'ssd_conv_w', 'ssd_conv_b', 'ssd_dt_bias', 'ssd_a_log', 'ssd_d', 'ssd_out_norm', 'ssd_out', 'sb_qkv', 'sb_q_norm', 'sb_k_norm', 'sb_out', 'ffn_norm', 'ffn_gate', 'ffn_up', 'ffn_down', 'loss_target', 'm_mix_norm', 'm_pool_in', 'm_pool_group', 'm_pool_scale', 'm_ssd_in', 'm_ssd_conv_w', 'm_ssd_conv_b', 'm_ssd_dt_bias', 'm_ssd_a_log', 'm_ssd_d', 'm_ssd_out_norm', 'm_ssd_out', 'm_sb_qkv', 'm_sb_q_norm', 'm_sb_k_norm', 'm_sb_out', 'm_ffn_norm', 'm_ffn_gate', 'm_ffn_up', 'm_ffn_down', 'v_mix_norm', 'v_pool_in', 'v_pool_group', 'v_pool_scale', 'v_ssd_in', 'v_ssd_conv_w', 'v_ssd_conv_b', 'v_ssd_dt_bias', 'v_ssd_a_log', 'v_ssd_d', 'v_ssd_out_norm', 'v_ssd_out', 'v_sb_qkv', 'v_sb_q_norm', 'v_sb_k_norm', 'v_sb_out', 'v_ffn_norm', 'v_ffn_gate', 'v_ffn_up', 'v_ffn_down']
TWIN_OUTPUTS = ['loss', 'grad_x', 'grad_mix_norm', 'grad_pool_in', 'grad_pool_group', 'grad_pool_scale', 'grad_ssd_in', 'grad_ssd_conv_w', 'grad_ssd_conv_b', 'grad_ssd_dt_bias', 'grad_ssd_a_log', 'grad_ssd_d', 'grad_ssd_out_norm', 'grad_ssd_out', 'grad_sb_qkv', 'grad_sb_q_norm', 'grad_sb_k_norm', 'grad_sb_out', 'grad_ffn_norm', 'grad_ffn_gate', 'grad_ffn_up', 'grad_ffn_down', 'delta_mix_norm', 'delta_pool_in', 'delta_pool_group', 'delta_pool_scale', 'delta_ssd_in', 'delta_ssd_conv_w', 'delta_ssd_conv_b', 'delta_ssd_dt_bias', 'delta_ssd_a_log', 'delta_ssd_d', 'delta_ssd_out_norm', 'delta_ssd_out', 'delta_sb_qkv', 'delta_sb_q_norm', 'delta_sb_k_norm', 'delta_sb_out', 'delta_ffn_norm', 'delta_ffn_gate', 'delta_ffn_up', 'delta_ffn_down', 'new_m_mix_norm', 'new_m_pool_in', 'new_m_pool_group', 'new_m_pool_scale', 'new_m_ssd_in', 'new_m_ssd_conv_w', 'new_m_ssd_conv_b', 'new_m_ssd_dt_bias', 'new_m_ssd_a_log', 'new_m_ssd_d', 'new_m_ssd_out_norm', 'new_m_ssd_out', 'new_m_sb_qkv', 'new_m_sb_q_norm', 'new_m_sb_k_norm', 'new_m_sb_out', 'new_m_ffn_norm', 'new_m_ffn_gate', 'new_m_ffn_up', 'new_m_ffn_down', 'new_v_mix_norm', 'new_v_pool_in', 'new_v_pool_group', 'new_v_pool_scale', 'new_v_ssd_in', 'new_v_ssd_conv_w', 'new_v_ssd_conv_b', 'new_v_ssd_dt_bias', 'new_v_ssd_a_log', 'new_v_ssd_d', 'new_v_ssd_out_norm', 'new_v_ssd_out', 'new_v_sb_qkv', 'new_v_sb_q_norm', 'new_v_sb_k_norm', 'new_v_sb_out', 'new_v_ffn_norm', 'new_v_ffn_gate', 'new_v_ffn_up', 'new_v_ffn_down']
TWIN_LEAF_KINDS = {'loss': 'loss', 'grad_x': 'grad_x', 'grad_mix_norm': 'grad_w', 'grad_pool_in': 'grad_w', 'grad_pool_group': 'grad_w', 'grad_pool_scale': 'grad_w', 'grad_ssd_in': 'grad_w', 'grad_ssd_conv_w': 'grad_w', 'grad_ssd_conv_b': 'grad_w', 'grad_ssd_dt_bias': 'grad_w', 'grad_ssd_a_log': 'grad_w', 'grad_ssd_d': 'grad_w', 'grad_ssd_out_norm': 'grad_w', 'grad_ssd_out': 'grad_w', 'grad_sb_qkv': 'grad_w', 'grad_sb_q_norm': 'grad_w', 'grad_sb_k_norm': 'grad_w', 'grad_sb_out': 'grad_w', 'grad_ffn_norm': 'grad_w', 'grad_ffn_gate': 'grad_w', 'grad_ffn_up': 'grad_w', 'grad_ffn_down': 'grad_w', 'delta_mix_norm': 'delta_w', 'delta_pool_in': 'delta_w', 'delta_pool_group': 'delta_w', 'delta_pool_scale': 'delta_w', 'delta_ssd_in': 'delta_w', 'delta_ssd_conv_w': 'delta_w', 'delta_ssd_conv_b': 'delta_w', 'delta_ssd_dt_bias': 'delta_w', 'delta_ssd_a_log': 'delta_w', 'delta_ssd_d': 'delta_w', 'delta_ssd_out_norm': 'delta_w', 'delta_ssd_out': 'delta_w', 'delta_sb_qkv': 'delta_w', 'delta_sb_q_norm': 'delta_w', 'delta_sb_k_norm': 'delta_w', 'delta_sb_out': 'delta_w', 'delta_ffn_norm': 'delta_w', 'delta_ffn_gate': 'delta_w', 'delta_ffn_up': 'delta_w', 'delta_ffn_down': 'delta_w', 'new_m_mix_norm': 'new_m', 'new_m_pool_in': 'new_m', 'new_m_pool_group': 'new_m', 'new_m_pool_scale': 'new_m', 'new_m_ssd_in': 'new_m', 'new_m_ssd_conv_w': 'new_m', 'new_m_ssd_conv_b': 'new_m', 'new_m_ssd_dt_bias': 'new_m', 'new_m_ssd_a_log': 'new_m', 'new_m_ssd_d': 'new_m', 'new_m_ssd_out_norm': 'new_m', 'new_m_ssd_out': 'new_m', 'new_m_sb_qkv': 'new_m', 'new_m_sb_q_norm': 'new_m', 'new_m_sb_k_norm': 'new_m', 'new_m_sb_out': 'new_m', 'new_m_ffn_norm': 'new_m', 'new_m_ffn_gate': 'new_m', 'new_m_ffn_up': 'new_m', 'new_m_ffn_down': 'new_m', 'new_v_mix_norm': 'new_v', 'new_v_pool_in': 'new_v', 'new_v_pool_group': 'new_v', 'new_v_pool_scale': 'new_v', 'new_v_ssd_in': 'new_v', 'new_v_ssd_conv_w': 'new_v', 'new_v_ssd_conv_b': 'new_v', 'new_v_ssd_dt_bias': 'new_v', 'new_v_ssd_a_log': 'new_v', 'new_v_ssd_d': 'new_v', 'new_v_ssd_out_norm': 'new_v', 'new_v_ssd_out': 'new_v', 'new_v_sb_qkv': 'new_v', 'new_v_sb_q_norm': 'new_v', 'new_v_sb_k_norm': 'new_v', 'new_v_sb_out': 'new_v', 'new_v_ffn_norm': 'new_v', 'new_v_ffn_gate': 'new_v', 'new_v_ffn_up': 'new_v', 'new_v_ffn_down': 'new_v'}


def _forward(args):
    return _fwd_reference(*[args[k] for k in FWD_PARAMS])


def _output_shape():
    def fwd():
        inp = _fwd_setup_inputs(0)
        return _fwd_reference(*[inp[k] for k in FWD_PARAMS])
    out = _jax.eval_shape(fwd)
    return out.shape, out.dtype

N_MICROBATCH = 1
ADAM_LR = 0.001
ADAM_B1 = 0.9
ADAM_B2 = 0.999
ADAM_EPS = 1e-08
ADAM_WD = 0.01
ADAM_STEP = 10
PER_EXAMPLE_BATCH_AXIS = {'x': 0, 'loss_target': 0}
SHARED_INPUTS = []
_WEIGHT_DTYPES = {'mix_norm': _jnp.float32, 'pool_in': _jnp.float32, 'pool_group': _jnp.float32, 'pool_scale': _jnp.float32, 'ssd_in': _jnp.float32, 'ssd_conv_w': _jnp.float32, 'ssd_conv_b': _jnp.float32, 'ssd_dt_bias': _jnp.float32, 'ssd_a_log': _jnp.float32, 'ssd_d': _jnp.float32, 'ssd_out_norm': _jnp.float32, 'ssd_out': _jnp.float32, 'sb_qkv': _jnp.float32, 'sb_q_norm': _jnp.float32, 'sb_k_norm': _jnp.float32, 'sb_out': _jnp.float32, 'ffn_norm': _jnp.float32, 'ffn_gate': _jnp.float32, 'ffn_up': _jnp.float32, 'ffn_down': _jnp.float32}
MOMENT_SCALE = {'mix_norm': 3.662380e+01, 'pool_in': 3.029657e+00, 'pool_group': 3.797522e+00, 'pool_scale': 5.011240e+01, 'ssd_in': 8.261412e-01, 'ssd_conv_w': 1.300574e+00, 'ssd_conv_b': 4.870769e+00, 'ssd_dt_bias': 1.760733e+00, 'ssd_a_log': 9.744072e+00, 'ssd_d': 1.135943e+01, 'ssd_out_norm': 4.185266e+01, 'ssd_out': 4.121146e+00, 'sb_qkv': 1.023777e+00, 'sb_q_norm': 6.333961e+01, 'sb_k_norm': 6.347713e+01, 'sb_out': 1.609441e+00, 'ffn_norm': 4.969401e+01, 'ffn_gate': 6.127533e-01, 'ffn_up': 6.226376e-01, 'ffn_down': 1.013272e+00}


def _to_microbatches(a, axis):
    t = _jnp.moveaxis(a, axis, 0)
    t = t.reshape((N_MICROBATCH, t.shape[0] // N_MICROBATCH) + t.shape[1:])
    return _jnp.moveaxis(t, 1, axis + 1)


def setup_inputs(seed: int = 0) -> dict:
    inp = _fwd_setup_inputs(seed)
    key = _jax.random.fold_in(_jax.random.key(seed), 7919)
    shape, _ = _output_shape()
    out = dict(inp)
    out["loss_target"] = _jax.random.normal(_jax.random.fold_in(key, 0), shape, _jnp.float32)
    for i, name in enumerate(TWIN_WEIGHTS):
        w = inp[name].astype(_jnp.float32)
        if MOMENT_SCALE is None:
            s = _jnp.sqrt(_jnp.mean(_jnp.square(w)) + 1e-30)
        else:
            s = MOMENT_SCALE[name]
        km, kv = _jax.random.split(_jax.random.fold_in(key, i + 1))
        out[name] = w
        out["m_" + name] = s * _jax.random.normal(km, w.shape, _jnp.float32)
        out["v_" + name] = (s * s) * _jax.random.uniform(kv, w.shape, _jnp.float32, 0.5, 1.5)
    if N_MICROBATCH > 1:
        for name, axis in PER_EXAMPLE_BATCH_AXIS.items():
            out[name] = _to_microbatches(out[name], axis)
    return {'x': out['x'], 'mix_norm': out['mix_norm'], 'pool_in': out['pool_in'], 'pool_group': out['pool_group'], 'pool_scale': out['pool_scale'], 'ssd_in': out['ssd_in'], 'ssd_conv_w': out['ssd_conv_w'], 'ssd_conv_b': out['ssd_conv_b'], 'ssd_dt_bias': out['ssd_dt_bias'], 'ssd_a_log': out['ssd_a_log'], 'ssd_d': out['ssd_d'], 'ssd_out_norm': out['ssd_out_norm'], 'ssd_out': out['ssd_out'], 'sb_qkv': out['sb_qkv'], 'sb_q_norm': out['sb_q_norm'], 'sb_k_norm': out['sb_k_norm'], 'sb_out': out['sb_out'], 'ffn_norm': out['ffn_norm'], 'ffn_gate': out['ffn_gate'], 'ffn_up': out['ffn_up'], 'ffn_down': out['ffn_down'], 'loss_target': out['loss_target'], 'm_mix_norm': out['m_mix_norm'], 'm_pool_in': out['m_pool_in'], 'm_pool_group': out['m_pool_group'], 'm_pool_scale': out['m_pool_scale'], 'm_ssd_in': out['m_ssd_in'], 'm_ssd_conv_w': out['m_ssd_conv_w'], 'm_ssd_conv_b': out['m_ssd_conv_b'], 'm_ssd_dt_bias': out['m_ssd_dt_bias'], 'm_ssd_a_log': out['m_ssd_a_log'], 'm_ssd_d': out['m_ssd_d'], 'm_ssd_out_norm': out['m_ssd_out_norm'], 'm_ssd_out': out['m_ssd_out'], 'm_sb_qkv': out['m_sb_qkv'], 'm_sb_q_norm': out['m_sb_q_norm'], 'm_sb_k_norm': out['m_sb_k_norm'], 'm_sb_out': out['m_sb_out'], 'm_ffn_norm': out['m_ffn_norm'], 'm_ffn_gate': out['m_ffn_gate'], 'm_ffn_up': out['m_ffn_up'], 'm_ffn_down': out['m_ffn_down'], 'v_mix_norm': out['v_mix_norm'], 'v_pool_in': out['v_pool_in'], 'v_pool_group': out['v_pool_group'], 'v_pool_scale': out['v_pool_scale'], 'v_ssd_in': out['v_ssd_in'], 'v_ssd_conv_w': out['v_ssd_conv_w'], 'v_ssd_conv_b': out['v_ssd_conv_b'], 'v_ssd_dt_bias': out['v_ssd_dt_bias'], 'v_ssd_a_log': out['v_ssd_a_log'], 'v_ssd_d': out['v_ssd_d'], 'v_ssd_out_norm': out['v_ssd_out_norm'], 'v_ssd_out': out['v_ssd_out'], 'v_sb_qkv': out['v_sb_qkv'], 'v_sb_q_norm': out['v_sb_q_norm'], 'v_sb_k_norm': out['v_sb_k_norm'], 'v_sb_out': out['v_sb_out'], 'v_ffn_norm': out['v_ffn_norm'], 'v_ffn_gate': out['v_ffn_gate'], 'v_ffn_up': out['v_ffn_up'], 'v_ffn_down': out['v_ffn_down']}


def _loss(weights, diff, rest, loss_target):
    with _jax.named_scope("forward"):
        args = {**rest, TWIN_DIFF_INPUT: diff, **{k: w.astype(_WEIGHT_DTYPES[k]) for k, w in weights.items()}}
        y = _forward(args)
    with _jax.named_scope("loss_head"):
        err = _jnp.square(y.astype(_jnp.float32) - loss_target)
        return 0.5 * _jnp.sum(_jnp.mean(err, axis=-1)) if err.ndim else 0.5 * err


def _adamw(w, g, m, v):
    m = ADAM_B1 * m + (1.0 - ADAM_B1) * g
    v = ADAM_B2 * v + (1.0 - ADAM_B2) * _jnp.square(g)
    m_hat = m / (1.0 - ADAM_B1 ** ADAM_STEP)
    v_hat = v / (1.0 - ADAM_B2 ** ADAM_STEP)
    delta = -ADAM_LR * (m_hat / (_jnp.sqrt(v_hat) + ADAM_EPS) + ADAM_WD * w)
    return delta, m, v


def reference(x, mix_norm, pool_in, pool_group, pool_scale, ssd_in, ssd_conv_w, ssd_conv_b, ssd_dt_bias, ssd_a_log, ssd_d, ssd_out_norm, ssd_out, sb_qkv, sb_q_norm, sb_k_norm, sb_out, ffn_norm, ffn_gate, ffn_up, ffn_down, loss_target, m_mix_norm, m_pool_in, m_pool_group, m_pool_scale, m_ssd_in, m_ssd_conv_w, m_ssd_conv_b, m_ssd_dt_bias, m_ssd_a_log, m_ssd_d, m_ssd_out_norm, m_ssd_out, m_sb_qkv, m_sb_q_norm, m_sb_k_norm, m_sb_out, m_ffn_norm, m_ffn_gate, m_ffn_up, m_ffn_down, v_mix_norm, v_pool_in, v_pool_group, v_pool_scale, v_ssd_in, v_ssd_conv_w, v_ssd_conv_b, v_ssd_dt_bias, v_ssd_a_log, v_ssd_d, v_ssd_out_norm, v_ssd_out, v_sb_qkv, v_sb_q_norm, v_sb_k_norm, v_sb_out, v_ffn_norm, v_ffn_gate, v_ffn_up, v_ffn_down):
    given = dict(x=x, mix_norm=mix_norm, pool_in=pool_in, pool_group=pool_group, pool_scale=pool_scale, ssd_in=ssd_in, ssd_conv_w=ssd_conv_w, ssd_conv_b=ssd_conv_b, ssd_dt_bias=ssd_dt_bias, ssd_a_log=ssd_a_log, ssd_d=ssd_d, ssd_out_norm=ssd_out_norm, ssd_out=ssd_out, sb_qkv=sb_qkv, sb_q_norm=sb_q_norm, sb_k_norm=sb_k_norm, sb_out=sb_out, ffn_norm=ffn_norm, ffn_gate=ffn_gate, ffn_up=ffn_up, ffn_down=ffn_down, loss_target=loss_target, m_mix_norm=m_mix_norm, m_pool_in=m_pool_in, m_pool_group=m_pool_group, m_pool_scale=m_pool_scale, m_ssd_in=m_ssd_in, m_ssd_conv_w=m_ssd_conv_w, m_ssd_conv_b=m_ssd_conv_b, m_ssd_dt_bias=m_ssd_dt_bias, m_ssd_a_log=m_ssd_a_log, m_ssd_d=m_ssd_d, m_ssd_out_norm=m_ssd_out_norm, m_ssd_out=m_ssd_out, m_sb_qkv=m_sb_qkv, m_sb_q_norm=m_sb_q_norm, m_sb_k_norm=m_sb_k_norm, m_sb_out=m_sb_out, m_ffn_norm=m_ffn_norm, m_ffn_gate=m_ffn_gate, m_ffn_up=m_ffn_up, m_ffn_down=m_ffn_down, v_mix_norm=v_mix_norm, v_pool_in=v_pool_in, v_pool_group=v_pool_group, v_pool_scale=v_pool_scale, v_ssd_in=v_ssd_in, v_ssd_conv_w=v_ssd_conv_w, v_ssd_conv_b=v_ssd_conv_b, v_ssd_dt_bias=v_ssd_dt_bias, v_ssd_a_log=v_ssd_a_log, v_ssd_d=v_ssd_d, v_ssd_out_norm=v_ssd_out_norm, v_ssd_out=v_ssd_out, v_sb_qkv=v_sb_qkv, v_sb_q_norm=v_sb_q_norm, v_sb_k_norm=v_sb_k_norm, v_sb_out=v_sb_out, v_ffn_norm=v_ffn_norm, v_ffn_gate=v_ffn_gate, v_ffn_up=v_ffn_up, v_ffn_down=v_ffn_down)
    weights = {n: given[n] for n in TWIN_WEIGHTS}
    shared = {n: given[n] for n in SHARED_INPUTS}
    per_example = {n: given[n] for n in ['x']}
    grad_fn = _jax.value_and_grad(_loss, argnums=(0, 1))

    def one_microbatch(ex, loss_target):
        ex = dict(ex)
        diff = ex.pop(TWIN_DIFF_INPUT)
        return grad_fn(weights, diff, {**shared, **ex}, loss_target)

    if N_MICROBATCH == 1:
        loss, (grad_w, grad_x) = one_microbatch(per_example, given["loss_target"])
    else:
        def body(carry, xs):
            loss_sum, grad_sum = carry
            l_k, (gw_k, gx_k) = one_microbatch(xs[0], xs[1])
            with _jax.named_scope("update"):
                return (loss_sum + l_k, _jax.tree.map(_jnp.add, grad_sum, gw_k)), gx_k

        init = (_jnp.zeros((), _jnp.float32), _jax.tree.map(_jnp.zeros_like, weights))
        (loss, grad_w), grad_x = _jax.lax.scan(body, init, (per_example, given["loss_target"]))
    with _jax.named_scope("update"):
        delta_w, new_m, new_v = {}, {}, {}
        for n in TWIN_WEIGHTS:
            delta_w[n], new_m[n], new_v[n] = _adamw(weights[n], grad_w[n], given["m_" + n], given["v_" + n])
    return (loss, grad_x, *[grad_w[n] for n in TWIN_WEIGHTS], *[delta_w[n] for n in TWIN_WEIGHTS],
            *[new_m[n] for n in TWIN_WEIGHTS], *[new_v[n] for n in TWIN_WEIGHTS])
```

```python
import functools
import math

import jax
import jax.numpy as jnp
from jax import lax
from jax.experimental import pallas as pl
from jax.experimental.pallas import tpu as pltpu

F32 = jnp.float32
BF16 = jnp.bfloat16
HI = lax.Precision.HIGHEST

D = 1024
DEPTH = 4
EPS = 1e-6
POOL_WINDOWS = (2, 4, 8, 16)
PG = 256
DI = 2048
NH = 32
HP = 64
NG = 8
NS = 128
GW = 256
CH = 256
CONV_CH = 4096
SSD_IN = 6176
SBH = 16
SBD = 64
FH = 2816
N_CHIPS = 4
LANES = 128

ADAM_LR = 0.001
ADAM_B1 = 0.9
ADAM_B2 = 0.999
ADAM_EPS = 1e-08
ADAM_WD = 0.01
ADAM_STEP = 10

VMEM_LIMIT = 56 * 1024 * 1024


def _pcall(body, **kw):
    return pl.pallas_call(body, **kw)


def _cp(*sem):
    return pltpu.CompilerParams(dimension_semantics=sem, vmem_limit_bytes=VMEM_LIMIT)


def _dot(a, b, prec=None):
    return lax.dot_general(a, b, (((1,), (0,)), ((), ())), precision=prec, preferred_element_type=F32)


def _dot_nt(a, b, prec=None):
    return lax.dot_general(a, b, (((1,), (1,)), ((), ())), precision=prec, preferred_element_type=F32)


def _dot_tn(a, b, prec=None):
    return lax.dot_general(a, b, (((0,), (0,)), ((), ())), precision=prec, preferred_element_type=F32)


def _sigmoid(x):
    return 1.0 / (1.0 + jnp.exp(-x))


def _iota(shape, axis):
    return lax.broadcasted_iota(jnp.int32, shape, axis)


def linear(pairs, res=None, out_dtype=F32, tm=512, tn=None, name="linear"):
    M = pairs[0][0].shape[0]
    N = pairs[0][1].shape[1] if pairs[0][2] == "nn" else pairs[0][1].shape[0]
    tm = min(tm, M)
    tn = N if tn is None else min(tn, N)
    n_pairs = len(pairs)
    modes = [p[2] for p in pairs]

    def body(*refs):
        acc = None
        for k in range(n_pairs):
            a = refs[2 * k][...].astype(BF16)
            w = refs[2 * k + 1][...]
            t = _dot(a, w) if modes[k] == "nn" else _dot_nt(a, w)
            acc = t if acc is None else acc + t
        if res is not None:
            acc = acc + refs[2 * n_pairs][...]
        refs[-1][...] = acc.astype(out_dtype)

    in_specs, args = [], []
    for a, w, mode in pairs:
        K = a.shape[1]
        in_specs.append(pl.BlockSpec((tm, K), lambda j, i: (i, 0)))
        if mode == "nn":
            in_specs.append(pl.BlockSpec((K, tn), lambda j, i: (0, j)))
        else:
            in_specs.append(pl.BlockSpec((tn, K), lambda j, i: (j, 0)))
        args += [a, w]
    if res is not None:
        in_specs.append(pl.BlockSpec((tm, tn), lambda j, i: (i, j)))
        args.append(res)
    return _pcall(
        body, name=name, grid=(N // tn, M // tm), in_specs=in_specs,
        out_specs=pl.BlockSpec((tm, tn), lambda j, i: (i, j)),
        out_shape=jax.ShapeDtypeStruct((M, N), out_dtype),
        compiler_params=_cp("parallel", "arbitrary"))(*args)


def wgrad(a, gs, tk=1024, tn=None, tm=1024, name="wgrad"):
    M, Ka = a.shape
    N = gs[0].shape[1]
    tk, tm = min(tk, Ka), min(tm, M)
    tn = N if tn is None else min(tn, N)
    n_g = len(gs)

    def body(*refs):
        a_ref, g_refs, o_refs = refs[0], refs[1:1 + n_g], refs[1 + n_g:]
        m = pl.program_id(2)
        at = a_ref[...].astype(BF16)
        for g_ref, o_ref in zip(g_refs, o_refs):
            t = _dot_tn(at, g_ref[...].astype(BF16))

            @pl.when(m == 0)
            def _():
                o_ref[...] = t

            @pl.when(m > 0)
            def _():
                o_ref[...] += t

    out = _pcall(
        body, name=name, grid=(Ka // tk, N // tn, M // tm),
        in_specs=[pl.BlockSpec((tm, tk), lambda k, j, m: (m, k))]
        + [pl.BlockSpec((tm, tn), lambda k, j, m: (m, j))] * n_g,
        out_specs=[pl.BlockSpec((tk, tn), lambda k, j, m: (k, j))] * n_g,
        out_shape=[jax.ShapeDtypeStruct((Ka, N), F32)] * n_g,
        compiler_params=_cp("parallel", "parallel", "arbitrary"))(a, *gs)
    return out


def rmsnorm_fwd(x, gain, tm=512, name="rmsnorm_fwd"):
    S, Dm = x.shape
    tm = min(tm, S)

    def body(x_ref, g_ref, o_ref):
        xv = x_ref[...]
        r = lax.rsqrt(jnp.mean(xv * xv, axis=-1, keepdims=True) + EPS)
        o_ref[...] = (xv * r * g_ref[...]).astype(BF16)

    return _pcall(
        body, name=name, grid=(S // tm,),
        in_specs=[pl.BlockSpec((tm, Dm), lambda i: (i, 0)), pl.BlockSpec((1, Dm), lambda i: (0, 0))],
        out_specs=pl.BlockSpec((tm, Dm), lambda i: (i, 0)),
        out_shape=jax.ShapeDtypeStruct((S, Dm), BF16),
        compiler_params=_cp("parallel"))(x, gain.reshape(1, Dm))


def rmsnorm_bwd(x, gain, dh, dres, tm=512, name="rmsnorm_bwd"):
    S, Dm = x.shape
    tm = min(tm, S)

    def body(x_ref, g_ref, dh_ref, dr_ref, dx_ref, dg_ref):
        i = pl.program_id(0)
        xv = x_ref[...]
        r = lax.rsqrt(jnp.mean(xv * xv, axis=-1, keepdims=True) + EPS)
        y = xv * r
        dhv = dh_ref[...]
        dy = dhv * g_ref[...]
        dx_ref[...] = dr_ref[...] + r * (dy - y * jnp.mean(dy * y, axis=-1, keepdims=True))
        part = jnp.sum(dhv * y, axis=0, keepdims=True)

        @pl.when(i == 0)
        def _():
            dg_ref[...] = part

        @pl.when(i > 0)
        def _():
            dg_ref[...] += part

    return _pcall(
        body, name=name, grid=(S // tm,),
        in_specs=[pl.BlockSpec((tm, Dm), lambda i: (i, 0)), pl.BlockSpec((1, Dm), lambda i: (0, 0)),
                  pl.BlockSpec((tm, Dm), lambda i: (i, 0)), pl.BlockSpec((tm, Dm), lambda i: (i, 0))],
        out_specs=[pl.BlockSpec((tm, Dm), lambda i: (i, 0)), pl.BlockSpec((1, Dm), lambda i: (0, 0))],
        out_shape=[jax.ShapeDtypeStruct((S, Dm), F32), jax.ShapeDtypeStruct((1, Dm), F32)],
        compiler_params=_cp("arbitrary"))(x, gain.reshape(1, Dm), dh, dres)


def ffn_up(h, wg, wu, tm=256, tn=1408):
    S = h.shape[0]
    tm = min(tm, S)

    def body(h_ref, wg_ref, wu_ref, a_ref, b_ref, hid_ref):
        hv = h_ref[...]
        a = _dot(hv, wg_ref[...])
        b = _dot(hv, wu_ref[...])
        a_ref[...] = a
        b_ref[...] = b
        hid_ref[...] = (a * _sigmoid(a) * b).astype(BF16)

    return _pcall(
        body, name="ffn_up", grid=(FH // tn, S // tm),
        in_specs=[pl.BlockSpec((tm, D), lambda j, i: (i, 0)), pl.BlockSpec((D, tn), lambda j, i: (0, j)),
                  pl.BlockSpec((D, tn), lambda j, i: (0, j))],
        out_specs=[pl.BlockSpec((tm, tn), lambda j, i: (i, j))] * 3,
        out_shape=[jax.ShapeDtypeStruct((S, FH), F32), jax.ShapeDtypeStruct((S, FH), F32),
                   jax.ShapeDtypeStruct((S, FH), BF16)],
        compiler_params=_cp("parallel", "arbitrary"))(h, wg, wu)


def ffn_bwd_hidden(dout, wd, a, b, tm=256, tn=1408):
    S = dout.shape[0]
    tm = min(tm, S)

    def body(do_ref, wd_ref, a_ref, b_ref, da_ref, db_ref):
        dhid = _dot_nt(do_ref[...].astype(BF16), wd_ref[...])
        av, bv = a_ref[...], b_ref[...]
        s = _sigmoid(av)
        da_ref[...] = (dhid * bv * (s * (1.0 + av * (1.0 - s)))).astype(BF16)
        db_ref[...] = (dhid * (av * s)).astype(BF16)

    return _pcall(
        body, name="ffn_bwd_hidden", grid=(FH // tn, S // tm),
        in_specs=[pl.BlockSpec((tm, D), lambda j, i: (i, 0)), pl.BlockSpec((tn, D), lambda j, i: (j, 0)),
                  pl.BlockSpec((tm, tn), lambda j, i: (i, j)), pl.BlockSpec((tm, tn), lambda j, i: (i, j))],
        out_specs=[pl.BlockSpec((tm, tn), lambda j, i: (i, j))] * 2,
        out_shape=[jax.ShapeDtypeStruct((S, FH), BF16)] * 2,
        compiler_params=_cp("parallel", "arbitrary"))(dout, wd, a, b)


POOL_T = 128
POOL_HALO = 16


def pool_fwd(u, wgrp, scale, x_res):
    S = u.shape[0]
    T, HB = min(POOL_T, S), POOL_HALO
    per = T // HB

    def body(u_ref, tail_ref, wg_ref, sc_ref, x_ref, xo_ref, p_ref):
        i = pl.program_id(0)
        uc = u_ref[...]
        tail = jnp.where(i > 0, tail_ref[...], 0.0)
        d_cur = _iota((T, T), 0) - _iota((T, T), 1)
        d_tail = _iota((T, HB), 0) - _iota((T, HB), 1) + HB
        tg = i * T + _iota((T, 1), 0)
        for g, w in enumerate(POOL_WINDOWS):
            gs = slice(g * PG, (g + 1) * PG)
            band = ((d_cur >= 0) & (d_cur < w)).astype(F32)
            band_t = ((d_tail >= 0) & (d_tail < w)).astype(F32)
            ug = uc[:, gs]
            ws = _dot(band, ug, HI) + _dot(band_t, tail[:, gs], HI)
            cnt = jnp.minimum(tg + 1, w).astype(F32)
            pb = (ws / cnt - ug).astype(BF16)
            p_ref[:, gs] = pb
            xo_ref[:, gs] = x_ref[:, gs] + _dot(pb, wg_ref[g]) * sc_ref[:, gs]

    return _pcall(
        body, name="pool_fwd", grid=(S // T,),
        in_specs=[pl.BlockSpec((T, D), lambda i: (i, 0)),
                  pl.BlockSpec((HB, D), lambda i: (jnp.maximum(i * per - 1, 0), 0)),
                  pl.BlockSpec((4, PG, PG), lambda i: (0, 0, 0)), pl.BlockSpec((1, D), lambda i: (0, 0)),
                  pl.BlockSpec((T, D), lambda i: (i, 0))],
        out_specs=[pl.BlockSpec((T, D), lambda i: (i, 0))] * 2,
        out_shape=[jax.ShapeDtypeStruct((S, D), F32), jax.ShapeDtypeStruct((S, D), BF16)],
        compiler_params=_cp("parallel"))(u, u, wgrp, scale, x_res)


def pool_bwd_group(dm, p, wgrp, scale, tm=512):
    S = dm.shape[0]
    tm = min(tm, S)

    def body(dm_ref, p_ref, wg_ref, sc_ref, dp_ref, dwg_ref, dsc_ref):
        i = pl.program_id(0)

        @pl.when(i == 0)
        def _():
            dwg_ref[...] = jnp.zeros_like(dwg_ref)
            dsc_ref[...] = jnp.zeros_like(dsc_ref)

        for g in range(4):
            gs = slice(g * PG, (g + 1) * PG)
            dmg, pg, wg = dm_ref[:, gs], p_ref[:, gs], wg_ref[g]
            dsc_ref[:, gs] += jnp.sum(dmg * _dot(pg, wg), axis=0, keepdims=True)
            dy = (dmg * sc_ref[:, gs]).astype(BF16)
            dp_ref[:, gs] = _dot_nt(dy, wg)
            dwg_ref[g] += _dot_tn(pg, dy)

    return _pcall(
        body, name="pool_bwd_group", grid=(S // tm,),
        in_specs=[pl.BlockSpec((tm, D), lambda i: (i, 0)), pl.BlockSpec((tm, D), lambda i: (i, 0)),
                  pl.BlockSpec((4, PG, PG), lambda i: (0, 0, 0)), pl.BlockSpec((1, D), lambda i: (0, 0))],
        out_specs=[pl.BlockSpec((tm, D), lambda i: (i, 0)), pl.BlockSpec((4, PG, PG), lambda i: (0, 0, 0)),
                   pl.BlockSpec((1, D), lambda i: (0, 0))],
        out_shape=[jax.ShapeDtypeStruct((S, D), F32), jax.ShapeDtypeStruct((4, PG, PG), F32),
                   jax.ShapeDtypeStruct((1, D), F32)],
        compiler_params=_cp("arbitrary"))(dm, p, wgrp, scale)


def pool_bwd_window(dp):
    S = dp.shape[0]
    T, HB = min(POOL_T, S), POOL_HALO
    per = T // HB
    nt = S // T

    def body(dp_ref, nxt_ref, du_ref):
        i = pl.program_id(0)
        dc = dp_ref[...]
        nxt = jnp.where(i < nt - 1, nxt_ref[...], 0.0)
        d_cur = _iota((T, T), 1) - _iota((T, T), 0)
        d_nxt = _iota((T, HB), 1) - _iota((T, HB), 0) + T
        tg = i * T + _iota((T, 1), 0)
        tn_ = (i + 1) * T + _iota((HB, 1), 0)
        for g, w in enumerate(POOL_WINDOWS):
            gs = slice(g * PG, (g + 1) * PG)
            band = ((d_cur >= 0) & (d_cur < w)).astype(F32)
            band_n = ((d_nxt >= 0) & (d_nxt < w)).astype(F32)
            dcg = dc[:, gs]
            cur = dcg / jnp.minimum(tg + 1, w).astype(F32)
            nx = nxt[:, gs] / jnp.minimum(tn_ + 1, w).astype(F32)
            du_ref[:, gs] = (_dot(band, cur, HI) + _dot(band_n, nx, HI) - dcg).astype(BF16)

    return _pcall(
        body, name="pool_bwd_window", grid=(nt,),
        in_specs=[pl.BlockSpec((T, D), lambda i: (i, 0)),
                  pl.BlockSpec((HB, D), lambda i: (jnp.minimum((i + 1) * per, S // HB - 1), 0))],
        out_specs=pl.BlockSpec((T, D), lambda i: (i, 0)),
        out_shape=jax.ShapeDtypeStruct((S, D), BF16),
        compiler_params=_cp("parallel"))(dp, dp)


CONV_T = 256


def _shift_down(xc, prev8, j):
    if j == 0:
        return xc
    T = xc.shape[0]
    body = pltpu.roll(xc, j, 0)
    first = jnp.where(_iota((8, 1), 0) < j, pltpu.roll(prev8, j, 0), body[0:8])
    return jnp.concatenate([first, body[8:T]], axis=0)


def _shift_up(dc, next8, j):
    if j == 0:
        return dc
    T = dc.shape[0]
    body = pltpu.roll(dc, T - j, 0)
    last = jnp.where(_iota((8, 1), 0) + j < 8, body[T - 8:T], pltpu.roll(next8, 8 - j, 0))
    return jnp.concatenate([body[0:T - 8], last], axis=0)


def conv_fwd(xbc, conv_w, conv_b):
    S = xbc.shape[0]
    T = min(CONV_T, S)
    CB = 1024

    def body(x_ref, prev_ref, w_ref, b_ref, o_ref):
        i = pl.program_id(1)
        xc = x_ref[...]
        prev8 = jnp.where(i > 0, prev_ref[...], 0.0)
        pre = b_ref[...] + w_ref[3:4, :] * xc
        for j in range(1, 4):
            pre = pre + w_ref[3 - j:4 - j, :] * _shift_down(xc, prev8, j)
        o_ref[...] = pre * _sigmoid(pre)

    return _pcall(
        body, name="conv_fwd", grid=(CONV_CH // CB, S // T),
        in_specs=[pl.BlockSpec((T, CB), lambda c, i: (i, c)),
                  pl.BlockSpec((8, CB), lambda c, i: (jnp.maximum(i * (T // 8) - 1, 0), c)),
                  pl.BlockSpec((4, CB), lambda c, i: (0, c)), pl.BlockSpec((1, CB), lambda c, i: (0, c))],
        out_specs=pl.BlockSpec((T, CB), lambda c, i: (i, c)),
        out_shape=jax.ShapeDtypeStruct((S, CONV_CH), F32),
        compiler_params=_cp("parallel", "parallel"))(xbc, xbc, conv_w, conv_b)


def conv_bwd_pre(dact, xbc, conv_w, conv_b):
    S = xbc.shape[0]
    T = min(CONV_T, S)
    CB = 1024

    def body(da_ref, x_ref, prev_ref, w_ref, b_ref, dpre_ref, dw_ref, db_ref):
        i = pl.program_id(1)
        xc = x_ref[...]
        prev8 = jnp.where(i > 0, prev_ref[...], 0.0)
        sh = [_shift_down(xc, prev8, j) for j in range(4)]
        pre = b_ref[...] + w_ref[3:4, :] * sh[0]
        for j in range(1, 4):
            pre = pre + w_ref[3 - j:4 - j, :] * sh[j]
        s = _sigmoid(pre)
        dpre = da_ref[...] * (s * (1.0 + pre * (1.0 - s)))
        dpre_ref[...] = dpre
        rows = [jnp.sum(dpre * sh[3 - k], axis=0, keepdims=True) for k in range(4)]
        dw = jnp.concatenate(rows + [jnp.zeros((4, CB), F32)], axis=0)
        db = jnp.sum(dpre, axis=0, keepdims=True)

        @pl.when(i == 0)
        def _():
            dw_ref[...] = dw
            db_ref[...] = db

        @pl.when(i > 0)
        def _():
            dw_ref[...] += dw
            db_ref[...] += db

    return _pcall(
        body, name="conv_bwd_pre", grid=(CONV_CH // CB, S // T),
        in_specs=[pl.BlockSpec((T, CB), lambda c, i: (i, c)), pl.BlockSpec((T, CB), lambda c, i: (i, c)),
                  pl.BlockSpec((8, CB), lambda c, i: (jnp.maximum(i * (T // 8) - 1, 0), c)),
                  pl.BlockSpec((4, CB), lambda c, i: (0, c)), pl.BlockSpec((1, CB), lambda c, i: (0, c))],
        out_specs=[pl.BlockSpec((T, CB), lambda c, i: (i, c)), pl.BlockSpec((8, CB), lambda c, i: (0, c)),
                   pl.BlockSpec((1, CB), lambda c, i: (0, c))],
        out_shape=[jax.ShapeDtypeStruct((S, CONV_CH), F32), jax.ShapeDtypeStruct((8, CONV_CH), F32),
                   jax.ShapeDtypeStruct((1, CONV_CH), F32)],
        compiler_params=_cp("parallel", "arbitrary"))(dact, xbc, xbc, conv_w, conv_b)


def conv_bwd_input(dpre, conv_w):
    S = dpre.shape[0]
    T = min(CONV_T, S)
    CB = 1024
    nt = S // T

    def body(d_ref, nxt_ref, w_ref, o_ref):
        i = pl.program_id(1)
        dc = d_ref[...]
        next8 = jnp.where(i < nt - 1, nxt_ref[...], 0.0)
        acc = w_ref[3:4, :] * dc
        for j in range(1, 4):
            acc = acc + w_ref[3 - j:4 - j, :] * _shift_up(dc, next8, j)
        o_ref[...] = acc.astype(BF16)

    return _pcall(
        body, name="conv_bwd_input", grid=(CONV_CH // CB, nt),
        in_specs=[pl.BlockSpec((T, CB), lambda c, i: (i, c)),
                  pl.BlockSpec((8, CB), lambda c, i: (jnp.minimum((i + 1) * (T // 8), S // 8 - 1), c)),
                  pl.BlockSpec((4, CB), lambda c, i: (0, c))],
        out_specs=pl.BlockSpec((T, CB), lambda c, i: (i, c)),
        out_shape=jax.ShapeDtypeStruct((S, CONV_CH), BF16),
        compiler_params=_cp("parallel", "parallel"))(dpre, dpre, conv_w)


def _ssd_chunk_terms(dt_ref, bias_ref, alog_ref):
    L = CH
    dtp = dt_ref[...] + bias_ref[...]
    dt = jnp.maximum(dtp, 0.0) + jnp.log(1.0 + jnp.exp(-jnp.abs(dtp)))
    a = -jnp.exp(alog_ref[...])
    da = dt * a
    tri = (_iota((L, L), 0) >= _iota((L, L), 1)).astype(F32)
    acum = _dot(tri, da, HI)
    triu = (_iota((L, L), 0) <= _iota((L, L), 1)).astype(F32)
    acum_row = _dot_tn(da, triu, HI)
    expand = (_iota((NH, DI), 1) // HP == _iota((NH, DI), 0)).astype(F32)
    return dtp, dt, a, da, acum, acum_row, expand


def ssd_scan_fwd(xbc_act, dt_raw, dt_bias, a_log, d_full):
    S = xbc_act.shape[0]
    L = CH
    nc = S // L

    def body(xs_ref, b_ref, c_ref, dt_ref, bias_ref, alog_ref, d_ref, y_ref, st_ref, state):
        c = pl.program_id(0)

        @pl.when(c == 0)
        def _():
            state[...] = jnp.zeros_like(state)

        st_ref[0] = state[...]
        _, dt, _, _, acum, acum_row, expand = _ssd_chunk_terms(dt_ref, bias_ref, alog_ref)
        e_full = _dot(jnp.exp(acum), expand, HI)
        w_full = _dot(jnp.exp(acum[L - 1:L, :] - acum), expand, HI)
        dt_full = _dot(dt, expand, HI)
        causal = _iota((L, L), 0) >= _iota((L, L), 1)
        lane_head = _iota((1, GW), 1) // HP
        for g in range(NG):
            gs = slice(g * GW, (g + 1) * GW)
            ns = slice(g * NS, (g + 1) * NS)
            xs_g = xs_ref[:, gs]
            xdt_g = xs_g * dt_full[:, gs]
            cg = c_ref[:, ns].astype(BF16)
            bg = b_ref[:, ns].astype(BF16)
            gmat = _dot_nt(cg, bg)
            yg = jnp.zeros((L, GW), F32)
            for hh in range(4):
                h = 4 * g + hh
                diff = acum[:, h:h + 1] - acum_row[h:h + 1, :]
                dk = jnp.exp(jnp.where(causal, diff, -1e30))
                xm = jnp.where(lane_head == hh, xdt_g, 0.0).astype(BF16)
                yg = yg + _dot((gmat * dk).astype(BF16), xm)
            sg = state[g]
            yoff = _dot(cg, sg.astype(BF16)) * e_full[:, gs]
            y_ref[:, gs] = yg + yoff + d_ref[:, gs] * xs_g
            state[g] = sg * e_full[L - 1:L, gs] + _dot_tn(bg, (w_full[:, gs] * xdt_g).astype(BF16))

    return _pcall(
        body, name="ssd_scan_fwd", grid=(nc,),
        in_specs=[pl.BlockSpec((L, DI), lambda c: (c, 0)), pl.BlockSpec((L, 1024), lambda c: (c, 2)),
                  pl.BlockSpec((L, 1024), lambda c: (c, 3)), pl.BlockSpec((L, NH), lambda c: (c, 0)),
                  pl.BlockSpec((1, NH), lambda c: (0, 0)), pl.BlockSpec((1, NH), lambda c: (0, 0)),
                  pl.BlockSpec((1, DI), lambda c: (0, 0))],
        out_specs=[pl.BlockSpec((L, DI), lambda c: (c, 0)), pl.BlockSpec((1, NG, NS, GW), lambda c: (c, 0, 0, 0))],
        out_shape=[jax.ShapeDtypeStruct((S, DI), F32), jax.ShapeDtypeStruct((nc, NG, NS, GW), F32)],
        scratch_shapes=[pltpu.VMEM((NG, NS, GW), F32)],
        compiler_params=_cp("arbitrary"))(xbc_act, xbc_act, xbc_act, dt_raw, dt_bias, a_log, d_full)


def ssd_scan_bwd(dy, xbc_act, dt_raw, dt_bias, a_log, d_full, states):
    S = xbc_act.shape[0]
    L = CH
    nc = S // L

    def body(dy_ref, xs_ref, b_ref, c_ref, dt_ref, bias_ref, alog_ref, d_ref, st_ref,
             dxbc_ref, ddt_ref, dbias_ref, dalog_ref, dd_ref, dstate):
        c = pl.program_id(0)

        @pl.when(c == 0)
        def _():
            dstate[...] = jnp.zeros_like(dstate)
            dbias_ref[...] = jnp.zeros_like(dbias_ref)
            dalog_ref[...] = jnp.zeros_like(dalog_ref)
            dd_ref[...] = jnp.zeros_like(dd_ref)

        dtp, dt, a, _, acum, acum_row, expand = _ssd_chunk_terms(dt_ref, bias_ref, alog_ref)
        e_full = _dot(jnp.exp(acum), expand, HI)
        w_h = jnp.exp(acum[L - 1:L, :] - acum)
        w_full = _dot(w_h, expand, HI)
        dt_full = _dot(dt, expand, HI)
        causal = _iota((L, L), 0) >= _iota((L, L), 1)
        lane_head = _iota((1, GW), 1) // HP
        ones_l = jnp.ones((L, LANES), F32)
        head_id = _iota((1, NH), 1)
        dacum = jnp.zeros((L, NH), F32)
        red_parts = []
        dxdt_parts = []
        alast_parts = []
        for g in range(NG):
            gs = slice(g * GW, (g + 1) * GW)
            ns = slice(g * NS, (g + 1) * NS)
            xs_g = xs_ref[:, gs]
            xdt_g = xs_g * dt_full[:, gs]
            dy_g = dy_ref[:, gs]
            cg = c_ref[:, ns].astype(BF16)
            bg = b_ref[:, ns].astype(BF16)
            gmat = _dot_nt(cg, bg)
            sg = st_ref[0, g]
            dsg = dstate[g]
            sgb, dsgb = sg.astype(BF16), dsg.astype(BF16)
            cs = _dot(cg, sgb)
            bds = _dot(bg, dsgb)
            e_g, w_g = e_full[:, gs], w_full[:, gs]
            dxdt = w_g * bds
            dgsum = jnp.zeros((L, L), F32)
            for hh in range(4):
                h = 4 * g + hh
                hm = lane_head == hh
                diff = acum[:, h:h + 1] - acum_row[h:h + 1, :]
                dk = jnp.exp(jnp.where(causal, diff, -1e30))
                m = gmat * dk
                dym = jnp.where(hm, dy_g, 0.0).astype(BF16)
                xm = jnp.where(hm, xdt_g, 0.0).astype(BF16)
                dm = _dot_nt(dym, xm)
                dxdt = dxdt + _dot_tn(m.astype(BF16), dym)
                dgsum = dgsum + dm * dk
                em = dm * m
                rs = _dot(em, ones_l, HI)[:, 0:1]
                cs_ = _dot_tn(em, ones_l, HI)[:, 0:1]
                dacum = dacum + (rs - cs_) * (head_id == h).astype(F32)
            dgb = dgsum.astype(BF16)
            edy = (e_g * dy_g).astype(BF16)
            wx = (w_g * xdt_g).astype(BF16)
            dc_g = _dot(dgb, bg) + _dot_nt(edy, sgb)
            db_g = _dot_tn(dgb, cg) + _dot_nt(wx, dsgb)
            dxbc_ref[:, DI + g * NS:DI + (g + 1) * NS] = db_g
            dxbc_ref[:, DI + 1024 + g * NS:DI + 1024 + (g + 1) * NS] = dc_g
            p2w = bds * xdt_g * w_g
            red_parts.append(dy_g * cs * e_g - p2w)
            alast_parts.append(jnp.sum(p2w, axis=0, keepdims=True)
                               + e_full[L - 1:L, gs] * jnp.sum(dsg * sg, axis=0, keepdims=True))
            dxdt_parts.append(dxdt)
            dstate[g] = e_full[L - 1:L, gs] * dsg + _dot_tn(cg, edy)
            dxbc_ref[:, gs] = dxdt * dt_full[:, gs] + dy_g * d_ref[:, gs]
            dd_ref[:, gs] += jnp.sum(dy_g * xs_g, axis=0, keepdims=True)
        red = jnp.concatenate(red_parts, axis=1)
        dxdt_all = jnp.concatenate(dxdt_parts, axis=1)
        alast = jnp.concatenate(alast_parts, axis=1)
        dacum = dacum + _dot_nt(red, expand, HI)
        dalast = _dot_nt(jnp.broadcast_to(alast, (8, DI)), expand, HI)[0:1, :]
        dacum = dacum + jnp.where(_iota((L, 1), 0) == L - 1, dalast, 0.0)
        triu = (_iota((L, L), 0) <= _iota((L, L), 1)).astype(F32)
        dda = _dot(triu, dacum, HI)
        ddt = _dot_nt(dxdt_all * xs_ref[...], expand, HI) + dda * a
        dalog_ref[...] += jnp.sum(dda * dt, axis=0, keepdims=True) * a
        ddt_raw = ddt * _sigmoid(dtp)
        ddt_ref[...] = ddt_raw
        dbias_ref[...] += jnp.sum(ddt_raw, axis=0, keepdims=True)

    rev = lambda c: (nc - 1 - c, 0)
    return _pcall(
        body, name="ssd_scan_bwd", grid=(nc,),
        in_specs=[pl.BlockSpec((L, DI), rev), pl.BlockSpec((L, DI), rev),
                  pl.BlockSpec((L, 1024), lambda c: (nc - 1 - c, 2)), pl.BlockSpec((L, 1024), lambda c: (nc - 1 - c, 3)),
                  pl.BlockSpec((L, NH), rev), pl.BlockSpec((1, NH), lambda c: (0, 0)),
                  pl.BlockSpec((1, NH), lambda c: (0, 0)), pl.BlockSpec((1, DI), lambda c: (0, 0)),
                  pl.BlockSpec((1, NG, NS, GW), lambda c: (nc - 1 - c, 0, 0, 0))],
        out_specs=[pl.BlockSpec((L, CONV_CH), rev), pl.BlockSpec((L, NH), rev),
                   pl.BlockSpec((1, NH), lambda c: (0, 0)), pl.BlockSpec((1, NH), lambda c: (0, 0)),
                   pl.BlockSpec((1, DI), lambda c: (0, 0))],
        out_shape=[jax.ShapeDtypeStruct((S, CONV_CH), F32), jax.ShapeDtypeStruct((S, NH), F32),
                   jax.ShapeDtypeStruct((1, NH), F32), jax.ShapeDtypeStruct((1, NH), F32),
                   jax.ShapeDtypeStruct((1, DI), F32)],
        scratch_shapes=[pltpu.VMEM((NG, NS, GW), F32)],
        compiler_params=_cp("arbitrary"))(dy, xbc_act, xbc_act, xbc_act, dt_raw, dt_bias, a_log, d_full, states)


def gate_norm_fwd(y, z, out_norm, tm=256):
    S = y.shape[0]
    tm = min(tm, S)

    def body(y_ref, z_ref, on_ref, o_ref):
        zv = z_ref[...]
        gin = y_ref[...] * (zv * _sigmoid(zv))
        for g in range(NG):
            gs = slice(g * GW, (g + 1) * GW)
            blk = gin[:, gs]
            r = lax.rsqrt(jnp.mean(blk * blk, axis=-1, keepdims=True) + EPS)
            o_ref[:, gs] = (blk * r * on_ref[:, gs]).astype(BF16)

    return _pcall(
        body, name="gate_norm_fwd", grid=(S // tm,),
        in_specs=[pl.BlockSpec((tm, DI), lambda i: (i, 0)), pl.BlockSpec((tm, DI), lambda i: (i, 0)),
                  pl.BlockSpec((1, DI), lambda i: (0, 0))],
        out_specs=pl.BlockSpec((tm, DI), lambda i: (i, 0)),
        out_shape=jax.ShapeDtypeStruct((S, DI), BF16),
        compiler_params=_cp("parallel"))(y, z, out_norm)


def gate_norm_bwd(dgn, y, z, out_norm, tm=256):
    S = y.shape[0]
    tm = min(tm, S)

    def body(dg_ref, y_ref, z_ref, on_ref, dy_ref, dz_ref, don_ref):
        i = pl.program_id(0)

        @pl.when(i == 0)
        def _():
            don_ref[...] = jnp.zeros_like(don_ref)

        zv, yv = z_ref[...], y_ref[...]
        s = _sigmoid(zv)
        sz = zv * s
        gin = yv * sz
        for g in range(NG):
            gs = slice(g * GW, (g + 1) * GW)
            blk = gin[:, gs]
            r = lax.rsqrt(jnp.mean(blk * blk, axis=-1, keepdims=True) + EPS)
            n = blk * r
            dg = dg_ref[:, gs]
            don_ref[:, gs] += jnp.sum(dg * n, axis=0, keepdims=True)
            dn = dg * on_ref[:, gs]
            dgin = r * (dn - n * jnp.mean(dn * n, axis=-1, keepdims=True))
            dy_ref[:, gs] = dgin * sz[:, gs]
            dz_ref[:, gs] = (dgin * yv[:, gs] * (s[:, gs] * (1.0 + zv[:, gs] * (1.0 - s[:, gs])))).astype(BF16)

    return _pcall(
        body, name="gate_norm_bwd", grid=(S // tm,),
        in_specs=[pl.BlockSpec((tm, DI), lambda i: (i, 0))] * 3 + [pl.BlockSpec((1, DI), lambda i: (0, 0))],
        out_specs=[pl.BlockSpec((tm, DI), lambda i: (i, 0)), pl.BlockSpec((tm, DI), lambda i: (i, 0)),
                   pl.BlockSpec((1, DI), lambda i: (0, 0))],
        out_shape=[jax.ShapeDtypeStruct((S, DI), F32), jax.ShapeDtypeStruct((S, DI), BF16),
                   jax.ShapeDtypeStruct((1, DI), F32)],
        compiler_params=_cp("arbitrary"))(dgn, y, z, out_norm)


SB_T = 256
SB_QSCALE = 0.125


def _head_norm(xv, lo):
    sq = xv * xv
    s0 = jnp.sum(jnp.where(lo, sq, 0.0), axis=-1, keepdims=True)
    s1 = jnp.sum(jnp.where(lo, 0.0, sq), axis=-1, keepdims=True)
    return jnp.where(lo, lax.rsqrt(s0 / SBD + EPS), lax.rsqrt(s1 / SBD + EPS))


def sb_prep_fwd(qkv, qg, kg, tm=256):
    S = qkv.shape[0]
    tm = min(tm, S)

    def body(x_ref, qg_ref, kg_ref, q_ref, k_ref, v_ref):
        lo = _iota((1, LANES), 1) < SBD
        for sl in range(D // LANES):
            cs = slice(sl * LANES, (sl + 1) * LANES)
            xq = x_ref[:, cs]
            q_ref[:, cs] = ((xq * _head_norm(xq, lo) * qg_ref[...]).astype(BF16).astype(F32) * SB_QSCALE).astype(BF16)
            xk = x_ref[:, D + sl * LANES:D + (sl + 1) * LANES]
            k_ref[:, cs] = (xk * _head_norm(xk, lo) * kg_ref[...]).astype(BF16)
        v_ref[...] = x_ref[:, 2 * D:3 * D].astype(BF16)

    return _pcall(
        body, name="sb_prep_fwd", grid=(S // tm,),
        in_specs=[pl.BlockSpec((tm, 3 * D), lambda i: (i, 0)), pl.BlockSpec((1, LANES), lambda i: (0, 0)),
                  pl.BlockSpec((1, LANES), lambda i: (0, 0))],
        out_specs=[pl.BlockSpec((tm, D), lambda i: (i, 0))] * 3,
        out_shape=[jax.ShapeDtypeStruct((S, D), BF16)] * 3,
        compiler_params=_cp("parallel"))(qkv, qg, kg)


def sb_prep_bwd(dqs, dkn, dv, qkv, qg, kg, tm=256):
    S = qkv.shape[0]
    tm = min(tm, S)

    def body(dq_ref, dk_ref, dv_ref, x_ref, qg_ref, kg_ref, dx_ref, dqg_ref, dkg_ref):
        i = pl.program_id(0)

        @pl.when(i == 0)
        def _():
            dqg_ref[...] = jnp.zeros_like(dqg_ref)
            dkg_ref[...] = jnp.zeros_like(dkg_ref)

        lo = _iota((1, LANES), 1) < SBD

        def one(xv, dh, gain):
            r = _head_norm(xv, lo)
            y = xv * r
            dy = dh * gain
            t = dy * y
            m0 = jnp.sum(jnp.where(lo, t, 0.0), axis=-1, keepdims=True)
            m1 = jnp.sum(jnp.where(lo, 0.0, t), axis=-1, keepdims=True)
            dx = r * (dy - y * (jnp.where(lo, m0, m1) / SBD))
            return dx, jnp.sum(dh * y, axis=0, keepdims=True)

        for sl in range(D // LANES):
            cs = slice(sl * LANES, (sl + 1) * LANES)
            dx, dg = one(x_ref[:, cs], dq_ref[:, cs] * SB_QSCALE, qg_ref[...])
            dx_ref[:, cs] = dx.astype(BF16)
            dqg_ref[:, cs] += dg
            ks = slice(D + sl * LANES, D + (sl + 1) * LANES)
            dx, dg = one(x_ref[:, ks], dk_ref[:, cs], kg_ref[...])
            dx_ref[:, ks] = dx.astype(BF16)
            dkg_ref[:, cs] += dg
        dx_ref[:, 2 * D:3 * D] = dv_ref[...].astype(BF16)

    return _pcall(
        body, name="sb_prep_bwd", grid=(S // tm,),
        in_specs=[pl.BlockSpec((tm, D), lambda i: (i, 0))] * 3
        + [pl.BlockSpec((tm, 3 * D), lambda i: (i, 0)), pl.BlockSpec((1, LANES), lambda i: (0, 0)),
           pl.BlockSpec((1, LANES), lambda i: (0, 0))],
        out_specs=[pl.BlockSpec((tm, 3 * D), lambda i: (i, 0)), pl.BlockSpec((1, D), lambda i: (0, 0)),
                   pl.BlockSpec((1, D), lambda i: (0, 0))],
        out_shape=[jax.ShapeDtypeStruct((S, 3 * D), BF16), jax.ShapeDtypeStruct((1, D), F32),
                   jax.ShapeDtypeStruct((1, D), F32)],
        compiler_params=_cp("arbitrary"))(dqs, dkn, dv, qkv, qg, kg)


def _split_dot(x, u):
    hi = x.astype(BF16)
    lo = (x - hi.astype(F32)).astype(BF16)
    return _dot(hi, u) + _dot(lo, u)


def _sb_logits(qh, kb, valid):
    z = _dot_nt(qh, kb)
    e = jnp.exp(-jnp.abs(z))
    lp = jnp.log(1.0 + e)
    lb = jnp.minimum(z, 0.0) - lp
    l1m = jnp.where(valid, lb - z, 0.0)
    return z, e, lb, l1m


def sb_fwd(qs, kn, v):
    S = qs.shape[0]
    T = min(SB_T, S)
    nq = S // T

    def body(q_ref, k_ref, v_ref, o_ref, r_ref, oacc, rrun):
        i = pl.program_id(1)
        qb = q_ref[...]
        lo = _iota((1, LANES), 1) < SBD
        row, col = _iota((T, T), 0), _iota((T, T), 1)
        u = (row > col).astype(BF16)
        lane_blk = _iota((T, LANES), 1)
        oacc[...] = jnp.zeros_like(oacc)
        for hh in range(2):
            hm = lo if hh == 0 else jnp.logical_not(lo)
            qh = jnp.where(hm, qb, jnp.zeros_like(qb))
            rrun[...] = jnp.zeros_like(rrun)
            r_ref[hh] = jnp.zeros((T, LANES), F32)

            def step(s, carry, hm=hm, qh=qh, hh=hh):
                j = i - s
                off = pl.multiple_of(j * T, T)
                kb = k_ref[pl.ds(off, T), :]
                vb = v_ref[pl.ds(off, T), :]
                vb = jnp.where(hm, vb, jnp.zeros_like(vb))
                valid = (j * T + col) < (i * T + row)
                _, _, lb, l1m = _sb_logits(qh, kb, valid)
                r = rrun[...]
                aft = _split_dot(l1m, u) + r
                a = jnp.where(valid, jnp.exp(lb + aft), 0.0)
                oacc[...] += _dot(a.astype(BF16), vb)
                r_ref[hh] = jnp.where(lane_blk == j, r, r_ref[hh])
                rrun[...] = r + jnp.sum(l1m, axis=-1, keepdims=True)
                return carry

            lax.fori_loop(0, i + 1, step, 0)
        o_ref[...] = oacc[...].astype(BF16)

    return _pcall(
        body, name="sb_fwd", grid=(D // LANES, nq),
        in_specs=[pl.BlockSpec((T, LANES), lambda h, i: (i, h)), pl.BlockSpec((S, LANES), lambda h, i: (0, h)),
                  pl.BlockSpec((S, LANES), lambda h, i: (0, h))],
        out_specs=[pl.BlockSpec((T, LANES), lambda h, i: (i, h)), pl.BlockSpec((2, T, LANES), lambda h, i: (h, i, 0))],
        out_shape=[jax.ShapeDtypeStruct((S, D), BF16), jax.ShapeDtypeStruct((SBH, S, LANES), F32)],
        scratch_shapes=[pltpu.VMEM((T, LANES), F32), pltpu.VMEM((T, 1), F32)],
        compiler_params=_cp("parallel", "arbitrary"))(qs, kn, v)


def sb_bwd(qs, kn, v, do, rsave):
    S = qs.shape[0]
    T = min(SB_T, S)
    nq = S // T

    def body(q_ref, k_ref, v_ref, do_ref, r_ref, dq_ref, dk_ref, dv_ref, crun):
        i = pl.program_id(1)

        @pl.when(i == 0)
        def _():
            dk_ref[...] = jnp.zeros_like(dk_ref)
            dv_ref[...] = jnp.zeros_like(dv_ref)

        qb, dob = q_ref[...], do_ref[...]
        lo = _iota((1, LANES), 1) < SBD
        row, col = _iota((T, T), 0), _iota((T, T), 1)
        u = (row > col).astype(BF16)
        u2 = (row < col).astype(BF16)
        lane_blk = _iota((T, LANES), 1)
        dq_ref[...] = jnp.zeros_like(dq_ref)
        for hh in range(2):
            hm = lo if hh == 0 else jnp.logical_not(lo)
            qh = jnp.where(hm, qb, jnp.zeros_like(qb))
            doh = jnp.where(hm, dob, jnp.zeros_like(dob))
            crun[...] = jnp.zeros_like(crun)

            def step(j, carry, hm=hm, qh=qh, doh=doh, hh=hh):
                off = pl.multiple_of(j * T, T)
                kb = k_ref[pl.ds(off, T), :]
                vb = v_ref[pl.ds(off, T), :]
                vb = jnp.where(hm, vb, jnp.zeros_like(vb))
                valid = (j * T + col) < (i * T + row)
                z, e, lb, l1m = _sb_logits(qh, kb, valid)
                r = jnp.sum(jnp.where(lane_blk == j, r_ref[hh], 0.0), axis=-1, keepdims=True)
                aft = _split_dot(l1m, u) + r
                a = jnp.where(valid, jnp.exp(lb + aft), 0.0)
                w = a * _dot_nt(doh, vb)
                cprev = crun[...]
                cw = _split_dot(w, u2) + cprev
                inv = 1.0 / (1.0 + e)
                pos = z >= 0.0
                beta = jnp.where(pos, 1.0, e) * inv
                onem = jnp.where(pos, e, 1.0) * inv
                dz = jnp.where(valid, w * onem - beta * cw, 0.0).astype(BF16)
                dq_ref[...] += _dot(dz, jnp.where(hm, kb, jnp.zeros_like(kb)))
                dk_ref[pl.ds(off, T), :] += _dot_tn(dz, qh)
                dv_ref[pl.ds(off, T), :] += _dot_tn(a.astype(BF16), doh)
                crun[...] = cprev + jnp.sum(w, axis=-1, keepdims=True)
                return carry

            lax.fori_loop(0, i + 1, step, 0)

    return _pcall(
        body, name="sb_bwd", grid=(D // LANES, nq),
        in_specs=[pl.BlockSpec((T, LANES), lambda h, i: (i, h)), pl.BlockSpec((S, LANES), lambda h, i: (0, h)),
                  pl.BlockSpec((S, LANES), lambda h, i: (0, h)), pl.BlockSpec((T, LANES), lambda h, i: (i, h)),
                  pl.BlockSpec((2, T, LANES), lambda h, i: (h, i, 0))],
        out_specs=[pl.BlockSpec((T, LANES), lambda h, i: (i, h)), pl.BlockSpec((S, LANES), lambda h, i: (0, h)),
                   pl.BlockSpec((S, LANES), lambda h, i: (0, h))],
        out_shape=[jax.ShapeDtypeStruct((S, D), F32)] * 3,
        scratch_shapes=[pltpu.VMEM((T, 1), F32)],
        compiler_params=_cp("parallel", "arbitrary"))(qs, kn, v, do, rsave)


def loss_head(y, target, tm=512):
    S = y.shape[0]
    tm = min(tm, S)

    def body(y_ref, t_ref, ls_ref, dy_ref):
        i = pl.program_id(0)
        err = y_ref[...] - t_ref[...]
        dy_ref[...] = err * (1.0 / D)
        part = jnp.sum(err * err, axis=0, keepdims=True)

        @pl.when(i == 0)
        def _():
            ls_ref[...] = part

        @pl.when(i > 0)
        def _():
            ls_ref[...] += part

    return _pcall(
        body, name="loss_head", grid=(S // tm,),
        in_specs=[pl.BlockSpec((tm, D), lambda i: (i, 0))] * 2,
        out_specs=[pl.BlockSpec((1, D), lambda i: (0, 0)), pl.BlockSpec((tm, D), lambda i: (i, 0))],
        out_shape=[jax.ShapeDtypeStruct((1, D), F32), jax.ShapeDtypeStruct((S, D), F32)],
        compiler_params=_cp("arbitrary"))(y, target)


def adamw(w, gs, m, v, tr=1024, name="adamw"):
    R = w.shape[0]
    tr = min(tr, R)
    n_g = len(gs)
    c1 = 1.0 / (1.0 - ADAM_B1 ** ADAM_STEP)
    c2 = 1.0 / (1.0 - ADAM_B2 ** ADAM_STEP)

    def body(*refs):
        w_ref, g_refs = refs[0], refs[1:1 + n_g]
        m_ref, v_ref, go_ref, d_ref, mo_ref, vo_ref = refs[1 + n_g:]
        g = g_refs[0][...]
        for r in g_refs[1:]:
            g = g + r[...]
        mn = ADAM_B1 * m_ref[...] + (1.0 - ADAM_B1) * g
        vn = ADAM_B2 * v_ref[...] + (1.0 - ADAM_B2) * (g * g)
        go_ref[...] = g
        mo_ref[...] = mn
        vo_ref[...] = vn
        d_ref[...] = -ADAM_LR * ((mn * c1) / (jnp.sqrt(vn * c2) + ADAM_EPS) + ADAM_WD * w_ref[...])

    spec = pl.BlockSpec((tr, LANES), lambda i: (i, 0))
    return _pcall(
        body, name=name, grid=(R // tr,), in_specs=[spec] * (3 + n_g), out_specs=[spec] * 4,
        out_shape=[jax.ShapeDtypeStruct((R, LANES), F32)] * 4,
        compiler_params=_cp("parallel"))(w, *gs, m, v)


def sum4(parts, tr=1024):
    R = parts[0].shape[0]
    tr = min(tr, R)

    def body(a_ref, b_ref, c_ref, d_ref, o_ref):
        o_ref[...] = ((a_ref[...] + b_ref[...]) + c_ref[...]) + d_ref[...]

    spec = pl.BlockSpec((tr, LANES), lambda i: (i, 0))
    return _pcall(
        body, name="sum4", grid=(R // tr,), in_specs=[spec] * 4, out_specs=spec,
        out_shape=jax.ShapeDtypeStruct((R, LANES), F32), compiler_params=_cp("parallel"))(*parts)


MESH = pl.DeviceIdType.MESH
ANY = pl.BlockSpec(memory_space=pl.ANY)


def _other_chips(x, y):
    return [(1 - x, y), (x, 1 - y), (1 - x, 1 - y)]


def gather_chips(shard):
    def body(in_ref, out_ref, send_sems, recv_sems, local_sem):
        x, y, c = lax.axis_index("x"), lax.axis_index("y"), lax.axis_index("c")
        me = 2 * x + y
        mine = pltpu.make_async_copy(in_ref, out_ref.at[me], local_sem)
        mine.start()
        copies = [pltpu.make_async_remote_copy(
            src_ref=in_ref, dst_ref=out_ref.at[me], send_sem=send_sems.at[k], recv_sem=recv_sems.at[k],
            device_id=(px, py, c), device_id_type=MESH) for k, (px, py) in enumerate(_other_chips(x, y))]
        for cp in copies:
            cp.start()
        for cp in copies:
            cp.wait()
        mine.wait()

    return _pcall(
        body, name="gather_chips", in_specs=[ANY], out_specs=ANY,
        out_shape=jax.ShapeDtypeStruct((N_CHIPS,) + shard.shape, shard.dtype),
        scratch_shapes=[pltpu.SemaphoreType.DMA((3,)), pltpu.SemaphoreType.DMA((3,)), pltpu.SemaphoreType.DMA])(shard)


def scatter_chips(gstack):
    def body(in_ref, out_ref, send_sems, recv_sems):
        x, y, c = lax.axis_index("x"), lax.axis_index("y"), lax.axis_index("c")
        copies = [pltpu.make_async_remote_copy(
            src_ref=in_ref.at[2 * px + py], dst_ref=out_ref.at[k], send_sem=send_sems.at[k],
            recv_sem=recv_sems.at[k], device_id=(px, py, c), device_id_type=MESH)
            for k, (px, py) in enumerate(_other_chips(x, y))]
        for cp in copies:
            cp.start()
        for cp in copies:
            cp.wait()

    return _pcall(
        body, name="scatter_chips", in_specs=[ANY], out_specs=ANY,
        out_shape=jax.ShapeDtypeStruct((3,) + gstack.shape[1:], gstack.dtype),
        scratch_shapes=[pltpu.SemaphoreType.DMA((3,)), pltpu.SemaphoreType.DMA((3,))])(gstack)


def swap_cores(part):
    def body(in_ref, out_ref, send_sem, recv_sem):
        x, y, c = lax.axis_index("x"), lax.axis_index("y"), lax.axis_index("c")
        cp = pltpu.make_async_remote_copy(src_ref=in_ref, dst_ref=out_ref, send_sem=send_sem, recv_sem=recv_sem,
                                          device_id=(x, y, 1 - c), device_id_type=MESH)
        cp.start()
        cp.wait()

    return _pcall(
        body, name="swap_cores", in_specs=[ANY], out_specs=ANY,
        out_shape=jax.ShapeDtypeStruct(part.shape, part.dtype),
        scratch_shapes=[pltpu.SemaphoreType.DMA, pltpu.SemaphoreType.DMA])(part)


def allreduce_small(vec):
    R = vec.shape[0]

    def body(in_ref, out_ref, buf, send_sems, recv_sems):
        x, y, c = lax.axis_index("x"), lax.axis_index("y"), lax.axis_index("c")
        me = 4 * x + 2 * y + c
        buf[me] = in_ref[...]
        copies = []
        for k in range(1, 8):
            peer = (x ^ (k >> 2), y ^ ((k >> 1) & 1), c ^ (k & 1))
            copies.append(pltpu.make_async_remote_copy(
                src_ref=in_ref, dst_ref=buf.at[me], send_sem=send_sems.at[k - 1], recv_sem=recv_sems.at[k - 1],
                device_id=peer, device_id_type=MESH))
        for cp in copies:
            cp.start()
        for cp in copies:
            cp.wait()
        acc = buf[0]
        for d in range(1, 8):
            acc = acc + buf[d]
        out_ref[...] = acc

    vm = pl.BlockSpec(memory_space=pltpu.VMEM)
    return _pcall(
        body, name="allreduce_small", in_specs=[vm], out_specs=vm,
        out_shape=jax.ShapeDtypeStruct((R, LANES), F32),
        scratch_shapes=[pltpu.VMEM((8, R, LANES), F32), pltpu.SemaphoreType.DMA((7,)), pltpu.SemaphoreType.DMA((7,))])(vec)


SHARDED = [("pool_in", 1, "bf16"), ("pool_group", 2, "bf16"), ("pool_scale", 1, "f32"), ("ssd_in", 2, "bf16"),
           ("ssd_conv_w", 2, "f32"), ("ssd_out", 1, "bf16"), ("sb_qkv", 2, "bf16"), ("sb_out", 1, "bf16"),
           ("ffn_gate", 2, "bf16"), ("ffn_up", 2, "bf16"), ("ffn_down", 1, "bf16")]
REPLICATED = ["mix_norm", "ssd_conv_b", "ssd_dt_bias", "ssd_a_log", "ssd_d", "ssd_out_norm", "sb_q_norm",
              "sb_k_norm", "ffn_norm"]
WEIGHT_ORDER = ["mix_norm", "pool_in", "pool_group", "pool_scale", "ssd_in", "ssd_conv_w", "ssd_conv_b",
                "ssd_dt_bias", "ssd_a_log", "ssd_d", "ssd_out_norm", "ssd_out", "sb_qkv", "sb_q_norm", "sb_k_norm",
                "sb_out", "ffn_norm", "ffn_gate", "ffn_up", "ffn_down"]
ROW_PAD = 1024


def _pad_rows(flat, width=LANES):
    n = flat.shape[0]
    rows = -(-n // width)
    rows = -(-rows // ROW_PAD) * ROW_PAD
    return jnp.pad(flat, (0, rows * width - n)).reshape(rows, width)


def _pack(arrs):
    return _pad_rows(jnp.concatenate([a.reshape(-1) for a in arrs]))


def _unpack(flat2d, shapes):
    flat = flat2d.reshape(-1)
    out, off = [], 0
    for s in shapes:
        n = math.prod(s)
        out.append(flat[off:off + n].reshape(s))
        off += n
    return out


def _pack_payload(shards):
    parts = []
    for name, _, kind in SHARDED:
        w = shards[name]
        if kind == "bf16":
            parts.append(w.astype(BF16).reshape(-1))
        else:
            parts.append(lax.bitcast_convert_type(w, BF16).reshape(-1))
    return _pad_rows(jnp.concatenate(parts))


def _unpack_payload(gathered, shards):
    flat = gathered.reshape(N_CHIPS, -1)
    full, off = {}, 0
    for name, axis, kind in SHARDED:
        shp = shards[name].shape
        n = math.prod(shp)
        if kind == "bf16":
            pieces = flat[:, off:off + n].reshape((N_CHIPS,) + shp)
            off += n
        else:
            pieces = lax.bitcast_convert_type(flat[:, off:off + 2 * n].reshape((N_CHIPS,) + shp + (2,)), F32)
            off += 2 * n
        full[name] = jnp.concatenate([pieces[j] for j in range(N_CHIPS)], axis=axis)
    return full


def _split_shards(full, axis):
    return jnp.stack(jnp.split(full, N_CHIPS, axis=axis))


def _ffn_fwd(x, gain, wg, wu, wd):
    h = rmsnorm_fwd(x, gain, name="ffn_norm_fwd")
    a, b, hid = ffn_up(h, wg, wu)
    xo = linear([(hid, wd, "nn")], res=x, name="ffn_down")
    return xo, (x, h, a, b, hid)


def _ffn_bwd(dout, saved, gain, wg, wu, wd):
    x, h, a, b, hid = saved
    da, db = ffn_bwd_hidden(dout, wd, a, b)
    (dwd,) = wgrad(hid, [dout], tk=1408, name="ffn_dwd")
    dwg, dwu = wgrad(h, [da, db], tn=1408, tm=512, name="ffn_dwgu")
    dh = linear([(da, wg, "nt"), (db, wu, "nt")], tm=256, name="ffn_dh")
    dx, dgain = rmsnorm_bwd(x, gain, dh, dout, name="ffn_norm_bwd")
    return dx, dgain, dwg, dwu, dwd


def _pool_layer_fwd(x, gain, w_in, wgrp, scale):
    h = rmsnorm_fwd(x, gain, name="pool_norm_fwd")
    u = linear([(h, w_in, "nn")], name="pool_in")
    xo, p = pool_fwd(u, wgrp, scale, x)
    return xo, (x, h, p)


def _pool_layer_bwd(dout, saved, gain, w_in, wgrp, scale):
    x, h, p = saved
    dp, dwgrp, dscale = pool_bwd_group(dout, p, wgrp, scale)
    du = pool_bwd_window(dp)
    (dw_in,) = wgrad(h, [du], name="pool_dwin")
    dh = linear([(du, w_in, "nt")], name="pool_dh")
    dx, dgain = rmsnorm_bwd(x, gain, dh, dout, name="pool_norm_bwd")
    return dx, dgain, dw_in, dwgrp, dscale


def _ssd_layer_fwd(x, gain, w_z, w_xbc, w_dt, conv_w, conv_b, dt_bias, a_log, d_full, out_norm, w_out):
    h = rmsnorm_fwd(x, gain, name="ssd_norm_fwd")
    z = linear([(h, w_z, "nn")], name="ssd_in_z")
    xbc = linear([(h, w_xbc, "nn")], tn=2048, name="ssd_in_xbc")
    dt_raw = linear([(h, w_dt, "nn")], name="ssd_in_dt")
    act = conv_fwd(xbc, conv_w, conv_b)
    y, states = ssd_scan_fwd(act, dt_raw, dt_bias, a_log, d_full)
    gn = gate_norm_fwd(y, z, out_norm)
    xo = linear([(gn, w_out, "nn")], res=x, name="ssd_out")
    return xo, (x, h, z, xbc, dt_raw, act, y, states, gn)


def _ssd_layer_bwd(dout, saved, gain, w_z, w_xbc, w_dt, conv_w, conv_b, dt_bias, a_log, d_full, out_norm, w_out):
    x, h, z, xbc, dt_raw, act, y, states, gn = saved
    dgn = linear([(dout, w_out, "nt")], name="ssd_dgn")
    (dw_out,) = wgrad(gn, [dout], name="ssd_dwout")
    dy, dz, dout_norm = gate_norm_bwd(dgn, y, z, out_norm)
    dact, ddt_raw, dbias, dalog, dd_full = ssd_scan_bwd(dy, act, dt_raw, dt_bias, a_log, d_full, states)
    dpre, dconv_w8, dconv_b = conv_bwd_pre(dact, xbc, conv_w, conv_b)
    dxbc = conv_bwd_input(dpre, conv_w)
    ddt_b = ddt_raw.astype(BF16)
    (dw_z,) = wgrad(h, [dz], name="ssd_dwz")
    (dw_xbc,) = wgrad(h, [dxbc], tn=2048, name="ssd_dwxbc")
    (dw_dt,) = wgrad(h, [ddt_b], name="ssd_dwdt")
    dh = linear([(dz, w_z, "nt"), (dxbc, w_xbc, "nt"), (ddt_b, w_dt, "nt")], tm=256, name="ssd_dh")
    dx, dgain = rmsnorm_bwd(x, gain, dh, dout, name="ssd_norm_bwd")
    dw_in = jnp.concatenate([dw_z, dw_xbc, dw_dt], axis=1)
    dd = dd_full.reshape(NH, HP).sum(axis=1).reshape(1, NH)
    return dx, dgain, dw_in, dconv_w8[:4], dconv_b, dbias, dalog, dd, dout_norm, dw_out


def _sb_layer_fwd(x, gain, w_qkv, qg, kg, w_out):
    h = rmsnorm_fwd(x, gain, name="sb_norm_fwd")
    qkv = linear([(h, w_qkv, "nn")], tn=1024, name="sb_qkv")
    qs, kn, v = sb_prep_fwd(qkv, qg, kg)
    o, rsave = sb_fwd(qs, kn, v)
    xo = linear([(o, w_out, "nn")], res=x, name="sb_out")
    return xo, (x, h, qkv, qs, kn, v, o, rsave)


def _sb_layer_bwd(dout, saved, gain, w_qkv, qg, kg, w_out):
    x, h, qkv, qs, kn, v, o, rsave = saved
    do = linear([(dout, w_out, "nt")], out_dtype=BF16, name="sb_do")
    (dw_out,) = wgrad(o, [dout], name="sb_dwout")
    dqs, dkn, dv = sb_bwd(qs, kn, v, do, rsave)
    dqkv, dqg, dkg = sb_prep_bwd(dqs, dkn, dv, qkv, qg, kg)
    (dw_qkv,) = wgrad(h, [dqkv], tn=1024, name="sb_dwqkv")
    dh = linear([(dqkv, w_qkv, "nt")], name="sb_dh")
    dx, dgain = rmsnorm_bwd(x, gain, dh, dout, name="sb_norm_bwd")
    dqg = dqg.reshape(SBH, SBD).sum(axis=0).reshape(1, SBD)
    dkg = dkg.reshape(SBH, SBD).sum(axis=0).reshape(1, SBD)
    return dx, dgain, dw_qkv, dqg, dkg, dw_out


def _local_step(x, target, full, rep):
    S = x.shape[0]
    d_full = jnp.repeat(rep["ssd_d"][0], HP).reshape(1, DI)
    qg = jnp.tile(rep["sb_q_norm"][0], 2).reshape(1, LANES)
    kg = jnp.tile(rep["sb_k_norm"][0], 2).reshape(1, LANES)
    ssd_in = full["ssd_in"][0]
    w_z, w_xbc, w_dt = ssd_in[:, :DI], ssd_in[:, DI:DI + CONV_CH], ssd_in[:, DI + CONV_CH:]
    conv_w = full["ssd_conv_w"][0]
    conv_b = rep["ssd_conv_b"]
    pool_scale = full["pool_scale"]

    def mixer_args(i):
        kind, j = i % 3, i // 3
        if kind == 0:
            return (full["pool_in"][j], full["pool_group"][j], pool_scale[j:j + 1])
        if kind == 1:
            return (w_z, w_xbc, w_dt, conv_w, conv_b, rep["ssd_dt_bias"], rep["ssd_a_log"], d_full,
                    rep["ssd_out_norm"], full["ssd_out"][0])
        return (full["sb_qkv"][0], qg, kg, full["sb_out"][0])

    fwd = (_pool_layer_fwd, _ssd_layer_fwd, _sb_layer_fwd)
    bwd = (_pool_layer_bwd, _ssd_layer_bwd, _sb_layer_bwd)
    saved = []
    for i in range(DEPTH):
        x, sm = fwd[i % 3](x, rep["mix_norm"][i], *mixer_args(i))
        x, sf = _ffn_fwd(x, rep["ffn_norm"][i], full["ffn_gate"][i], full["ffn_up"][i], full["ffn_down"][i])
        saved.append((sm, sf))

    colsq, dx = loss_head(x, target)
    loss = 0.5 * jnp.sum(colsq) / D

    g = {n: [None] * DEPTH for n in ("mix_norm", "ffn_norm", "ffn_gate", "ffn_up", "ffn_down")}
    g["pool_in"], g["pool_group"], g["pool_scale"] = [None] * 2, [None] * 2, [None] * 2
    for i in reversed(range(DEPTH)):
        sm, sf = saved[i]
        dx, g["ffn_norm"][i], g["ffn_gate"][i], g["ffn_up"][i], g["ffn_down"][i] = _ffn_bwd(
            dx, sf, rep["ffn_norm"][i], full["ffn_gate"][i], full["ffn_up"][i], full["ffn_down"][i])
        kind, j = i % 3, i // 3
        res = bwd[kind](dx, sm, rep["mix_norm"][i], *mixer_args(i))
        dx, g["mix_norm"][i] = res[0], res[1]
        if kind == 0:
            g["pool_in"][j], g["pool_group"][j], g["pool_scale"][j] = res[2:]
        elif kind == 1:
            dw_in, dconv_w, dconv_b, dbias, dalog, dd, don, dw_out = res[2:]
            g.update(ssd_in=dw_in[None], ssd_conv_w=dconv_w[None], ssd_conv_b=dconv_b, ssd_dt_bias=dbias,
                     ssd_a_log=dalog, ssd_d=dd, ssd_out_norm=don, ssd_out=dw_out[None])
        else:
            dw_qkv, dqg, dkg, dw_out = res[2:]
            g.update(sb_qkv=dw_qkv[None], sb_q_norm=dqg, sb_k_norm=dkg, sb_out=dw_out[None])
    for n in ("mix_norm", "ffn_norm", "pool_scale"):
        g[n] = jnp.concatenate(g[n], axis=0)
    for n in ("ffn_gate", "ffn_up", "ffn_down", "pool_in", "pool_group"):
        g[n] = jnp.stack(g[n])
    return loss, dx, g


def kernel(x, mix_norm, pool_in, pool_group, pool_scale, ssd_in, ssd_conv_w, ssd_conv_b, ssd_dt_bias, ssd_a_log, ssd_d, ssd_out_norm, ssd_out, sb_qkv, sb_q_norm, sb_k_norm, sb_out, ffn_norm, ffn_gate, ffn_up, ffn_down, loss_target, m_mix_norm, m_pool_in, m_pool_group, m_pool_scale, m_ssd_in, m_ssd_conv_w, m_ssd_conv_b, m_ssd_dt_bias, m_ssd_a_log, m_ssd_d, m_ssd_out_norm, m_ssd_out, m_sb_qkv, m_sb_q_norm, m_sb_k_norm, m_sb_out, m_ffn_norm, m_ffn_gate, m_ffn_up, m_ffn_down, v_mix_norm, v_pool_in, v_pool_group, v_pool_scale, v_ssd_in, v_ssd_conv_w, v_ssd_conv_b, v_ssd_dt_bias, v_ssd_a_log, v_ssd_d, v_ssd_out_norm, v_ssd_out, v_sb_qkv, v_sb_q_norm, v_sb_k_norm, v_sb_out, v_ffn_norm, v_ffn_gate, v_ffn_up, v_ffn_down):
    given = dict(locals())
    w = {n: given[n] for n in WEIGHT_ORDER}
    m = {n: given["m_" + n] for n in WEIGHT_ORDER}
    v = {n: given["v_" + n] for n in WEIGHT_ORDER}
    sharded_names = [s[0] for s in SHARDED]

    gathered = gather_chips(_pack_payload(w))
    full = _unpack_payload(gathered, w)
    rep = {n: w[n] for n in REPLICATED}

    loss, dx, g = _local_step(x[0], loss_target[0], full, rep)
    loss = lax.psum(loss, ("x", "y", "c"))

    gstack = jnp.stack([_pack([_split_shards(g[n], ax)[j] for n, ax, _ in SHARDED]) for j in range(N_CHIPS)])
    recv = scatter_chips(gstack)
    me = 2 * lax.axis_index("x") + lax.axis_index("y")
    own = lax.dynamic_index_in_dim(gstack, me, axis=0, keepdims=False)
    part = sum4([own, recv[0], recv[1], recv[2]])
    other = swap_cores(part)
    shard_shapes = [w[n].shape for n in sharded_names]
    gs, ds, ms, vs = adamw(_pack([w[n] for n in sharded_names]), [part, other],
                           _pack([m[n] for n in sharded_names]), _pack([v[n] for n in sharded_names]),
                           name="adamw_sharded")
    out = {}
    for key, flat in (("g", gs), ("d", ds), ("m", ms), ("v", vs)):
        for n, a in zip(sharded_names, _unpack(flat, shard_shapes)):
            out[key, n] = a

    rep_shapes = [w[n].shape for n in REPLICATED]
    gsum = allreduce_small(_pack([g[n] for n in REPLICATED]))
    gs, ds, ms, vs = adamw(_pack([w[n] for n in REPLICATED]), [gsum], _pack([m[n] for n in REPLICATED]),
                           _pack([v[n] for n in REPLICATED]), name="adamw_replicated")
    for key, flat in (("g", gs), ("d", ds), ("m", ms), ("v", vs)):
        for n, a in zip(REPLICATED, _unpack(flat, rep_shapes)):
            out[key, n] = a

    return (loss, dx[None], *[out["g", n] for n in WEIGHT_ORDER], *[out["d", n] for n in WEIGHT_ORDER],
            *[out["m", n] for n in WEIGHT_ORDER], *[out["v", n] for n in WEIGHT_ORDER])
```

```python
import functools
import math

import jax
import jax.numpy as jnp
from jax import lax
from jax.experimental import pallas as pl
from jax.experimental.pallas import tpu as pltpu

F32 = jnp.float32
BF16 = jnp.bfloat16
HI = lax.Precision.HIGHEST

D = 1024
DEPTH = 4
EPS = 1e-6
POOL_WINDOWS = (2, 4, 8, 16)
PG = 256
DI = 2048
NH = 32
HP = 64
NG = 8
NS = 128
GW = 256
CH = 256
CONV_CH = 4096
SSD_IN = 6176
SBH = 16
SBD = 64
FH = 2816
N_CHIPS = 4
LANES = 128

ADAM_LR = 0.001
ADAM_B1 = 0.9
ADAM_B2 = 0.999
ADAM_EPS = 1e-08
ADAM_WD = 0.01
ADAM_STEP = 10

VMEM_LIMIT = 56 * 1024 * 1024


def _pcall(body, **kw):
    return pl.pallas_call(body, **kw)


def _cp(*sem):
    return pltpu.CompilerParams(dimension_semantics=sem, vmem_limit_bytes=VMEM_LIMIT)


def _dot(a, b, prec=None):
    return lax.dot_general(a, b, (((1,), (0,)), ((), ())), precision=prec, preferred_element_type=F32)


def _dot_nt(a, b, prec=None):
    return lax.dot_general(a, b, (((1,), (1,)), ((), ())), precision=prec, preferred_element_type=F32)


def _dot_tn(a, b, prec=None):
    return lax.dot_general(a, b, (((0,), (0,)), ((), ())), precision=prec, preferred_element_type=F32)


def _sigmoid(x):
    return 1.0 / (1.0 + jnp.exp(-x))


def _iota(shape, axis):
    return lax.broadcasted_iota(jnp.int32, shape, axis)


def linear(pairs, res=None, out_dtype=F32, tm=512, tn=None, name="linear"):
    M = pairs[0][0].shape[0]
    N = pairs[0][1].shape[1] if pairs[0][2] == "nn" else pairs[0][1].shape[0]
    tm = min(tm, M)
    tn = N if tn is None else min(tn, N)
    n_pairs = len(pairs)
    modes = [p[2] for p in pairs]

    def body(*refs):
        acc = None
        for k in range(n_pairs):
            a = refs[2 * k][...].astype(BF16)
            w = refs[2 * k + 1][...]
            t = _dot(a, w) if modes[k] == "nn" else _dot_nt(a, w)
            acc = t if acc is None else acc + t
        if res is not None:
            acc = acc + refs[2 * n_pairs][...]
        refs[-1][...] = acc.astype(out_dtype)

    in_specs, args = [], []
    for a, w, mode in pairs:
        K = a.shape[1]
        in_specs.append(pl.BlockSpec((tm, K), lambda j, i: (i, 0)))
        if mode == "nn":
            in_specs.append(pl.BlockSpec((K, tn), lambda j, i: (0, j)))
        else:
            in_specs.append(pl.BlockSpec((tn, K), lambda j, i: (j, 0)))
        args += [a, w]
    if res is not None:
        in_specs.append(pl.BlockSpec((tm, tn), lambda j, i: (i, j)))
        args.append(res)
    return _pcall(
        body, name=name, grid=(N // tn, M // tm), in_specs=in_specs,
        out_specs=pl.BlockSpec((tm, tn), lambda j, i: (i, j)),
        out_shape=jax.ShapeDtypeStruct((M, N), out_dtype),
        compiler_params=_cp("parallel", "arbitrary"))(*args)


def wgrad(a, gs, tk=1024, tn=None, tm=1024, name="wgrad"):
    M, Ka = a.shape
    N = gs[0].shape[1]
    tk, tm = min(tk, Ka), min(tm, M)
    tn = N if tn is None else min(tn, N)
    n_g = len(gs)

    def body(*refs):
        a_ref, g_refs, o_refs = refs[0], refs[1:1 + n_g], refs[1 + n_g:]
        m = pl.program_id(2)
        at = a_ref[...].astype(BF16)
        for g_ref, o_ref in zip(g_refs, o_refs):
            t = _dot_tn(at, g_ref[...].astype(BF16))

            @pl.when(m == 0)
            def _():
                o_ref[...] = t

            @pl.when(m > 0)
            def _():
                o_ref[...] += t

    out = _pcall(
        body, name=name, grid=(Ka // tk, N // tn, M // tm),
        in_specs=[pl.BlockSpec((tm, tk), lambda k, j, m: (m, k))]
        + [pl.BlockSpec((tm, tn), lambda k, j, m: (m, j))] * n_g,
        out_specs=[pl.BlockSpec((tk, tn), lambda k, j, m: (k, j))] * n_g,
        out_shape=[jax.ShapeDtypeStruct((Ka, N), F32)] * n_g,
        compiler_params=_cp("parallel", "parallel", "arbitrary"))(a, *gs)
    return out


def rmsnorm_fwd(x, gain, tm=512, name="rmsnorm_fwd"):
    S, Dm = x.shape
    tm = min(tm, S)

    def body(x_ref, g_ref, o_ref):
        xv = x_ref[...]
        r = lax.rsqrt(jnp.mean(xv * xv, axis=-1, keepdims=True) + EPS)
        o_ref[...] = (xv * r * g_ref[...]).astype(BF16)

    return _pcall(
        body, name=name, grid=(S // tm,),
        in_specs=[pl.BlockSpec((tm, Dm), lambda i: (i, 0)), pl.BlockSpec((1, Dm), lambda i: (0, 0))],
        out_specs=pl.BlockSpec((tm, Dm), lambda i: (i, 0)),
        out_shape=jax.ShapeDtypeStruct((S, Dm), BF16),
        compiler_params=_cp("parallel"))(x, gain.reshape(1, Dm))


def rmsnorm_bwd(x, gain, dh, dres, tm=512, name="rmsnorm_bwd"):
    S, Dm = x.shape
    tm = min(tm, S)

    def body(x_ref, g_ref, dh_ref, dr_ref, dx_ref, dg_ref):
        i = pl.program_id(0)
        xv = x_ref[...]
        r = lax.rsqrt(jnp.mean(xv * xv, axis=-1, keepdims=True) + EPS)
        y = xv * r
        dhv = dh_ref[...]
        dy = dhv * g_ref[...]
        dx_ref[...] = dr_ref[...] + r * (dy - y * jnp.mean(dy * y, axis=-1, keepdims=True))
        part = jnp.sum(dhv * y, axis=0, keepdims=True)

        @pl.when(i == 0)
        def _():
            dg_ref[...] = part

        @pl.when(i > 0)
        def _():
            dg_ref[...] += part

    return _pcall(
        body, name=name, grid=(S // tm,),
        in_specs=[pl.BlockSpec((tm, Dm), lambda i: (i, 0)), pl.BlockSpec((1, Dm), lambda i: (0, 0)),
                  pl.BlockSpec((tm, Dm), lambda i: (i, 0)), pl.BlockSpec((tm, Dm), lambda i: (i, 0))],
        out_specs=[pl.BlockSpec((tm, Dm), lambda i: (i, 0)), pl.BlockSpec((1, Dm), lambda i: (0, 0))],
        out_shape=[jax.ShapeDtypeStruct((S, Dm), F32), jax.ShapeDtypeStruct((1, Dm), F32)],
        compiler_params=_cp("arbitrary"))(x, gain.reshape(1, Dm), dh, dres)


def ffn_up(h, wg, wu, tm=256, tn=1408):
    S = h.shape[0]
    tm = min(tm, S)

    def body(h_ref, wg_ref, wu_ref, a_ref, b_ref, hid_ref):
        hv = h_ref[...]
        a = _dot(hv, wg_ref[...])
        b = _dot(hv, wu_ref[...])
        a_ref[...] = a
        b_ref[...] = b
        hid_ref[...] = (a * _sigmoid(a) * b).astype(BF16)

    return _pcall(
        body, name="ffn_up", grid=(FH // tn, S // tm),
        in_specs=[pl.BlockSpec((tm, D), lambda j, i: (i, 0)), pl.BlockSpec((D, tn), lambda j, i: (0, j)),
                  pl.BlockSpec((D, tn), lambda j, i: (0, j))],
        out_specs=[pl.BlockSpec((tm, tn), lambda j, i: (i, j))] * 3,
        out_shape=[jax.ShapeDtypeStruct((S, FH), F32), jax.ShapeDtypeStruct((S, FH), F32),
                   jax.ShapeDtypeStruct((S, FH), BF16)],
        compiler_params=_cp("parallel", "arbitrary"))(h, wg, wu)


def ffn_bwd_hidden(dout, wd, a, b, tm=256, tn=1408):
    S = dout.shape[0]
    tm = min(tm, S)

    def body(do_ref, wd_ref, a_ref, b_ref, da_ref, db_ref):
        dhid = _dot_nt(do_ref[...].astype(BF16), wd_ref[...])
        av, bv = a_ref[...], b_ref[...]
        s = _sigmoid(av)
        da_ref[...] = (dhid * bv * (s * (1.0 + av * (1.0 - s)))).astype(BF16)
        db_ref[...] = (dhid * (av * s)).astype(BF16)

    return _pcall(
        body, name="ffn_bwd_hidden", grid=(FH // tn, S // tm),
        in_specs=[pl.BlockSpec((tm, D), lambda j, i: (i, 0)), pl.BlockSpec((tn, D), lambda j, i: (j, 0)),
                  pl.BlockSpec((tm, tn), lambda j, i: (i, j)), pl.BlockSpec((tm, tn), lambda j, i: (i, j))],
        out_specs=[pl.BlockSpec((tm, tn), lambda j, i: (i, j))] * 2,
        out_shape=[jax.ShapeDtypeStruct((S, FH), BF16)] * 2,
        compiler_params=_cp("parallel", "arbitrary"))(dout, wd, a, b)


POOL_T = 128
POOL_HALO = 16


def pool_fwd(u, wgrp, scale, x_res):
    S = u.shape[0]
    T, HB = min(POOL_T, S), POOL_HALO
    per = T // HB

    def body(u_ref, tail_ref, wg_ref, sc_ref, x_ref, xo_ref, p_ref):
        i = pl.program_id(0)
        uc = u_ref[...]
        tail = jnp.where(i > 0, tail_ref[...], 0.0)
        d_cur = _iota((T, T), 0) - _iota((T, T), 1)
        d_tail = _iota((T, HB), 0) - _iota((T, HB), 1) + HB
        tg = i * T + _iota((T, 1), 0)
        for g, w in enumerate(POOL_WINDOWS):
            gs = slice(g * PG, (g + 1) * PG)
            band = ((d_cur >= 0) & (d_cur < w)).astype(F32)
            band_t = ((d_tail >= 0) & (d_tail < w)).astype(F32)
            ug = uc[:, gs]
            ws = _dot(band, ug, HI) + _dot(band_t, tail[:, gs], HI)
            cnt = jnp.minimum(tg + 1, w).astype(F32)
            pb = (ws / cnt - ug).astype(BF16)
            p_ref[:, gs] = pb
            xo_ref[:, gs] = x_ref[:, gs] + _dot(pb, wg_ref[g]) * sc_ref[:, gs]

    return _pcall(
        body, name="pool_fwd", grid=(S // T,),
        in_specs=[pl.BlockSpec((T, D), lambda i: (i, 0)),
                  pl.BlockSpec((HB, D), lambda i: (jnp.maximum(i * per - 1, 0), 0)),
                  pl.BlockSpec((4, PG, PG), lambda i: (0, 0, 0)), pl.BlockSpec((1, D), lambda i: (0, 0)),
                  pl.BlockSpec((T, D), lambda i: (i, 0))],
        out_specs=[pl.BlockSpec((T, D), lambda i: (i, 0))] * 2,
        out_shape=[jax.ShapeDtypeStruct((S, D), F32), jax.ShapeDtypeStruct((S, D), BF16)],
        compiler_params=_cp("parallel"))(u, u, wgrp, scale, x_res)


def pool_bwd_group(dm, p, wgrp, scale, tm=512):
    S = dm.shape[0]
    tm = min(tm, S)

    def body(dm_ref, p_ref, wg_ref, sc_ref, dp_ref, dwg_ref, dsc_ref):
        i = pl.program_id(0)

        @pl.when(i == 0)
        def _():
            dwg_ref[...] = jnp.zeros_like(dwg_ref)
            dsc_ref[...] = jnp.zeros_like(dsc_ref)

        for g in range(4):
            gs = slice(g * PG, (g + 1) * PG)
            dmg, pg, wg = dm_ref[:, gs], p_ref[:, gs], wg_ref[g]
            dsc_ref[:, gs] += jnp.sum(dmg * _dot(pg, wg), axis=0, keepdims=True)
            dy = (dmg * sc_ref[:, gs]).astype(BF16)
            dp_ref[:, gs] = _dot_nt(dy, wg)
            dwg_ref[g] += _dot_tn(pg, dy)

    return _pcall(
        body, name="pool_bwd_group", grid=(S // tm,),
        in_specs=[pl.BlockSpec((tm, D), lambda i: (i, 0)), pl.BlockSpec((tm, D), lambda i: (i, 0)),
                  pl.BlockSpec((4, PG, PG), lambda i: (0, 0, 0)), pl.BlockSpec((1, D), lambda i: (0, 0))],
        out_specs=[pl.BlockSpec((tm, D), lambda i: (i, 0)), pl.BlockSpec((4, PG, PG), lambda i: (0, 0, 0)),
                   pl.BlockSpec((1, D), lambda i: (0, 0))],
        out_shape=[jax.ShapeDtypeStruct((S, D), F32), jax.ShapeDtypeStruct((4, PG, PG), F32),
                   jax.ShapeDtypeStruct((1, D), F32)],
        compiler_params=_cp("arbitrary"))(dm, p, wgrp, scale)


def pool_bwd_window(dp):
    S = dp.shape[0]
    T, HB = min(POOL_T, S), POOL_HALO
    per = T // HB
    nt = S // T

    def body(dp_ref, nxt_ref, du_ref):
        i = pl.program_id(0)
        dc = dp_ref[...]
        nxt = jnp.where(i < nt - 1, nxt_ref[...], 0.0)
        d_cur = _iota((T, T), 1) - _iota((T, T), 0)
        d_nxt = _iota((T, HB), 1) - _iota((T, HB), 0) + T
        tg = i * T + _iota((T, 1), 0)
        tn_ = (i + 1) * T + _iota((HB, 1), 0)
        for g, w in enumerate(POOL_WINDOWS):
            gs = slice(g * PG, (g + 1) * PG)
            band = ((d_cur >= 0) & (d_cur < w)).astype(F32)
            band_n = ((d_nxt >= 0) & (d_nxt < w)).astype(F32)
            dcg = dc[:, gs]
            cur = dcg / jnp.minimum(tg + 1, w).astype(F32)
            nx = nxt[:, gs] / jnp.minimum(tn_ + 1, w).astype(F32)
            du_ref[:, gs] = (_dot(band, cur, HI) + _dot(band_n, nx, HI) - dcg).astype(BF16)

    return _pcall(
        body, name="pool_bwd_window", grid=(nt,),
        in_specs=[pl.BlockSpec((T, D), lambda i: (i, 0)),
                  pl.BlockSpec((HB, D), lambda i: (jnp.minimum((i + 1) * per, S // HB - 1), 0))],
        out_specs=pl.BlockSpec((T, D), lambda i: (i, 0)),
        out_shape=jax.ShapeDtypeStruct((S, D), BF16),
        compiler_params=_cp("parallel"))(dp, dp)


CONV_T = 256


def _shift_down(xc, prev8, j):
    if j == 0:
        return xc
    T = xc.shape[0]
    body = pltpu.roll(xc, j, 0)
    first = jnp.where(_iota((8, 1), 0) < j, pltpu.roll(prev8, j, 0), body[0:8])
    return jnp.concatenate([first, body[8:T]], axis=0)


def _shift_up(dc, next8, j):
    if j == 0:
        return dc
    T = dc.shape[0]
    body = pltpu.roll(dc, T - j, 0)
    last = jnp.where(_iota((8, 1), 0) + j < 8, body[T - 8:T], pltpu.roll(next8, 8 - j, 0))
    return jnp.concatenate([body[0:T - 8], last], axis=0)


def conv_fwd(xbc, conv_w, conv_b):
    S = xbc.shape[0]
    T = min(CONV_T, S)
    CB = 1024

    def body(x_ref, prev_ref, w_ref, b_ref, o_ref):
        i = pl.program_id(1)
        xc = x_ref[...]
        prev8 = jnp.where(i > 0, prev_ref[...], 0.0)
        pre = b_ref[...] + w_ref[3:4, :] * xc
        for j in range(1, 4):
            pre = pre + w_ref[3 - j:4 - j, :] * _shift_down(xc, prev8, j)
        o_ref[...] = pre * _sigmoid(pre)

    return _pcall(
        body, name="conv_fwd", grid=(CONV_CH // CB, S // T),
        in_specs=[pl.BlockSpec((T, CB), lambda c, i: (i, c)),
                  pl.BlockSpec((8, CB), lambda c, i: (jnp.maximum(i * (T // 8) - 1, 0), c)),
                  pl.BlockSpec((4, CB), lambda c, i: (0, c)), pl.BlockSpec((1, CB), lambda c, i: (0, c))],
        out_specs=pl.BlockSpec((T, CB), lambda c, i: (i, c)),
        out_shape=jax.ShapeDtypeStruct((S, CONV_CH), F32),
        compiler_params=_cp("parallel", "parallel"))(xbc, xbc, conv_w, conv_b)


def conv_bwd_pre(dact, xbc, conv_w, conv_b):
    S = xbc.shape[0]
    T = min(CONV_T, S)
    CB = 1024

    def body(da_ref, x_ref, prev_ref, w_ref, b_ref, dpre_ref, dw_ref, db_ref):
        i = pl.program_id(1)
        xc = x_ref[...]
        prev8 = jnp.where(i > 0, prev_ref[...], 0.0)
        sh = [_shift_down(xc, prev8, j) for j in range(4)]
        pre = b_ref[...] + w_ref[3:4, :] * sh[0]
        for j in range(1, 4):
            pre = pre + w_ref[3 - j:4 - j, :] * sh[j]
        s = _sigmoid(pre)
        dpre = da_ref[...] * (s * (1.0 + pre * (1.0 - s)))
        dpre_ref[...] = dpre
        rows = [jnp.sum(dpre * sh[3 - k], axis=0, keepdims=True) for k in range(4)]
        dw = jnp.concatenate(rows + [jnp.zeros((4, CB), F32)], axis=0)
        db = jnp.sum(dpre, axis=0, keepdims=True)

        @pl.when(i == 0)
        def _():
            dw_ref[...] = dw
            db_ref[...] = db

        @pl.when(i > 0)
        def _():
            dw_ref[...] += dw
            db_ref[...] += db

    return _pcall(
        body, name="conv_bwd_pre", grid=(CONV_CH // CB, S // T),
        in_specs=[pl.BlockSpec((T, CB), lambda c, i: (i, c)), pl.BlockSpec((T, CB), lambda c, i: (i, c)),
                  pl.BlockSpec((8, CB), lambda c, i: (jnp.maximum(i * (T // 8) - 1, 0), c)),
                  pl.BlockSpec((4, CB), lambda c, i: (0, c)), pl.BlockSpec((1, CB), lambda c, i: (0, c))],
        out_specs=[pl.BlockSpec((T, CB), lambda c, i: (i, c)), pl.BlockSpec((8, CB), lambda c, i: (0, c)),
                   pl.BlockSpec((1, CB), lambda c, i: (0, c))],
        out_shape=[jax.ShapeDtypeStruct((S, CONV_CH), F32), jax.ShapeDtypeStruct((8, CONV_CH), F32),
                   jax.ShapeDtypeStruct((1, CONV_CH), F32)],
        compiler_params=_cp("parallel", "arbitrary"))(dact, xbc, xbc, conv_w, conv_b)


def conv_bwd_input(dpre, conv_w):
    S = dpre.shape[0]
    T = min(CONV_T, S)
    CB = 1024
    nt = S // T

    def body(d_ref, nxt_ref, w_ref, o_ref):
        i = pl.program_id(1)
        dc = d_ref[...]
        next8 = jnp.where(i < nt - 1, nxt_ref[...], 0.0)
        acc = w_ref[3:4, :] * dc
        for j in range(1, 4):
            acc = acc + w_ref[3 - j:4 - j, :] * _shift_up(dc, next8, j)
        o_ref[...] = acc.astype(BF16)

    return _pcall(
        body, name="conv_bwd_input", grid=(CONV_CH // CB, nt),
        in_specs=[pl.BlockSpec((T, CB), lambda c, i: (i, c)),
                  pl.BlockSpec((8, CB), lambda c, i: (jnp.minimum((i + 1) * (T // 8), S // 8 - 1), c)),
                  pl.BlockSpec((4, CB), lambda c, i: (0, c))],
        out_specs=pl.BlockSpec((T, CB), lambda c, i: (i, c)),
        out_shape=jax.ShapeDtypeStruct((S, CONV_CH), BF16),
        compiler_params=_cp("parallel", "parallel"))(dpre, dpre, conv_w)


def _ssd_chunk_terms(dt_ref, bias_ref, alog_ref):
    L = CH
    dtp = dt_ref[...] + bias_ref[...]
    dt = jnp.maximum(dtp, 0.0) + jnp.log(1.0 + jnp.exp(-jnp.abs(dtp)))
    a = -jnp.exp(alog_ref[...])
    da = dt * a
    tri = (_iota((L, L), 0) >= _iota((L, L), 1)).astype(F32)
    acum = _dot(tri, da, HI)
    triu = (_iota((L, L), 0) <= _iota((L, L), 1)).astype(F32)
    acum_row = _dot_tn(da, triu, HI)
    expand = (_iota((NH, DI), 1) // HP == _iota((NH, DI), 0)).astype(F32)
    return dtp, dt, a, da, acum, acum_row, expand


def ssd_scan_fwd(xbc_act, dt_raw, dt_bias, a_log, d_full):
    S = xbc_act.shape[0]
    L = CH
    nc = S // L

    def body(xs_ref, b_ref, c_ref, dt_ref, bias_ref, alog_ref, d_ref, y_ref, st_ref, state):
        c = pl.program_id(0)

        @pl.when(c == 0)
        def _():
            state[...] = jnp.zeros_like(state)

        st_ref[0] = state[...]
        _, dt, _, _, acum, acum_row, expand = _ssd_chunk_terms(dt_ref, bias_ref, alog_ref)
        e_full = _dot(jnp.exp(acum), expand, HI)
        w_full = _dot(jnp.exp(acum[L - 1:L, :] - acum), expand, HI)
        dt_full = _dot(dt, expand, HI)
        causal = _iota((L, L), 0) >= _iota((L, L), 1)
        lane_head = _iota((1, GW), 1) // HP
        for g in range(NG):
            gs = slice(g * GW, (g + 1) * GW)
            ns = slice(g * NS, (g + 1) * NS)
            xs_g = xs_ref[:, gs]
            xdt_g = xs_g * dt_full[:, gs]
            cg = c_ref[:, ns].astype(BF16)
            bg = b_ref[:, ns].astype(BF16)
            gmat = _dot_nt(cg, bg)
            yg = jnp.zeros((L, GW), F32)
            for hh in range(4):
                h = 4 * g + hh
                diff = acum[:, h:h + 1] - acum_row[h:h + 1, :]
                dk = jnp.exp(jnp.where(causal, diff, -1e30))
                xm = jnp.where(lane_head == hh, xdt_g, 0.0).astype(BF16)
                yg = yg + _dot((gmat * dk).astype(BF16), xm)
            sg = state[g]
            yoff = _dot(cg, sg.astype(BF16)) * e_full[:, gs]
            y_ref[:, gs] = yg + yoff + d_ref[:, gs] * xs_g
            state[g] = sg * e_full[L - 1:L, gs] + _dot_tn(bg, (w_full[:, gs] * xdt_g).astype(BF16))

    return _pcall(
        body, name="ssd_scan_fwd", grid=(nc,),
        in_specs=[pl.BlockSpec((L, DI), lambda c: (c, 0)), pl.BlockSpec((L, 1024), lambda c: (c, 2)),
                  pl.BlockSpec((L, 1024), lambda c: (c, 3)), pl.BlockSpec((L, NH), lambda c: (c, 0)),
                  pl.BlockSpec((1, NH), lambda c: (0, 0)), pl.BlockSpec((1, NH), lambda c: (0, 0)),
                  pl.BlockSpec((1, DI), lambda c: (0, 0))],
        out_specs=[pl.BlockSpec((L, DI), lambda c: (c, 0)), pl.BlockSpec((1, NG, NS, GW), lambda c: (c, 0, 0, 0))],
        out_shape=[jax.ShapeDtypeStruct((S, DI), F32), jax.ShapeDtypeStruct((nc, NG, NS, GW), F32)],
        scratch_shapes=[pltpu.VMEM((NG, NS, GW), F32)],
        compiler_params=_cp("arbitrary"))(xbc_act, xbc_act, xbc_act, dt_raw, dt_bias, a_log, d_full)


def ssd_scan_bwd(dy, xbc_act, dt_raw, dt_bias, a_log, d_full, states):
    S = xbc_act.shape[0]
    L = CH
    nc = S // L

    def body(dy_ref, xs_ref, b_ref, c_ref, dt_ref, bias_ref, alog_ref, d_ref, st_ref,
             dxbc_ref, ddt_ref, dbias_ref, dalog_ref, dd_ref, dstate):
        c = pl.program_id(0)

        @pl.when(c == 0)
        def _():
            dstate[...] = jnp.zeros_like(dstate)
            dbias_ref[...] = jnp.zeros_like(dbias_ref)
            dalog_ref[...] = jnp.zeros_like(dalog_ref)
            dd_ref[...] = jnp.zeros_like(dd_ref)

        dtp, dt, a, _, acum, acum_row, expand = _ssd_chunk_terms(dt_ref, bias_ref, alog_ref)
        e_full = _dot(jnp.exp(acum), expand, HI)
        w_h = jnp.exp(acum[L - 1:L, :] - acum)
        w_full = _dot(w_h, expand, HI)
        dt_full = _dot(dt, expand, HI)
        causal = _iota((L, L), 0) >= _iota((L, L), 1)
        lane_head = _iota((1, GW), 1) // HP
        ones_l = jnp.ones((L, LANES), F32)
        head_id = _iota((1, NH), 1)
        dacum = jnp.zeros((L, NH), F32)
        red_parts = []
        dxdt_parts = []
        alast_parts = []
        for g in range(NG):
            gs = slice(g * GW, (g + 1) * GW)
            ns = slice(g * NS, (g + 1) * NS)
            xs_g = xs_ref[:, gs]
            xdt_g = xs_g * dt_full[:, gs]
            dy_g = dy_ref[:, gs]
            cg = c_ref[:, ns].astype(BF16)
            bg = b_ref[:, ns].astype(BF16)
            gmat = _dot_nt(cg, bg)
            sg = st_ref[0, g]
            dsg = dstate[g]
            sgb, dsgb = sg.astype(BF16), dsg.astype(BF16)
            cs = _dot(cg, sgb)
            bds = _dot(bg, dsgb)
            e_g, w_g = e_full[:, gs], w_full[:, gs]
            dxdt = w_g * bds
            dgsum = jnp.zeros((L, L), F32)
            for hh in range(4):
                h = 4 * g + hh
                hm = lane_head == hh
                diff = acum[:, h:h + 1] - acum_row[h:h + 1, :]
                dk = jnp.exp(jnp.where(causal, diff, -1e30))
                m = gmat * dk
                dym = jnp.where(hm, dy_g, 0.0).astype(BF16)
                xm = jnp.where(hm, xdt_g, 0.0).astype(BF16)
                dm = _dot_nt(dym, xm)
                dxdt = dxdt + _dot_tn(m.astype(BF16), dym)
                dgsum = dgsum + dm * dk
                em = dm * m
                rs = _dot(em, ones_l, HI)[:, 0:1]
                cs_ = _dot_tn(em, ones_l, HI)[:, 0:1]
                dacum = dacum + (rs - cs_) * (head_id == h).astype(F32)
            dgb = dgsum.astype(BF16)
            edy = (e_g * dy_g).astype(BF16)
            wx = (w_g * xdt_g).astype(BF16)
            dc_g = _dot(dgb, bg) + _dot_nt(edy, sgb)
            db_g = _dot_tn(dgb, cg) + _dot_nt(wx, dsgb)
            dxbc_ref[:, DI + g * NS:DI + (g + 1) * NS] = db_g
            dxbc_ref[:, DI + 1024 + g * NS:DI + 1024 + (g + 1) * NS] = dc_g
            p2w = bds * xdt_g * w_g
            red_parts.append(dy_g * cs * e_g - p2w)
            alast_parts.append(jnp.sum(p2w, axis=0, keepdims=True)
                               + e_full[L - 1:L, gs] * jnp.sum(dsg * sg, axis=0, keepdims=True))
            dxdt_parts.append(dxdt)
            dstate[g] = e_full[L - 1:L, gs] * dsg + _dot_tn(cg, edy)
            dxbc_ref[:, gs] = dxdt * dt_full[:, gs] + dy_g * d_ref[:, gs]
            dd_ref[:, gs] += jnp.sum(dy_g * xs_g, axis=0, keepdims=True)
        red = jnp.concatenate(red_parts, axis=1)
        dxdt_all = jnp.concatenate(dxdt_parts, axis=1)
        alast = jnp.concatenate(alast_parts, axis=1)
        dacum = dacum + _dot_nt(red, expand, HI)
        dalast = _dot_nt(jnp.broadcast_to(alast, (8, DI)), expand, HI)[0:1, :]
        dacum = dacum + jnp.where(_iota((L, 1), 0) == L - 1, dalast, 0.0)
        triu = (_iota((L, L), 0) <= _iota((L, L), 1)).astype(F32)
        dda = _dot(triu, dacum, HI)
        ddt = _dot_nt(dxdt_all * xs_ref[...], expand, HI) + dda * a
        dalog_ref[...] += jnp.sum(dda * dt, axis=0, keepdims=True) * a
        ddt_raw = ddt * _sigmoid(dtp)
        ddt_ref[...] = ddt_raw
        dbias_ref[...] += jnp.sum(ddt_raw, axis=0, keepdims=True)

    rev = lambda c: (nc - 1 - c, 0)
    return _pcall(
        body, name="ssd_scan_bwd", grid=(nc,),
        in_specs=[pl.BlockSpec((L, DI), rev), pl.BlockSpec((L, DI), rev),
                  pl.BlockSpec((L, 1024), lambda c: (nc - 1 - c, 2)), pl.BlockSpec((L, 1024), lambda c: (nc - 1 - c, 3)),
                  pl.BlockSpec((L, NH), rev), pl.BlockSpec((1, NH), lambda c: (0, 0)),
                  pl.BlockSpec((1, NH), lambda c: (0, 0)), pl.BlockSpec((1, DI), lambda c: (0, 0)),
                  pl.BlockSpec((1, NG, NS, GW), lambda c: (nc - 1 - c, 0, 0, 0))],
        out_specs=[pl.BlockSpec((L, CONV_CH), rev), pl.BlockSpec((L, NH), rev),
                   pl.BlockSpec((1, NH), lambda c: (0, 0)), pl.BlockSpec((1, NH), lambda c: (0, 0)),
                   pl.BlockSpec((1, DI), lambda c: (0, 0))],
        out_shape=[jax.ShapeDtypeStruct((S, CONV_CH), F32), jax.ShapeDtypeStruct((S, NH), F32),
                   jax.ShapeDtypeStruct((1, NH), F32), jax.ShapeDtypeStruct((1, NH), F32),
                   jax.ShapeDtypeStruct((1, DI), F32)],
        scratch_shapes=[pltpu.VMEM((NG, NS, GW), F32)],
        compiler_params=_cp("arbitrary"))(dy, xbc_act, xbc_act, xbc_act, dt_raw, dt_bias, a_log, d_full, states)


def gate_norm_fwd(y, z, out_norm, tm=256):
    S = y.shape[0]
    tm = min(tm, S)

    def body(y_ref, z_ref, on_ref, o_ref):
        zv = z_ref[...]
        gin = y_ref[...] * (zv * _sigmoid(zv))
        for g in range(NG):
            gs = slice(g * GW, (g + 1) * GW)
            blk = gin[:, gs]
            r = lax.rsqrt(jnp.mean(blk * blk, axis=-1, keepdims=True) + EPS)
            o_ref[:, gs] = (blk * r * on_ref[:, gs]).astype(BF16)

    return _pcall(
        body, name="gate_norm_fwd", grid=(S // tm,),
        in_specs=[pl.BlockSpec((tm, DI), lambda i: (i, 0)), pl.BlockSpec((tm, DI), lambda i: (i, 0)),
                  pl.BlockSpec((1, DI), lambda i: (0, 0))],
        out_specs=pl.BlockSpec((tm, DI), lambda i: (i, 0)),
        out_shape=jax.ShapeDtypeStruct((S, DI), BF16),
        compiler_params=_cp("parallel"))(y, z, out_norm)


def gate_norm_bwd(dgn, y, z, out_norm, tm=256):
    S = y.shape[0]
    tm = min(tm, S)

    def body(dg_ref, y_ref, z_ref, on_ref, dy_ref, dz_ref, don_ref):
        i = pl.program_id(0)

        @pl.when(i == 0)
        def _():
            don_ref[...] = jnp.zeros_like(don_ref)

        zv, yv = z_ref[...], y_ref[...]
        s = _sigmoid(zv)
        sz = zv * s
        gin = yv * sz
        for g in range(NG):
            gs = slice(g * GW, (g + 1) * GW)
            blk = gin[:, gs]
            r = lax.rsqrt(jnp.mean(blk * blk, axis=-1, keepdims=True) + EPS)
            n = blk * r
            dg = dg_ref[:, gs]
            don_ref[:, gs] += jnp.sum(dg * n, axis=0, keepdims=True)
            dn = dg * on_ref[:, gs]
            dgin = r * (dn - n * jnp.mean(dn * n, axis=-1, keepdims=True))
            dy_ref[:, gs] = dgin * sz[:, gs]
            dz_ref[:, gs] = (dgin * yv[:, gs] * (s[:, gs] * (1.0 + zv[:, gs] * (1.0 - s[:, gs])))).astype(BF16)

    return _pcall(
        body, name="gate_norm_bwd", grid=(S // tm,),
        in_specs=[pl.BlockSpec((tm, DI), lambda i: (i, 0))] * 3 + [pl.BlockSpec((1, DI), lambda i: (0, 0))],
        out_specs=[pl.BlockSpec((tm, DI), lambda i: (i, 0)), pl.BlockSpec((tm, DI), lambda i: (i, 0)),
                   pl.BlockSpec((1, DI), lambda i: (0, 0))],
        out_shape=[jax.ShapeDtypeStruct((S, DI), F32), jax.ShapeDtypeStruct((S, DI), BF16),
                   jax.ShapeDtypeStruct((1, DI), F32)],
        compiler_params=_cp("arbitrary"))(dgn, y, z, out_norm)


SB_T = 256
SB_QSCALE = 0.125
SB_DEAD = -110.0
SB_UNSEEN = -1e30


def _head_norm(xv, lo):
    sq = xv * xv
    s0 = jnp.sum(jnp.where(lo, sq, 0.0), axis=-1, keepdims=True)
    s1 = jnp.sum(jnp.where(lo, 0.0, sq), axis=-1, keepdims=True)
    return jnp.where(lo, lax.rsqrt(s0 / SBD + EPS), lax.rsqrt(s1 / SBD + EPS))


def sb_prep_fwd(qkv, qg, kg, tm=256):
    S = qkv.shape[0]
    tm = min(tm, S)

    def body(x_ref, qg_ref, kg_ref, q_ref, k_ref, v_ref):
        lo = _iota((1, LANES), 1) < SBD
        for sl in range(D // LANES):
            cs = slice(sl * LANES, (sl + 1) * LANES)
            xq = x_ref[:, cs]
            q_ref[:, cs] = ((xq * _head_norm(xq, lo) * qg_ref[...]).astype(BF16).astype(F32) * SB_QSCALE).astype(BF16)
            xk = x_ref[:, D + sl * LANES:D + (sl + 1) * LANES]
            k_ref[:, cs] = (xk * _head_norm(xk, lo) * kg_ref[...]).astype(BF16)
        v_ref[...] = x_ref[:, 2 * D:3 * D].astype(BF16)

    return _pcall(
        body, name="sb_prep_fwd", grid=(S // tm,),
        in_specs=[pl.BlockSpec((tm, 3 * D), lambda i: (i, 0)), pl.BlockSpec((1, LANES), lambda i: (0, 0)),
                  pl.BlockSpec((1, LANES), lambda i: (0, 0))],
        out_specs=[pl.BlockSpec((tm, D), lambda i: (i, 0))] * 3,
        out_shape=[jax.ShapeDtypeStruct((S, D), BF16)] * 3,
        compiler_params=_cp("parallel"))(qkv, qg, kg)


def sb_prep_bwd(dqs, dkn, dv, qkv, qg, kg, tm=256):
    S = qkv.shape[0]
    tm = min(tm, S)

    def body(dq_ref, dk_ref, dv_ref, x_ref, qg_ref, kg_ref, dx_ref, dqg_ref, dkg_ref):
        i = pl.program_id(0)

        @pl.when(i == 0)
        def _():
            dqg_ref[...] = jnp.zeros_like(dqg_ref)
            dkg_ref[...] = jnp.zeros_like(dkg_ref)

        lo = _iota((1, LANES), 1) < SBD

        def one(xv, dh, gain):
            r = _head_norm(xv, lo)
            y = xv * r
            dy = dh * gain
            t = dy * y
            m0 = jnp.sum(jnp.where(lo, t, 0.0), axis=-1, keepdims=True)
            m1 = jnp.sum(jnp.where(lo, 0.0, t), axis=-1, keepdims=True)
            dx = r * (dy - y * (jnp.where(lo, m0, m1) / SBD))
            return dx, jnp.sum(dh * y, axis=0, keepdims=True)

        for sl in range(D // LANES):
            cs = slice(sl * LANES, (sl + 1) * LANES)
            dx, dg = one(x_ref[:, cs], dq_ref[:, cs] * SB_QSCALE, qg_ref[...])
            dx_ref[:, cs] = dx.astype(BF16)
            dqg_ref[:, cs] += dg
            ks = slice(D + sl * LANES, D + (sl + 1) * LANES)
            dx, dg = one(x_ref[:, ks], dk_ref[:, cs], kg_ref[...])
            dx_ref[:, ks] = dx.astype(BF16)
            dkg_ref[:, cs] += dg
        dx_ref[:, 2 * D:3 * D] = dv_ref[...].astype(BF16)

    return _pcall(
        body, name="sb_prep_bwd", grid=(S // tm,),
        in_specs=[pl.BlockSpec((tm, D), lambda i: (i, 0))] * 3
        + [pl.BlockSpec((tm, 3 * D), lambda i: (i, 0)), pl.BlockSpec((1, LANES), lambda i: (0, 0)),
           pl.BlockSpec((1, LANES), lambda i: (0, 0))],
        out_specs=[pl.BlockSpec((tm, 3 * D), lambda i: (i, 0)), pl.BlockSpec((1, D), lambda i: (0, 0)),
                   pl.BlockSpec((1, D), lambda i: (0, 0))],
        out_shape=[jax.ShapeDtypeStruct((S, 3 * D), BF16), jax.ShapeDtypeStruct((1, D), F32),
                   jax.ShapeDtypeStruct((1, D), F32)],
        compiler_params=_cp("arbitrary"))(dqs, dkn, dv, qkv, qg, kg)


def _split_dot(x, u):
    hi = x.astype(BF16)
    lo = (x - hi.astype(F32)).astype(BF16)
    return _dot(hi, u) + _dot(lo, u)


def _sb_logits(qh, kb, valid):
    z = _dot_nt(qh, kb)
    e = jnp.exp(-jnp.abs(z))
    lp = jnp.log(1.0 + e)
    lb = jnp.minimum(z, 0.0) - lp
    l1m = jnp.where(valid, lb - z, 0.0)
    return z, e, lb, l1m


def sb_fwd(qs, kn, v):
    S = qs.shape[0]
    T = min(SB_T, S)
    nq = S // T

    def body(q_ref, k_ref, v_ref, o_ref, r_ref, oacc, rrun):
        i = pl.program_id(1)
        qb = q_ref[...]
        lo = _iota((1, LANES), 1) < SBD
        row, col = _iota((T, T), 0), _iota((T, T), 1)
        u = (row > col).astype(BF16)
        lane_blk = _iota((T, LANES), 1)
        oacc[...] = jnp.zeros_like(oacc)
        for hh in range(2):
            hm = lo if hh == 0 else jnp.logical_not(lo)
            qh = jnp.where(hm, qb, jnp.zeros_like(qb))
            rrun[...] = jnp.zeros_like(rrun)
            r_ref[hh] = jnp.full((T, LANES), SB_UNSEEN, F32)

            def live(carry):
                s, rmax = carry
                return jnp.logical_and(s <= i, rmax > SB_DEAD)

            def step(carry, hm=hm, qh=qh, hh=hh):
                s, _ = carry
                j = i - s
                off = pl.multiple_of(j * T, T)
                kb = k_ref[pl.ds(off, T), :]
                vb = v_ref[pl.ds(off, T), :]
                vb = jnp.where(hm, vb, jnp.zeros_like(vb))
                valid = (j * T + col) < (i * T + row)
                _, _, lb, l1m = _sb_logits(qh, kb, valid)
                r = rrun[...]
                aft = _split_dot(l1m, u) + r
                a = jnp.where(valid, jnp.exp(lb + aft), 0.0)
                oacc[...] += _dot(a.astype(BF16), vb)
                r_ref[hh] = jnp.where(lane_blk == j, r, r_ref[hh])
                rnew = r + jnp.sum(l1m, axis=-1, keepdims=True)
                rrun[...] = rnew
                return s + 1, jnp.max(rnew)

            lax.while_loop(live, step, (jnp.int32(0), jnp.float32(0.0)))
        o_ref[...] = oacc[...].astype(BF16)

    return _pcall(
        body, name="sb_fwd", grid=(D // LANES, nq),
        in_specs=[pl.BlockSpec((T, LANES), lambda h, i: (i, h)), pl.BlockSpec((S, LANES), lambda h, i: (0, h)),
                  pl.BlockSpec((S, LANES), lambda h, i: (0, h))],
        out_specs=[pl.BlockSpec((T, LANES), lambda h, i: (i, h)), pl.BlockSpec((2, T, LANES), lambda h, i: (h, i, 0))],
        out_shape=[jax.ShapeDtypeStruct((S, D), BF16), jax.ShapeDtypeStruct((SBH, S, LANES), F32)],
        scratch_shapes=[pltpu.VMEM((T, LANES), F32), pltpu.VMEM((T, 1), F32)],
        compiler_params=_cp("parallel", "arbitrary"))(qs, kn, v)


def sb_bwd(qs, kn, v, do, rsave):
    S = qs.shape[0]
    T = min(SB_T, S)
    nq = S // T

    def body(q_ref, k_ref, v_ref, do_ref, r_ref, dq_ref, dk_ref, dv_ref, crun):
        i = pl.program_id(1)

        @pl.when(i == 0)
        def _():
            dk_ref[...] = jnp.zeros_like(dk_ref)
            dv_ref[...] = jnp.zeros_like(dv_ref)

        qb, dob = q_ref[...], do_ref[...]
        lo = _iota((1, LANES), 1) < SBD
        row, col = _iota((T, T), 0), _iota((T, T), 1)
        u = (row > col).astype(BF16)
        u2 = (row < col).astype(BF16)
        lane_blk = _iota((T, LANES), 1)
        dq_ref[...] = jnp.zeros_like(dq_ref)
        for hh in range(2):
            hm = lo if hh == 0 else jnp.logical_not(lo)
            qh = jnp.where(hm, qb, jnp.zeros_like(qb))
            doh = jnp.where(hm, dob, jnp.zeros_like(dob))
            crun[...] = jnp.zeros_like(crun)

            def step(j, carry, hm=hm, qh=qh, doh=doh, hh=hh):
                off = pl.multiple_of(j * T, T)
                kb = k_ref[pl.ds(off, T), :]
                vb = v_ref[pl.ds(off, T), :]
                vb = jnp.where(hm, vb, jnp.zeros_like(vb))
                valid = (j * T + col) < (i * T + row)
                z, e, lb, l1m = _sb_logits(qh, kb, valid)
                r = jnp.sum(jnp.where(lane_blk == j, r_ref[hh], 0.0), axis=-1, keepdims=True)
                aft = _split_dot(l1m, u) + r
                a = jnp.where(valid, jnp.exp(lb + aft), 0.0)
                w = a * _dot_nt(doh, vb)
                cprev = crun[...]
                cw = _split_dot(w, u2) + cprev
                inv = 1.0 / (1.0 + e)
                pos = z >= 0.0
                beta = jnp.where(pos, 1.0, e) * inv
                onem = jnp.where(pos, e, 1.0) * inv
                dz = jnp.where(valid, w * onem - beta * cw, 0.0).astype(BF16)
                dq_ref[...] += _dot(dz, jnp.where(hm, kb, jnp.zeros_like(kb)))
                dk_ref[pl.ds(off, T), :] += _dot_tn(dz, qh)
                dv_ref[pl.ds(off, T), :] += _dot_tn(a.astype(BF16), doh)
                crun[...] = cprev + jnp.sum(w, axis=-1, keepdims=True)
                return carry

            col_max = jnp.max(r_ref[hh], axis=0, keepdims=True)
            seen = jnp.logical_and(col_max > SB_DEAD, _iota((1, LANES), 1) <= i)
            n_live = jnp.sum(seen.astype(jnp.int32))
            lax.fori_loop(i + 1 - n_live, i + 1, step, 0)

    return _pcall(
        body, name="sb_bwd", grid=(D // LANES, nq),
        in_specs=[pl.BlockSpec((T, LANES), lambda h, i: (i, h)), pl.BlockSpec((S, LANES), lambda h, i: (0, h)),
                  pl.BlockSpec((S, LANES), lambda h, i: (0, h)), pl.BlockSpec((T, LANES), lambda h, i: (i, h)),
                  pl.BlockSpec((2, T, LANES), lambda h, i: (h, i, 0))],
        out_specs=[pl.BlockSpec((T, LANES), lambda h, i: (i, h)), pl.BlockSpec((S, LANES), lambda h, i: (0, h)),
                   pl.BlockSpec((S, LANES), lambda h, i: (0, h))],
        out_shape=[jax.ShapeDtypeStruct((S, D), F32)] * 3,
        scratch_shapes=[pltpu.VMEM((T, 1), F32)],
        compiler_params=_cp("parallel", "arbitrary"))(qs, kn, v, do, rsave)


def loss_head(y, target, tm=512):
    S = y.shape[0]
    tm = min(tm, S)

    def body(y_ref, t_ref, ls_ref, dy_ref):
        i = pl.program_id(0)
        err = y_ref[...] - t_ref[...]
        dy_ref[...] = err * (1.0 / D)
        part = jnp.sum(err * err, axis=0, keepdims=True)

        @pl.when(i == 0)
        def _():
            ls_ref[...] = part

        @pl.when(i > 0)
        def _():
            ls_ref[...] += part

    return _pcall(
        body, name="loss_head", grid=(S // tm,),
        in_specs=[pl.BlockSpec((tm, D), lambda i: (i, 0))] * 2,
        out_specs=[pl.BlockSpec((1, D), lambda i: (0, 0)), pl.BlockSpec((tm, D), lambda i: (i, 0))],
        out_shape=[jax.ShapeDtypeStruct((1, D), F32), jax.ShapeDtypeStruct((S, D), F32)],
        compiler_params=_cp("arbitrary"))(y, target)


def adamw(w, gs, m, v, tr=1024, name="adamw"):
    R = w.shape[0]
    tr = min(tr, R)
    n_g = len(gs)
    c1 = 1.0 / (1.0 - ADAM_B1 ** ADAM_STEP)
    c2 = 1.0 / (1.0 - ADAM_B2 ** ADAM_STEP)

    def body(*refs):
        w_ref, g_refs = refs[0], refs[1:1 + n_g]
        m_ref, v_ref, go_ref, d_ref, mo_ref, vo_ref = refs[1 + n_g:]
        g = g_refs[0][...]
        for r in g_refs[1:]:
            g = g + r[...]
        mn = ADAM_B1 * m_ref[...] + (1.0 - ADAM_B1) * g
        vn = ADAM_B2 * v_ref[...] + (1.0 - ADAM_B2) * (g * g)
        go_ref[...] = g
        mo_ref[...] = mn
        vo_ref[...] = vn
        d_ref[...] = -ADAM_LR * ((mn * c1) / (jnp.sqrt(vn * c2) + ADAM_EPS) + ADAM_WD * w_ref[...])

    spec = pl.BlockSpec((tr, LANES), lambda i: (i, 0))
    return _pcall(
        body, name=name, grid=(R // tr,), in_specs=[spec] * (3 + n_g), out_specs=[spec] * 4,
        out_shape=[jax.ShapeDtypeStruct((R, LANES), F32)] * 4,
        compiler_params=_cp("parallel"))(w, *gs, m, v)


def sum4(parts, tr=1024):
    R = parts[0].shape[0]
    tr = min(tr, R)

    def body(a_ref, b_ref, c_ref, d_ref, o_ref):
        o_ref[...] = ((a_ref[...] + b_ref[...]) + c_ref[...]) + d_ref[...]

    spec = pl.BlockSpec((tr, LANES), lambda i: (i, 0))
    return _pcall(
        body, name="sum4", grid=(R // tr,), in_specs=[spec] * 4, out_specs=spec,
        out_shape=jax.ShapeDtypeStruct((R, LANES), F32), compiler_params=_cp("parallel"))(*parts)


MESH = pl.DeviceIdType.MESH
ANY = pl.BlockSpec(memory_space=pl.ANY)


def _other_chips(x, y):
    return [(1 - x, y), (x, 1 - y), (1 - x, 1 - y)]


def gather_chips(shard, name):
    def body(in_ref, out_ref, send_sems, recv_sems, local_sem):
        x, y, c = lax.axis_index("x"), lax.axis_index("y"), lax.axis_index("c")
        me = 2 * x + y
        mine = pltpu.make_async_copy(in_ref, out_ref.at[me], local_sem)
        mine.start()
        copies = [pltpu.make_async_remote_copy(
            src_ref=in_ref, dst_ref=out_ref.at[me], send_sem=send_sems.at[k], recv_sem=recv_sems.at[k],
            device_id=(px, py, c), device_id_type=MESH) for k, (px, py) in enumerate(_other_chips(x, y))]
        for cp in copies:
            cp.start()
        for cp in copies:
            cp.wait()
        mine.wait()

    return _pcall(
        body, name=name, in_specs=[ANY], out_specs=ANY,
        out_shape=jax.ShapeDtypeStruct((N_CHIPS,) + shard.shape, shard.dtype),
        scratch_shapes=[pltpu.SemaphoreType.DMA((3,)), pltpu.SemaphoreType.DMA((3,)), pltpu.SemaphoreType.DMA])(shard)


def scatter_chips(gstack):
    def body(in_ref, out_ref, send_sems, recv_sems):
        x, y, c = lax.axis_index("x"), lax.axis_index("y"), lax.axis_index("c")
        copies = [pltpu.make_async_remote_copy(
            src_ref=in_ref.at[2 * px + py], dst_ref=out_ref.at[k], send_sem=send_sems.at[k],
            recv_sem=recv_sems.at[k], device_id=(px, py, c), device_id_type=MESH)
            for k, (px, py) in enumerate(_other_chips(x, y))]
        for cp in copies:
            cp.start()
        for cp in copies:
            cp.wait()

    return _pcall(
        body, name="scatter_chips", in_specs=[ANY], out_specs=ANY,
        out_shape=jax.ShapeDtypeStruct((3,) + gstack.shape[1:], gstack.dtype),
        scratch_shapes=[pltpu.SemaphoreType.DMA((3,)), pltpu.SemaphoreType.DMA((3,))])(gstack)


def swap_cores(part):
    def body(in_ref, out_ref, send_sem, recv_sem):
        x, y, c = lax.axis_index("x"), lax.axis_index("y"), lax.axis_index("c")
        cp = pltpu.make_async_remote_copy(src_ref=in_ref, dst_ref=out_ref, send_sem=send_sem, recv_sem=recv_sem,
                                          device_id=(x, y, 1 - c), device_id_type=MESH)
        cp.start()
        cp.wait()

    return _pcall(
        body, name="swap_cores", in_specs=[ANY], out_specs=ANY,
        out_shape=jax.ShapeDtypeStruct(part.shape, part.dtype),
        scratch_shapes=[pltpu.SemaphoreType.DMA, pltpu.SemaphoreType.DMA])(part)


def allreduce_small(vec):
    R = vec.shape[0]

    def body(in_ref, out_ref, buf, send_sems, recv_sems):
        x, y, c = lax.axis_index("x"), lax.axis_index("y"), lax.axis_index("c")
        me = 4 * x + 2 * y + c
        buf[me] = in_ref[...]
        copies = []
        for k in range(1, 8):
            peer = (x ^ (k >> 2), y ^ ((k >> 1) & 1), c ^ (k & 1))
            copies.append(pltpu.make_async_remote_copy(
                src_ref=in_ref, dst_ref=buf.at[me], send_sem=send_sems.at[k - 1], recv_sem=recv_sems.at[k - 1],
                device_id=peer, device_id_type=MESH))
        for cp in copies:
            cp.start()
        for cp in copies:
            cp.wait()
        acc = buf[0]
        for d in range(1, 8):
            acc = acc + buf[d]
        out_ref[...] = acc

    vm = pl.BlockSpec(memory_space=pltpu.VMEM)
    return _pcall(
        body, name="allreduce_small", in_specs=[vm], out_specs=vm,
        out_shape=jax.ShapeDtypeStruct((R, LANES), F32),
        scratch_shapes=[pltpu.VMEM((8, R, LANES), F32), pltpu.SemaphoreType.DMA((7,)), pltpu.SemaphoreType.DMA((7,))])(vec)


SHARDED = [("pool_in", 1, "bf16"), ("pool_group", 2, "bf16"), ("pool_scale", 1, "f32"), ("ssd_in", 2, "bf16"),
           ("ssd_conv_w", 2, "f32"), ("ssd_out", 1, "bf16"), ("sb_qkv", 2, "bf16"), ("sb_out", 1, "bf16"),
           ("ffn_gate", 2, "bf16"), ("ffn_up", 2, "bf16"), ("ffn_down", 1, "bf16")]
REPLICATED = ["mix_norm", "ssd_conv_b", "ssd_dt_bias", "ssd_a_log", "ssd_d", "ssd_out_norm", "sb_q_norm",
              "sb_k_norm", "ffn_norm"]
WEIGHT_ORDER = ["mix_norm", "pool_in", "pool_group", "pool_scale", "ssd_in", "ssd_conv_w", "ssd_conv_b",
                "ssd_dt_bias", "ssd_a_log", "ssd_d", "ssd_out_norm", "ssd_out", "sb_qkv", "sb_q_norm", "sb_k_norm",
                "sb_out", "ffn_norm", "ffn_gate", "ffn_up", "ffn_down"]
ROW_PAD = 1024


def _piece_rows(n, mult):
    rows = -(-n // LANES)
    return -(-rows // mult) * mult


def _as_rows(a, mult):
    flat = a.reshape(-1)
    rows = _piece_rows(flat.shape[0], mult)
    if rows * LANES != flat.shape[0]:
        flat = jnp.pad(flat, (0, rows * LANES - flat.shape[0]))
    return flat.reshape(rows, LANES)


def _pack(arrs, mult=8, row_pad=ROW_PAD):
    parts = [_as_rows(a, mult) for a in arrs]
    rows = sum(p.shape[0] for p in parts)
    pad = -rows % row_pad
    if pad:
        parts.append(jnp.zeros((pad, LANES), parts[0].dtype))
    return jnp.concatenate(parts, axis=0)


def _unpack(packed, shapes, mult=8, lead=()):
    out, off = [], 0
    for s in shapes:
        n = math.prod(s)
        rows = _piece_rows(n, mult)
        piece = packed[..., off:off + rows, :].reshape(lead + (rows * LANES,))
        out.append(piece[..., :n].reshape(lead + tuple(s)))
        off += rows
    return out


def _gather_weights(shards):
    full = {}
    for kind, dtype, mult, row_pad in (("bf16", BF16, 16, ROW_PAD), ("f32", F32, 8, 8)):
        group = [(n, ax) for n, ax, k in SHARDED if k == kind]
        gathered = gather_chips(_pack([shards[n].astype(dtype) for n, _ in group], mult, row_pad),
                                name="gather_chips_" + kind)
        pieces = _unpack(gathered, [shards[n].shape for n, _ in group], mult, lead=(N_CHIPS,))
        for (n, ax), p in zip(group, pieces):
            full[n] = jnp.concatenate([p[j] for j in range(N_CHIPS)], axis=ax)
    return full


def _split_shards(full, axis):
    return jnp.stack(jnp.split(full, N_CHIPS, axis=axis))


def _ffn_fwd(x, gain, wg, wu, wd):
    h = rmsnorm_fwd(x, gain, name="ffn_norm_fwd")
    a, b, hid = ffn_up(h, wg, wu)
    xo = linear([(hid, wd, "nn")], res=x, name="ffn_down")
    return xo, (x, h, a, b, hid)


def _ffn_bwd(dout, saved, gain, wg, wu, wd):
    x, h, a, b, hid = saved
    da, db = ffn_bwd_hidden(dout, wd, a, b)
    (dwd,) = wgrad(hid, [dout], tk=1408, name="ffn_dwd")
    dwg, dwu = wgrad(h, [da, db], tn=1408, tm=512, name="ffn_dwgu")
    dh = linear([(da, wg, "nt"), (db, wu, "nt")], tm=256, name="ffn_dh")
    dx, dgain = rmsnorm_bwd(x, gain, dh, dout, name="ffn_norm_bwd")
    return dx, dgain, dwg, dwu, dwd


def _pool_layer_fwd(x, gain, w_in, wgrp, scale):
    h = rmsnorm_fwd(x, gain, name="pool_norm_fwd")
    u = linear([(h, w_in, "nn")], name="pool_in")
    xo, p = pool_fwd(u, wgrp, scale, x)
    return xo, (x, h, p)


def _pool_layer_bwd(dout, saved, gain, w_in, wgrp, scale):
    x, h, p = saved
    dp, dwgrp, dscale = pool_bwd_group(dout, p, wgrp, scale)
    du = pool_bwd_window(dp)
    (dw_in,) = wgrad(h, [du], name="pool_dwin")
    dh = linear([(du, w_in, "nt")], name="pool_dh")
    dx, dgain = rmsnorm_bwd(x, gain, dh, dout, name="pool_norm_bwd")
    return dx, dgain, dw_in, dwgrp, dscale


def _ssd_layer_fwd(x, gain, w_z, w_xbc, w_dt, conv_w, conv_b, dt_bias, a_log, d_full, out_norm, w_out):
    h = rmsnorm_fwd(x, gain, name="ssd_norm_fwd")
    z = linear([(h, w_z, "nn")], name="ssd_in_z")
    xbc = linear([(h, w_xbc, "nn")], tn=2048, name="ssd_in_xbc")
    dt_raw = linear([(h, w_dt, "nn")], name="ssd_in_dt")
    act = conv_fwd(xbc, conv_w, conv_b)
    y, states = ssd_scan_fwd(act, dt_raw, dt_bias, a_log, d_full)
    gn = gate_norm_fwd(y, z, out_norm)
    xo = linear([(gn, w_out, "nn")], res=x, name="ssd_out")
    return xo, (x, h, z, xbc, dt_raw, act, y, states, gn)


def _ssd_layer_bwd(dout, saved, gain, w_z, w_xbc, w_dt, conv_w, conv_b, dt_bias, a_log, d_full, out_norm, w_out):
    x, h, z, xbc, dt_raw, act, y, states, gn = saved
    dgn = linear([(dout, w_out, "nt")], name="ssd_dgn")
    (dw_out,) = wgrad(gn, [dout], name="ssd_dwout")
    dy, dz, dout_norm = gate_norm_bwd(dgn, y, z, out_norm)
    dact, ddt_raw, dbias, dalog, dd_full = ssd_scan_bwd(dy, act, dt_raw, dt_bias, a_log, d_full, states)
    dpre, dconv_w8, dconv_b = conv_bwd_pre(dact, xbc, conv_w, conv_b)
    dxbc = conv_bwd_input(dpre, conv_w)
    ddt_b = ddt_raw.astype(BF16)
    (dw_z,) = wgrad(h, [dz], name="ssd_dwz")
    (dw_xbc,) = wgrad(h, [dxbc], tn=2048, name="ssd_dwxbc")
    (dw_dt,) = wgrad(h, [ddt_b], name="ssd_dwdt")
    dh = linear([(dz, w_z, "nt"), (dxbc, w_xbc, "nt"), (ddt_b, w_dt, "nt")], tm=256, name="ssd_dh")
    dx, dgain = rmsnorm_bwd(x, gain, dh, dout, name="ssd_norm_bwd")
    dw_in = jnp.concatenate([dw_z, dw_xbc, dw_dt], axis=1)
    dd = dd_full.reshape(NH, HP).sum(axis=1).reshape(1, NH)
    return dx, dgain, dw_in, dconv_w8[:4], dconv_b, dbias, dalog, dd, dout_norm, dw_out


def _sb_layer_fwd(x, gain, w_qkv, qg, kg, w_out):
    h = rmsnorm_fwd(x, gain, name="sb_norm_fwd")
    qkv = linear([(h, w_qkv, "nn")], tn=1024, name="sb_qkv")
    qs, kn, v = sb_prep_fwd(qkv, qg, kg)
    o, rsave = sb_fwd(qs, kn, v)
    xo = linear([(o, w_out, "nn")], res=x, name="sb_out")
    return xo, (x, h, qkv, qs, kn, v, o, rsave)


def _sb_layer_bwd(dout, saved, gain, w_qkv, qg, kg, w_out):
    x, h, qkv, qs, kn, v, o, rsave = saved
    do = linear([(dout, w_out, "nt")], out_dtype=BF16, name="sb_do")
    (dw_out,) = wgrad(o, [dout], name="sb_dwout")
    dqs, dkn, dv = sb_bwd(qs, kn, v, do, rsave)
    dqkv, dqg, dkg = sb_prep_bwd(dqs, dkn, dv, qkv, qg, kg)
    (dw_qkv,) = wgrad(h, [dqkv], tn=1024, name="sb_dwqkv")
    dh = linear([(dqkv, w_qkv, "nt")], name="sb_dh")
    dx, dgain = rmsnorm_bwd(x, gain, dh, dout, name="sb_norm_bwd")
    dqg = dqg.reshape(SBH, SBD).sum(axis=0).reshape(1, SBD)
    dkg = dkg.reshape(SBH, SBD).sum(axis=0).reshape(1, SBD)
    return dx, dgain, dw_qkv, dqg, dkg, dw_out


def _local_step(x, target, full, rep):
    S = x.shape[0]
    d_full = jnp.repeat(rep["ssd_d"][0], HP).reshape(1, DI)
    qg = jnp.tile(rep["sb_q_norm"][0], 2).reshape(1, LANES)
    kg = jnp.tile(rep["sb_k_norm"][0], 2).reshape(1, LANES)
    ssd_in = full["ssd_in"][0]
    w_z, w_xbc, w_dt = ssd_in[:, :DI], ssd_in[:, DI:DI + CONV_CH], ssd_in[:, DI + CONV_CH:]
    conv_w = full["ssd_conv_w"][0]
    conv_b = rep["ssd_conv_b"]
    pool_scale = full["pool_scale"]

    def mixer_args(i):
        kind, j = i % 3, i // 3
        if kind == 0:
            return (full["pool_in"][j], full["pool_group"][j], pool_scale[j:j + 1])
        if kind == 1:
            return (w_z, w_xbc, w_dt, conv_w, conv_b, rep["ssd_dt_bias"], rep["ssd_a_log"], d_full,
                    rep["ssd_out_norm"], full["ssd_out"][0])
        return (full["sb_qkv"][0], qg, kg, full["sb_out"][0])

    fwd = (_pool_layer_fwd, _ssd_layer_fwd, _sb_layer_fwd)
    bwd = (_pool_layer_bwd, _ssd_layer_bwd, _sb_layer_bwd)
    saved = []
    for i in range(DEPTH):
        x, sm = fwd[i % 3](x, rep["mix_norm"][i], *mixer_args(i))
        x, sf = _ffn_fwd(x, rep["ffn_norm"][i], full["ffn_gate"][i], full["ffn_up"][i], full["ffn_down"][i])
        saved.append((sm, sf))

    colsq, dx = loss_head(x, target)
    loss = 0.5 * jnp.sum(colsq) / D

    g = {n: [None] * DEPTH for n in ("mix_norm", "ffn_norm", "ffn_gate", "ffn_up", "ffn_down")}
    g["pool_in"], g["pool_group"], g["pool_scale"] = [None] * 2, [None] * 2, [None] * 2
    for i in reversed(range(DEPTH)):
        sm, sf = saved[i]
        dx, g["ffn_norm"][i], g["ffn_gate"][i], g["ffn_up"][i], g["ffn_down"][i] = _ffn_bwd(
            dx, sf, rep["ffn_norm"][i], full["ffn_gate"][i], full["ffn_up"][i], full["ffn_down"][i])
        kind, j = i % 3, i // 3
        res = bwd[kind](dx, sm, rep["mix_norm"][i], *mixer_args(i))
        dx, g["mix_norm"][i] = res[0], res[1]
        if kind == 0:
            g["pool_in"][j], g["pool_group"][j], g["pool_scale"][j] = res[2:]
        elif kind == 1:
            dw_in, dconv_w, dconv_b, dbias, dalog, dd, don, dw_out = res[2:]
            g.update(ssd_in=dw_in[None], ssd_conv_w=dconv_w[None], ssd_conv_b=dconv_b, ssd_dt_bias=dbias,
                     ssd_a_log=dalog, ssd_d=dd, ssd_out_norm=don, ssd_out=dw_out[None])
        else:
            dw_qkv, dqg, dkg, dw_out = res[2:]
            g.update(sb_qkv=dw_qkv[None], sb_q_norm=dqg, sb_k_norm=dkg, sb_out=dw_out[None])
    for n in ("mix_norm", "ffn_norm", "pool_scale"):
        g[n] = jnp.concatenate(g[n], axis=0)
    for n in ("ffn_gate", "ffn_up", "ffn_down", "pool_in", "pool_group"):
        g[n] = jnp.stack(g[n])
    return loss, dx, g


def kernel(x, mix_norm, pool_in, pool_group, pool_scale, ssd_in, ssd_conv_w, ssd_conv_b, ssd_dt_bias, ssd_a_log, ssd_d, ssd_out_norm, ssd_out, sb_qkv, sb_q_norm, sb_k_norm, sb_out, ffn_norm, ffn_gate, ffn_up, ffn_down, loss_target, m_mix_norm, m_pool_in, m_pool_group, m_pool_scale, m_ssd_in, m_ssd_conv_w, m_ssd_conv_b, m_ssd_dt_bias, m_ssd_a_log, m_ssd_d, m_ssd_out_norm, m_ssd_out, m_sb_qkv, m_sb_q_norm, m_sb_k_norm, m_sb_out, m_ffn_norm, m_ffn_gate, m_ffn_up, m_ffn_down, v_mix_norm, v_pool_in, v_pool_group, v_pool_scale, v_ssd_in, v_ssd_conv_w, v_ssd_conv_b, v_ssd_dt_bias, v_ssd_a_log, v_ssd_d, v_ssd_out_norm, v_ssd_out, v_sb_qkv, v_sb_q_norm, v_sb_k_norm, v_sb_out, v_ffn_norm, v_ffn_gate, v_ffn_up, v_ffn_down):
    given = dict(locals())
    w = {n: given[n] for n in WEIGHT_ORDER}
    m = {n: given["m_" + n] for n in WEIGHT_ORDER}
    v = {n: given["v_" + n] for n in WEIGHT_ORDER}
    sharded_names = [s[0] for s in SHARDED]

    full = _gather_weights(w)
    rep = {n: w[n] for n in REPLICATED}

    loss, dx, g = _local_step(x[0], loss_target[0], full, rep)
    loss = lax.psum(loss, ("x", "y", "c"))

    gstack = jnp.stack([_pack([_split_shards(g[n], ax)[j] for n, ax, _ in SHARDED]) for j in range(N_CHIPS)])
    recv = scatter_chips(gstack)
    me = 2 * lax.axis_index("x") + lax.axis_index("y")
    own = lax.dynamic_index_in_dim(gstack, me, axis=0, keepdims=False)
    part = sum4([own, recv[0], recv[1], recv[2]])
    other = swap_cores(part)
    shard_shapes = [w[n].shape for n in sharded_names]
    gs, ds, ms, vs = adamw(_pack([w[n] for n in sharded_names]), [part, other],
                           _pack([m[n] for n in sharded_names]), _pack([v[n] for n in sharded_names]),
                           name="adamw_sharded")
    out = {}
    for key, flat in (("g", gs), ("d", ds), ("m", ms), ("v", vs)):
        for n, a in zip(sharded_names, _unpack(flat, shard_shapes)):
            out[key, n] = a

    rep_shapes = [w[n].shape for n in REPLICATED]
    pack_small = functools.partial(_pack, row_pad=8)
    gsum = allreduce_small(pack_small([g[n] for n in REPLICATED]))
    gs, ds, ms, vs = adamw(pack_small([w[n] for n in REPLICATED]), [gsum], pack_small([m[n] for n in REPLICATED]),
                           pack_small([v[n] for n in REPLICATED]), name="adamw_replicated")
    for key, flat in (("g", gs), ("d", ds), ("m", ms), ("v", vs)):
        for n, a in zip(REPLICATED, _unpack(flat, rep_shapes)):
            out[key, n] = a

    return (loss, dx[None], *[out["g", n] for n in WEIGHT_ORDER], *[out["d", n] for n in WEIGHT_ORDER],
            *[out["m", n] for n in WEIGHT_ORDER], *[out["v", n] for n in WEIGHT_ORDER])
```

```python
import functools
import math

import jax
import jax.numpy as jnp
from jax import lax
from jax.experimental import pallas as pl
from jax.experimental.pallas import tpu as pltpu

F32 = jnp.float32
BF16 = jnp.bfloat16
HI = lax.Precision.HIGHEST

D = 1024
DEPTH = 4
EPS = 1e-6
POOL_WINDOWS = (2, 4, 8, 16)
PG = 256
DI = 2048
NH = 32
HP = 64
NG = 8
NS = 128
GW = 256
CH = 256
CONV_CH = 4096
SSD_IN = 6176
SBH = 16
SBD = 64
FH = 2816
N_CHIPS = 4
LANES = 128

ADAM_LR = 0.001
ADAM_B1 = 0.9
ADAM_B2 = 0.999
ADAM_EPS = 1e-08
ADAM_WD = 0.01
ADAM_STEP = 10

VMEM_LIMIT = 56 * 1024 * 1024


def _pcall(body, **kw):
    return pl.pallas_call(body, **kw)


def _cp(*sem):
    return pltpu.CompilerParams(dimension_semantics=sem, vmem_limit_bytes=VMEM_LIMIT)


def _dot(a, b, prec=None):
    return lax.dot_general(a, b, (((1,), (0,)), ((), ())), precision=prec, preferred_element_type=F32)


def _dot_nt(a, b, prec=None):
    return lax.dot_general(a, b, (((1,), (1,)), ((), ())), precision=prec, preferred_element_type=F32)


def _dot_tn(a, b, prec=None):
    return lax.dot_general(a, b, (((0,), (0,)), ((), ())), precision=prec, preferred_element_type=F32)


def _sigmoid(x):
    return 1.0 / (1.0 + jnp.exp(-x))


def _iota(shape, axis):
    return lax.broadcasted_iota(jnp.int32, shape, axis)


def linear(pairs, res=None, out_dtype=F32, tm=512, tn=None, name="linear"):
    M = pairs[0][0].shape[0]
    N = pairs[0][1].shape[1] if pairs[0][2] == "nn" else pairs[0][1].shape[0]
    tm = min(tm, M)
    tn = N if tn is None else min(tn, N)
    n_pairs = len(pairs)
    modes = [p[2] for p in pairs]

    def body(*refs):
        acc = None
        for k in range(n_pairs):
            a = refs[2 * k][...].astype(BF16)
            w = refs[2 * k + 1][...]
            t = _dot(a, w) if modes[k] == "nn" else _dot_nt(a, w)
            acc = t if acc is None else acc + t
        if res is not None:
            acc = acc + refs[2 * n_pairs][...]
        refs[-1][...] = acc.astype(out_dtype)

    in_specs, args = [], []
    for a, w, mode in pairs:
        K = a.shape[1]
        in_specs.append(pl.BlockSpec((tm, K), lambda j, i: (i, 0)))
        if mode == "nn":
            in_specs.append(pl.BlockSpec((K, tn), lambda j, i: (0, j)))
        else:
            in_specs.append(pl.BlockSpec((tn, K), lambda j, i: (j, 0)))
        args += [a, w]
    if res is not None:
        in_specs.append(pl.BlockSpec((tm, tn), lambda j, i: (i, j)))
        args.append(res)
    return _pcall(
        body, name=name, grid=(N // tn, M // tm), in_specs=in_specs,
        out_specs=pl.BlockSpec((tm, tn), lambda j, i: (i, j)),
        out_shape=jax.ShapeDtypeStruct((M, N), out_dtype),
        compiler_params=_cp("parallel", "arbitrary"))(*args)


def wgrad(a, gs, tk=1024, tn=None, tm=1024, name="wgrad"):
    M, Ka = a.shape
    N = gs[0].shape[1]
    tk, tm = min(tk, Ka), min(tm, M)
    tn = N if tn is None else min(tn, N)
    n_g = len(gs)

    def body(*refs):
        a_ref, g_refs, o_refs = refs[0], refs[1:1 + n_g], refs[1 + n_g:]
        m = pl.program_id(2)
        at = a_ref[...].astype(BF16)
        for g_ref, o_ref in zip(g_refs, o_refs):
            t = _dot_tn(at, g_ref[...].astype(BF16))

            @pl.when(m == 0)
            def _():
                o_ref[...] = t

            @pl.when(m > 0)
            def _():
                o_ref[...] += t

    out = _pcall(
        body, name=name, grid=(Ka // tk, N // tn, M // tm),
        in_specs=[pl.BlockSpec((tm, tk), lambda k, j, m: (m, k))]
        + [pl.BlockSpec((tm, tn), lambda k, j, m: (m, j))] * n_g,
        out_specs=[pl.BlockSpec((tk, tn), lambda k, j, m: (k, j))] * n_g,
        out_shape=[jax.ShapeDtypeStruct((Ka, N), F32)] * n_g,
        compiler_params=_cp("parallel", "parallel", "arbitrary"))(a, *gs)
    return out


def rmsnorm_fwd(x, gain, tm=512, name="rmsnorm_fwd"):
    S, Dm = x.shape
    tm = min(tm, S)

    def body(x_ref, g_ref, o_ref):
        xv = x_ref[...]
        r = lax.rsqrt(jnp.mean(xv * xv, axis=-1, keepdims=True) + EPS)
        o_ref[...] = (xv * r * g_ref[...]).astype(BF16)

    return _pcall(
        body, name=name, grid=(S // tm,),
        in_specs=[pl.BlockSpec((tm, Dm), lambda i: (i, 0)), pl.BlockSpec((1, Dm), lambda i: (0, 0))],
        out_specs=pl.BlockSpec((tm, Dm), lambda i: (i, 0)),
        out_shape=jax.ShapeDtypeStruct((S, Dm), BF16),
        compiler_params=_cp("parallel"))(x, gain.reshape(1, Dm))


def rmsnorm_bwd(x, gain, dh, dres, tm=512, name="rmsnorm_bwd"):
    S, Dm = x.shape
    tm = min(tm, S)

    def body(x_ref, g_ref, dh_ref, dr_ref, dx_ref, dg_ref):
        i = pl.program_id(0)
        xv = x_ref[...]
        r = lax.rsqrt(jnp.mean(xv * xv, axis=-1, keepdims=True) + EPS)
        y = xv * r
        dhv = dh_ref[...]
        dy = dhv * g_ref[...]
        dx_ref[...] = dr_ref[...] + r * (dy - y * jnp.mean(dy * y, axis=-1, keepdims=True))
        part = jnp.sum(dhv * y, axis=0, keepdims=True)

        @pl.when(i == 0)
        def _():
            dg_ref[...] = part

        @pl.when(i > 0)
        def _():
            dg_ref[...] += part

    return _pcall(
        body, name=name, grid=(S // tm,),
        in_specs=[pl.BlockSpec((tm, Dm), lambda i: (i, 0)), pl.BlockSpec((1, Dm), lambda i: (0, 0)),
                  pl.BlockSpec((tm, Dm), lambda i: (i, 0)), pl.BlockSpec((tm, Dm), lambda i: (i, 0))],
        out_specs=[pl.BlockSpec((tm, Dm), lambda i: (i, 0)), pl.BlockSpec((1, Dm), lambda i: (0, 0))],
        out_shape=[jax.ShapeDtypeStruct((S, Dm), F32), jax.ShapeDtypeStruct((1, Dm), F32)],
        compiler_params=_cp("arbitrary"))(x, gain.reshape(1, Dm), dh, dres)


def ffn_up(h, wg, wu, tm=256, tn=1408):
    S = h.shape[0]
    tm = min(tm, S)

    def body(h_ref, wg_ref, wu_ref, a_ref, b_ref, hid_ref):
        hv = h_ref[...]
        a = _dot(hv, wg_ref[...])
        b = _dot(hv, wu_ref[...])
        a_ref[...] = a
        b_ref[...] = b
        hid_ref[...] = (a * _sigmoid(a) * b).astype(BF16)

    return _pcall(
        body, name="ffn_up", grid=(FH // tn, S // tm),
        in_specs=[pl.BlockSpec((tm, D), lambda j, i: (i, 0)), pl.BlockSpec((D, tn), lambda j, i: (0, j)),
                  pl.BlockSpec((D, tn), lambda j, i: (0, j))],
        out_specs=[pl.BlockSpec((tm, tn), lambda j, i: (i, j))] * 3,
        out_shape=[jax.ShapeDtypeStruct((S, FH), F32), jax.ShapeDtypeStruct((S, FH), F32),
                   jax.ShapeDtypeStruct((S, FH), BF16)],
        compiler_params=_cp("parallel", "arbitrary"))(h, wg, wu)


def ffn_bwd_hidden(dout, wd, a, b, tm=256, tn=1408):
    S = dout.shape[0]
    tm = min(tm, S)

    def body(do_ref, wd_ref, a_ref, b_ref, da_ref, db_ref):
        dhid = _dot_nt(do_ref[...].astype(BF16), wd_ref[...])
        av, bv = a_ref[...], b_ref[...]
        s = _sigmoid(av)
        da_ref[...] = (dhid * bv * (s * (1.0 + av * (1.0 - s)))).astype(BF16)
        db_ref[...] = (dhid * (av * s)).astype(BF16)

    return _pcall(
        body, name="ffn_bwd_hidden", grid=(FH // tn, S // tm),
        in_specs=[pl.BlockSpec((tm, D), lambda j, i: (i, 0)), pl.BlockSpec((tn, D), lambda j, i: (j, 0)),
                  pl.BlockSpec((tm, tn), lambda j, i: (i, j)), pl.BlockSpec((tm, tn), lambda j, i: (i, j))],
        out_specs=[pl.BlockSpec((tm, tn), lambda j, i: (i, j))] * 2,
        out_shape=[jax.ShapeDtypeStruct((S, FH), BF16)] * 2,
        compiler_params=_cp("parallel", "arbitrary"))(dout, wd, a, b)


POOL_T = 128
POOL_HALO = 16


def pool_fwd(u, wgrp, scale, x_res):
    S = u.shape[0]
    T, HB = min(POOL_T, S), POOL_HALO
    per = T // HB

    def body(u_ref, tail_ref, wg_ref, sc_ref, x_ref, xo_ref, p_ref):
        i = pl.program_id(0)
        uc = u_ref[...]
        tail = jnp.where(i > 0, tail_ref[...], 0.0)
        d_cur = _iota((T, T), 0) - _iota((T, T), 1)
        d_tail = _iota((T, HB), 0) - _iota((T, HB), 1) + HB
        tg = i * T + _iota((T, 1), 0)
        for g, w in enumerate(POOL_WINDOWS):
            gs = slice(g * PG, (g + 1) * PG)
            band = ((d_cur >= 0) & (d_cur < w)).astype(F32)
            band_t = ((d_tail >= 0) & (d_tail < w)).astype(F32)
            ug = uc[:, gs]
            ws = _dot(band, ug, HI) + _dot(band_t, tail[:, gs], HI)
            cnt = jnp.minimum(tg + 1, w).astype(F32)
            pb = (ws / cnt - ug).astype(BF16)
            p_ref[:, gs] = pb
            xo_ref[:, gs] = x_ref[:, gs] + _dot(pb, wg_ref[g]) * sc_ref[:, gs]

    return _pcall(
        body, name="pool_fwd", grid=(S // T,),
        in_specs=[pl.BlockSpec((T, D), lambda i: (i, 0)),
                  pl.BlockSpec((HB, D), lambda i: (jnp.maximum(i * per - 1, 0), 0)),
                  pl.BlockSpec((4, PG, PG), lambda i: (0, 0, 0)), pl.BlockSpec((1, D), lambda i: (0, 0)),
                  pl.BlockSpec((T, D), lambda i: (i, 0))],
        out_specs=[pl.BlockSpec((T, D), lambda i: (i, 0))] * 2,
        out_shape=[jax.ShapeDtypeStruct((S, D), F32), jax.ShapeDtypeStruct((S, D), BF16)],
        compiler_params=_cp("parallel"))(u, u, wgrp, scale, x_res)


def pool_bwd_group(dm, p, wgrp, scale, tm=512):
    S = dm.shape[0]
    tm = min(tm, S)

    def body(dm_ref, p_ref, wg_ref, sc_ref, dp_ref, dwg_ref, dsc_ref):
        i = pl.program_id(0)

        @pl.when(i == 0)
        def _():
            dwg_ref[...] = jnp.zeros_like(dwg_ref)
            dsc_ref[...] = jnp.zeros_like(dsc_ref)

        for g in range(4):
            gs = slice(g * PG, (g + 1) * PG)
            dmg, pg, wg = dm_ref[:, gs], p_ref[:, gs], wg_ref[g]
            dsc_ref[:, gs] += jnp.sum(dmg * _dot(pg, wg), axis=0, keepdims=True)
            dy = (dmg * sc_ref[:, gs]).astype(BF16)
            dp_ref[:, gs] = _dot_nt(dy, wg)
            dwg_ref[g] += _dot_tn(pg, dy)

    return _pcall(
        body, name="pool_bwd_group", grid=(S // tm,),
        in_specs=[pl.BlockSpec((tm, D), lambda i: (i, 0)), pl.BlockSpec((tm, D), lambda i: (i, 0)),
                  pl.BlockSpec((4, PG, PG), lambda i: (0, 0, 0)), pl.BlockSpec((1, D), lambda i: (0, 0))],
        out_specs=[pl.BlockSpec((tm, D), lambda i: (i, 0)), pl.BlockSpec((4, PG, PG), lambda i: (0, 0, 0)),
                   pl.BlockSpec((1, D), lambda i: (0, 0))],
        out_shape=[jax.ShapeDtypeStruct((S, D), F32), jax.ShapeDtypeStruct((4, PG, PG), F32),
                   jax.ShapeDtypeStruct((1, D), F32)],
        compiler_params=_cp("arbitrary"))(dm, p, wgrp, scale)


def pool_bwd_window(dp):
    S = dp.shape[0]
    T, HB = min(POOL_T, S), POOL_HALO
    per = T // HB
    nt = S // T

    def body(dp_ref, nxt_ref, du_ref):
        i = pl.program_id(0)
        dc = dp_ref[...]
        nxt = jnp.where(i < nt - 1, nxt_ref[...], 0.0)
        d_cur = _iota((T, T), 1) - _iota((T, T), 0)
        d_nxt = _iota((T, HB), 1) - _iota((T, HB), 0) + T
        tg = i * T + _iota((T, 1), 0)
        tn_ = (i + 1) * T + _iota((HB, 1), 0)
        for g, w in enumerate(POOL_WINDOWS):
            gs = slice(g * PG, (g + 1) * PG)
            band = ((d_cur >= 0) & (d_cur < w)).astype(F32)
            band_n = ((d_nxt >= 0) & (d_nxt < w)).astype(F32)
            dcg = dc[:, gs]
            cur = dcg / jnp.minimum(tg + 1, w).astype(F32)
            nx = nxt[:, gs] / jnp.minimum(tn_ + 1, w).astype(F32)
            du_ref[:, gs] = (_dot(band, cur, HI) + _dot(band_n, nx, HI) - dcg).astype(BF16)

    return _pcall(
        body, name="pool_bwd_window", grid=(nt,),
        in_specs=[pl.BlockSpec((T, D), lambda i: (i, 0)),
                  pl.BlockSpec((HB, D), lambda i: (jnp.minimum((i + 1) * per, S // HB - 1), 0))],
        out_specs=pl.BlockSpec((T, D), lambda i: (i, 0)),
        out_shape=jax.ShapeDtypeStruct((S, D), BF16),
        compiler_params=_cp("parallel"))(dp, dp)


CONV_T = 256


def _shift_down(xc, prev8, j):
    if j == 0:
        return xc
    T = xc.shape[0]
    body = pltpu.roll(xc, j, 0)
    first = jnp.where(_iota((8, 1), 0) < j, pltpu.roll(prev8, j, 0), body[0:8])
    return jnp.concatenate([first, body[8:T]], axis=0)


def _shift_up(dc, next8, j):
    if j == 0:
        return dc
    T = dc.shape[0]
    body = pltpu.roll(dc, T - j, 0)
    last = jnp.where(_iota((8, 1), 0) + j < 8, body[T - 8:T], pltpu.roll(next8, 8 - j, 0))
    return jnp.concatenate([body[0:T - 8], last], axis=0)


def conv_fwd(xbc, conv_w, conv_b):
    S = xbc.shape[0]
    T = min(CONV_T, S)
    CB = 1024

    def body(x_ref, prev_ref, w_ref, b_ref, o_ref):
        i = pl.program_id(1)
        xc = x_ref[...]
        prev8 = jnp.where(i > 0, prev_ref[...], 0.0)
        pre = b_ref[...] + w_ref[3:4, :] * xc
        for j in range(1, 4):
            pre = pre + w_ref[3 - j:4 - j, :] * _shift_down(xc, prev8, j)
        o_ref[...] = pre * _sigmoid(pre)

    return _pcall(
        body, name="conv_fwd", grid=(CONV_CH // CB, S // T),
        in_specs=[pl.BlockSpec((T, CB), lambda c, i: (i, c)),
                  pl.BlockSpec((8, CB), lambda c, i: (jnp.maximum(i * (T // 8) - 1, 0), c)),
                  pl.BlockSpec((4, CB), lambda c, i: (0, c)), pl.BlockSpec((1, CB), lambda c, i: (0, c))],
        out_specs=pl.BlockSpec((T, CB), lambda c, i: (i, c)),
        out_shape=jax.ShapeDtypeStruct((S, CONV_CH), F32),
        compiler_params=_cp("parallel", "parallel"))(xbc, xbc, conv_w, conv_b)


def conv_bwd_pre(dact, xbc, conv_w, conv_b):
    S = xbc.shape[0]
    T = min(CONV_T, S)
    CB = 1024

    def body(da_ref, x_ref, prev_ref, w_ref, b_ref, dpre_ref, dw_ref, db_ref):
        i = pl.program_id(1)
        xc = x_ref[...]
        prev8 = jnp.where(i > 0, prev_ref[...], 0.0)
        sh = [_shift_down(xc, prev8, j) for j in range(4)]
        pre = b_ref[...] + w_ref[3:4, :] * sh[0]
        for j in range(1, 4):
            pre = pre + w_ref[3 - j:4 - j, :] * sh[j]
        s = _sigmoid(pre)
        dpre = da_ref[...] * (s * (1.0 + pre * (1.0 - s)))
        dpre_ref[...] = dpre
        rows = [jnp.sum(dpre * sh[3 - k], axis=0, keepdims=True) for k in range(4)]
        dw = jnp.concatenate(rows + [jnp.zeros((4, CB), F32)], axis=0)
        db = jnp.sum(dpre, axis=0, keepdims=True)

        @pl.when(i == 0)
        def _():
            dw_ref[...] = dw
            db_ref[...] = db

        @pl.when(i > 0)
        def _():
            dw_ref[...] += dw
            db_ref[...] += db

    return _pcall(
        body, name="conv_bwd_pre", grid=(CONV_CH // CB, S // T),
        in_specs=[pl.BlockSpec((T, CB), lambda c, i: (i, c)), pl.BlockSpec((T, CB), lambda c, i: (i, c)),
                  pl.BlockSpec((8, CB), lambda c, i: (jnp.maximum(i * (T // 8) - 1, 0), c)),
                  pl.BlockSpec((4, CB), lambda c, i: (0, c)), pl.BlockSpec((1, CB), lambda c, i: (0, c))],
        out_specs=[pl.BlockSpec((T, CB), lambda c, i: (i, c)), pl.BlockSpec((8, CB), lambda c, i: (0, c)),
                   pl.BlockSpec((1, CB), lambda c, i: (0, c))],
        out_shape=[jax.ShapeDtypeStruct((S, CONV_CH), F32), jax.ShapeDtypeStruct((8, CONV_CH), F32),
                   jax.ShapeDtypeStruct((1, CONV_CH), F32)],
        compiler_params=_cp("parallel", "arbitrary"))(dact, xbc, xbc, conv_w, conv_b)


def conv_bwd_input(dpre, conv_w):
    S = dpre.shape[0]
    T = min(CONV_T, S)
    CB = 1024
    nt = S // T

    def body(d_ref, nxt_ref, w_ref, o_ref):
        i = pl.program_id(1)
        dc = d_ref[...]
        next8 = jnp.where(i < nt - 1, nxt_ref[...], 0.0)
        acc = w_ref[3:4, :] * dc
        for j in range(1, 4):
            acc = acc + w_ref[3 - j:4 - j, :] * _shift_up(dc, next8, j)
        o_ref[...] = acc.astype(BF16)

    return _pcall(
        body, name="conv_bwd_input", grid=(CONV_CH // CB, nt),
        in_specs=[pl.BlockSpec((T, CB), lambda c, i: (i, c)),
                  pl.BlockSpec((8, CB), lambda c, i: (jnp.minimum((i + 1) * (T // 8), S // 8 - 1), c)),
                  pl.BlockSpec((4, CB), lambda c, i: (0, c))],
        out_specs=pl.BlockSpec((T, CB), lambda c, i: (i, c)),
        out_shape=jax.ShapeDtypeStruct((S, CONV_CH), BF16),
        compiler_params=_cp("parallel", "parallel"))(dpre, dpre, conv_w)


def _ssd_chunk_terms(dt_ref, bias_ref, alog_ref):
    L = CH
    dtp = dt_ref[...] + bias_ref[...]
    dt = jnp.maximum(dtp, 0.0) + jnp.log(1.0 + jnp.exp(-jnp.abs(dtp)))
    a = -jnp.exp(alog_ref[...])
    da = dt * a
    tri = (_iota((L, L), 0) >= _iota((L, L), 1)).astype(F32)
    acum = _dot(tri, da, HI)
    triu = (_iota((L, L), 0) <= _iota((L, L), 1)).astype(F32)
    acum_row = _dot_tn(da, triu, HI)
    expand = (_iota((NH, DI), 1) // HP == _iota((NH, DI), 0)).astype(F32)
    return dtp, dt, a, da, acum, acum_row, expand


def ssd_scan_fwd(xbc_act, dt_raw, dt_bias, a_log, d_full):
    S = xbc_act.shape[0]
    L = CH
    nc = S // L

    def body(xs_ref, b_ref, c_ref, dt_ref, bias_ref, alog_ref, d_ref, y_ref, st_ref, state):
        c = pl.program_id(0)

        @pl.when(c == 0)
        def _():
            state[...] = jnp.zeros_like(state)

        st_ref[0] = state[...]
        _, dt, _, _, acum, acum_row, expand = _ssd_chunk_terms(dt_ref, bias_ref, alog_ref)
        e_full = _dot(jnp.exp(acum), expand, HI)
        w_full = _dot(jnp.exp(acum[L - 1:L, :] - acum), expand, HI)
        dt_full = _dot(dt, expand, HI)
        causal = _iota((L, L), 0) >= _iota((L, L), 1)
        lane_head = _iota((1, GW), 1) // HP
        for g in range(NG):
            gs = slice(g * GW, (g + 1) * GW)
            ns = slice(g * NS, (g + 1) * NS)
            xs_g = xs_ref[:, gs]
            xdt_g = xs_g * dt_full[:, gs]
            cg = c_ref[:, ns].astype(BF16)
            bg = b_ref[:, ns].astype(BF16)
            gmat = _dot_nt(cg, bg)
            yg = jnp.zeros((L, GW), F32)
            for hh in range(4):
                h = 4 * g + hh
                diff = acum[:, h:h + 1] - acum_row[h:h + 1, :]
                dk = jnp.exp(jnp.where(causal, diff, -1e30))
                xm = jnp.where(lane_head == hh, xdt_g, 0.0).astype(BF16)
                yg = yg + _dot((gmat * dk).astype(BF16), xm)
            sg = state[g]
            yoff = _dot(cg, sg.astype(BF16)) * e_full[:, gs]
            y_ref[:, gs] = yg + yoff + d_ref[:, gs] * xs_g
            state[g] = sg * e_full[L - 1:L, gs] + _dot_tn(bg, (w_full[:, gs] * xdt_g).astype(BF16))

    return _pcall(
        body, name="ssd_scan_fwd", grid=(nc,),
        in_specs=[pl.BlockSpec((L, DI), lambda c: (c, 0)), pl.BlockSpec((L, 1024), lambda c: (c, 2)),
                  pl.BlockSpec((L, 1024), lambda c: (c, 3)), pl.BlockSpec((L, NH), lambda c: (c, 0)),
                  pl.BlockSpec((1, NH), lambda c: (0, 0)), pl.BlockSpec((1, NH), lambda c: (0, 0)),
                  pl.BlockSpec((1, DI), lambda c: (0, 0))],
        out_specs=[pl.BlockSpec((L, DI), lambda c: (c, 0)), pl.BlockSpec((1, NG, NS, GW), lambda c: (c, 0, 0, 0))],
        out_shape=[jax.ShapeDtypeStruct((S, DI), F32), jax.ShapeDtypeStruct((nc, NG, NS, GW), F32)],
        scratch_shapes=[pltpu.VMEM((NG, NS, GW), F32)],
        compiler_params=_cp("arbitrary"))(xbc_act, xbc_act, xbc_act, dt_raw, dt_bias, a_log, d_full)


def ssd_scan_bwd(dy, xbc_act, dt_raw, dt_bias, a_log, d_full, states):
    S = xbc_act.shape[0]
    L = CH
    nc = S // L

    def body(dy_ref, xs_ref, b_ref, c_ref, dt_ref, bias_ref, alog_ref, d_ref, st_ref,
             dxbc_ref, ddt_ref, dbias_ref, dalog_ref, dd_ref, dstate):
        c = pl.program_id(0)

        @pl.when(c == 0)
        def _():
            dstate[...] = jnp.zeros_like(dstate)
            dbias_ref[...] = jnp.zeros_like(dbias_ref)
            dalog_ref[...] = jnp.zeros_like(dalog_ref)
            dd_ref[...] = jnp.zeros_like(dd_ref)

        dtp, dt, a, _, acum, acum_row, expand = _ssd_chunk_terms(dt_ref, bias_ref, alog_ref)
        e_full = _dot(jnp.exp(acum), expand, HI)
        w_h = jnp.exp(acum[L - 1:L, :] - acum)
        w_full = _dot(w_h, expand, HI)
        dt_full = _dot(dt, expand, HI)
        causal = _iota((L, L), 0) >= _iota((L, L), 1)
        lane_head = _iota((1, GW), 1) // HP
        ones_l = jnp.ones((L, LANES), F32)
        head_id = _iota((1, NH), 1)
        dacum = jnp.zeros((L, NH), F32)
        red_parts = []
        dxdt_parts = []
        alast_parts = []
        for g in range(NG):
            gs = slice(g * GW, (g + 1) * GW)
            ns = slice(g * NS, (g + 1) * NS)
            xs_g = xs_ref[:, gs]
            xdt_g = xs_g * dt_full[:, gs]
            dy_g = dy_ref[:, gs]
            cg = c_ref[:, ns].astype(BF16)
            bg = b_ref[:, ns].astype(BF16)
            gmat = _dot_nt(cg, bg)
            sg = st_ref[0, g]
            dsg = dstate[g]
            sgb, dsgb = sg.astype(BF16), dsg.astype(BF16)
            cs = _dot(cg, sgb)
            bds = _dot(bg, dsgb)
            e_g, w_g = e_full[:, gs], w_full[:, gs]
            dxdt = w_g * bds
            dgsum = jnp.zeros((L, L), F32)
            for hh in range(4):
                h = 4 * g + hh
                hm = lane_head == hh
                diff = acum[:, h:h + 1] - acum_row[h:h + 1, :]
                dk = jnp.exp(jnp.where(causal, diff, -1e30))
                m = gmat * dk
                dym = jnp.where(hm, dy_g, 0.0).astype(BF16)
                xm = jnp.where(hm, xdt_g, 0.0).astype(BF16)
                dm = _dot_nt(dym, xm)
                dxdt = dxdt + _dot_tn(m.astype(BF16), dym)
                dgsum = dgsum + dm * dk
                em = dm * m
                rs = _dot(em, ones_l, HI)[:, 0:1]
                cs_ = _dot_tn(em, ones_l, HI)[:, 0:1]
                dacum = dacum + (rs - cs_) * (head_id == h).astype(F32)
            dgb = dgsum.astype(BF16)
            edy = (e_g * dy_g).astype(BF16)
            wx = (w_g * xdt_g).astype(BF16)
            dc_g = _dot(dgb, bg) + _dot_nt(edy, sgb)
            db_g = _dot_tn(dgb, cg) + _dot_nt(wx, dsgb)
            dxbc_ref[:, DI + g * NS:DI + (g + 1) * NS] = db_g
            dxbc_ref[:, DI + 1024 + g * NS:DI + 1024 + (g + 1) * NS] = dc_g
            p2w = bds * xdt_g * w_g
            red_parts.append(dy_g * cs * e_g - p2w)
            alast_parts.append(jnp.sum(p2w, axis=0, keepdims=True)
                               + e_full[L - 1:L, gs] * jnp.sum(dsg * sg, axis=0, keepdims=True))
            dxdt_parts.append(dxdt)
            dstate[g] = e_full[L - 1:L, gs] * dsg + _dot_tn(cg, edy)
            dxbc_ref[:, gs] = dxdt * dt_full[:, gs] + dy_g * d_ref[:, gs]
            dd_ref[:, gs] += jnp.sum(dy_g * xs_g, axis=0, keepdims=True)
        red = jnp.concatenate(red_parts, axis=1)
        dxdt_all = jnp.concatenate(dxdt_parts, axis=1)
        alast = jnp.concatenate(alast_parts, axis=1)
        dacum = dacum + _dot_nt(red, expand, HI)
        dalast = _dot_nt(jnp.broadcast_to(alast, (8, DI)), expand, HI)[0:1, :]
        dacum = dacum + jnp.where(_iota((L, 1), 0) == L - 1, dalast, 0.0)
        triu = (_iota((L, L), 0) <= _iota((L, L), 1)).astype(F32)
        dda = _dot(triu, dacum, HI)
        ddt = _dot_nt(dxdt_all * xs_ref[...], expand, HI) + dda * a
        dalog_ref[...] += jnp.sum(dda * dt, axis=0, keepdims=True) * a
        ddt_raw = ddt * _sigmoid(dtp)
        ddt_ref[...] = ddt_raw
        dbias_ref[...] += jnp.sum(ddt_raw, axis=0, keepdims=True)

    rev = lambda c: (nc - 1 - c, 0)
    return _pcall(
        body, name="ssd_scan_bwd", grid=(nc,),
        in_specs=[pl.BlockSpec((L, DI), rev), pl.BlockSpec((L, DI), rev),
                  pl.BlockSpec((L, 1024), lambda c: (nc - 1 - c, 2)), pl.BlockSpec((L, 1024), lambda c: (nc - 1 - c, 3)),
                  pl.BlockSpec((L, NH), rev), pl.BlockSpec((1, NH), lambda c: (0, 0)),
                  pl.BlockSpec((1, NH), lambda c: (0, 0)), pl.BlockSpec((1, DI), lambda c: (0, 0)),
                  pl.BlockSpec((1, NG, NS, GW), lambda c: (nc - 1 - c, 0, 0, 0))],
        out_specs=[pl.BlockSpec((L, CONV_CH), rev), pl.BlockSpec((L, NH), rev),
                   pl.BlockSpec((1, NH), lambda c: (0, 0)), pl.BlockSpec((1, NH), lambda c: (0, 0)),
                   pl.BlockSpec((1, DI), lambda c: (0, 0))],
        out_shape=[jax.ShapeDtypeStruct((S, CONV_CH), F32), jax.ShapeDtypeStruct((S, NH), F32),
                   jax.ShapeDtypeStruct((1, NH), F32), jax.ShapeDtypeStruct((1, NH), F32),
                   jax.ShapeDtypeStruct((1, DI), F32)],
        scratch_shapes=[pltpu.VMEM((NG, NS, GW), F32)],
        compiler_params=_cp("arbitrary"))(dy, xbc_act, xbc_act, xbc_act, dt_raw, dt_bias, a_log, d_full, states)


def gate_norm_fwd(y, z, out_norm, tm=256):
    S = y.shape[0]
    tm = min(tm, S)

    def body(y_ref, z_ref, on_ref, o_ref):
        zv = z_ref[...]
        gin = y_ref[...] * (zv * _sigmoid(zv))
        for g in range(NG):
            gs = slice(g * GW, (g + 1) * GW)
            blk = gin[:, gs]
            r = lax.rsqrt(jnp.mean(blk * blk, axis=-1, keepdims=True) + EPS)
            o_ref[:, gs] = (blk * r * on_ref[:, gs]).astype(BF16)

    return _pcall(
        body, name="gate_norm_fwd", grid=(S // tm,),
        in_specs=[pl.BlockSpec((tm, DI), lambda i: (i, 0)), pl.BlockSpec((tm, DI), lambda i: (i, 0)),
                  pl.BlockSpec((1, DI), lambda i: (0, 0))],
        out_specs=pl.BlockSpec((tm, DI), lambda i: (i, 0)),
        out_shape=jax.ShapeDtypeStruct((S, DI), BF16),
        compiler_params=_cp("parallel"))(y, z, out_norm)


def gate_norm_bwd(dgn, y, z, out_norm, tm=256):
    S = y.shape[0]
    tm = min(tm, S)

    def body(dg_ref, y_ref, z_ref, on_ref, dy_ref, dz_ref, don_ref):
        i = pl.program_id(0)

        @pl.when(i == 0)
        def _():
            don_ref[...] = jnp.zeros_like(don_ref)

        zv, yv = z_ref[...], y_ref[...]
        s = _sigmoid(zv)
        sz = zv * s
        gin = yv * sz
        for g in range(NG):
            gs = slice(g * GW, (g + 1) * GW)
            blk = gin[:, gs]
            r = lax.rsqrt(jnp.mean(blk * blk, axis=-1, keepdims=True) + EPS)
            n = blk * r
            dg = dg_ref[:, gs]
            don_ref[:, gs] += jnp.sum(dg * n, axis=0, keepdims=True)
            dn = dg * on_ref[:, gs]
            dgin = r * (dn - n * jnp.mean(dn * n, axis=-1, keepdims=True))
            dy_ref[:, gs] = dgin * sz[:, gs]
            dz_ref[:, gs] = (dgin * yv[:, gs] * (s[:, gs] * (1.0 + zv[:, gs] * (1.0 - s[:, gs])))).astype(BF16)

    return _pcall(
        body, name="gate_norm_bwd", grid=(S // tm,),
        in_specs=[pl.BlockSpec((tm, DI), lambda i: (i, 0))] * 3 + [pl.BlockSpec((1, DI), lambda i: (0, 0))],
        out_specs=[pl.BlockSpec((tm, DI), lambda i: (i, 0)), pl.BlockSpec((tm, DI), lambda i: (i, 0)),
                   pl.BlockSpec((1, DI), lambda i: (0, 0))],
        out_shape=[jax.ShapeDtypeStruct((S, DI), F32), jax.ShapeDtypeStruct((S, DI), BF16),
                   jax.ShapeDtypeStruct((1, DI), F32)],
        compiler_params=_cp("arbitrary"))(dgn, y, z, out_norm)


SB_T = 256
SB_QSCALE = 0.125
SB_DEAD = -110.0
SB_UNSEEN = -1e30


def _head_norm(xv, lo):
    sq = xv * xv
    s0 = jnp.sum(jnp.where(lo, sq, 0.0), axis=-1, keepdims=True)
    s1 = jnp.sum(jnp.where(lo, 0.0, sq), axis=-1, keepdims=True)
    return jnp.where(lo, lax.rsqrt(s0 / SBD + EPS), lax.rsqrt(s1 / SBD + EPS))


def sb_prep_fwd(qkv, qg, kg, tm=256):
    S = qkv.shape[0]
    tm = min(tm, S)

    def body(x_ref, qg_ref, kg_ref, q_ref, k_ref, v_ref):
        lo = _iota((1, LANES), 1) < SBD
        for sl in range(D // LANES):
            cs = slice(sl * LANES, (sl + 1) * LANES)
            xq = x_ref[:, cs]
            q_ref[:, cs] = ((xq * _head_norm(xq, lo) * qg_ref[...]).astype(BF16).astype(F32) * SB_QSCALE).astype(BF16)
            xk = x_ref[:, D + sl * LANES:D + (sl + 1) * LANES]
            k_ref[:, cs] = (xk * _head_norm(xk, lo) * kg_ref[...]).astype(BF16)
        v_ref[...] = x_ref[:, 2 * D:3 * D].astype(BF16)

    return _pcall(
        body, name="sb_prep_fwd", grid=(S // tm,),
        in_specs=[pl.BlockSpec((tm, 3 * D), lambda i: (i, 0)), pl.BlockSpec((1, LANES), lambda i: (0, 0)),
                  pl.BlockSpec((1, LANES), lambda i: (0, 0))],
        out_specs=[pl.BlockSpec((tm, D), lambda i: (i, 0))] * 3,
        out_shape=[jax.ShapeDtypeStruct((S, D), BF16)] * 3,
        compiler_params=_cp("parallel"))(qkv, qg, kg)


def sb_prep_bwd(dqs, dkn, dv, qkv, qg, kg, tm=256):
    S = qkv.shape[0]
    tm = min(tm, S)

    def body(dq_ref, dk_ref, dv_ref, x_ref, qg_ref, kg_ref, dx_ref, dqg_ref, dkg_ref):
        i = pl.program_id(0)

        @pl.when(i == 0)
        def _():
            dqg_ref[...] = jnp.zeros_like(dqg_ref)
            dkg_ref[...] = jnp.zeros_like(dkg_ref)

        lo = _iota((1, LANES), 1) < SBD

        def one(xv, dh, gain):
            r = _head_norm(xv, lo)
            y = xv * r
            dy = dh * gain
            t = dy * y
            m0 = jnp.sum(jnp.where(lo, t, 0.0), axis=-1, keepdims=True)
            m1 = jnp.sum(jnp.where(lo, 0.0, t), axis=-1, keepdims=True)
            dx = r * (dy - y * (jnp.where(lo, m0, m1) / SBD))
            return dx, jnp.sum(dh * y, axis=0, keepdims=True)

        for sl in range(D // LANES):
            cs = slice(sl * LANES, (sl + 1) * LANES)
            dx, dg = one(x_ref[:, cs], dq_ref[:, cs] * SB_QSCALE, qg_ref[...])
            dx_ref[:, cs] = dx.astype(BF16)
            dqg_ref[:, cs] += dg
            ks = slice(D + sl * LANES, D + (sl + 1) * LANES)
            dx, dg = one(x_ref[:, ks], dk_ref[:, cs], kg_ref[...])
            dx_ref[:, ks] = dx.astype(BF16)
            dkg_ref[:, cs] += dg
        dx_ref[:, 2 * D:3 * D] = dv_ref[...].astype(BF16)

    return _pcall(
        body, name="sb_prep_bwd", grid=(S // tm,),
        in_specs=[pl.BlockSpec((tm, D), lambda i: (i, 0))] * 3
        + [pl.BlockSpec((tm, 3 * D), lambda i: (i, 0)), pl.BlockSpec((1, LANES), lambda i: (0, 0)),
           pl.BlockSpec((1, LANES), lambda i: (0, 0))],
        out_specs=[pl.BlockSpec((tm, 3 * D), lambda i: (i, 0)), pl.BlockSpec((1, D), lambda i: (0, 0)),
                   pl.BlockSpec((1, D), lambda i: (0, 0))],
        out_shape=[jax.ShapeDtypeStruct((S, 3 * D), BF16), jax.ShapeDtypeStruct((1, D), F32),
                   jax.ShapeDtypeStruct((1, D), F32)],
        compiler_params=_cp("arbitrary"))(dqs, dkn, dv, qkv, qg, kg)


def _split_dot(x, u):
    hi = x.astype(BF16)
    lo = (x - hi.astype(F32)).astype(BF16)
    return _dot(hi, u) + _dot(lo, u)


def _sb_logits(qh, kb, valid):
    z = _dot_nt(qh, kb)
    e = jnp.exp(-jnp.abs(z))
    lp = jnp.log(1.0 + e)
    lb = jnp.minimum(z, 0.0) - lp
    l1m = jnp.where(valid, lb - z, 0.0)
    return z, e, lb, l1m


def sb_fwd(qs, kn, v):
    S = qs.shape[0]
    T = min(SB_T, S)
    nq = S // T

    def body(q_ref, k_ref, v_ref, o_ref, r_ref, oacc, rrun):
        i = pl.program_id(1)
        qb = q_ref[...]
        lo = _iota((1, LANES), 1) < SBD
        row, col = _iota((T, T), 0), _iota((T, T), 1)
        u = (row > col).astype(BF16)
        lane_blk = _iota((T, LANES), 1)
        oacc[...] = jnp.zeros_like(oacc)
        for hh in range(2):
            hm = lo if hh == 0 else jnp.logical_not(lo)
            qh = jnp.where(hm, qb, jnp.zeros_like(qb))
            rrun[...] = jnp.zeros_like(rrun)
            r_ref[hh] = jnp.full((T, LANES), SB_UNSEEN, F32)

            def live(carry):
                s, rmax = carry
                return jnp.logical_and(s <= i, rmax > SB_DEAD)

            def step(carry, hm=hm, qh=qh, hh=hh):
                s, _ = carry
                j = i - s
                off = pl.multiple_of(j * T, T)
                kb = k_ref[pl.ds(off, T), :]
                vb = v_ref[pl.ds(off, T), :]
                vb = jnp.where(hm, vb, jnp.zeros_like(vb))
                valid = (j * T + col) < (i * T + row)
                _, _, lb, l1m = _sb_logits(qh, kb, valid)
                r = rrun[...]
                aft = _split_dot(l1m, u) + r
                a = jnp.where(valid, jnp.exp(lb + aft), 0.0)
                oacc[...] += _dot(a.astype(BF16), vb)
                r_ref[hh] = jnp.where(lane_blk == j, r, r_ref[hh])
                rnew = r + jnp.sum(l1m, axis=-1, keepdims=True)
                rrun[...] = rnew
                return s + 1, jnp.max(rnew)

            lax.while_loop(live, step, (jnp.int32(0), jnp.float32(0.0)))
        o_ref[...] = oacc[...].astype(BF16)

    return _pcall(
        body, name="sb_fwd", grid=(D // LANES, nq),
        in_specs=[pl.BlockSpec((T, LANES), lambda h, i: (i, h)), pl.BlockSpec((S, LANES), lambda h, i: (0, h)),
                  pl.BlockSpec((S, LANES), lambda h, i: (0, h))],
        out_specs=[pl.BlockSpec((T, LANES), lambda h, i: (i, h)), pl.BlockSpec((2, T, LANES), lambda h, i: (h, i, 0))],
        out_shape=[jax.ShapeDtypeStruct((S, D), BF16), jax.ShapeDtypeStruct((SBH, S, LANES), F32)],
        scratch_shapes=[pltpu.VMEM((T, LANES), F32), pltpu.VMEM((T, 1), F32)],
        compiler_params=_cp("parallel", "arbitrary"))(qs, kn, v)


def sb_bwd(qs, kn, v, do, rsave):
    S = qs.shape[0]
    T = min(SB_T, S)
    nq = S // T

    def body(q_ref, k_ref, v_ref, do_ref, r_ref, dq_ref, dk_ref, dv_ref, crun):
        i = pl.program_id(1)

        @pl.when(i == 0)
        def _():
            dk_ref[...] = jnp.zeros_like(dk_ref)
            dv_ref[...] = jnp.zeros_like(dv_ref)

        qb, dob = q_ref[...], do_ref[...]
        lo = _iota((1, LANES), 1) < SBD
        row, col = _iota((T, T), 0), _iota((T, T), 1)
        u = (row > col).astype(BF16)
        u2 = (row < col).astype(BF16)
        lane_blk = _iota((T, LANES), 1)
        dq_ref[...] = jnp.zeros_like(dq_ref)
        for hh in range(2):
            hm = lo if hh == 0 else jnp.logical_not(lo)
            qh = jnp.where(hm, qb, jnp.zeros_like(qb))
            doh = jnp.where(hm, dob, jnp.zeros_like(dob))
            crun[...] = jnp.zeros_like(crun)

            def step(j, carry, hm=hm, qh=qh, doh=doh, hh=hh):
                off = pl.multiple_of(j * T, T)
                kb = k_ref[pl.ds(off, T), :]
                vb = v_ref[pl.ds(off, T), :]
                vb = jnp.where(hm, vb, jnp.zeros_like(vb))
                valid = (j * T + col) < (i * T + row)
                z, e, lb, l1m = _sb_logits(qh, kb, valid)
                r = jnp.sum(jnp.where(lane_blk == j, r_ref[hh], 0.0), axis=-1, keepdims=True)
                aft = _split_dot(l1m, u) + r
                a = jnp.where(valid, jnp.exp(lb + aft), 0.0)
                w = a * _dot_nt(doh, vb)
                cprev = crun[...]
                cw = _split_dot(w, u2) + cprev
                inv = 1.0 / (1.0 + e)
                pos = z >= 0.0
                beta = jnp.where(pos, 1.0, e) * inv
                onem = jnp.where(pos, e, 1.0) * inv
                dz = jnp.where(valid, w * onem - beta * cw, 0.0).astype(BF16)
                dq_ref[...] += _dot(dz, jnp.where(hm, kb, jnp.zeros_like(kb)))
                dk_ref[pl.ds(off, T), :] += _dot_tn(dz, qh)
                dv_ref[pl.ds(off, T), :] += _dot_tn(a.astype(BF16), doh)
                crun[...] = cprev + jnp.sum(w, axis=-1, keepdims=True)
                return carry

            col_max = jnp.max(r_ref[hh], axis=0, keepdims=True)
            seen = jnp.logical_and(col_max > SB_DEAD, _iota((1, LANES), 1) <= i)
            n_live = jnp.sum(seen.astype(jnp.int32))
            lax.fori_loop(i + 1 - n_live, i + 1, step, 0)

    return _pcall(
        body, name="sb_bwd", grid=(D // LANES, nq),
        in_specs=[pl.BlockSpec((T, LANES), lambda h, i: (i, h)), pl.BlockSpec((S, LANES), lambda h, i: (0, h)),
                  pl.BlockSpec((S, LANES), lambda h, i: (0, h)), pl.BlockSpec((T, LANES), lambda h, i: (i, h)),
                  pl.BlockSpec((2, T, LANES), lambda h, i: (h, i, 0))],
        out_specs=[pl.BlockSpec((T, LANES), lambda h, i: (i, h)), pl.BlockSpec((S, LANES), lambda h, i: (0, h)),
                   pl.BlockSpec((S, LANES), lambda h, i: (0, h))],
        out_shape=[jax.ShapeDtypeStruct((S, D), F32)] * 3,
        scratch_shapes=[pltpu.VMEM((T, 1), F32)],
        compiler_params=_cp("parallel", "arbitrary"))(qs, kn, v, do, rsave)


def loss_head(y, target, tm=512):
    S = y.shape[0]
    tm = min(tm, S)

    def body(y_ref, t_ref, ls_ref, dy_ref):
        i = pl.program_id(0)
        err = y_ref[...] - t_ref[...]
        dy_ref[...] = err * (1.0 / D)
        part = jnp.sum(err * err, axis=0, keepdims=True)

        @pl.when(i == 0)
        def _():
            ls_ref[...] = part

        @pl.when(i > 0)
        def _():
            ls_ref[...] += part

    return _pcall(
        body, name="loss_head", grid=(S // tm,),
        in_specs=[pl.BlockSpec((tm, D), lambda i: (i, 0))] * 2,
        out_specs=[pl.BlockSpec((1, D), lambda i: (0, 0)), pl.BlockSpec((tm, D), lambda i: (i, 0))],
        out_shape=[jax.ShapeDtypeStruct((1, D), F32), jax.ShapeDtypeStruct((S, D), F32)],
        compiler_params=_cp("arbitrary"))(y, target)


def adamw(w, gs, m, v, tr=1024, name="adamw"):
    R = w.shape[0]
    tr = min(tr, R)
    n_g = len(gs)
    c1 = 1.0 / (1.0 - ADAM_B1 ** ADAM_STEP)
    c2 = 1.0 / (1.0 - ADAM_B2 ** ADAM_STEP)

    def body(*refs):
        w_ref, g_refs = refs[0], refs[1:1 + n_g]
        m_ref, v_ref, go_ref, d_ref, mo_ref, vo_ref = refs[1 + n_g:]
        g = g_refs[0][...]
        for r in g_refs[1:]:
            g = g + r[...]
        mn = ADAM_B1 * m_ref[...] + (1.0 - ADAM_B1) * g
        vn = ADAM_B2 * v_ref[...] + (1.0 - ADAM_B2) * (g * g)
        go_ref[...] = g
        mo_ref[...] = mn
        vo_ref[...] = vn
        d_ref[...] = -ADAM_LR * ((mn * c1) / (jnp.sqrt(vn * c2) + ADAM_EPS) + ADAM_WD * w_ref[...])

    spec = pl.BlockSpec((tr, LANES), lambda i: (i, 0))
    return _pcall(
        body, name=name, grid=(R // tr,), in_specs=[spec] * (3 + n_g), out_specs=[spec] * 4,
        out_shape=[jax.ShapeDtypeStruct((R, LANES), F32)] * 4,
        compiler_params=_cp("parallel"))(w, *gs, m, v)


def pair_sum(gstacks, halves, tr=1024):
    c = lax.axis_index("c")
    me = 2 * lax.axis_index("x") + lax.axis_index("y")
    where = jnp.stack([c, me]).astype(jnp.int32)
    outs = []
    for g, xh in zip(gstacks, halves):
        _, H, C = xh.shape
        t = min(tr, H)
        n_i = H // t

        def body(s_ref, g_ref, x_ref, qb_ref, own_ref):
            j = pl.program_id(1)
            q = g_ref[0] + x_ref[0]
            qb_ref[0] = q.astype(BF16)

            @pl.when(j == s_ref[1])
            def _():
                own_ref[...] = q

        outs.append(_pcall(
            body, name="pair_sum",
            grid_spec=pltpu.PrefetchScalarGridSpec(
                num_scalar_prefetch=1, grid=(n_i, N_CHIPS),
                in_specs=[pl.BlockSpec((1, t, C), lambda i, j, s, n_i=n_i: (j, s[0] * n_i + i, 0)),
                          pl.BlockSpec((1, t, C), lambda i, j, s: (j, i, 0))],
                out_specs=[pl.BlockSpec((1, t, C), lambda i, j, s: (j, i, 0)),
                           pl.BlockSpec((t, C), lambda i, j, s: (i, 0))]),
            out_shape=[jax.ShapeDtypeStruct((N_CHIPS, H, C), BF16), jax.ShapeDtypeStruct((H, C), F32)],
            compiler_params=_cp("parallel", "arbitrary"))(where, g, xh))
    return [o[0] for o in outs], [o[1] for o in outs]


def chip_sum(owns, recvs, tr=1024):
    outs = []
    for own, rc in zip(owns, recvs):
        H, C = own.shape
        t = min(tr, H)

        def body(o_ref, r_ref, t_ref):
            t_ref[...] = ((o_ref[...] + r_ref[0].astype(F32)) + r_ref[1].astype(F32)) + r_ref[2].astype(F32)

        outs.append(_pcall(
            body, name="chip_sum", grid=(H // t,),
            in_specs=[pl.BlockSpec((t, C), lambda i: (i, 0)), pl.BlockSpec((3, t, C), lambda i: (0, i, 0))],
            out_specs=pl.BlockSpec((t, C), lambda i: (i, 0)),
            out_shape=jax.ShapeDtypeStruct((H, C), F32), compiler_params=_cp("parallel"))(own, rc))
    return outs


MESH = pl.DeviceIdType.MESH
ANY = pl.BlockSpec(memory_space=pl.ANY)
SPLIT_MIN_BYTES = 1 << 20


def _other_chips(x, y):
    return [(1 - x, y), (x, 1 - y), (1 - x, 1 - y)]


def _half_rows(rows, who):
    half = rows // 2
    return pl.ds(pl.multiple_of(who * half, 16), half)


def gather_all(shards):
    n = len(shards)
    rows = [s.shape[0] for s in shards]
    split = [r % 32 == 0 and s.size * s.dtype.itemsize >= SPLIT_MIN_BYTES for r, s in zip(rows, shards)]

    def body(*refs):
        ins, outs = refs[:n], refs[n:2 * n]
        ici_send, ici_recv, d2d_send, d2d_recv, local_sems = refs[2 * n:]
        x, y, c = lax.axis_index("x"), lax.axis_index("y"), lax.axis_index("c")
        me, sib, chips = 2 * x + y, (x, y, 1 - c), _other_chips(x, y)

        def part(k, who):
            return _half_rows(rows[k], who) if split[k] else pl.ds(0, rows[k])

        def ici(k, r, block):
            px, py = chips[r]
            return pltpu.make_async_remote_copy(
                src_ref=ins[k].at[part(k, c)], dst_ref=outs[k].at[block, part(k, c)],
                send_sem=ici_send.at[3 * k + r], recv_sem=ici_recv.at[3 * k + r],
                device_id=(px, py, c), device_id_type=MESH)

        def d2d(k, r, who):
            px, py = chips[r]
            blk = outs[k].at[2 * px + py, part(k, who)]
            return pltpu.make_async_remote_copy(
                src_ref=blk, dst_ref=blk, send_sem=d2d_send.at[3 * k + r], recv_sem=d2d_recv.at[3 * k + r],
                device_id=sib, device_id_type=MESH)

        local = [pltpu.make_async_copy(ins[k], outs[k].at[me], local_sems.at[k]) for k in range(n)]
        for cp in local:
            cp.start()
        sends = [ici(k, r, me) for k in range(n) for r in range(3)]
        for cp in sends:
            cp.start()
        for r in range(3):
            px, py = chips[r]
            for k in range(n):
                ici(k, r, 2 * px + py).wait_recv()
                if split[k]:
                    fwd = d2d(k, r, c)
                    fwd.start()
                    sends.append(fwd)
        for r in range(3):
            for k in range(n):
                if split[k]:
                    d2d(k, r, 1 - c).wait_recv()
        for cp in sends:
            cp.wait_send()
        for cp in local:
            cp.wait()

    return _pcall(
        body, name="gather_all", in_specs=[ANY] * n, out_specs=[ANY] * n,
        out_shape=[jax.ShapeDtypeStruct((N_CHIPS,) + s.shape, s.dtype) for s in shards],
        scratch_shapes=[pltpu.SemaphoreType.DMA((3 * n,))] * 4 + [pltpu.SemaphoreType.DMA((n,))])(*shards)


def swap_halves(gstacks):
    n = len(gstacks)

    def body(*refs):
        ins, outs, send_sems, recv_sems = refs[:n], refs[n:2 * n], refs[2 * n], refs[2 * n + 1]
        x, y, c = lax.axis_index("x"), lax.axis_index("y"), lax.axis_index("c")
        copies = [pltpu.make_async_remote_copy(
            src_ref=ins[k].at[:, _half_rows(ins[k].shape[1], 1 - c)], dst_ref=outs[k],
            send_sem=send_sems.at[k], recv_sem=recv_sems.at[k], device_id=(x, y, 1 - c), device_id_type=MESH)
            for k in range(n)]
        for cp in copies:
            cp.start()
        for cp in copies:
            cp.wait()

    return _pcall(
        body, name="swap_halves", in_specs=[ANY] * n, out_specs=[ANY] * n,
        out_shape=[jax.ShapeDtypeStruct((g.shape[0], g.shape[1] // 2, g.shape[2]), g.dtype) for g in gstacks],
        scratch_shapes=[pltpu.SemaphoreType.DMA((n,)), pltpu.SemaphoreType.DMA((n,))])(*gstacks)


def scatter_chips(stacks):
    n = len(stacks)

    def body(*refs):
        ins, outs, send_sems, recv_sems = refs[:n], refs[n:2 * n], refs[2 * n], refs[2 * n + 1]
        x, y, c = lax.axis_index("x"), lax.axis_index("y"), lax.axis_index("c")
        copies = [pltpu.make_async_remote_copy(
            src_ref=ins[k].at[2 * px + py], dst_ref=outs[k].at[r], send_sem=send_sems.at[3 * k + r],
            recv_sem=recv_sems.at[3 * k + r], device_id=(px, py, c), device_id_type=MESH)
            for k in range(n) for r, (px, py) in enumerate(_other_chips(x, y))]
        for cp in copies:
            cp.start()
        for cp in copies:
            cp.wait()

    return _pcall(
        body, name="scatter_chips", in_specs=[ANY] * n, out_specs=[ANY] * n,
        out_shape=[jax.ShapeDtypeStruct((3,) + s.shape[1:], s.dtype) for s in stacks],
        scratch_shapes=[pltpu.SemaphoreType.DMA((3 * n,)), pltpu.SemaphoreType.DMA((3 * n,))])(*stacks)


def join_halves(totals):
    n = len(totals)

    def body(*refs):
        ins, outs, send_sems, recv_sems, local_sems = refs[:n], refs[n:2 * n], *refs[2 * n:]
        x, y, c = lax.axis_index("x"), lax.axis_index("y"), lax.axis_index("c")
        local = [pltpu.make_async_copy(ins[k], outs[k].at[_half_rows(outs[k].shape[0], c)], local_sems.at[k])
                 for k in range(n)]
        for cp in local:
            cp.start()

        def d2d(k, who):
            return pltpu.make_async_remote_copy(
                src_ref=ins[k], dst_ref=outs[k].at[_half_rows(outs[k].shape[0], who)], send_sem=send_sems.at[k],
                recv_sem=recv_sems.at[k], device_id=(x, y, 1 - c), device_id_type=MESH)

        sends = [d2d(k, c) for k in range(n)]
        for cp in sends:
            cp.start()
        for k in range(n):
            d2d(k, 1 - c).wait_recv()
        for cp in sends:
            cp.wait_send()
        for cp in local:
            cp.wait()

    return _pcall(
        body, name="join_halves", in_specs=[ANY] * n, out_specs=[ANY] * n,
        out_shape=[jax.ShapeDtypeStruct((2 * t.shape[0], t.shape[1]), t.dtype) for t in totals],
        scratch_shapes=[pltpu.SemaphoreType.DMA((n,))] * 3)(*totals)


def reduce_scatter(gstacks):
    halves = swap_halves(gstacks)
    payload, own = pair_sum(gstacks, halves)
    recv = scatter_chips(payload)
    return join_halves(chip_sum(own, recv))


def allreduce_small(vec):
    R = vec.shape[0]

    def body(in_ref, out_ref, buf, send_sems, recv_sems):
        x, y, c = lax.axis_index("x"), lax.axis_index("y"), lax.axis_index("c")
        me = 4 * x + 2 * y + c
        buf[me] = in_ref[...]
        copies = []
        for k in range(1, 8):
            peer = (x ^ (k >> 2), y ^ ((k >> 1) & 1), c ^ (k & 1))
            copies.append(pltpu.make_async_remote_copy(
                src_ref=in_ref, dst_ref=buf.at[me], send_sem=send_sems.at[k - 1], recv_sem=recv_sems.at[k - 1],
                device_id=peer, device_id_type=MESH))
        for cp in copies:
            cp.start()
        for cp in copies:
            cp.wait()
        acc = buf[0]
        for d in range(1, 8):
            acc = acc + buf[d]
        out_ref[...] = acc

    vm = pl.BlockSpec(memory_space=pltpu.VMEM)
    return _pcall(
        body, name="allreduce_small", in_specs=[vm], out_specs=vm,
        out_shape=jax.ShapeDtypeStruct((R, LANES), F32),
        scratch_shapes=[pltpu.VMEM((8, R, LANES), F32), pltpu.SemaphoreType.DMA((7,)), pltpu.SemaphoreType.DMA((7,))])(vec)


SHARDED = [("pool_in", 1, "bf16"), ("pool_group", 2, "bf16"), ("pool_scale", 1, "f32"), ("ssd_in", 2, "bf16"),
           ("ssd_conv_w", 2, "f32"), ("ssd_out", 1, "bf16"), ("sb_qkv", 2, "bf16"), ("sb_out", 1, "bf16"),
           ("ffn_gate", 2, "bf16"), ("ffn_up", 2, "bf16"), ("ffn_down", 1, "bf16")]
REPLICATED = ["mix_norm", "ssd_conv_b", "ssd_dt_bias", "ssd_a_log", "ssd_d", "ssd_out_norm", "sb_q_norm",
              "sb_k_norm", "ffn_norm"]
WEIGHT_ORDER = ["mix_norm", "pool_in", "pool_group", "pool_scale", "ssd_in", "ssd_conv_w", "ssd_conv_b",
                "ssd_dt_bias", "ssd_a_log", "ssd_d", "ssd_out_norm", "ssd_out", "sb_qkv", "sb_q_norm", "sb_k_norm",
                "sb_out", "ffn_norm", "ffn_gate", "ffn_up", "ffn_down"]
ROW_PAD = 1024


def _piece_rows(n, mult):
    rows = -(-n // LANES)
    return -(-rows // mult) * mult


def _as_rows(a, mult):
    flat = a.reshape(-1)
    rows = _piece_rows(flat.shape[0], mult)
    if rows * LANES != flat.shape[0]:
        flat = jnp.pad(flat, (0, rows * LANES - flat.shape[0]))
    return flat.reshape(rows, LANES)


def _pack(arrs, mult=8, row_pad=ROW_PAD):
    parts = [_as_rows(a, mult) for a in arrs]
    rows = sum(p.shape[0] for p in parts)
    pad = -rows % row_pad
    if pad:
        parts.append(jnp.zeros((pad, LANES), parts[0].dtype))
    return jnp.concatenate(parts, axis=0)


def _unpack(packed, shapes, mult=8, lead=()):
    out, off = [], 0
    for s in shapes:
        n = math.prod(s)
        rows = _piece_rows(n, mult)
        piece = packed[..., off:off + rows, :].reshape(lead + (rows * LANES,))
        out.append(piece[..., :n].reshape(lead + tuple(s)))
        off += rows
    return out


def _gather_weights(shards):
    kinds = (("bf16", BF16, 16, ROW_PAD), ("f32", F32, 8, 8))
    groups = [[(n, ax) for n, ax, k in SHARDED if k == kind] for kind, _, _, _ in kinds]
    gathered = gather_all([_pack([shards[n].astype(dtype) for n, _ in group], mult, row_pad)
                           for group, (_, dtype, mult, row_pad) in zip(groups, kinds)])
    full = {}
    for group, got, (_, _, mult, _) in zip(groups, gathered, kinds):
        pieces = _unpack(got, [shards[n].shape for n, _ in group], mult, lead=(N_CHIPS,))
        for (n, ax), p in zip(group, pieces):
            full[n] = jnp.concatenate([p[j] for j in range(N_CHIPS)], axis=ax)
    return full


def _split_shards(full, axis):
    return jnp.stack(jnp.split(full, N_CHIPS, axis=axis))


def _ffn_fwd(x, gain, wg, wu, wd):
    h = rmsnorm_fwd(x, gain, name="ffn_norm_fwd")
    a, b, hid = ffn_up(h, wg, wu)
    xo = linear([(hid, wd, "nn")], res=x, name="ffn_down")
    return xo, (x, h, a, b, hid)


def _ffn_bwd(dout, saved, gain, wg, wu, wd):
    x, h, a, b, hid = saved
    da, db = ffn_bwd_hidden(dout, wd, a, b)
    (dwd,) = wgrad(hid, [dout], tk=1408, name="ffn_dwd")
    dwg, dwu = wgrad(h, [da, db], tn=1408, tm=512, name="ffn_dwgu")
    dh = linear([(da, wg, "nt"), (db, wu, "nt")], tm=256, name="ffn_dh")
    dx, dgain = rmsnorm_bwd(x, gain, dh, dout, name="ffn_norm_bwd")
    return dx, dgain, dwg, dwu, dwd


def _pool_layer_fwd(x, gain, w_in, wgrp, scale):
    h = rmsnorm_fwd(x, gain, name="pool_norm_fwd")
    u = linear([(h, w_in, "nn")], name="pool_in")
    xo, p = pool_fwd(u, wgrp, scale, x)
    return xo, (x, h, p)


def _pool_layer_bwd(dout, saved, gain, w_in, wgrp, scale):
    x, h, p = saved
    dp, dwgrp, dscale = pool_bwd_group(dout, p, wgrp, scale)
    du = pool_bwd_window(dp)
    (dw_in,) = wgrad(h, [du], name="pool_dwin")
    dh = linear([(du, w_in, "nt")], name="pool_dh")
    dx, dgain = rmsnorm_bwd(x, gain, dh, dout, name="pool_norm_bwd")
    return dx, dgain, dw_in, dwgrp, dscale


def _ssd_layer_fwd(x, gain, w_z, w_xbc, w_dt, conv_w, conv_b, dt_bias, a_log, d_full, out_norm, w_out):
    h = rmsnorm_fwd(x, gain, name="ssd_norm_fwd")
    z = linear([(h, w_z, "nn")], name="ssd_in_z")
    xbc = linear([(h, w_xbc, "nn")], tn=2048, name="ssd_in_xbc")
    dt_raw = linear([(h, w_dt, "nn")], name="ssd_in_dt")
    act = conv_fwd(xbc, conv_w, conv_b)
    y, states = ssd_scan_fwd(act, dt_raw, dt_bias, a_log, d_full)
    gn = gate_norm_fwd(y, z, out_norm)
    xo = linear([(gn, w_out, "nn")], res=x, name="ssd_out")
    return xo, (x, h, z, xbc, dt_raw, act, y, states, gn)


def _ssd_layer_bwd(dout, saved, gain, w_z, w_xbc, w_dt, conv_w, conv_b, dt_bias, a_log, d_full, out_norm, w_out):
    x, h, z, xbc, dt_raw, act, y, states, gn = saved
    dgn = linear([(dout, w_out, "nt")], name="ssd_dgn")
    (dw_out,) = wgrad(gn, [dout], name="ssd_dwout")
    dy, dz, dout_norm = gate_norm_bwd(dgn, y, z, out_norm)
    dact, ddt_raw, dbias, dalog, dd_full = ssd_scan_bwd(dy, act, dt_raw, dt_bias, a_log, d_full, states)
    dpre, dconv_w8, dconv_b = conv_bwd_pre(dact, xbc, conv_w, conv_b)
    dxbc = conv_bwd_input(dpre, conv_w)
    ddt_b = ddt_raw.astype(BF16)
    (dw_z,) = wgrad(h, [dz], name="ssd_dwz")
    (dw_xbc,) = wgrad(h, [dxbc], tn=2048, name="ssd_dwxbc")
    (dw_dt,) = wgrad(h, [ddt_b], name="ssd_dwdt")
    dh = linear([(dz, w_z, "nt"), (dxbc, w_xbc, "nt"), (ddt_b, w_dt, "nt")], tm=256, name="ssd_dh")
    dx, dgain = rmsnorm_bwd(x, gain, dh, dout, name="ssd_norm_bwd")
    dw_in = jnp.concatenate([dw_z, dw_xbc, dw_dt], axis=1)
    dd = dd_full.reshape(NH, HP).sum(axis=1).reshape(1, NH)
    return dx, dgain, dw_in, dconv_w8[:4], dconv_b, dbias, dalog, dd, dout_norm, dw_out


def _sb_layer_fwd(x, gain, w_qkv, qg, kg, w_out):
    h = rmsnorm_fwd(x, gain, name="sb_norm_fwd")
    qkv = linear([(h, w_qkv, "nn")], tn=1024, name="sb_qkv")
    qs, kn, v = sb_prep_fwd(qkv, qg, kg)
    o, rsave = sb_fwd(qs, kn, v)
    xo = linear([(o, w_out, "nn")], res=x, name="sb_out")
    return xo, (x, h, qkv, qs, kn, v, o, rsave)


def _sb_layer_bwd(dout, saved, gain, w_qkv, qg, kg, w_out):
    x, h, qkv, qs, kn, v, o, rsave = saved
    do = linear([(dout, w_out, "nt")], out_dtype=BF16, name="sb_do")
    (dw_out,) = wgrad(o, [dout], name="sb_dwout")
    dqs, dkn, dv = sb_bwd(qs, kn, v, do, rsave)
    dqkv, dqg, dkg = sb_prep_bwd(dqs, dkn, dv, qkv, qg, kg)
    (dw_qkv,) = wgrad(h, [dqkv], tn=1024, name="sb_dwqkv")
    dh = linear([(dqkv, w_qkv, "nt")], name="sb_dh")
    dx, dgain = rmsnorm_bwd(x, gain, dh, dout, name="sb_norm_bwd")
    dqg = dqg.reshape(SBH, SBD).sum(axis=0).reshape(1, SBD)
    dkg = dkg.reshape(SBH, SBD).sum(axis=0).reshape(1, SBD)
    return dx, dgain, dw_qkv, dqg, dkg, dw_out


def _local_step(x, target, full, rep):
    S = x.shape[0]
    d_full = jnp.repeat(rep["ssd_d"][0], HP).reshape(1, DI)
    qg = jnp.tile(rep["sb_q_norm"][0], 2).reshape(1, LANES)
    kg = jnp.tile(rep["sb_k_norm"][0], 2).reshape(1, LANES)
    ssd_in = full["ssd_in"][0]
    w_z, w_xbc, w_dt = ssd_in[:, :DI], ssd_in[:, DI:DI + CONV_CH], ssd_in[:, DI + CONV_CH:]
    conv_w = full["ssd_conv_w"][0]
    conv_b = rep["ssd_conv_b"]
    pool_scale = full["pool_scale"]

    def mixer_args(i):
        kind, j = i % 3, i // 3
        if kind == 0:
            return (full["pool_in"][j], full["pool_group"][j], pool_scale[j:j + 1])
        if kind == 1:
            return (w_z, w_xbc, w_dt, conv_w, conv_b, rep["ssd_dt_bias"], rep["ssd_a_log"], d_full,
                    rep["ssd_out_norm"], full["ssd_out"][0])
        return (full["sb_qkv"][0], qg, kg, full["sb_out"][0])

    fwd = (_pool_layer_fwd, _ssd_layer_fwd, _sb_layer_fwd)
    bwd = (_pool_layer_bwd, _ssd_layer_bwd, _sb_layer_bwd)
    saved = []
    for i in range(DEPTH):
        x, sm = fwd[i % 3](x, rep["mix_norm"][i], *mixer_args(i))
        x, sf = _ffn_fwd(x, rep["ffn_norm"][i], full["ffn_gate"][i], full["ffn_up"][i], full["ffn_down"][i])
        saved.append((sm, sf))

    colsq, dx = loss_head(x, target)
    loss = 0.5 * jnp.sum(colsq) / D

    g = {n: [None] * DEPTH for n in ("mix_norm", "ffn_norm", "ffn_gate", "ffn_up", "ffn_down")}
    g["pool_in"], g["pool_group"], g["pool_scale"] = [None] * 2, [None] * 2, [None] * 2
    for i in reversed(range(DEPTH)):
        sm, sf = saved[i]
        dx, g["ffn_norm"][i], g["ffn_gate"][i], g["ffn_up"][i], g["ffn_down"][i] = _ffn_bwd(
            dx, sf, rep["ffn_norm"][i], full["ffn_gate"][i], full["ffn_up"][i], full["ffn_down"][i])
        kind, j = i % 3, i // 3
        res = bwd[kind](dx, sm, rep["mix_norm"][i], *mixer_args(i))
        dx, g["mix_norm"][i] = res[0], res[1]
        if kind == 0:
            g["pool_in"][j], g["pool_group"][j], g["pool_scale"][j] = res[2:]
        elif kind == 1:
            dw_in, dconv_w, dconv_b, dbias, dalog, dd, don, dw_out = res[2:]
            g.update(ssd_in=dw_in[None], ssd_conv_w=dconv_w[None], ssd_conv_b=dconv_b, ssd_dt_bias=dbias,
                     ssd_a_log=dalog, ssd_d=dd, ssd_out_norm=don, ssd_out=dw_out[None])
        else:
            dw_qkv, dqg, dkg, dw_out = res[2:]
            g.update(sb_qkv=dw_qkv[None], sb_q_norm=dqg, sb_k_norm=dkg, sb_out=dw_out[None])
    for n in ("mix_norm", "ffn_norm", "pool_scale"):
        g[n] = jnp.concatenate(g[n], axis=0)
    for n in ("ffn_gate", "ffn_up", "ffn_down", "pool_in", "pool_group"):
        g[n] = jnp.stack(g[n])
    return loss, dx, g


def kernel(x, mix_norm, pool_in, pool_group, pool_scale, ssd_in, ssd_conv_w, ssd_conv_b, ssd_dt_bias, ssd_a_log, ssd_d, ssd_out_norm, ssd_out, sb_qkv, sb_q_norm, sb_k_norm, sb_out, ffn_norm, ffn_gate, ffn_up, ffn_down, loss_target, m_mix_norm, m_pool_in, m_pool_group, m_pool_scale, m_ssd_in, m_ssd_conv_w, m_ssd_conv_b, m_ssd_dt_bias, m_ssd_a_log, m_ssd_d, m_ssd_out_norm, m_ssd_out, m_sb_qkv, m_sb_q_norm, m_sb_k_norm, m_sb_out, m_ffn_norm, m_ffn_gate, m_ffn_up, m_ffn_down, v_mix_norm, v_pool_in, v_pool_group, v_pool_scale, v_ssd_in, v_ssd_conv_w, v_ssd_conv_b, v_ssd_dt_bias, v_ssd_a_log, v_ssd_d, v_ssd_out_norm, v_ssd_out, v_sb_qkv, v_sb_q_norm, v_sb_k_norm, v_sb_out, v_ffn_norm, v_ffn_gate, v_ffn_up, v_ffn_down):
    given = dict(locals())
    w = {n: given[n] for n in WEIGHT_ORDER}
    m = {n: given["m_" + n] for n in WEIGHT_ORDER}
    v = {n: given["v_" + n] for n in WEIGHT_ORDER}
    sharded_names = [s[0] for s in SHARDED]

    full = _gather_weights(w)
    rep = {n: w[n] for n in REPLICATED}

    loss, dx, g = _local_step(x[0], loss_target[0], full, rep)
    loss = lax.psum(loss, ("x", "y", "c"))

    gstack = jnp.stack([_pack([_split_shards(g[n], ax)[j] for n, ax, _ in SHARDED]) for j in range(N_CHIPS)])
    (gsum_sharded,) = reduce_scatter([gstack])
    shard_shapes = [w[n].shape for n in sharded_names]
    gs, ds, ms, vs = adamw(_pack([w[n] for n in sharded_names]), [gsum_sharded],
                           _pack([m[n] for n in sharded_names]), _pack([v[n] for n in sharded_names]),
                           name="adamw_sharded")
    out = {}
    for key, flat in (("g", gs), ("d", ds), ("m", ms), ("v", vs)):
        for n, a in zip(sharded_names, _unpack(flat, shard_shapes)):
            out[key, n] = a

    rep_shapes = [w[n].shape for n in REPLICATED]
    pack_small = functools.partial(_pack, row_pad=8)
    gsum = allreduce_small(pack_small([g[n] for n in REPLICATED]))
    gs, ds, ms, vs = adamw(pack_small([w[n] for n in REPLICATED]), [gsum], pack_small([m[n] for n in REPLICATED]),
                           pack_small([v[n] for n in REPLICATED]), name="adamw_replicated")
    for key, flat in (("g", gs), ("d", ds), ("m", ms), ("v", vs)):
        for n, a in zip(REPLICATED, _unpack(flat, rep_shapes)):
            out[key, n] = a

    return (loss, dx[None], *[out["g", n] for n in WEIGHT_ORDER], *[out["d", n] for n in WEIGHT_ORDER],
            *[out["m", n] for n in WEIGHT_ORDER], *[out["v", n] for n in WEIGHT_ORDER])
```

```python
import functools
import math

import jax
import jax.numpy as jnp
from jax import lax
from jax.experimental import pallas as pl
from jax.experimental.pallas import tpu as pltpu

F32 = jnp.float32
BF16 = jnp.bfloat16
HI = lax.Precision.HIGHEST

D = 1024
DEPTH = 4
EPS = 1e-6
POOL_WINDOWS = (2, 4, 8, 16)
PG = 256
DI = 2048
NH = 32
HP = 64
NG = 8
NS = 128
GW = 256
CH = 256
CONV_CH = 4096
SSD_IN = 6176
SBH = 16
SBD = 64
FH = 2816
N_CHIPS = 4
LANES = 128

ADAM_LR = 0.001
ADAM_B1 = 0.9
ADAM_B2 = 0.999
ADAM_EPS = 1e-08
ADAM_WD = 0.01
ADAM_STEP = 10

VMEM_LIMIT = 56 * 1024 * 1024


def _pcall(body, **kw):
    return pl.pallas_call(body, **kw)


def _cp(*sem):
    return pltpu.CompilerParams(dimension_semantics=sem, vmem_limit_bytes=VMEM_LIMIT)


def _dot(a, b, prec=None):
    return lax.dot_general(a, b, (((1,), (0,)), ((), ())), precision=prec, preferred_element_type=F32)


def _dot_nt(a, b, prec=None):
    return lax.dot_general(a, b, (((1,), (1,)), ((), ())), precision=prec, preferred_element_type=F32)


def _dot_tn(a, b, prec=None):
    return lax.dot_general(a, b, (((0,), (0,)), ((), ())), precision=prec, preferred_element_type=F32)


def _sigmoid(x):
    return 1.0 / (1.0 + jnp.exp(-x))


def _iota(shape, axis):
    return lax.broadcasted_iota(jnp.int32, shape, axis)


def linear(pairs, res=None, out_dtype=F32, tm=512, tn=None, name="linear"):
    M = pairs[0][0].shape[0]
    N = pairs[0][1].shape[1] if pairs[0][2] == "nn" else pairs[0][1].shape[0]
    tm = min(tm, M)
    tn = N if tn is None else min(tn, N)
    n_pairs = len(pairs)
    modes = [p[2] for p in pairs]

    def body(*refs):
        acc = None
        for k in range(n_pairs):
            a = refs[2 * k][...].astype(BF16)
            w = refs[2 * k + 1][...]
            t = _dot(a, w) if modes[k] == "nn" else _dot_nt(a, w)
            acc = t if acc is None else acc + t
        if res is not None:
            acc = acc + refs[2 * n_pairs][...]
        refs[-1][...] = acc.astype(out_dtype)

    in_specs, args = [], []
    for a, w, mode in pairs:
        K = a.shape[1]
        in_specs.append(pl.BlockSpec((tm, K), lambda j, i: (i, 0)))
        if mode == "nn":
            in_specs.append(pl.BlockSpec((K, tn), lambda j, i: (0, j)))
        else:
            in_specs.append(pl.BlockSpec((tn, K), lambda j, i: (j, 0)))
        args += [a, w]
    if res is not None:
        in_specs.append(pl.BlockSpec((tm, tn), lambda j, i: (i, j)))
        args.append(res)
    return _pcall(
        body, name=name, grid=(N // tn, M // tm), in_specs=in_specs,
        out_specs=pl.BlockSpec((tm, tn), lambda j, i: (i, j)),
        out_shape=jax.ShapeDtypeStruct((M, N), out_dtype),
        compiler_params=_cp("parallel", "arbitrary"))(*args)


def wgrad(a, gs, tk=1024, tn=None, tm=1024, name="wgrad"):
    M, Ka = a.shape
    N = gs[0].shape[1]
    tk, tm = min(tk, Ka), min(tm, M)
    tn = N if tn is None else min(tn, N)
    n_g = len(gs)

    def body(*refs):
        a_ref, g_refs, o_refs = refs[0], refs[1:1 + n_g], refs[1 + n_g:]
        m = pl.program_id(2)
        at = a_ref[...].astype(BF16)
        for g_ref, o_ref in zip(g_refs, o_refs):
            t = _dot_tn(at, g_ref[...].astype(BF16))

            @pl.when(m == 0)
            def _():
                o_ref[...] = t

            @pl.when(m > 0)
            def _():
                o_ref[...] += t

    out = _pcall(
        body, name=name, grid=(Ka // tk, N // tn, M // tm),
        in_specs=[pl.BlockSpec((tm, tk), lambda k, j, m: (m, k))]
        + [pl.BlockSpec((tm, tn), lambda k, j, m: (m, j))] * n_g,
        out_specs=[pl.BlockSpec((tk, tn), lambda k, j, m: (k, j))] * n_g,
        out_shape=[jax.ShapeDtypeStruct((Ka, N), F32)] * n_g,
        compiler_params=_cp("parallel", "parallel", "arbitrary"))(a, *gs)
    return out


def rmsnorm_fwd(x, gain, tm=512, name="rmsnorm_fwd"):
    S, Dm = x.shape
    tm = min(tm, S)

    def body(x_ref, g_ref, o_ref):
        xv = x_ref[...]
        r = lax.rsqrt(jnp.mean(xv * xv, axis=-1, keepdims=True) + EPS)
        o_ref[...] = (xv * r * g_ref[...]).astype(BF16)

    return _pcall(
        body, name=name, grid=(S // tm,),
        in_specs=[pl.BlockSpec((tm, Dm), lambda i: (i, 0)), pl.BlockSpec((1, Dm), lambda i: (0, 0))],
        out_specs=pl.BlockSpec((tm, Dm), lambda i: (i, 0)),
        out_shape=jax.ShapeDtypeStruct((S, Dm), BF16),
        compiler_params=_cp("parallel"))(x, gain.reshape(1, Dm))


def rmsnorm_bwd(x, gain, dh, dres, tm=512, name="rmsnorm_bwd"):
    S, Dm = x.shape
    tm = min(tm, S)

    def body(x_ref, g_ref, dh_ref, dr_ref, dx_ref, dg_ref):
        i = pl.program_id(0)
        xv = x_ref[...]
        r = lax.rsqrt(jnp.mean(xv * xv, axis=-1, keepdims=True) + EPS)
        y = xv * r
        dhv = dh_ref[...]
        dy = dhv * g_ref[...]
        dx_ref[...] = dr_ref[...] + r * (dy - y * jnp.mean(dy * y, axis=-1, keepdims=True))
        part = jnp.sum(dhv * y, axis=0, keepdims=True)

        @pl.when(i == 0)
        def _():
            dg_ref[...] = part

        @pl.when(i > 0)
        def _():
            dg_ref[...] += part

    return _pcall(
        body, name=name, grid=(S // tm,),
        in_specs=[pl.BlockSpec((tm, Dm), lambda i: (i, 0)), pl.BlockSpec((1, Dm), lambda i: (0, 0)),
                  pl.BlockSpec((tm, Dm), lambda i: (i, 0)), pl.BlockSpec((tm, Dm), lambda i: (i, 0))],
        out_specs=[pl.BlockSpec((tm, Dm), lambda i: (i, 0)), pl.BlockSpec((1, Dm), lambda i: (0, 0))],
        out_shape=[jax.ShapeDtypeStruct((S, Dm), F32), jax.ShapeDtypeStruct((1, Dm), F32)],
        compiler_params=_cp("arbitrary"))(x, gain.reshape(1, Dm), dh, dres)


def ffn_up(h, wg, wu, tm=256, tn=1408):
    S = h.shape[0]
    tm = min(tm, S)

    def body(h_ref, wg_ref, wu_ref, a_ref, b_ref, hid_ref):
        hv = h_ref[...]
        a = _dot(hv, wg_ref[...])
        b = _dot(hv, wu_ref[...])
        a_ref[...] = a
        b_ref[...] = b
        hid_ref[...] = (a * _sigmoid(a) * b).astype(BF16)

    return _pcall(
        body, name="ffn_up", grid=(FH // tn, S // tm),
        in_specs=[pl.BlockSpec((tm, D), lambda j, i: (i, 0)), pl.BlockSpec((D, tn), lambda j, i: (0, j)),
                  pl.BlockSpec((D, tn), lambda j, i: (0, j))],
        out_specs=[pl.BlockSpec((tm, tn), lambda j, i: (i, j))] * 3,
        out_shape=[jax.ShapeDtypeStruct((S, FH), F32), jax.ShapeDtypeStruct((S, FH), F32),
                   jax.ShapeDtypeStruct((S, FH), BF16)],
        compiler_params=_cp("parallel", "arbitrary"))(h, wg, wu)


def ffn_bwd_hidden(dout, wd, a, b, tm=256, tn=1408):
    S = dout.shape[0]
    tm = min(tm, S)

    def body(do_ref, wd_ref, a_ref, b_ref, da_ref, db_ref):
        dhid = _dot_nt(do_ref[...].astype(BF16), wd_ref[...])
        av, bv = a_ref[...], b_ref[...]
        s = _sigmoid(av)
        da_ref[...] = (dhid * bv * (s * (1.0 + av * (1.0 - s)))).astype(BF16)
        db_ref[...] = (dhid * (av * s)).astype(BF16)

    return _pcall(
        body, name="ffn_bwd_hidden", grid=(FH // tn, S // tm),
        in_specs=[pl.BlockSpec((tm, D), lambda j, i: (i, 0)), pl.BlockSpec((tn, D), lambda j, i: (j, 0)),
                  pl.BlockSpec((tm, tn), lambda j, i: (i, j)), pl.BlockSpec((tm, tn), lambda j, i: (i, j))],
        out_specs=[pl.BlockSpec((tm, tn), lambda j, i: (i, j))] * 2,
        out_shape=[jax.ShapeDtypeStruct((S, FH), BF16)] * 2,
        compiler_params=_cp("parallel", "arbitrary"))(dout, wd, a, b)


POOL_T = 128
POOL_HALO = 16


def pool_fwd(u, wgrp, scale, x_res):
    S = u.shape[0]
    T, HB = min(POOL_T, S), POOL_HALO
    per = T // HB

    def body(u_ref, tail_ref, wg_ref, sc_ref, x_ref, xo_ref, p_ref):
        i = pl.program_id(0)
        uc = u_ref[...]
        tail = jnp.where(i > 0, tail_ref[...], 0.0)
        d_cur = _iota((T, T), 0) - _iota((T, T), 1)
        d_tail = _iota((T, HB), 0) - _iota((T, HB), 1) + HB
        tg = i * T + _iota((T, 1), 0)
        for g, w in enumerate(POOL_WINDOWS):
            gs = slice(g * PG, (g + 1) * PG)
            band = ((d_cur >= 0) & (d_cur < w)).astype(F32)
            band_t = ((d_tail >= 0) & (d_tail < w)).astype(F32)
            ug = uc[:, gs]
            ws = _dot(band, ug, HI) + _dot(band_t, tail[:, gs], HI)
            cnt = jnp.minimum(tg + 1, w).astype(F32)
            pb = (ws / cnt - ug).astype(BF16)
            p_ref[:, gs] = pb
            xo_ref[:, gs] = x_ref[:, gs] + _dot(pb, wg_ref[g]) * sc_ref[:, gs]

    return _pcall(
        body, name="pool_fwd", grid=(S // T,),
        in_specs=[pl.BlockSpec((T, D), lambda i: (i, 0)),
                  pl.BlockSpec((HB, D), lambda i: (jnp.maximum(i * per - 1, 0), 0)),
                  pl.BlockSpec((4, PG, PG), lambda i: (0, 0, 0)), pl.BlockSpec((1, D), lambda i: (0, 0)),
                  pl.BlockSpec((T, D), lambda i: (i, 0))],
        out_specs=[pl.BlockSpec((T, D), lambda i: (i, 0))] * 2,
        out_shape=[jax.ShapeDtypeStruct((S, D), F32), jax.ShapeDtypeStruct((S, D), BF16)],
        compiler_params=_cp("parallel"))(u, u, wgrp, scale, x_res)


def pool_bwd_group(dm, p, wgrp, scale, tm=512):
    S = dm.shape[0]
    tm = min(tm, S)

    def body(dm_ref, p_ref, wg_ref, sc_ref, dp_ref, dwg_ref, dsc_ref):
        i = pl.program_id(0)

        @pl.when(i == 0)
        def _():
            dwg_ref[...] = jnp.zeros_like(dwg_ref)
            dsc_ref[...] = jnp.zeros_like(dsc_ref)

        for g in range(4):
            gs = slice(g * PG, (g + 1) * PG)
            dmg, pg, wg = dm_ref[:, gs], p_ref[:, gs], wg_ref[g]
            dsc_ref[:, gs] += jnp.sum(dmg * _dot(pg, wg), axis=0, keepdims=True)
            dy = (dmg * sc_ref[:, gs]).astype(BF16)
            dp_ref[:, gs] = _dot_nt(dy, wg)
            dwg_ref[g] += _dot_tn(pg, dy)

    return _pcall(
        body, name="pool_bwd_group", grid=(S // tm,),
        in_specs=[pl.BlockSpec((tm, D), lambda i: (i, 0)), pl.BlockSpec((tm, D), lambda i: (i, 0)),
                  pl.BlockSpec((4, PG, PG), lambda i: (0, 0, 0)), pl.BlockSpec((1, D), lambda i: (0, 0))],
        out_specs=[pl.BlockSpec((tm, D), lambda i: (i, 0)), pl.BlockSpec((4, PG, PG), lambda i: (0, 0, 0)),
                   pl.BlockSpec((1, D), lambda i: (0, 0))],
        out_shape=[jax.ShapeDtypeStruct((S, D), F32), jax.ShapeDtypeStruct((4, PG, PG), F32),
                   jax.ShapeDtypeStruct((1, D), F32)],
        compiler_params=_cp("arbitrary"))(dm, p, wgrp, scale)


def pool_bwd_window(dp):
    S = dp.shape[0]
    T, HB = min(POOL_T, S), POOL_HALO
    per = T // HB
    nt = S // T

    def body(dp_ref, nxt_ref, du_ref):
        i = pl.program_id(0)
        dc = dp_ref[...]
        nxt = jnp.where(i < nt - 1, nxt_ref[...], 0.0)
        d_cur = _iota((T, T), 1) - _iota((T, T), 0)
        d_nxt = _iota((T, HB), 1) - _iota((T, HB), 0) + T
        tg = i * T + _iota((T, 1), 0)
        tn_ = (i + 1) * T + _iota((HB, 1), 0)
        for g, w in enumerate(POOL_WINDOWS):
            gs = slice(g * PG, (g + 1) * PG)
            band = ((d_cur >= 0) & (d_cur < w)).astype(F32)
            band_n = ((d_nxt >= 0) & (d_nxt < w)).astype(F32)
            dcg = dc[:, gs]
            cur = dcg / jnp.minimum(tg + 1, w).astype(F32)
            nx = nxt[:, gs] / jnp.minimum(tn_ + 1, w).astype(F32)
            du_ref[:, gs] = (_dot(band, cur, HI) + _dot(band_n, nx, HI) - dcg).astype(BF16)

    return _pcall(
        body, name="pool_bwd_window", grid=(nt,),
        in_specs=[pl.BlockSpec((T, D), lambda i: (i, 0)),
                  pl.BlockSpec((HB, D), lambda i: (jnp.minimum((i + 1) * per, S // HB - 1), 0))],
        out_specs=pl.BlockSpec((T, D), lambda i: (i, 0)),
        out_shape=jax.ShapeDtypeStruct((S, D), BF16),
        compiler_params=_cp("parallel"))(dp, dp)


CONV_T = 256


def _shift_down(xc, prev8, j):
    if j == 0:
        return xc
    T = xc.shape[0]
    body = pltpu.roll(xc, j, 0)
    first = jnp.where(_iota((8, 1), 0) < j, pltpu.roll(prev8, j, 0), body[0:8])
    return jnp.concatenate([first, body[8:T]], axis=0)


def _shift_up(dc, next8, j):
    if j == 0:
        return dc
    T = dc.shape[0]
    body = pltpu.roll(dc, T - j, 0)
    last = jnp.where(_iota((8, 1), 0) + j < 8, body[T - 8:T], pltpu.roll(next8, 8 - j, 0))
    return jnp.concatenate([body[0:T - 8], last], axis=0)


def conv_fwd(xbc, conv_w, conv_b):
    S = xbc.shape[0]
    T = min(CONV_T, S)
    CB = 1024

    def body(x_ref, prev_ref, w_ref, b_ref, o_ref):
        i = pl.program_id(1)
        xc = x_ref[...]
        prev8 = jnp.where(i > 0, prev_ref[...], 0.0)
        pre = b_ref[...] + w_ref[3:4, :] * xc
        for j in range(1, 4):
            pre = pre + w_ref[3 - j:4 - j, :] * _shift_down(xc, prev8, j)
        o_ref[...] = pre * _sigmoid(pre)

    return _pcall(
        body, name="conv_fwd", grid=(CONV_CH // CB, S // T),
        in_specs=[pl.BlockSpec((T, CB), lambda c, i: (i, c)),
                  pl.BlockSpec((8, CB), lambda c, i: (jnp.maximum(i * (T // 8) - 1, 0), c)),
                  pl.BlockSpec((4, CB), lambda c, i: (0, c)), pl.BlockSpec((1, CB), lambda c, i: (0, c))],
        out_specs=pl.BlockSpec((T, CB), lambda c, i: (i, c)),
        out_shape=jax.ShapeDtypeStruct((S, CONV_CH), F32),
        compiler_params=_cp("parallel", "parallel"))(xbc, xbc, conv_w, conv_b)


def conv_bwd_pre(dact, xbc, conv_w, conv_b):
    S = xbc.shape[0]
    T = min(CONV_T, S)
    CB = 1024

    def body(da_ref, x_ref, prev_ref, w_ref, b_ref, dpre_ref, dw_ref, db_ref):
        i = pl.program_id(1)
        xc = x_ref[...]
        prev8 = jnp.where(i > 0, prev_ref[...], 0.0)
        sh = [_shift_down(xc, prev8, j) for j in range(4)]
        pre = b_ref[...] + w_ref[3:4, :] * sh[0]
        for j in range(1, 4):
            pre = pre + w_ref[3 - j:4 - j, :] * sh[j]
        s = _sigmoid(pre)
        dpre = da_ref[...] * (s * (1.0 + pre * (1.0 - s)))
        dpre_ref[...] = dpre
        rows = [jnp.sum(dpre * sh[3 - k], axis=0, keepdims=True) for k in range(4)]
        dw = jnp.concatenate(rows + [jnp.zeros((4, CB), F32)], axis=0)
        db = jnp.sum(dpre, axis=0, keepdims=True)

        @pl.when(i == 0)
        def _():
            dw_ref[...] = dw
            db_ref[...] = db

        @pl.when(i > 0)
        def _():
            dw_ref[...] += dw
            db_ref[...] += db

    return _pcall(
        body, name="conv_bwd_pre", grid=(CONV_CH // CB, S // T),
        in_specs=[pl.BlockSpec((T, CB), lambda c, i: (i, c)), pl.BlockSpec((T, CB), lambda c, i: (i, c)),
                  pl.BlockSpec((8, CB), lambda c, i: (jnp.maximum(i * (T // 8) - 1, 0), c)),
                  pl.BlockSpec((4, CB), lambda c, i: (0, c)), pl.BlockSpec((1, CB), lambda c, i: (0, c))],
        out_specs=[pl.BlockSpec((T, CB), lambda c, i: (i, c)), pl.BlockSpec((8, CB), lambda c, i: (0, c)),
                   pl.BlockSpec((1, CB), lambda c, i: (0, c))],
        out_shape=[jax.ShapeDtypeStruct((S, CONV_CH), F32), jax.ShapeDtypeStruct((8, CONV_CH), F32),
                   jax.ShapeDtypeStruct((1, CONV_CH), F32)],
        compiler_params=_cp("parallel", "arbitrary"))(dact, xbc, xbc, conv_w, conv_b)


def conv_bwd_input(dpre, conv_w):
    S = dpre.shape[0]
    T = min(CONV_T, S)
    CB = 1024
    nt = S // T

    def body(d_ref, nxt_ref, w_ref, o_ref):
        i = pl.program_id(1)
        dc = d_ref[...]
        next8 = jnp.where(i < nt - 1, nxt_ref[...], 0.0)
        acc = w_ref[3:4, :] * dc
        for j in range(1, 4):
            acc = acc + w_ref[3 - j:4 - j, :] * _shift_up(dc, next8, j)
        o_ref[...] = acc.astype(BF16)

    return _pcall(
        body, name="conv_bwd_input", grid=(CONV_CH // CB, nt),
        in_specs=[pl.BlockSpec((T, CB), lambda c, i: (i, c)),
                  pl.BlockSpec((8, CB), lambda c, i: (jnp.minimum((i + 1) * (T // 8), S // 8 - 1), c)),
                  pl.BlockSpec((4, CB), lambda c, i: (0, c))],
        out_specs=pl.BlockSpec((T, CB), lambda c, i: (i, c)),
        out_shape=jax.ShapeDtypeStruct((S, CONV_CH), BF16),
        compiler_params=_cp("parallel", "parallel"))(dpre, dpre, conv_w)


def _ssd_chunk_terms(dt_ref, bias_ref, alog_ref):
    L = CH
    dtp = dt_ref[...] + bias_ref[...]
    dt = jnp.maximum(dtp, 0.0) + jnp.log(1.0 + jnp.exp(-jnp.abs(dtp)))
    a = -jnp.exp(alog_ref[...])
    da = dt * a
    tri = (_iota((L, L), 0) >= _iota((L, L), 1)).astype(F32)
    acum = _dot(tri, da, HI)
    triu = (_iota((L, L), 0) <= _iota((L, L), 1)).astype(F32)
    acum_row = _dot_tn(da, triu, HI)
    expand = (_iota((NH, DI), 1) // HP == _iota((NH, DI), 0)).astype(F32)
    return dtp, dt, a, da, acum, acum_row, expand


def ssd_scan_fwd(xbc_act, dt_raw, dt_bias, a_log, d_full):
    S = xbc_act.shape[0]
    L = CH
    nc = S // L

    def body(xs_ref, b_ref, c_ref, dt_ref, bias_ref, alog_ref, d_ref, y_ref, st_ref, state):
        c = pl.program_id(0)

        @pl.when(c == 0)
        def _():
            state[...] = jnp.zeros_like(state)

        st_ref[0] = state[...]
        _, dt, _, _, acum, acum_row, expand = _ssd_chunk_terms(dt_ref, bias_ref, alog_ref)
        e_full = _dot(jnp.exp(acum), expand, HI)
        w_full = _dot(jnp.exp(acum[L - 1:L, :] - acum), expand, HI)
        dt_full = _dot(dt, expand, HI)
        causal = _iota((L, L), 0) >= _iota((L, L), 1)
        lane_head = _iota((1, GW), 1) // HP
        for g in range(NG):
            gs = slice(g * GW, (g + 1) * GW)
            ns = slice(g * NS, (g + 1) * NS)
            xs_g = xs_ref[:, gs]
            xdt_g = xs_g * dt_full[:, gs]
            cg = c_ref[:, ns].astype(BF16)
            bg = b_ref[:, ns].astype(BF16)
            gmat = _dot_nt(cg, bg)
            yg = jnp.zeros((L, GW), F32)
            for hh in range(4):
                h = 4 * g + hh
                diff = acum[:, h:h + 1] - acum_row[h:h + 1, :]
                dk = jnp.exp(jnp.where(causal, diff, -1e30))
                xm = jnp.where(lane_head == hh, xdt_g, 0.0).astype(BF16)
                yg = yg + _dot((gmat * dk).astype(BF16), xm)
            sg = state[g]
            yoff = _dot(cg, sg.astype(BF16)) * e_full[:, gs]
            y_ref[:, gs] = yg + yoff + d_ref[:, gs] * xs_g
            state[g] = sg * e_full[L - 1:L, gs] + _dot_tn(bg, (w_full[:, gs] * xdt_g).astype(BF16))

    return _pcall(
        body, name="ssd_scan_fwd", grid=(nc,),
        in_specs=[pl.BlockSpec((L, DI), lambda c: (c, 0)), pl.BlockSpec((L, 1024), lambda c: (c, 2)),
                  pl.BlockSpec((L, 1024), lambda c: (c, 3)), pl.BlockSpec((L, NH), lambda c: (c, 0)),
                  pl.BlockSpec((1, NH), lambda c: (0, 0)), pl.BlockSpec((1, NH), lambda c: (0, 0)),
                  pl.BlockSpec((1, DI), lambda c: (0, 0))],
        out_specs=[pl.BlockSpec((L, DI), lambda c: (c, 0)), pl.BlockSpec((1, NG, NS, GW), lambda c: (c, 0, 0, 0))],
        out_shape=[jax.ShapeDtypeStruct((S, DI), F32), jax.ShapeDtypeStruct((nc, NG, NS, GW), F32)],
        scratch_shapes=[pltpu.VMEM((NG, NS, GW), F32)],
        compiler_params=_cp("arbitrary"))(xbc_act, xbc_act, xbc_act, dt_raw, dt_bias, a_log, d_full)


def ssd_scan_bwd(dy, xbc_act, dt_raw, dt_bias, a_log, d_full, states):
    S = xbc_act.shape[0]
    L = CH
    nc = S // L

    def body(dy_ref, xs_ref, b_ref, c_ref, dt_ref, bias_ref, alog_ref, d_ref, st_ref,
             dxbc_ref, ddt_ref, dbias_ref, dalog_ref, dd_ref, dstate):
        c = pl.program_id(0)

        @pl.when(c == 0)
        def _():
            dstate[...] = jnp.zeros_like(dstate)
            dbias_ref[...] = jnp.zeros_like(dbias_ref)
            dalog_ref[...] = jnp.zeros_like(dalog_ref)
            dd_ref[...] = jnp.zeros_like(dd_ref)

        dtp, dt, a, _, acum, acum_row, expand = _ssd_chunk_terms(dt_ref, bias_ref, alog_ref)
        e_full = _dot(jnp.exp(acum), expand, HI)
        w_h = jnp.exp(acum[L - 1:L, :] - acum)
        w_full = _dot(w_h, expand, HI)
        dt_full = _dot(dt, expand, HI)
        causal = _iota((L, L), 0) >= _iota((L, L), 1)
        lane_head = _iota((1, GW), 1) // HP
        ones_l = jnp.ones((L, LANES), F32)
        head_id = _iota((1, NH), 1)
        dacum = jnp.zeros((L, NH), F32)
        red_parts = []
        dxdt_parts = []
        alast_parts = []
        for g in range(NG):
            gs = slice(g * GW, (g + 1) * GW)
            ns = slice(g * NS, (g + 1) * NS)
            xs_g = xs_ref[:, gs]
            xdt_g = xs_g * dt_full[:, gs]
            dy_g = dy_ref[:, gs]
            cg = c_ref[:, ns].astype(BF16)
            bg = b_ref[:, ns].astype(BF16)
            gmat = _dot_nt(cg, bg)
            sg = st_ref[0, g]
            dsg = dstate[g]
            sgb, dsgb = sg.astype(BF16), dsg.astype(BF16)
            cs = _dot(cg, sgb)
            bds = _dot(bg, dsgb)
            e_g, w_g = e_full[:, gs], w_full[:, gs]
            dxdt = w_g * bds
            dgsum = jnp.zeros((L, L), F32)
            for hh in range(4):
                h = 4 * g + hh
                hm = lane_head == hh
                diff = acum[:, h:h + 1] - acum_row[h:h + 1, :]
                dk = jnp.exp(jnp.where(causal, diff, -1e30))
                m = gmat * dk
                dym = jnp.where(hm, dy_g, 0.0).astype(BF16)
                xm = jnp.where(hm, xdt_g, 0.0).astype(BF16)
                dm = _dot_nt(dym, xm)
                dxdt = dxdt + _dot_tn(m.astype(BF16), dym)
                dgsum = dgsum + dm * dk
                em = dm * m
                rs = _dot(em, ones_l, HI)[:, 0:1]
                cs_ = _dot_tn(em, ones_l, HI)[:, 0:1]
                dacum = dacum + (rs - cs_) * (head_id == h).astype(F32)
            dgb = dgsum.astype(BF16)
            edy = (e_g * dy_g).astype(BF16)
            wx = (w_g * xdt_g).astype(BF16)
            dc_g = _dot(dgb, bg) + _dot_nt(edy, sgb)
            db_g = _dot_tn(dgb, cg) + _dot_nt(wx, dsgb)
            dxbc_ref[:, DI + g * NS:DI + (g + 1) * NS] = db_g
            dxbc_ref[:, DI + 1024 + g * NS:DI + 1024 + (g + 1) * NS] = dc_g
            p2w = bds * xdt_g * w_g
            red_parts.append(dy_g * cs * e_g - p2w)
            alast_parts.append(jnp.sum(p2w, axis=0, keepdims=True)
                               + e_full[L - 1:L, gs] * jnp.sum(dsg * sg, axis=0, keepdims=True))
            dxdt_parts.append(dxdt)
            dstate[g] = e_full[L - 1:L, gs] * dsg + _dot_tn(cg, edy)
            dxbc_ref[:, gs] = dxdt * dt_full[:, gs] + dy_g * d_ref[:, gs]
            dd_ref[:, gs] += jnp.sum(dy_g * xs_g, axis=0, keepdims=True)
        red = jnp.concatenate(red_parts, axis=1)
        dxdt_all = jnp.concatenate(dxdt_parts, axis=1)
        alast = jnp.concatenate(alast_parts, axis=1)
        dacum = dacum + _dot_nt(red, expand, HI)
        dalast = _dot_nt(jnp.broadcast_to(alast, (8, DI)), expand, HI)[0:1, :]
        dacum = dacum + jnp.where(_iota((L, 1), 0) == L - 1, dalast, 0.0)
        triu = (_iota((L, L), 0) <= _iota((L, L), 1)).astype(F32)
        dda = _dot(triu, dacum, HI)
        ddt = _dot_nt(dxdt_all * xs_ref[...], expand, HI) + dda * a
        dalog_ref[...] += jnp.sum(dda * dt, axis=0, keepdims=True) * a
        ddt_raw = ddt * _sigmoid(dtp)
        ddt_ref[...] = ddt_raw
        dbias_ref[...] += jnp.sum(ddt_raw, axis=0, keepdims=True)

    rev = lambda c: (nc - 1 - c, 0)
    return _pcall(
        body, name="ssd_scan_bwd", grid=(nc,),
        in_specs=[pl.BlockSpec((L, DI), rev), pl.BlockSpec((L, DI), rev),
                  pl.BlockSpec((L, 1024), lambda c: (nc - 1 - c, 2)), pl.BlockSpec((L, 1024), lambda c: (nc - 1 - c, 3)),
                  pl.BlockSpec((L, NH), rev), pl.BlockSpec((1, NH), lambda c: (0, 0)),
                  pl.BlockSpec((1, NH), lambda c: (0, 0)), pl.BlockSpec((1, DI), lambda c: (0, 0)),
                  pl.BlockSpec((1, NG, NS, GW), lambda c: (nc - 1 - c, 0, 0, 0))],
        out_specs=[pl.BlockSpec((L, CONV_CH), rev), pl.BlockSpec((L, NH), rev),
                   pl.BlockSpec((1, NH), lambda c: (0, 0)), pl.BlockSpec((1, NH), lambda c: (0, 0)),
                   pl.BlockSpec((1, DI), lambda c: (0, 0))],
        out_shape=[jax.ShapeDtypeStruct((S, CONV_CH), F32), jax.ShapeDtypeStruct((S, NH), F32),
                   jax.ShapeDtypeStruct((1, NH), F32), jax.ShapeDtypeStruct((1, NH), F32),
                   jax.ShapeDtypeStruct((1, DI), F32)],
        scratch_shapes=[pltpu.VMEM((NG, NS, GW), F32)],
        compiler_params=_cp("arbitrary"))(dy, xbc_act, xbc_act, xbc_act, dt_raw, dt_bias, a_log, d_full, states)


def gate_norm_fwd(y, z, out_norm, tm=256):
    S = y.shape[0]
    tm = min(tm, S)

    def body(y_ref, z_ref, on_ref, o_ref):
        zv = z_ref[...]
        gin = y_ref[...] * (zv * _sigmoid(zv))
        for g in range(NG):
            gs = slice(g * GW, (g + 1) * GW)
            blk = gin[:, gs]
            r = lax.rsqrt(jnp.mean(blk * blk, axis=-1, keepdims=True) + EPS)
            o_ref[:, gs] = (blk * r * on_ref[:, gs]).astype(BF16)

    return _pcall(
        body, name="gate_norm_fwd", grid=(S // tm,),
        in_specs=[pl.BlockSpec((tm, DI), lambda i: (i, 0)), pl.BlockSpec((tm, DI), lambda i: (i, 0)),
                  pl.BlockSpec((1, DI), lambda i: (0, 0))],
        out_specs=pl.BlockSpec((tm, DI), lambda i: (i, 0)),
        out_shape=jax.ShapeDtypeStruct((S, DI), BF16),
        compiler_params=_cp("parallel"))(y, z, out_norm)


def gate_norm_bwd(dgn, y, z, out_norm, tm=256):
    S = y.shape[0]
    tm = min(tm, S)

    def body(dg_ref, y_ref, z_ref, on_ref, dy_ref, dz_ref, don_ref):
        i = pl.program_id(0)

        @pl.when(i == 0)
        def _():
            don_ref[...] = jnp.zeros_like(don_ref)

        zv, yv = z_ref[...], y_ref[...]
        s = _sigmoid(zv)
        sz = zv * s
        gin = yv * sz
        for g in range(NG):
            gs = slice(g * GW, (g + 1) * GW)
            blk = gin[:, gs]
            r = lax.rsqrt(jnp.mean(blk * blk, axis=-1, keepdims=True) + EPS)
            n = blk * r
            dg = dg_ref[:, gs]
            don_ref[:, gs] += jnp.sum(dg * n, axis=0, keepdims=True)
            dn = dg * on_ref[:, gs]
            dgin = r * (dn - n * jnp.mean(dn * n, axis=-1, keepdims=True))
            dy_ref[:, gs] = dgin * sz[:, gs]
            dz_ref[:, gs] = (dgin * yv[:, gs] * (s[:, gs] * (1.0 + zv[:, gs] * (1.0 - s[:, gs])))).astype(BF16)

    return _pcall(
        body, name="gate_norm_bwd", grid=(S // tm,),
        in_specs=[pl.BlockSpec((tm, DI), lambda i: (i, 0))] * 3 + [pl.BlockSpec((1, DI), lambda i: (0, 0))],
        out_specs=[pl.BlockSpec((tm, DI), lambda i: (i, 0)), pl.BlockSpec((tm, DI), lambda i: (i, 0)),
                   pl.BlockSpec((1, DI), lambda i: (0, 0))],
        out_shape=[jax.ShapeDtypeStruct((S, DI), F32), jax.ShapeDtypeStruct((S, DI), BF16),
                   jax.ShapeDtypeStruct((1, DI), F32)],
        compiler_params=_cp("arbitrary"))(dgn, y, z, out_norm)


SB_T = 256
SB_QSCALE = 0.125
SB_DEAD = -110.0
SB_UNSEEN = -1e30


def _head_norm(xv, lo):
    sq = xv * xv
    s0 = jnp.sum(jnp.where(lo, sq, 0.0), axis=-1, keepdims=True)
    s1 = jnp.sum(jnp.where(lo, 0.0, sq), axis=-1, keepdims=True)
    return jnp.where(lo, lax.rsqrt(s0 / SBD + EPS), lax.rsqrt(s1 / SBD + EPS))


def sb_prep_fwd(qkv, qg, kg, tm=256):
    S = qkv.shape[0]
    tm = min(tm, S)

    def body(x_ref, qg_ref, kg_ref, q_ref, k_ref, v_ref):
        lo = _iota((1, LANES), 1) < SBD
        for sl in range(D // LANES):
            cs = slice(sl * LANES, (sl + 1) * LANES)
            xq = x_ref[:, cs]
            q_ref[:, cs] = ((xq * _head_norm(xq, lo) * qg_ref[...]).astype(BF16).astype(F32) * SB_QSCALE).astype(BF16)
            xk = x_ref[:, D + sl * LANES:D + (sl + 1) * LANES]
            k_ref[:, cs] = (xk * _head_norm(xk, lo) * kg_ref[...]).astype(BF16)
        v_ref[...] = x_ref[:, 2 * D:3 * D].astype(BF16)

    return _pcall(
        body, name="sb_prep_fwd", grid=(S // tm,),
        in_specs=[pl.BlockSpec((tm, 3 * D), lambda i: (i, 0)), pl.BlockSpec((1, LANES), lambda i: (0, 0)),
                  pl.BlockSpec((1, LANES), lambda i: (0, 0))],
        out_specs=[pl.BlockSpec((tm, D), lambda i: (i, 0))] * 3,
        out_shape=[jax.ShapeDtypeStruct((S, D), BF16)] * 3,
        compiler_params=_cp("parallel"))(qkv, qg, kg)


def sb_prep_bwd(dqs, dkn, dv, qkv, qg, kg, tm=256):
    S = qkv.shape[0]
    tm = min(tm, S)

    def body(dq_ref, dk_ref, dv_ref, x_ref, qg_ref, kg_ref, dx_ref, dqg_ref, dkg_ref):
        i = pl.program_id(0)

        @pl.when(i == 0)
        def _():
            dqg_ref[...] = jnp.zeros_like(dqg_ref)
            dkg_ref[...] = jnp.zeros_like(dkg_ref)

        lo = _iota((1, LANES), 1) < SBD

        def one(xv, dh, gain):
            r = _head_norm(xv, lo)
            y = xv * r
            dy = dh * gain
            t = dy * y
            m0 = jnp.sum(jnp.where(lo, t, 0.0), axis=-1, keepdims=True)
            m1 = jnp.sum(jnp.where(lo, 0.0, t), axis=-1, keepdims=True)
            dx = r * (dy - y * (jnp.where(lo, m0, m1) / SBD))
            return dx, jnp.sum(dh * y, axis=0, keepdims=True)

        for sl in range(D // LANES):
            cs = slice(sl * LANES, (sl + 1) * LANES)
            dx, dg = one(x_ref[:, cs], dq_ref[:, cs] * SB_QSCALE, qg_ref[...])
            dx_ref[:, cs] = dx.astype(BF16)
            dqg_ref[:, cs] += dg
            ks = slice(D + sl * LANES, D + (sl + 1) * LANES)
            dx, dg = one(x_ref[:, ks], dk_ref[:, cs], kg_ref[...])
            dx_ref[:, ks] = dx.astype(BF16)
            dkg_ref[:, cs] += dg
        dx_ref[:, 2 * D:3 * D] = dv_ref[...].astype(BF16)

    return _pcall(
        body, name="sb_prep_bwd", grid=(S // tm,),
        in_specs=[pl.BlockSpec((tm, D), lambda i: (i, 0))] * 3
        + [pl.BlockSpec((tm, 3 * D), lambda i: (i, 0)), pl.BlockSpec((1, LANES), lambda i: (0, 0)),
           pl.BlockSpec((1, LANES), lambda i: (0, 0))],
        out_specs=[pl.BlockSpec((tm, 3 * D), lambda i: (i, 0)), pl.BlockSpec((1, D), lambda i: (0, 0)),
                   pl.BlockSpec((1, D), lambda i: (0, 0))],
        out_shape=[jax.ShapeDtypeStruct((S, 3 * D), BF16), jax.ShapeDtypeStruct((1, D), F32),
                   jax.ShapeDtypeStruct((1, D), F32)],
        compiler_params=_cp("arbitrary"))(dqs, dkn, dv, qkv, qg, kg)


def _split_dot(x, u):
    hi = x.astype(BF16)
    lo = (x - hi.astype(F32)).astype(BF16)
    return _dot(hi, u) + _dot(lo, u)


def _sb_logits(qh, kb, valid):
    z = _dot_nt(qh, kb)
    e = jnp.exp(-jnp.abs(z))
    lp = jnp.log(1.0 + e)
    lb = jnp.minimum(z, 0.0) - lp
    l1m = jnp.where(valid, lb - z, 0.0)
    return z, e, lb, l1m


def sb_fwd(qs, kn, v):
    S = qs.shape[0]
    T = min(SB_T, S)
    nq = S // T

    def body(q_ref, k_ref, v_ref, o_ref, r_ref, oacc, rrun):
        i = pl.program_id(1)
        qb = q_ref[...]
        lo = _iota((1, LANES), 1) < SBD
        row, col = _iota((T, T), 0), _iota((T, T), 1)
        u = (row > col).astype(BF16)
        lane_blk = _iota((T, LANES), 1)
        oacc[...] = jnp.zeros_like(oacc)
        for hh in range(2):
            hm = lo if hh == 0 else jnp.logical_not(lo)
            qh = jnp.where(hm, qb, jnp.zeros_like(qb))
            rrun[...] = jnp.zeros_like(rrun)
            r_ref[hh] = jnp.full((T, LANES), SB_UNSEEN, F32)

            def live(carry):
                s, rmax = carry
                return jnp.logical_and(s <= i, rmax > SB_DEAD)

            def step(carry, hm=hm, qh=qh, hh=hh):
                s, _ = carry
                j = i - s
                off = pl.multiple_of(j * T, T)
                kb = k_ref[pl.ds(off, T), :]
                vb = v_ref[pl.ds(off, T), :]
                vb = jnp.where(hm, vb, jnp.zeros_like(vb))
                valid = (j * T + col) < (i * T + row)
                _, _, lb, l1m = _sb_logits(qh, kb, valid)
                r = rrun[...]
                aft = _split_dot(l1m, u) + r
                a = jnp.where(valid, jnp.exp(lb + aft), 0.0)
                oacc[...] += _dot(a.astype(BF16), vb)
                r_ref[hh] = jnp.where(lane_blk == j, r, r_ref[hh])
                rnew = r + jnp.sum(l1m, axis=-1, keepdims=True)
                rrun[...] = rnew
                return s + 1, jnp.max(rnew)

            lax.while_loop(live, step, (jnp.int32(0), jnp.float32(0.0)))
        o_ref[...] = oacc[...].astype(BF16)

    return _pcall(
        body, name="sb_fwd", grid=(D // LANES, nq),
        in_specs=[pl.BlockSpec((T, LANES), lambda h, i: (i, h)), pl.BlockSpec((S, LANES), lambda h, i: (0, h)),
                  pl.BlockSpec((S, LANES), lambda h, i: (0, h))],
        out_specs=[pl.BlockSpec((T, LANES), lambda h, i: (i, h)), pl.BlockSpec((2, T, LANES), lambda h, i: (h, i, 0))],
        out_shape=[jax.ShapeDtypeStruct((S, D), BF16), jax.ShapeDtypeStruct((SBH, S, LANES), F32)],
        scratch_shapes=[pltpu.VMEM((T, LANES), F32), pltpu.VMEM((T, 1), F32)],
        compiler_params=_cp("parallel", "arbitrary"))(qs, kn, v)


def sb_bwd(qs, kn, v, do, rsave):
    S = qs.shape[0]
    T = min(SB_T, S)
    nq = S // T

    def body(q_ref, k_ref, v_ref, do_ref, r_ref, dq_ref, dk_ref, dv_ref, crun):
        i = pl.program_id(1)

        @pl.when(i == 0)
        def _():
            dk_ref[...] = jnp.zeros_like(dk_ref)
            dv_ref[...] = jnp.zeros_like(dv_ref)

        qb, dob = q_ref[...], do_ref[...]
        lo = _iota((1, LANES), 1) < SBD
        row, col = _iota((T, T), 0), _iota((T, T), 1)
        u = (row > col).astype(BF16)
        u2 = (row < col).astype(BF16)
        lane_blk = _iota((T, LANES), 1)
        dq_ref[...] = jnp.zeros_like(dq_ref)
        for hh in range(2):
            hm = lo if hh == 0 else jnp.logical_not(lo)
            qh = jnp.where(hm, qb, jnp.zeros_like(qb))
            doh = jnp.where(hm, dob, jnp.zeros_like(dob))
            crun[...] = jnp.zeros_like(crun)

            def step(j, carry, hm=hm, qh=qh, doh=doh, hh=hh):
                off = pl.multiple_of(j * T, T)
                kb = k_ref[pl.ds(off, T), :]
                vb = v_ref[pl.ds(off, T), :]
                vb = jnp.where(hm, vb, jnp.zeros_like(vb))
                valid = (j * T + col) < (i * T + row)
                z, e, lb, l1m = _sb_logits(qh, kb, valid)
                r = jnp.sum(jnp.where(lane_blk == j, r_ref[hh], 0.0), axis=-1, keepdims=True)
                aft = _split_dot(l1m, u) + r
                a = jnp.where(valid, jnp.exp(lb + aft), 0.0)
                w = a * _dot_nt(doh, vb)
                cprev = crun[...]
                cw = _split_dot(w, u2) + cprev
                inv = 1.0 / (1.0 + e)
                pos = z >= 0.0
                beta = jnp.where(pos, 1.0, e) * inv
                onem = jnp.where(pos, e, 1.0) * inv
                dz = jnp.where(valid, w * onem - beta * cw, 0.0).astype(BF16)
                dq_ref[...] += _dot(dz, jnp.where(hm, kb, jnp.zeros_like(kb)))
                dk_ref[pl.ds(off, T), :] += _dot_tn(dz, qh)
                dv_ref[pl.ds(off, T), :] += _dot_tn(a.astype(BF16), doh)
                crun[...] = cprev + jnp.sum(w, axis=-1, keepdims=True)
                return carry

            col_max = jnp.max(r_ref[hh], axis=0, keepdims=True)
            seen = jnp.logical_and(col_max > SB_DEAD, _iota((1, LANES), 1) <= i)
            n_live = jnp.sum(seen.astype(jnp.int32))
            lax.fori_loop(i + 1 - n_live, i + 1, step, 0)

    return _pcall(
        body, name="sb_bwd", grid=(D // LANES, nq),
        in_specs=[pl.BlockSpec((T, LANES), lambda h, i: (i, h)), pl.BlockSpec((S, LANES), lambda h, i: (0, h)),
                  pl.BlockSpec((S, LANES), lambda h, i: (0, h)), pl.BlockSpec((T, LANES), lambda h, i: (i, h)),
                  pl.BlockSpec((2, T, LANES), lambda h, i: (h, i, 0))],
        out_specs=[pl.BlockSpec((T, LANES), lambda h, i: (i, h)), pl.BlockSpec((S, LANES), lambda h, i: (0, h)),
                   pl.BlockSpec((S, LANES), lambda h, i: (0, h))],
        out_shape=[jax.ShapeDtypeStruct((S, D), F32)] * 3,
        scratch_shapes=[pltpu.VMEM((T, 1), F32)],
        compiler_params=_cp("parallel", "arbitrary"))(qs, kn, v, do, rsave)


def loss_head(y, target, tm=512):
    S = y.shape[0]
    tm = min(tm, S)

    def body(y_ref, t_ref, ls_ref, dy_ref):
        i = pl.program_id(0)
        err = y_ref[...] - t_ref[...]
        dy_ref[...] = err * (1.0 / D)
        part = jnp.sum(err * err, axis=0, keepdims=True)

        @pl.when(i == 0)
        def _():
            ls_ref[...] = part

        @pl.when(i > 0)
        def _():
            ls_ref[...] += part

    return _pcall(
        body, name="loss_head", grid=(S // tm,),
        in_specs=[pl.BlockSpec((tm, D), lambda i: (i, 0))] * 2,
        out_specs=[pl.BlockSpec((1, D), lambda i: (0, 0)), pl.BlockSpec((tm, D), lambda i: (i, 0))],
        out_shape=[jax.ShapeDtypeStruct((1, D), F32), jax.ShapeDtypeStruct((S, D), F32)],
        compiler_params=_cp("arbitrary"))(y, target)


def _adamw_update(w, g, m, v):
    c1 = 1.0 / (1.0 - ADAM_B1 ** ADAM_STEP)
    c2 = 1.0 / (1.0 - ADAM_B2 ** ADAM_STEP)
    mn = ADAM_B1 * m + (1.0 - ADAM_B1) * g
    vn = ADAM_B2 * v + (1.0 - ADAM_B2) * (g * g)
    return -ADAM_LR * ((mn * c1) / (jnp.sqrt(vn * c2) + ADAM_EPS) + ADAM_WD * w), mn, vn


def adamw(w, g, m, v, tr=1024, name="adamw"):
    R, C = w.shape
    tr = min(tr, R)

    def body(w_ref, g_ref, m_ref, v_ref, d_ref, mo_ref, vo_ref):
        d_ref[...], mo_ref[...], vo_ref[...] = _adamw_update(w_ref[...], g_ref[...], m_ref[...], v_ref[...])

    spec = pl.BlockSpec((tr, C), lambda i: (i, 0))
    return _pcall(
        body, name=name, grid=(R // tr,), in_specs=[spec] * 4, out_specs=[spec] * 3,
        out_shape=[jax.ShapeDtypeStruct((R, C), F32)] * 3,
        compiler_params=_cp("parallel"))(w, g, m, v)


def adamw_halves(w, g_mine, g_other, m, v, tr=1024, name="adamw_halves"):
    R, C = w.shape
    H = R // 2
    tr = min(tr, H)
    n_i = H // tr
    where = lax.axis_index("c").astype(jnp.int32).reshape(1)

    def body(s_ref, w_ref, gm_ref, go_ref, m_ref, v_ref, g_ref, d_ref, mo_ref, vo_ref):
        g = jnp.where(pl.program_id(0) == s_ref[0], gm_ref[...], go_ref[...])
        g_ref[...] = g
        d_ref[...], mo_ref[...], vo_ref[...] = _adamw_update(w_ref[...], g, m_ref[...], v_ref[...])

    full = pl.BlockSpec((tr, C), lambda h, i, s: (h * n_i + i, 0))
    half = pl.BlockSpec((tr, C), lambda h, i, s: (i, 0))
    return _pcall(
        body, name=name,
        grid_spec=pltpu.PrefetchScalarGridSpec(
            num_scalar_prefetch=1, grid=(2, n_i), in_specs=[full, half, half, full, full], out_specs=[full] * 4),
        out_shape=[jax.ShapeDtypeStruct((R, C), F32)] * 4,
        compiler_params=_cp("parallel", "parallel"))(where, w, g_mine, g_other, m, v)


def pair_sum(gstacks, halves, tr=1024):
    c = lax.axis_index("c")
    me = 2 * lax.axis_index("x") + lax.axis_index("y")
    where = jnp.stack([c, me]).astype(jnp.int32)
    outs = []
    for g, xh in zip(gstacks, halves):
        _, H, C = xh.shape
        t = min(tr, H)
        n_i = H // t

        def body(s_ref, g_ref, x_ref, qb_ref, own_ref):
            j = pl.program_id(1)
            q = g_ref[0] + x_ref[0]
            qb_ref[0] = q.astype(BF16)

            @pl.when(j == s_ref[1])
            def _():
                own_ref[...] = q

        outs.append(_pcall(
            body, name="pair_sum",
            grid_spec=pltpu.PrefetchScalarGridSpec(
                num_scalar_prefetch=1, grid=(n_i, N_CHIPS),
                in_specs=[pl.BlockSpec((1, t, C), lambda i, j, s, n_i=n_i: (j, s[0] * n_i + i, 0)),
                          pl.BlockSpec((1, t, C), lambda i, j, s: (j, i, 0))],
                out_specs=[pl.BlockSpec((1, t, C), lambda i, j, s: (j, i, 0)),
                           pl.BlockSpec((t, C), lambda i, j, s: (i, 0))]),
            out_shape=[jax.ShapeDtypeStruct((N_CHIPS, H, C), BF16), jax.ShapeDtypeStruct((H, C), F32)],
            compiler_params=_cp("parallel", "arbitrary"))(where, g, xh))
    return [o[0] for o in outs], [o[1] for o in outs]


def chip_sum(owns, recvs, tr=1024):
    outs = []
    for own, rc in zip(owns, recvs):
        H, C = own.shape
        t = min(tr, H)

        def body(o_ref, r_ref, t_ref):
            t_ref[...] = ((o_ref[...] + r_ref[0].astype(F32)) + r_ref[1].astype(F32)) + r_ref[2].astype(F32)

        outs.append(_pcall(
            body, name="chip_sum", grid=(H // t,),
            in_specs=[pl.BlockSpec((t, C), lambda i: (i, 0)), pl.BlockSpec((3, t, C), lambda i: (0, i, 0))],
            out_specs=pl.BlockSpec((t, C), lambda i: (i, 0)),
            out_shape=jax.ShapeDtypeStruct((H, C), F32), compiler_params=_cp("parallel"))(own, rc))
    return outs


MESH = pl.DeviceIdType.MESH
ANY = pl.BlockSpec(memory_space=pl.ANY)
SPLIT_MIN_BYTES = 1 << 20


def _other_chips(x, y):
    return [(1 - x, y), (x, 1 - y), (1 - x, 1 - y)]


def _half_rows(rows, who):
    half = rows // 2
    return pl.ds(pl.multiple_of(who * half, 16), half)


def gather_all(shards):
    n = len(shards)
    rows = [s.shape[0] for s in shards]
    split = [r % 32 == 0 and s.size * s.dtype.itemsize >= SPLIT_MIN_BYTES for r, s in zip(rows, shards)]

    def body(*refs):
        ins, outs = refs[:n], refs[n:2 * n]
        ici_send, ici_recv, d2d_send, d2d_recv = refs[2 * n:]
        x, y, c = lax.axis_index("x"), lax.axis_index("y"), lax.axis_index("c")
        me, sib, chips = 2 * x + y, (x, y, 1 - c), _other_chips(x, y)

        def part(k, who):
            return _half_rows(rows[k], who) if split[k] else pl.ds(0, rows[k])

        def ici(k, r, block):
            px, py = chips[r]
            return pltpu.make_async_remote_copy(
                src_ref=ins[k].at[part(k, c)], dst_ref=outs[k].at[block, part(k, c)],
                send_sem=ici_send.at[3 * k + r], recv_sem=ici_recv.at[3 * k + r],
                device_id=(px, py, c), device_id_type=MESH)

        def d2d(k, r, who):
            px, py = chips[r]
            blk = outs[k].at[2 * px + py, part(k, who)]
            return pltpu.make_async_remote_copy(
                src_ref=blk, dst_ref=blk, send_sem=d2d_send.at[3 * k + r], recv_sem=d2d_recv.at[3 * k + r],
                device_id=sib, device_id_type=MESH)

        sends = [ici(k, r, me) for k in range(n) for r in range(3)]
        for cp in sends:
            cp.start()
        for r in range(3):
            px, py = chips[r]
            for k in range(n):
                ici(k, r, 2 * px + py).wait_recv()
                if split[k]:
                    fwd = d2d(k, r, c)
                    fwd.start()
                    sends.append(fwd)
        for r in range(3):
            for k in range(n):
                if split[k]:
                    d2d(k, r, 1 - c).wait_recv()
        for cp in sends:
            cp.wait_send()

    return _pcall(
        body, name="gather_all", in_specs=[ANY] * n, out_specs=[ANY] * n,
        out_shape=[jax.ShapeDtypeStruct((N_CHIPS,) + s.shape, s.dtype) for s in shards],
        scratch_shapes=[pltpu.SemaphoreType.DMA((3 * n,))] * 4)(*shards)


def swap_halves(gstacks):
    n = len(gstacks)

    def body(*refs):
        ins, outs, send_sems, recv_sems = refs[:n], refs[n:2 * n], refs[2 * n], refs[2 * n + 1]
        x, y, c = lax.axis_index("x"), lax.axis_index("y"), lax.axis_index("c")
        copies = [pltpu.make_async_remote_copy(
            src_ref=ins[k].at[:, _half_rows(ins[k].shape[1], 1 - c)], dst_ref=outs[k],
            send_sem=send_sems.at[k], recv_sem=recv_sems.at[k], device_id=(x, y, 1 - c), device_id_type=MESH)
            for k in range(n)]
        for cp in copies:
            cp.start()
        for cp in copies:
            cp.wait()

    return _pcall(
        body, name="swap_halves", in_specs=[ANY] * n, out_specs=[ANY] * n,
        out_shape=[jax.ShapeDtypeStruct((g.shape[0], g.shape[1] // 2, g.shape[2]), g.dtype) for g in gstacks],
        scratch_shapes=[pltpu.SemaphoreType.DMA((n,)), pltpu.SemaphoreType.DMA((n,))])(*gstacks)


def scatter_chips(stacks):
    n = len(stacks)

    def body(*refs):
        ins, outs, send_sems, recv_sems = refs[:n], refs[n:2 * n], refs[2 * n], refs[2 * n + 1]
        x, y, c = lax.axis_index("x"), lax.axis_index("y"), lax.axis_index("c")
        copies = [pltpu.make_async_remote_copy(
            src_ref=ins[k].at[2 * px + py], dst_ref=outs[k].at[r], send_sem=send_sems.at[3 * k + r],
            recv_sem=recv_sems.at[3 * k + r], device_id=(px, py, c), device_id_type=MESH)
            for k in range(n) for r, (px, py) in enumerate(_other_chips(x, y))]
        for cp in copies:
            cp.start()
        for cp in copies:
            cp.wait()

    return _pcall(
        body, name="scatter_chips", in_specs=[ANY] * n, out_specs=[ANY] * n,
        out_shape=[jax.ShapeDtypeStruct((3,) + s.shape[1:], s.dtype) for s in stacks],
        scratch_shapes=[pltpu.SemaphoreType.DMA((3 * n,)), pltpu.SemaphoreType.DMA((3 * n,))])(*stacks)


def swap_totals(totals):
    n = len(totals)

    def body(*refs):
        ins, outs, send_sems, recv_sems = refs[:n], refs[n:2 * n], refs[2 * n], refs[2 * n + 1]
        x, y, c = lax.axis_index("x"), lax.axis_index("y"), lax.axis_index("c")
        copies = [pltpu.make_async_remote_copy(
            src_ref=ins[k], dst_ref=outs[k], send_sem=send_sems.at[k], recv_sem=recv_sems.at[k],
            device_id=(x, y, 1 - c), device_id_type=MESH) for k in range(n)]
        for cp in copies:
            cp.start()
        for cp in copies:
            cp.wait()

    return _pcall(
        body, name="swap_totals", in_specs=[ANY] * n, out_specs=[ANY] * n,
        out_shape=[jax.ShapeDtypeStruct(t.shape, t.dtype) for t in totals],
        scratch_shapes=[pltpu.SemaphoreType.DMA((n,)), pltpu.SemaphoreType.DMA((n,))])(*totals)


def reduce_scatter(gstacks):
    halves = swap_halves(gstacks)
    payload, own = pair_sum(gstacks, halves)
    recv = scatter_chips(payload)
    mine = chip_sum(own, recv)
    return mine, swap_totals(mine)


def allreduce_small(vec):
    R = vec.shape[0]

    def body(in_ref, out_ref, buf, send_sems, recv_sems):
        x, y, c = lax.axis_index("x"), lax.axis_index("y"), lax.axis_index("c")
        me = 4 * x + 2 * y + c
        buf[me] = in_ref[...]
        copies = []
        for k in range(1, 8):
            peer = (x ^ (k >> 2), y ^ ((k >> 1) & 1), c ^ (k & 1))
            copies.append(pltpu.make_async_remote_copy(
                src_ref=in_ref, dst_ref=buf.at[me], send_sem=send_sems.at[k - 1], recv_sem=recv_sems.at[k - 1],
                device_id=peer, device_id_type=MESH))
        for cp in copies:
            cp.start()
        for cp in copies:
            cp.wait()
        acc = buf[0]
        for d in range(1, 8):
            acc = acc + buf[d]
        out_ref[...] = acc

    vm = pl.BlockSpec(memory_space=pltpu.VMEM)
    return _pcall(
        body, name="allreduce_small", in_specs=[vm], out_specs=vm,
        out_shape=jax.ShapeDtypeStruct((R, LANES), F32),
        scratch_shapes=[pltpu.VMEM((8, R, LANES), F32), pltpu.SemaphoreType.DMA((7,)), pltpu.SemaphoreType.DMA((7,))])(vec)


SHARDED = [("pool_in", 1, "bf16"), ("pool_group", 2, "bf16"), ("pool_scale", 1, "f32"), ("ssd_in", 2, "bf16"),
           ("ssd_conv_w", 2, "f32"), ("ssd_out", 1, "bf16"), ("sb_qkv", 2, "bf16"), ("sb_out", 1, "bf16"),
           ("ffn_gate", 2, "bf16"), ("ffn_up", 2, "bf16"), ("ffn_down", 1, "bf16")]
REPLICATED = ["mix_norm", "ssd_conv_b", "ssd_dt_bias", "ssd_a_log", "ssd_d", "ssd_out_norm", "sb_q_norm",
              "sb_k_norm", "ffn_norm"]
WEIGHT_ORDER = ["mix_norm", "pool_in", "pool_group", "pool_scale", "ssd_in", "ssd_conv_w", "ssd_conv_b",
                "ssd_dt_bias", "ssd_a_log", "ssd_d", "ssd_out_norm", "ssd_out", "sb_qkv", "sb_q_norm", "sb_k_norm",
                "sb_out", "ffn_norm", "ffn_gate", "ffn_up", "ffn_down"]
ROW_PAD = 1024


def _piece_rows(n, mult):
    rows = -(-n // LANES)
    return -(-rows // mult) * mult


def _as_rows(a, mult):
    flat = a.reshape(-1)
    rows = _piece_rows(flat.shape[0], mult)
    if rows * LANES != flat.shape[0]:
        flat = jnp.pad(flat, (0, rows * LANES - flat.shape[0]))
    return flat.reshape(rows, LANES)


def _pack(arrs, mult=8, row_pad=ROW_PAD):
    parts = [_as_rows(a, mult) for a in arrs]
    rows = sum(p.shape[0] for p in parts)
    pad = -rows % row_pad
    if pad:
        parts.append(jnp.zeros((pad, LANES), parts[0].dtype))
    return jnp.concatenate(parts, axis=0)


def _unpack(packed, shapes, mult=8, lead=()):
    out, off = [], 0
    for s in shapes:
        n = math.prod(s)
        rows = _piece_rows(n, mult)
        piece = packed[..., off:off + rows, :].reshape(lead + (rows * LANES,))
        out.append(piece[..., :n].reshape(lead + tuple(s)))
        off += rows
    return out


def _gather_weights(shards):
    kinds = (("bf16", BF16, 16, ROW_PAD), ("f32", F32, 8, 8))
    groups = [[(n, ax) for n, ax, k in SHARDED if k == kind] for kind, _, _, _ in kinds]
    gathered = gather_all([_pack([shards[n].astype(dtype) for n, _ in group], mult, row_pad)
                           for group, (_, dtype, mult, row_pad) in zip(groups, kinds)])
    full = {}
    me = 2 * lax.axis_index("x") + lax.axis_index("y")
    for group, got, (_, _, mult, _) in zip(groups, gathered, kinds):
        pieces = _unpack(got, [shards[n].shape for n, _ in group], mult, lead=(N_CHIPS,))
        for (n, ax), p in zip(group, pieces):
            own = shards[n].astype(p.dtype)
            full[n] = jnp.concatenate([jnp.where(me == j, own, p[j]) for j in range(N_CHIPS)], axis=ax)
    return full


def _split_shards(full, axis):
    return jnp.stack(jnp.split(full, N_CHIPS, axis=axis))


def _ffn_fwd(x, gain, wg, wu, wd):
    h = rmsnorm_fwd(x, gain, name="ffn_norm_fwd")
    a, b, hid = ffn_up(h, wg, wu)
    xo = linear([(hid, wd, "nn")], res=x, name="ffn_down")
    return xo, (x, h, a, b, hid)


def _ffn_bwd(dout, saved, gain, wg, wu, wd):
    x, h, a, b, hid = saved
    da, db = ffn_bwd_hidden(dout, wd, a, b)
    (dwd,) = wgrad(hid, [dout], tk=1408, name="ffn_dwd")
    dwg, dwu = wgrad(h, [da, db], tn=1408, tm=512, name="ffn_dwgu")
    dh = linear([(da, wg, "nt"), (db, wu, "nt")], tm=256, name="ffn_dh")
    dx, dgain = rmsnorm_bwd(x, gain, dh, dout, name="ffn_norm_bwd")
    return dx, dgain, dwg, dwu, dwd


def _pool_layer_fwd(x, gain, w_in, wgrp, scale):
    h = rmsnorm_fwd(x, gain, name="pool_norm_fwd")
    u = linear([(h, w_in, "nn")], name="pool_in")
    xo, p = pool_fwd(u, wgrp, scale, x)
    return xo, (x, h, p)


def _pool_layer_bwd(dout, saved, gain, w_in, wgrp, scale):
    x, h, p = saved
    dp, dwgrp, dscale = pool_bwd_group(dout, p, wgrp, scale)
    du = pool_bwd_window(dp)
    (dw_in,) = wgrad(h, [du], name="pool_dwin")
    dh = linear([(du, w_in, "nt")], name="pool_dh")
    dx, dgain = rmsnorm_bwd(x, gain, dh, dout, name="pool_norm_bwd")
    return dx, dgain, dw_in, dwgrp, dscale


def _ssd_layer_fwd(x, gain, w_z, w_xbc, w_dt, conv_w, conv_b, dt_bias, a_log, d_full, out_norm, w_out):
    h = rmsnorm_fwd(x, gain, name="ssd_norm_fwd")
    z = linear([(h, w_z, "nn")], name="ssd_in_z")
    xbc = linear([(h, w_xbc, "nn")], tn=2048, name="ssd_in_xbc")
    dt_raw = linear([(h, w_dt, "nn")], name="ssd_in_dt")
    act = conv_fwd(xbc, conv_w, conv_b)
    y, states = ssd_scan_fwd(act, dt_raw, dt_bias, a_log, d_full)
    gn = gate_norm_fwd(y, z, out_norm)
    xo = linear([(gn, w_out, "nn")], res=x, name="ssd_out")
    return xo, (x, h, z, xbc, dt_raw, act, y, states, gn)


def _ssd_layer_bwd(dout, saved, gain, w_z, w_xbc, w_dt, conv_w, conv_b, dt_bias, a_log, d_full, out_norm, w_out):
    x, h, z, xbc, dt_raw, act, y, states, gn = saved
    dgn = linear([(dout, w_out, "nt")], name="ssd_dgn")
    (dw_out,) = wgrad(gn, [dout], name="ssd_dwout")
    dy, dz, dout_norm = gate_norm_bwd(dgn, y, z, out_norm)
    dact, ddt_raw, dbias, dalog, dd_full = ssd_scan_bwd(dy, act, dt_raw, dt_bias, a_log, d_full, states)
    dpre, dconv_w8, dconv_b = conv_bwd_pre(dact, xbc, conv_w, conv_b)
    dxbc = conv_bwd_input(dpre, conv_w)
    ddt_b = ddt_raw.astype(BF16)
    (dw_z,) = wgrad(h, [dz], name="ssd_dwz")
    (dw_xbc,) = wgrad(h, [dxbc], tn=2048, name="ssd_dwxbc")
    (dw_dt,) = wgrad(h, [ddt_b], name="ssd_dwdt")
    dh = linear([(dz, w_z, "nt"), (dxbc, w_xbc, "nt"), (ddt_b, w_dt, "nt")], tm=256, name="ssd_dh")
    dx, dgain = rmsnorm_bwd(x, gain, dh, dout, name="ssd_norm_bwd")
    dw_in = jnp.concatenate([dw_z, dw_xbc, dw_dt], axis=1)
    dd = dd_full.reshape(NH, HP).sum(axis=1).reshape(1, NH)
    return dx, dgain, dw_in, dconv_w8[:4], dconv_b, dbias, dalog, dd, dout_norm, dw_out


def _sb_layer_fwd(x, gain, w_qkv, qg, kg, w_out):
    h = rmsnorm_fwd(x, gain, name="sb_norm_fwd")
    qkv = linear([(h, w_qkv, "nn")], tn=1024, name="sb_qkv")
    qs, kn, v = sb_prep_fwd(qkv, qg, kg)
    o, rsave = sb_fwd(qs, kn, v)
    xo = linear([(o, w_out, "nn")], res=x, name="sb_out")
    return xo, (x, h, qkv, qs, kn, v, o, rsave)


def _sb_layer_bwd(dout, saved, gain, w_qkv, qg, kg, w_out):
    x, h, qkv, qs, kn, v, o, rsave = saved
    do = linear([(dout, w_out, "nt")], out_dtype=BF16, name="sb_do")
    (dw_out,) = wgrad(o, [dout], name="sb_dwout")
    dqs, dkn, dv = sb_bwd(qs, kn, v, do, rsave)
    dqkv, dqg, dkg = sb_prep_bwd(dqs, dkn, dv, qkv, qg, kg)
    (dw_qkv,) = wgrad(h, [dqkv], tn=1024, name="sb_dwqkv")
    dh = linear([(dqkv, w_qkv, "nt")], name="sb_dh")
    dx, dgain = rmsnorm_bwd(x, gain, dh, dout, name="sb_norm_bwd")
    dqg = dqg.reshape(SBH, SBD).sum(axis=0).reshape(1, SBD)
    dkg = dkg.reshape(SBH, SBD).sum(axis=0).reshape(1, SBD)
    return dx, dgain, dw_qkv, dqg, dkg, dw_out


def _local_step(x, target, full, rep):
    S = x.shape[0]
    d_full = jnp.repeat(rep["ssd_d"][0], HP).reshape(1, DI)
    qg = jnp.tile(rep["sb_q_norm"][0], 2).reshape(1, LANES)
    kg = jnp.tile(rep["sb_k_norm"][0], 2).reshape(1, LANES)
    ssd_in = full["ssd_in"][0]
    w_z, w_xbc, w_dt = ssd_in[:, :DI], ssd_in[:, DI:DI + CONV_CH], ssd_in[:, DI + CONV_CH:]
    conv_w = full["ssd_conv_w"][0]
    conv_b = rep["ssd_conv_b"]
    pool_scale = full["pool_scale"]

    def mixer_args(i):
        kind, j = i % 3, i // 3
        if kind == 0:
            return (full["pool_in"][j], full["pool_group"][j], pool_scale[j:j + 1])
        if kind == 1:
            return (w_z, w_xbc, w_dt, conv_w, conv_b, rep["ssd_dt_bias"], rep["ssd_a_log"], d_full,
                    rep["ssd_out_norm"], full["ssd_out"][0])
        return (full["sb_qkv"][0], qg, kg, full["sb_out"][0])

    fwd = (_pool_layer_fwd, _ssd_layer_fwd, _sb_layer_fwd)
    bwd = (_pool_layer_bwd, _ssd_layer_bwd, _sb_layer_bwd)
    saved = []
    for i in range(DEPTH):
        x, sm = fwd[i % 3](x, rep["mix_norm"][i], *mixer_args(i))
        x, sf = _ffn_fwd(x, rep["ffn_norm"][i], full["ffn_gate"][i], full["ffn_up"][i], full["ffn_down"][i])
        saved.append((sm, sf))

    colsq, dx = loss_head(x, target)
    loss = 0.5 * jnp.sum(colsq) / D

    g = {n: [None] * DEPTH for n in ("mix_norm", "ffn_norm", "ffn_gate", "ffn_up", "ffn_down")}
    g["pool_in"], g["pool_group"], g["pool_scale"] = [None] * 2, [None] * 2, [None] * 2
    for i in reversed(range(DEPTH)):
        sm, sf = saved[i]
        dx, g["ffn_norm"][i], g["ffn_gate"][i], g["ffn_up"][i], g["ffn_down"][i] = _ffn_bwd(
            dx, sf, rep["ffn_norm"][i], full["ffn_gate"][i], full["ffn_up"][i], full["ffn_down"][i])
        kind, j = i % 3, i // 3
        res = bwd[kind](dx, sm, rep["mix_norm"][i], *mixer_args(i))
        dx, g["mix_norm"][i] = res[0], res[1]
        if kind == 0:
            g["pool_in"][j], g["pool_group"][j], g["pool_scale"][j] = res[2:]
        elif kind == 1:
            dw_in, dconv_w, dconv_b, dbias, dalog, dd, don, dw_out = res[2:]
            g.update(ssd_in=dw_in[None], ssd_conv_w=dconv_w[None], ssd_conv_b=dconv_b, ssd_dt_bias=dbias,
                     ssd_a_log=dalog, ssd_d=dd, ssd_out_norm=don, ssd_out=dw_out[None])
        else:
            dw_qkv, dqg, dkg, dw_out = res[2:]
            g.update(sb_qkv=dw_qkv[None], sb_q_norm=dqg, sb_k_norm=dkg, sb_out=dw_out[None])
    for n in ("mix_norm", "ffn_norm", "pool_scale"):
        g[n] = jnp.concatenate(g[n], axis=0)
    for n in ("ffn_gate", "ffn_up", "ffn_down", "pool_in", "pool_group"):
        g[n] = jnp.stack(g[n])
    return loss, dx, g


def kernel(x, mix_norm, pool_in, pool_group, pool_scale, ssd_in, ssd_conv_w, ssd_conv_b, ssd_dt_bias, ssd_a_log, ssd_d, ssd_out_norm, ssd_out, sb_qkv, sb_q_norm, sb_k_norm, sb_out, ffn_norm, ffn_gate, ffn_up, ffn_down, loss_target, m_mix_norm, m_pool_in, m_pool_group, m_pool_scale, m_ssd_in, m_ssd_conv_w, m_ssd_conv_b, m_ssd_dt_bias, m_ssd_a_log, m_ssd_d, m_ssd_out_norm, m_ssd_out, m_sb_qkv, m_sb_q_norm, m_sb_k_norm, m_sb_out, m_ffn_norm, m_ffn_gate, m_ffn_up, m_ffn_down, v_mix_norm, v_pool_in, v_pool_group, v_pool_scale, v_ssd_in, v_ssd_conv_w, v_ssd_conv_b, v_ssd_dt_bias, v_ssd_a_log, v_ssd_d, v_ssd_out_norm, v_ssd_out, v_sb_qkv, v_sb_q_norm, v_sb_k_norm, v_sb_out, v_ffn_norm, v_ffn_gate, v_ffn_up, v_ffn_down):
    given = dict(locals())
    w = {n: given[n] for n in WEIGHT_ORDER}
    m = {n: given["m_" + n] for n in WEIGHT_ORDER}
    v = {n: given["v_" + n] for n in WEIGHT_ORDER}
    sharded_names = [s[0] for s in SHARDED]

    full = _gather_weights(w)
    rep = {n: w[n] for n in REPLICATED}

    loss, dx, g = _local_step(x[0], loss_target[0], full, rep)
    loss = lax.psum(loss, ("x", "y", "c"))

    gstack = jnp.stack([_pack([_split_shards(g[n], ax)[j] for n, ax, _ in SHARDED]) for j in range(N_CHIPS)])
    (g_mine,), (g_other,) = reduce_scatter([gstack])
    shard_shapes = [w[n].shape for n in sharded_names]
    gs, ds, ms, vs = adamw_halves(_pack([w[n] for n in sharded_names]), g_mine, g_other,
                                  _pack([m[n] for n in sharded_names]), _pack([v[n] for n in sharded_names]),
                                  name="adamw_sharded")
    out = {}
    for key, flat in (("g", gs), ("d", ds), ("m", ms), ("v", vs)):
        for n, a in zip(sharded_names, _unpack(flat, shard_shapes)):
            out[key, n] = a

    rep_shapes = [w[n].shape for n in REPLICATED]
    pack_small = functools.partial(_pack, row_pad=8)
    gsum = allreduce_small(pack_small([g[n] for n in REPLICATED]))
    gs = gsum
    ds, ms, vs = adamw(pack_small([w[n] for n in REPLICATED]), gsum, pack_small([m[n] for n in REPLICATED]),
                       pack_small([v[n] for n in REPLICATED]), name="adamw_replicated")
    for key, flat in (("g", gs), ("d", ds), ("m", ms), ("v", vs)):
        for n, a in zip(REPLICATED, _unpack(flat, rep_shapes)):
            out[key, n] = a

    return (loss, dx[None], *[out["g", n] for n in WEIGHT_ORDER], *[out["d", n] for n in WEIGHT_ORDER],
            *[out["m", n] for n in WEIGHT_ORDER], *[out["v", n] for n in WEIGHT_ORDER])
```

```python
import functools
import math

import jax
import jax.numpy as jnp
from jax import lax
from jax.experimental import pallas as pl
from jax.experimental.pallas import tpu as pltpu

F32 = jnp.float32
BF16 = jnp.bfloat16
HI = lax.Precision.HIGHEST

D = 1024
DEPTH = 4
EPS = 1e-6
POOL_WINDOWS = (2, 4, 8, 16)
PG = 256
DI = 2048
NH = 32
HP = 64
NG = 8
NS = 128
GW = 256
CH = 256
CONV_CH = 4096
SSD_IN = 6176
SBH = 16
SBD = 64
FH = 2816
N_CHIPS = 4
LANES = 128

ADAM_LR = 0.001
ADAM_B1 = 0.9
ADAM_B2 = 0.999
ADAM_EPS = 1e-08
ADAM_WD = 0.01
ADAM_STEP = 10

VMEM_LIMIT = 56 * 1024 * 1024


def _pcall(body, **kw):
    return pl.pallas_call(body, **kw)


def _cp(*sem):
    return pltpu.CompilerParams(dimension_semantics=sem, vmem_limit_bytes=VMEM_LIMIT)


def _dot(a, b, prec=None):
    return lax.dot_general(a, b, (((1,), (0,)), ((), ())), precision=prec, preferred_element_type=F32)


def _dot_nt(a, b, prec=None):
    return lax.dot_general(a, b, (((1,), (1,)), ((), ())), precision=prec, preferred_element_type=F32)


def _dot_tn(a, b, prec=None):
    return lax.dot_general(a, b, (((0,), (0,)), ((), ())), precision=prec, preferred_element_type=F32)


def _sigmoid(x):
    return 1.0 / (1.0 + jnp.exp(-x))


def _iota(shape, axis):
    return lax.broadcasted_iota(jnp.int32, shape, axis)


def linear(pairs, res=None, out_dtype=F32, tm=512, tn=None, name="linear"):
    M = pairs[0][0].shape[0]
    N = pairs[0][1].shape[1] if pairs[0][2] == "nn" else pairs[0][1].shape[0]
    tm = min(tm, M)
    tn = N if tn is None else min(tn, N)
    n_pairs = len(pairs)
    modes = [p[2] for p in pairs]

    def body(*refs):
        acc = None
        for k in range(n_pairs):
            a = refs[2 * k][...].astype(BF16)
            w = refs[2 * k + 1][...]
            t = _dot(a, w) if modes[k] == "nn" else _dot_nt(a, w)
            acc = t if acc is None else acc + t
        if res is not None:
            acc = acc + refs[2 * n_pairs][...]
        refs[-1][...] = acc.astype(out_dtype)

    in_specs, args = [], []
    for a, w, mode in pairs:
        K = a.shape[1]
        in_specs.append(pl.BlockSpec((tm, K), lambda j, i: (i, 0)))
        if mode == "nn":
            in_specs.append(pl.BlockSpec((K, tn), lambda j, i: (0, j)))
        else:
            in_specs.append(pl.BlockSpec((tn, K), lambda j, i: (j, 0)))
        args += [a, w]
    if res is not None:
        in_specs.append(pl.BlockSpec((tm, tn), lambda j, i: (i, j)))
        args.append(res)
    return _pcall(
        body, name=name, grid=(N // tn, M // tm), in_specs=in_specs,
        out_specs=pl.BlockSpec((tm, tn), lambda j, i: (i, j)),
        out_shape=jax.ShapeDtypeStruct((M, N), out_dtype),
        compiler_params=_cp("parallel", "arbitrary"))(*args)


def wgrad(a, gs, tk=1024, tn=None, tm=1024, name="wgrad"):
    M, Ka = a.shape
    N = gs[0].shape[1]
    tk, tm = min(tk, Ka), min(tm, M)
    tn = N if tn is None else min(tn, N)
    n_g = len(gs)

    def body(*refs):
        a_ref, g_refs, o_refs = refs[0], refs[1:1 + n_g], refs[1 + n_g:]
        m = pl.program_id(2)
        at = a_ref[...].astype(BF16)
        for g_ref, o_ref in zip(g_refs, o_refs):
            t = _dot_tn(at, g_ref[...].astype(BF16))

            @pl.when(m == 0)
            def _():
                o_ref[...] = t

            @pl.when(m > 0)
            def _():
                o_ref[...] += t

    out = _pcall(
        body, name=name, grid=(Ka // tk, N // tn, M // tm),
        in_specs=[pl.BlockSpec((tm, tk), lambda k, j, m: (m, k))]
        + [pl.BlockSpec((tm, tn), lambda k, j, m: (m, j))] * n_g,
        out_specs=[pl.BlockSpec((tk, tn), lambda k, j, m: (k, j))] * n_g,
        out_shape=[jax.ShapeDtypeStruct((Ka, N), F32)] * n_g,
        compiler_params=_cp("parallel", "parallel", "arbitrary"))(a, *gs)
    return out


def rmsnorm_fwd(x, gain, tm=512, name="rmsnorm_fwd"):
    S, Dm = x.shape
    tm = min(tm, S)

    def body(x_ref, g_ref, o_ref):
        xv = x_ref[...]
        r = lax.rsqrt(jnp.mean(xv * xv, axis=-1, keepdims=True) + EPS)
        o_ref[...] = (xv * r * g_ref[...]).astype(BF16)

    return _pcall(
        body, name=name, grid=(S // tm,),
        in_specs=[pl.BlockSpec((tm, Dm), lambda i: (i, 0)), pl.BlockSpec((1, Dm), lambda i: (0, 0))],
        out_specs=pl.BlockSpec((tm, Dm), lambda i: (i, 0)),
        out_shape=jax.ShapeDtypeStruct((S, Dm), BF16),
        compiler_params=_cp("parallel"))(x, gain.reshape(1, Dm))


def rmsnorm_bwd(x, gain, dh, dres, tm=512, name="rmsnorm_bwd"):
    S, Dm = x.shape
    tm = min(tm, S)

    def body(x_ref, g_ref, dh_ref, dr_ref, dx_ref, dg_ref):
        i = pl.program_id(0)
        xv = x_ref[...]
        r = lax.rsqrt(jnp.mean(xv * xv, axis=-1, keepdims=True) + EPS)
        y = xv * r
        dhv = dh_ref[...]
        dy = dhv * g_ref[...]
        dx_ref[...] = dr_ref[...] + r * (dy - y * jnp.mean(dy * y, axis=-1, keepdims=True))
        part = jnp.sum(dhv * y, axis=0, keepdims=True)

        @pl.when(i == 0)
        def _():
            dg_ref[...] = part

        @pl.when(i > 0)
        def _():
            dg_ref[...] += part

    return _pcall(
        body, name=name, grid=(S // tm,),
        in_specs=[pl.BlockSpec((tm, Dm), lambda i: (i, 0)), pl.BlockSpec((1, Dm), lambda i: (0, 0)),
                  pl.BlockSpec((tm, Dm), lambda i: (i, 0)), pl.BlockSpec((tm, Dm), lambda i: (i, 0))],
        out_specs=[pl.BlockSpec((tm, Dm), lambda i: (i, 0)), pl.BlockSpec((1, Dm), lambda i: (0, 0))],
        out_shape=[jax.ShapeDtypeStruct((S, Dm), F32), jax.ShapeDtypeStruct((1, Dm), F32)],
        compiler_params=_cp("arbitrary"))(x, gain.reshape(1, Dm), dh, dres)


FS = FH // N_CHIPS


def ffn_up(h, wg4, wu4, layer, tm=512):
    S = h.shape[0]
    tm = min(tm, S)

    def body(h_ref, wg_ref, wu_ref, a_ref, b_ref, hid_ref):
        hv = h_ref[...]
        a = _dot(hv, wg_ref[0])
        b = _dot(hv, wu_ref[0])
        a_ref[0] = a
        b_ref[0] = b
        hid_ref[0] = (a * _sigmoid(a) * b).astype(BF16)

    wspec = pl.BlockSpec((1, D, FS), lambda j, i: (j, layer, 0))
    aspec = pl.BlockSpec((1, tm, FS), lambda j, i: (j, i, 0))
    return _pcall(
        body, name="ffn_up", grid=(N_CHIPS, S // tm),
        in_specs=[pl.BlockSpec((tm, D), lambda j, i: (i, 0)), wspec, wspec], out_specs=[aspec] * 3,
        out_shape=[jax.ShapeDtypeStruct((N_CHIPS, S, FS), F32), jax.ShapeDtypeStruct((N_CHIPS, S, FS), F32),
                   jax.ShapeDtypeStruct((N_CHIPS, S, FS), BF16)],
        compiler_params=_cp("parallel", "arbitrary"))(h, wg4, wu4)


def ffn_down(hid4, wd4, layer, x, tm=512):
    S = x.shape[0]
    tm = min(tm, S)

    def body(hid_ref, wd_ref, x_ref, o_ref):
        acc = x_ref[...]
        for j in range(N_CHIPS):
            acc = acc + _dot(hid_ref[j], wd_ref[j])
        o_ref[...] = acc

    return _pcall(
        body, name="ffn_down", grid=(S // tm,),
        in_specs=[pl.BlockSpec((N_CHIPS, tm, FS), lambda i: (0, i, 0)),
                  pl.BlockSpec((N_CHIPS, FS, D), lambda i: (0, layer, 0)), pl.BlockSpec((tm, D), lambda i: (i, 0))],
        out_specs=pl.BlockSpec((tm, D), lambda i: (i, 0)),
        out_shape=jax.ShapeDtypeStruct((S, D), F32), compiler_params=_cp("parallel"))(hid4, wd4, x)


def ffn_bwd_hidden(dout, wd4, layer, a4, b4, tm=512):
    S = dout.shape[0]
    tm = min(tm, S)

    def body(do_ref, wd_ref, a_ref, b_ref, da_ref, db_ref):
        dhid = _dot_nt(do_ref[...].astype(BF16), wd_ref[0])
        av, bv = a_ref[0], b_ref[0]
        s = _sigmoid(av)
        da_ref[0] = (dhid * bv * (s * (1.0 + av * (1.0 - s)))).astype(BF16)
        db_ref[0] = (dhid * (av * s)).astype(BF16)

    aspec = pl.BlockSpec((1, tm, FS), lambda j, i: (j, i, 0))
    return _pcall(
        body, name="ffn_bwd_hidden", grid=(N_CHIPS, S // tm),
        in_specs=[pl.BlockSpec((tm, D), lambda j, i: (i, 0)), pl.BlockSpec((1, FS, D), lambda j, i: (j, layer, 0)),
                  aspec, aspec],
        out_specs=[aspec] * 2, out_shape=[jax.ShapeDtypeStruct((N_CHIPS, S, FS), BF16)] * 2,
        compiler_params=_cp("parallel", "arbitrary"))(dout, wd4, a4, b4)


def ffn_wgrad_in(h, da4, db4, tm=512):
    S = h.shape[0]
    tm = min(tm, S)

    def body(h_ref, da_ref, db_ref, dg_ref, du_ref):
        m = pl.program_id(1)
        hv = h_ref[...]
        for g_ref, o_ref in ((da_ref, dg_ref), (db_ref, du_ref)):
            t = _dot_tn(hv, g_ref[0])

            @pl.when(m == 0)
            def _():
                o_ref[0] = t

            @pl.when(m > 0)
            def _():
                o_ref[0] += t

    aspec = pl.BlockSpec((1, tm, FS), lambda j, m: (j, m, 0))
    ospec = pl.BlockSpec((1, D, FS), lambda j, m: (j, 0, 0))
    return _pcall(
        body, name="ffn_wgrad_in", grid=(N_CHIPS, S // tm),
        in_specs=[pl.BlockSpec((tm, D), lambda j, m: (m, 0)), aspec, aspec], out_specs=[ospec] * 2,
        out_shape=[jax.ShapeDtypeStruct((N_CHIPS, D, FS), F32)] * 2,
        compiler_params=_cp("parallel", "arbitrary"))(h, da4, db4)


def ffn_wgrad_out(hid4, dout, tm=1024):
    S = dout.shape[0]
    tm = min(tm, S)

    def body(hid_ref, do_ref, o_ref):
        m = pl.program_id(1)
        t = _dot_tn(hid_ref[0], do_ref[...].astype(BF16))

        @pl.when(m == 0)
        def _():
            o_ref[0] = t

        @pl.when(m > 0)
        def _():
            o_ref[0] += t

    return _pcall(
        body, name="ffn_wgrad_out", grid=(N_CHIPS, S // tm),
        in_specs=[pl.BlockSpec((1, tm, FS), lambda j, m: (j, m, 0)), pl.BlockSpec((tm, D), lambda j, m: (m, 0))],
        out_specs=pl.BlockSpec((1, FS, D), lambda j, m: (j, 0, 0)),
        out_shape=jax.ShapeDtypeStruct((N_CHIPS, FS, D), F32),
        compiler_params=_cp("parallel", "arbitrary"))(hid4, dout)


def ffn_dh(da4, db4, wg4, wu4, layer, tm=256):
    S = da4.shape[1]
    tm = min(tm, S)

    def body(da_ref, db_ref, wg_ref, wu_ref, o_ref):
        acc = _dot_nt(da_ref[0], wg_ref[0]) + _dot_nt(db_ref[0], wu_ref[0])
        for j in range(1, N_CHIPS):
            acc = acc + _dot_nt(da_ref[j], wg_ref[j]) + _dot_nt(db_ref[j], wu_ref[j])
        o_ref[...] = acc

    aspec = pl.BlockSpec((N_CHIPS, tm, FS), lambda i: (0, i, 0))
    wspec = pl.BlockSpec((N_CHIPS, D, FS), lambda i: (0, layer, 0))
    return _pcall(
        body, name="ffn_dh", grid=(S // tm,), in_specs=[aspec, aspec, wspec, wspec],
        out_specs=pl.BlockSpec((tm, D), lambda i: (i, 0)),
        out_shape=jax.ShapeDtypeStruct((S, D), F32), compiler_params=_cp("parallel"))(da4, db4, wg4, wu4)


POOL_T = 128
POOL_HALO = 16


def pool_fwd(u, wgrp, scale, x_res):
    S = u.shape[0]
    T, HB = min(POOL_T, S), POOL_HALO
    per = T // HB

    def body(u_ref, tail_ref, wg_ref, sc_ref, x_ref, xo_ref, p_ref):
        i = pl.program_id(0)
        uc = u_ref[...]
        tail = jnp.where(i > 0, tail_ref[...], 0.0)
        d_cur = _iota((T, T), 0) - _iota((T, T), 1)
        d_tail = _iota((T, HB), 0) - _iota((T, HB), 1) + HB
        tg = i * T + _iota((T, 1), 0)
        for g, w in enumerate(POOL_WINDOWS):
            gs = slice(g * PG, (g + 1) * PG)
            band = ((d_cur >= 0) & (d_cur < w)).astype(F32)
            band_t = ((d_tail >= 0) & (d_tail < w)).astype(F32)
            ug = uc[:, gs]
            ws = _dot(band, ug, HI) + _dot(band_t, tail[:, gs], HI)
            cnt = jnp.minimum(tg + 1, w).astype(F32)
            pb = (ws / cnt - ug).astype(BF16)
            p_ref[:, gs] = pb
            xo_ref[:, gs] = x_ref[:, gs] + _dot(pb, wg_ref[g]) * sc_ref[:, gs]

    return _pcall(
        body, name="pool_fwd", grid=(S // T,),
        in_specs=[pl.BlockSpec((T, D), lambda i: (i, 0)),
                  pl.BlockSpec((HB, D), lambda i: (jnp.maximum(i * per - 1, 0), 0)),
                  pl.BlockSpec((4, PG, PG), lambda i: (0, 0, 0)), pl.BlockSpec((1, D), lambda i: (0, 0)),
                  pl.BlockSpec((T, D), lambda i: (i, 0))],
        out_specs=[pl.BlockSpec((T, D), lambda i: (i, 0))] * 2,
        out_shape=[jax.ShapeDtypeStruct((S, D), F32), jax.ShapeDtypeStruct((S, D), BF16)],
        compiler_params=_cp("parallel"))(u, u, wgrp, scale, x_res)


def pool_bwd_group(dm, p, wgrp, scale, tm=512):
    S = dm.shape[0]
    tm = min(tm, S)

    def body(dm_ref, p_ref, wg_ref, sc_ref, dp_ref, dwg_ref, dsc_ref):
        i = pl.program_id(0)

        @pl.when(i == 0)
        def _():
            dwg_ref[...] = jnp.zeros_like(dwg_ref)
            dsc_ref[...] = jnp.zeros_like(dsc_ref)

        for g in range(4):
            gs = slice(g * PG, (g + 1) * PG)
            dmg, pg, wg = dm_ref[:, gs], p_ref[:, gs], wg_ref[g]
            dsc_ref[:, gs] += jnp.sum(dmg * _dot(pg, wg), axis=0, keepdims=True)
            dy = (dmg * sc_ref[:, gs]).astype(BF16)
            dp_ref[:, gs] = _dot_nt(dy, wg)
            dwg_ref[g] += _dot_tn(pg, dy)

    return _pcall(
        body, name="pool_bwd_group", grid=(S // tm,),
        in_specs=[pl.BlockSpec((tm, D), lambda i: (i, 0)), pl.BlockSpec((tm, D), lambda i: (i, 0)),
                  pl.BlockSpec((4, PG, PG), lambda i: (0, 0, 0)), pl.BlockSpec((1, D), lambda i: (0, 0))],
        out_specs=[pl.BlockSpec((tm, D), lambda i: (i, 0)), pl.BlockSpec((4, PG, PG), lambda i: (0, 0, 0)),
                   pl.BlockSpec((1, D), lambda i: (0, 0))],
        out_shape=[jax.ShapeDtypeStruct((S, D), F32), jax.ShapeDtypeStruct((4, PG, PG), F32),
                   jax.ShapeDtypeStruct((1, D), F32)],
        compiler_params=_cp("arbitrary"))(dm, p, wgrp, scale)


def pool_bwd_window(dp):
    S = dp.shape[0]
    T, HB = min(POOL_T, S), POOL_HALO
    per = T // HB
    nt = S // T

    def body(dp_ref, nxt_ref, du_ref):
        i = pl.program_id(0)
        dc = dp_ref[...]
        nxt = jnp.where(i < nt - 1, nxt_ref[...], 0.0)
        d_cur = _iota((T, T), 1) - _iota((T, T), 0)
        d_nxt = _iota((T, HB), 1) - _iota((T, HB), 0) + T
        tg = i * T + _iota((T, 1), 0)
        tn_ = (i + 1) * T + _iota((HB, 1), 0)
        for g, w in enumerate(POOL_WINDOWS):
            gs = slice(g * PG, (g + 1) * PG)
            band = ((d_cur >= 0) & (d_cur < w)).astype(F32)
            band_n = ((d_nxt >= 0) & (d_nxt < w)).astype(F32)
            dcg = dc[:, gs]
            cur = dcg / jnp.minimum(tg + 1, w).astype(F32)
            nx = nxt[:, gs] / jnp.minimum(tn_ + 1, w).astype(F32)
            du_ref[:, gs] = (_dot(band, cur, HI) + _dot(band_n, nx, HI) - dcg).astype(BF16)

    return _pcall(
        body, name="pool_bwd_window", grid=(nt,),
        in_specs=[pl.BlockSpec((T, D), lambda i: (i, 0)),
                  pl.BlockSpec((HB, D), lambda i: (jnp.minimum((i + 1) * per, S // HB - 1), 0))],
        out_specs=pl.BlockSpec((T, D), lambda i: (i, 0)),
        out_shape=jax.ShapeDtypeStruct((S, D), BF16),
        compiler_params=_cp("parallel"))(dp, dp)


CONV_T = 256


def _shift_down(xc, prev8, j):
    if j == 0:
        return xc
    T = xc.shape[0]
    body = pltpu.roll(xc, j, 0)
    first = jnp.where(_iota((8, 1), 0) < j, pltpu.roll(prev8, j, 0), body[0:8])
    return jnp.concatenate([first, body[8:T]], axis=0)


def _shift_up(dc, next8, j):
    if j == 0:
        return dc
    T = dc.shape[0]
    body = pltpu.roll(dc, T - j, 0)
    last = jnp.where(_iota((8, 1), 0) + j < 8, body[T - 8:T], pltpu.roll(next8, 8 - j, 0))
    return jnp.concatenate([body[0:T - 8], last], axis=0)


def conv_fwd(xbc, conv_w, conv_b):
    S = xbc.shape[0]
    T = min(CONV_T, S)
    CB = 1024

    def body(x_ref, prev_ref, w_ref, b_ref, o_ref):
        i = pl.program_id(1)
        xc = x_ref[...]
        prev8 = jnp.where(i > 0, prev_ref[...], 0.0)
        pre = b_ref[...] + w_ref[3:4, :] * xc
        for j in range(1, 4):
            pre = pre + w_ref[3 - j:4 - j, :] * _shift_down(xc, prev8, j)
        o_ref[...] = pre * _sigmoid(pre)

    return _pcall(
        body, name="conv_fwd", grid=(CONV_CH // CB, S // T),
        in_specs=[pl.BlockSpec((T, CB), lambda c, i: (i, c)),
                  pl.BlockSpec((8, CB), lambda c, i: (jnp.maximum(i * (T // 8) - 1, 0), c)),
                  pl.BlockSpec((4, CB), lambda c, i: (0, c)), pl.BlockSpec((1, CB), lambda c, i: (0, c))],
        out_specs=pl.BlockSpec((T, CB), lambda c, i: (i, c)),
        out_shape=jax.ShapeDtypeStruct((S, CONV_CH), F32),
        compiler_params=_cp("parallel", "parallel"))(xbc, xbc, conv_w, conv_b)


def conv_bwd_pre(dact, xbc, conv_w, conv_b):
    S = xbc.shape[0]
    T = min(CONV_T, S)
    CB = 1024

    def body(da_ref, x_ref, prev_ref, w_ref, b_ref, dpre_ref, dw_ref, db_ref):
        i = pl.program_id(1)
        xc = x_ref[...]
        prev8 = jnp.where(i > 0, prev_ref[...], 0.0)
        sh = [_shift_down(xc, prev8, j) for j in range(4)]
        pre = b_ref[...] + w_ref[3:4, :] * sh[0]
        for j in range(1, 4):
            pre = pre + w_ref[3 - j:4 - j, :] * sh[j]
        s = _sigmoid(pre)
        dpre = da_ref[...] * (s * (1.0 + pre * (1.0 - s)))
        dpre_ref[...] = dpre
        rows = [jnp.sum(dpre * sh[3 - k], axis=0, keepdims=True) for k in range(4)]
        dw = jnp.concatenate(rows + [jnp.zeros((4, CB), F32)], axis=0)
        db = jnp.sum(dpre, axis=0, keepdims=True)

        @pl.when(i == 0)
        def _():
            dw_ref[...] = dw
            db_ref[...] = db

        @pl.when(i > 0)
        def _():
            dw_ref[...] += dw
            db_ref[...] += db

    return _pcall(
        body, name="conv_bwd_pre", grid=(CONV_CH // CB, S // T),
        in_specs=[pl.BlockSpec((T, CB), lambda c, i: (i, c)), pl.BlockSpec((T, CB), lambda c, i: (i, c)),
                  pl.BlockSpec((8, CB), lambda c, i: (jnp.maximum(i * (T // 8) - 1, 0), c)),
                  pl.BlockSpec((4, CB), lambda c, i: (0, c)), pl.BlockSpec((1, CB), lambda c, i: (0, c))],
        out_specs=[pl.BlockSpec((T, CB), lambda c, i: (i, c)), pl.BlockSpec((8, CB), lambda c, i: (0, c)),
                   pl.BlockSpec((1, CB), lambda c, i: (0, c))],
        out_shape=[jax.ShapeDtypeStruct((S, CONV_CH), F32), jax.ShapeDtypeStruct((8, CONV_CH), F32),
                   jax.ShapeDtypeStruct((1, CONV_CH), F32)],
        compiler_params=_cp("parallel", "arbitrary"))(dact, xbc, xbc, conv_w, conv_b)


def conv_bwd_input(dpre, conv_w):
    S = dpre.shape[0]
    T = min(CONV_T, S)
    CB = 1024
    nt = S // T

    def body(d_ref, nxt_ref, w_ref, o_ref):
        i = pl.program_id(1)
        dc = d_ref[...]
        next8 = jnp.where(i < nt - 1, nxt_ref[...], 0.0)
        acc = w_ref[3:4, :] * dc
        for j in range(1, 4):
            acc = acc + w_ref[3 - j:4 - j, :] * _shift_up(dc, next8, j)
        o_ref[...] = acc.astype(BF16)

    return _pcall(
        body, name="conv_bwd_input", grid=(CONV_CH // CB, nt),
        in_specs=[pl.BlockSpec((T, CB), lambda c, i: (i, c)),
                  pl.BlockSpec((8, CB), lambda c, i: (jnp.minimum((i + 1) * (T // 8), S // 8 - 1), c)),
                  pl.BlockSpec((4, CB), lambda c, i: (0, c))],
        out_specs=pl.BlockSpec((T, CB), lambda c, i: (i, c)),
        out_shape=jax.ShapeDtypeStruct((S, CONV_CH), BF16),
        compiler_params=_cp("parallel", "parallel"))(dpre, dpre, conv_w)


def _ssd_chunk_terms(dt_ref, bias_ref, alog_ref):
    L = CH
    dtp = dt_ref[...] + bias_ref[...]
    dt = jnp.maximum(dtp, 0.0) + jnp.log(1.0 + jnp.exp(-jnp.abs(dtp)))
    a = -jnp.exp(alog_ref[...])
    da = dt * a
    tri = (_iota((L, L), 0) >= _iota((L, L), 1)).astype(F32)
    acum = _dot(tri, da, HI)
    triu = (_iota((L, L), 0) <= _iota((L, L), 1)).astype(F32)
    acum_row = _dot_tn(da, triu, HI)
    expand = (_iota((NH, DI), 1) // HP == _iota((NH, DI), 0)).astype(F32)
    return dtp, dt, a, da, acum, acum_row, expand


def ssd_scan_fwd(xbc_act, dt_raw, dt_bias, a_log, d_full):
    S = xbc_act.shape[0]
    L = CH
    nc = S // L

    def body(xs_ref, b_ref, c_ref, dt_ref, bias_ref, alog_ref, d_ref, y_ref, st_ref, state):
        c = pl.program_id(0)

        @pl.when(c == 0)
        def _():
            state[...] = jnp.zeros_like(state)

        st_ref[0] = state[...]
        _, dt, _, _, acum, acum_row, expand = _ssd_chunk_terms(dt_ref, bias_ref, alog_ref)
        e_full = _dot(jnp.exp(acum), expand, HI)
        w_full = _dot(jnp.exp(acum[L - 1:L, :] - acum), expand, HI)
        dt_full = _dot(dt, expand, HI)
        causal = _iota((L, L), 0) >= _iota((L, L), 1)
        lane_head = _iota((1, GW), 1) // HP
        for g in range(NG):
            gs = slice(g * GW, (g + 1) * GW)
            ns = slice(g * NS, (g + 1) * NS)
            xs_g = xs_ref[:, gs]
            xdt_g = xs_g * dt_full[:, gs]
            cg = c_ref[:, ns].astype(BF16)
            bg = b_ref[:, ns].astype(BF16)
            gmat = _dot_nt(cg, bg)
            yg = jnp.zeros((L, GW), F32)
            for hh in range(4):
                h = 4 * g + hh
                diff = acum[:, h:h + 1] - acum_row[h:h + 1, :]
                dk = jnp.exp(jnp.where(causal, diff, -1e30))
                xm = jnp.where(lane_head == hh, xdt_g, 0.0).astype(BF16)
                yg = yg + _dot((gmat * dk).astype(BF16), xm)
            sg = state[g]
            yoff = _dot(cg, sg.astype(BF16)) * e_full[:, gs]
            y_ref[:, gs] = yg + yoff + d_ref[:, gs] * xs_g
            state[g] = sg * e_full[L - 1:L, gs] + _dot_tn(bg, (w_full[:, gs] * xdt_g).astype(BF16))

    return _pcall(
        body, name="ssd_scan_fwd", grid=(nc,),
        in_specs=[pl.BlockSpec((L, DI), lambda c: (c, 0)), pl.BlockSpec((L, 1024), lambda c: (c, 2)),
                  pl.BlockSpec((L, 1024), lambda c: (c, 3)), pl.BlockSpec((L, NH), lambda c: (c, 0)),
                  pl.BlockSpec((1, NH), lambda c: (0, 0)), pl.BlockSpec((1, NH), lambda c: (0, 0)),
                  pl.BlockSpec((1, DI), lambda c: (0, 0))],
        out_specs=[pl.BlockSpec((L, DI), lambda c: (c, 0)), pl.BlockSpec((1, NG, NS, GW), lambda c: (c, 0, 0, 0))],
        out_shape=[jax.ShapeDtypeStruct((S, DI), F32), jax.ShapeDtypeStruct((nc, NG, NS, GW), F32)],
        scratch_shapes=[pltpu.VMEM((NG, NS, GW), F32)],
        compiler_params=_cp("arbitrary"))(xbc_act, xbc_act, xbc_act, dt_raw, dt_bias, a_log, d_full)


def ssd_scan_bwd(dy, xbc_act, dt_raw, dt_bias, a_log, d_full, states):
    S = xbc_act.shape[0]
    L = CH
    nc = S // L

    def body(dy_ref, xs_ref, b_ref, c_ref, dt_ref, bias_ref, alog_ref, d_ref, st_ref,
             dxbc_ref, ddt_ref, dbias_ref, dalog_ref, dd_ref, dstate):
        c = pl.program_id(0)

        @pl.when(c == 0)
        def _():
            dstate[...] = jnp.zeros_like(dstate)
            dbias_ref[...] = jnp.zeros_like(dbias_ref)
            dalog_ref[...] = jnp.zeros_like(dalog_ref)
            dd_ref[...] = jnp.zeros_like(dd_ref)

        dtp, dt, a, _, acum, acum_row, expand = _ssd_chunk_terms(dt_ref, bias_ref, alog_ref)
        e_full = _dot(jnp.exp(acum), expand, HI)
        w_h = jnp.exp(acum[L - 1:L, :] - acum)
        w_full = _dot(w_h, expand, HI)
        dt_full = _dot(dt, expand, HI)
        causal = _iota((L, L), 0) >= _iota((L, L), 1)
        lane_head = _iota((1, GW), 1) // HP
        ones_l = jnp.ones((L, LANES), F32)
        head_id = _iota((1, NH), 1)
        dacum = jnp.zeros((L, NH), F32)
        red_parts = []
        dxdt_parts = []
        alast_parts = []
        for g in range(NG):
            gs = slice(g * GW, (g + 1) * GW)
            ns = slice(g * NS, (g + 1) * NS)
            xs_g = xs_ref[:, gs]
            xdt_g = xs_g * dt_full[:, gs]
            dy_g = dy_ref[:, gs]
            cg = c_ref[:, ns].astype(BF16)
            bg = b_ref[:, ns].astype(BF16)
            gmat = _dot_nt(cg, bg)
            sg = st_ref[0, g]
            dsg = dstate[g]
            sgb, dsgb = sg.astype(BF16), dsg.astype(BF16)
            cs = _dot(cg, sgb)
            bds = _dot(bg, dsgb)
            e_g, w_g = e_full[:, gs], w_full[:, gs]
            dxdt = w_g * bds
            dgsum = jnp.zeros((L, L), F32)
            for hh in range(4):
                h = 4 * g + hh
                hm = lane_head == hh
                diff = acum[:, h:h + 1] - acum_row[h:h + 1, :]
                dk = jnp.exp(jnp.where(causal, diff, -1e30))
                m = gmat * dk
                dym = jnp.where(hm, dy_g, 0.0).astype(BF16)
                xm = jnp.where(hm, xdt_g, 0.0).astype(BF16)
                dm = _dot_nt(dym, xm)
                dxdt = dxdt + _dot_tn(m.astype(BF16), dym)
                dgsum = dgsum + dm * dk
                em = dm * m
                rs = _dot(em, ones_l, HI)[:, 0:1]
                cs_ = _dot_tn(em, ones_l, HI)[:, 0:1]
                dacum = dacum + (rs - cs_) * (head_id == h).astype(F32)
            dgb = dgsum.astype(BF16)
            edy = (e_g * dy_g).astype(BF16)
            wx = (w_g * xdt_g).astype(BF16)
            dc_g = _dot(dgb, bg) + _dot_nt(edy, sgb)
            db_g = _dot_tn(dgb, cg) + _dot_nt(wx, dsgb)
            dxbc_ref[:, DI + g * NS:DI + (g + 1) * NS] = db_g
            dxbc_ref[:, DI + 1024 + g * NS:DI + 1024 + (g + 1) * NS] = dc_g
            p2w = bds * xdt_g * w_g
            red_parts.append(dy_g * cs * e_g - p2w)
            alast_parts.append(jnp.sum(p2w, axis=0, keepdims=True)
                               + e_full[L - 1:L, gs] * jnp.sum(dsg * sg, axis=0, keepdims=True))
            dxdt_parts.append(dxdt)
            dstate[g] = e_full[L - 1:L, gs] * dsg + _dot_tn(cg, edy)
            dxbc_ref[:, gs] = dxdt * dt_full[:, gs] + dy_g * d_ref[:, gs]
            dd_ref[:, gs] += jnp.sum(dy_g * xs_g, axis=0, keepdims=True)
        red = jnp.concatenate(red_parts, axis=1)
        dxdt_all = jnp.concatenate(dxdt_parts, axis=1)
        alast = jnp.concatenate(alast_parts, axis=1)
        dacum = dacum + _dot_nt(red, expand, HI)
        dalast = _dot_nt(jnp.broadcast_to(alast, (8, DI)), expand, HI)[0:1, :]
        dacum = dacum + jnp.where(_iota((L, 1), 0) == L - 1, dalast, 0.0)
        triu = (_iota((L, L), 0) <= _iota((L, L), 1)).astype(F32)
        dda = _dot(triu, dacum, HI)
        ddt = _dot_nt(dxdt_all * xs_ref[...], expand, HI) + dda * a
        dalog_ref[...] += jnp.sum(dda * dt, axis=0, keepdims=True) * a
        ddt_raw = ddt * _sigmoid(dtp)
        ddt_ref[...] = ddt_raw
        dbias_ref[...] += jnp.sum(ddt_raw, axis=0, keepdims=True)

    rev = lambda c: (nc - 1 - c, 0)
    return _pcall(
        body, name="ssd_scan_bwd", grid=(nc,),
        in_specs=[pl.BlockSpec((L, DI), rev), pl.BlockSpec((L, DI), rev),
                  pl.BlockSpec((L, 1024), lambda c: (nc - 1 - c, 2)), pl.BlockSpec((L, 1024), lambda c: (nc - 1 - c, 3)),
                  pl.BlockSpec((L, NH), rev), pl.BlockSpec((1, NH), lambda c: (0, 0)),
                  pl.BlockSpec((1, NH), lambda c: (0, 0)), pl.BlockSpec((1, DI), lambda c: (0, 0)),
                  pl.BlockSpec((1, NG, NS, GW), lambda c: (nc - 1 - c, 0, 0, 0))],
        out_specs=[pl.BlockSpec((L, CONV_CH), rev), pl.BlockSpec((L, NH), rev),
                   pl.BlockSpec((1, NH), lambda c: (0, 0)), pl.BlockSpec((1, NH), lambda c: (0, 0)),
                   pl.BlockSpec((1, DI), lambda c: (0, 0))],
        out_shape=[jax.ShapeDtypeStruct((S, CONV_CH), F32), jax.ShapeDtypeStruct((S, NH), F32),
                   jax.ShapeDtypeStruct((1, NH), F32), jax.ShapeDtypeStruct((1, NH), F32),
                   jax.ShapeDtypeStruct((1, DI), F32)],
        scratch_shapes=[pltpu.VMEM((NG, NS, GW), F32)],
        compiler_params=_cp("arbitrary"))(dy, xbc_act, xbc_act, xbc_act, dt_raw, dt_bias, a_log, d_full, states)


def gate_norm_fwd(y, z, out_norm, tm=256):
    S = y.shape[0]
    tm = min(tm, S)

    def body(y_ref, z_ref, on_ref, o_ref):
        zv = z_ref[...]
        gin = y_ref[...] * (zv * _sigmoid(zv))
        for g in range(NG):
            gs = slice(g * GW, (g + 1) * GW)
            blk = gin[:, gs]
            r = lax.rsqrt(jnp.mean(blk * blk, axis=-1, keepdims=True) + EPS)
            o_ref[:, gs] = (blk * r * on_ref[:, gs]).astype(BF16)

    return _pcall(
        body, name="gate_norm_fwd", grid=(S // tm,),
        in_specs=[pl.BlockSpec((tm, DI), lambda i: (i, 0)), pl.BlockSpec((tm, DI), lambda i: (i, 0)),
                  pl.BlockSpec((1, DI), lambda i: (0, 0))],
        out_specs=pl.BlockSpec((tm, DI), lambda i: (i, 0)),
        out_shape=jax.ShapeDtypeStruct((S, DI), BF16),
        compiler_params=_cp("parallel"))(y, z, out_norm)


def gate_norm_bwd(dgn, y, z, out_norm, tm=256):
    S = y.shape[0]
    tm = min(tm, S)

    def body(dg_ref, y_ref, z_ref, on_ref, dy_ref, dz_ref, don_ref):
        i = pl.program_id(0)

        @pl.when(i == 0)
        def _():
            don_ref[...] = jnp.zeros_like(don_ref)

        zv, yv = z_ref[...], y_ref[...]
        s = _sigmoid(zv)
        sz = zv * s
        gin = yv * sz
        for g in range(NG):
            gs = slice(g * GW, (g + 1) * GW)
            blk = gin[:, gs]
            r = lax.rsqrt(jnp.mean(blk * blk, axis=-1, keepdims=True) + EPS)
            n = blk * r
            dg = dg_ref[:, gs]
            don_ref[:, gs] += jnp.sum(dg * n, axis=0, keepdims=True)
            dn = dg * on_ref[:, gs]
            dgin = r * (dn - n * jnp.mean(dn * n, axis=-1, keepdims=True))
            dy_ref[:, gs] = dgin * sz[:, gs]
            dz_ref[:, gs] = (dgin * yv[:, gs] * (s[:, gs] * (1.0 + zv[:, gs] * (1.0 - s[:, gs])))).astype(BF16)

    return _pcall(
        body, name="gate_norm_bwd", grid=(S // tm,),
        in_specs=[pl.BlockSpec((tm, DI), lambda i: (i, 0))] * 3 + [pl.BlockSpec((1, DI), lambda i: (0, 0))],
        out_specs=[pl.BlockSpec((tm, DI), lambda i: (i, 0)), pl.BlockSpec((tm, DI), lambda i: (i, 0)),
                   pl.BlockSpec((1, DI), lambda i: (0, 0))],
        out_shape=[jax.ShapeDtypeStruct((S, DI), F32), jax.ShapeDtypeStruct((S, DI), BF16),
                   jax.ShapeDtypeStruct((1, DI), F32)],
        compiler_params=_cp("arbitrary"))(dgn, y, z, out_norm)


SB_T = 256
SB_QSCALE = 0.125
SB_DEAD = -110.0
SB_UNSEEN = -1e30


def _head_norm(xv, lo):
    sq = xv * xv
    s0 = jnp.sum(jnp.where(lo, sq, 0.0), axis=-1, keepdims=True)
    s1 = jnp.sum(jnp.where(lo, 0.0, sq), axis=-1, keepdims=True)
    return jnp.where(lo, lax.rsqrt(s0 / SBD + EPS), lax.rsqrt(s1 / SBD + EPS))


def sb_prep_fwd(qkv, qg, kg, tm=256):
    S = qkv.shape[0]
    tm = min(tm, S)

    def body(x_ref, qg_ref, kg_ref, q_ref, k_ref, v_ref):
        lo = _iota((1, LANES), 1) < SBD
        for sl in range(D // LANES):
            cs = slice(sl * LANES, (sl + 1) * LANES)
            xq = x_ref[:, cs]
            q_ref[:, cs] = ((xq * _head_norm(xq, lo) * qg_ref[...]).astype(BF16).astype(F32) * SB_QSCALE).astype(BF16)
            xk = x_ref[:, D + sl * LANES:D + (sl + 1) * LANES]
            k_ref[:, cs] = (xk * _head_norm(xk, lo) * kg_ref[...]).astype(BF16)
        v_ref[...] = x_ref[:, 2 * D:3 * D].astype(BF16)

    return _pcall(
        body, name="sb_prep_fwd", grid=(S // tm,),
        in_specs=[pl.BlockSpec((tm, 3 * D), lambda i: (i, 0)), pl.BlockSpec((1, LANES), lambda i: (0, 0)),
                  pl.BlockSpec((1, LANES), lambda i: (0, 0))],
        out_specs=[pl.BlockSpec((tm, D), lambda i: (i, 0))] * 3,
        out_shape=[jax.ShapeDtypeStruct((S, D), BF16)] * 3,
        compiler_params=_cp("parallel"))(qkv, qg, kg)


def sb_prep_bwd(dqs, dkn, dv, qkv, qg, kg, tm=256):
    S = qkv.shape[0]
    tm = min(tm, S)

    def body(dq_ref, dk_ref, dv_ref, x_ref, qg_ref, kg_ref, dx_ref, dqg_ref, dkg_ref):
        i = pl.program_id(0)

        @pl.when(i == 0)
        def _():
            dqg_ref[...] = jnp.zeros_like(dqg_ref)
            dkg_ref[...] = jnp.zeros_like(dkg_ref)

        lo = _iota((1, LANES), 1) < SBD

        def one(xv, dh, gain):
            r = _head_norm(xv, lo)
            y = xv * r
            dy = dh * gain
            t = dy * y
            m0 = jnp.sum(jnp.where(lo, t, 0.0), axis=-1, keepdims=True)
            m1 = jnp.sum(jnp.where(lo, 0.0, t), axis=-1, keepdims=True)
            dx = r * (dy - y * (jnp.where(lo, m0, m1) / SBD))
            return dx, jnp.sum(dh * y, axis=0, keepdims=True)

        for sl in range(D // LANES):
            cs = slice(sl * LANES, (sl + 1) * LANES)
            dx, dg = one(x_ref[:, cs], dq_ref[:, cs] * SB_QSCALE, qg_ref[...])
            dx_ref[:, cs] = dx.astype(BF16)
            dqg_ref[:, cs] += dg
            ks = slice(D + sl * LANES, D + (sl + 1) * LANES)
            dx, dg = one(x_ref[:, ks], dk_ref[:, cs], kg_ref[...])
            dx_ref[:, ks] = dx.astype(BF16)
            dkg_ref[:, cs] += dg
        dx_ref[:, 2 * D:3 * D] = dv_ref[...].astype(BF16)

    return _pcall(
        body, name="sb_prep_bwd", grid=(S // tm,),
        in_specs=[pl.BlockSpec((tm, D), lambda i: (i, 0))] * 3
        + [pl.BlockSpec((tm, 3 * D), lambda i: (i, 0)), pl.BlockSpec((1, LANES), lambda i: (0, 0)),
           pl.BlockSpec((1, LANES), lambda i: (0, 0))],
        out_specs=[pl.BlockSpec((tm, 3 * D), lambda i: (i, 0)), pl.BlockSpec((1, D), lambda i: (0, 0)),
                   pl.BlockSpec((1, D), lambda i: (0, 0))],
        out_shape=[jax.ShapeDtypeStruct((S, 3 * D), BF16), jax.ShapeDtypeStruct((1, D), F32),
                   jax.ShapeDtypeStruct((1, D), F32)],
        compiler_params=_cp("arbitrary"))(dqs, dkn, dv, qkv, qg, kg)


def _split_dot(x, u):
    hi = x.astype(BF16)
    lo = (x - hi.astype(F32)).astype(BF16)
    return _dot(hi, u) + _dot(lo, u)


def _sb_logits(qh, kb, valid):
    z = _dot_nt(qh, kb)
    e = jnp.exp(-jnp.abs(z))
    lp = jnp.log(1.0 + e)
    lb = jnp.minimum(z, 0.0) - lp
    l1m = jnp.where(valid, lb - z, 0.0)
    return z, e, lb, l1m


def sb_fwd(qs, kn, v):
    S = qs.shape[0]
    T = min(SB_T, S)
    nq = S // T

    def body(q_ref, k_ref, v_ref, o_ref, r_ref, oacc, rrun):
        i = pl.program_id(1)
        qb = q_ref[...]
        lo = _iota((1, LANES), 1) < SBD
        row, col = _iota((T, T), 0), _iota((T, T), 1)
        u = (row > col).astype(BF16)
        lane_blk = _iota((T, LANES), 1)
        oacc[...] = jnp.zeros_like(oacc)
        for hh in range(2):
            hm = lo if hh == 0 else jnp.logical_not(lo)
            qh = jnp.where(hm, qb, jnp.zeros_like(qb))
            rrun[...] = jnp.zeros_like(rrun)
            r_ref[hh] = jnp.full((T, LANES), SB_UNSEEN, F32)

            def live(carry):
                s, rmax = carry
                return jnp.logical_and(s <= i, rmax > SB_DEAD)

            def step(carry, hm=hm, qh=qh, hh=hh):
                s, _ = carry
                j = i - s
                off = pl.multiple_of(j * T, T)
                kb = k_ref[pl.ds(off, T), :]
                vb = v_ref[pl.ds(off, T), :]
                vb = jnp.where(hm, vb, jnp.zeros_like(vb))
                valid = (j * T + col) < (i * T + row)
                _, _, lb, l1m = _sb_logits(qh, kb, valid)
                r = rrun[...]
                aft = _split_dot(l1m, u) + r
                a = jnp.where(valid, jnp.exp(lb + aft), 0.0)
                oacc[...] += _dot(a.astype(BF16), vb)
                r_ref[hh] = jnp.where(lane_blk == j, r, r_ref[hh])
                rnew = r + jnp.sum(l1m, axis=-1, keepdims=True)
                rrun[...] = rnew
                return s + 1, jnp.max(rnew)

            lax.while_loop(live, step, (jnp.int32(0), jnp.float32(0.0)))
        o_ref[...] = oacc[...].astype(BF16)

    return _pcall(
        body, name="sb_fwd", grid=(D // LANES, nq),
        in_specs=[pl.BlockSpec((T, LANES), lambda h, i: (i, h)), pl.BlockSpec((S, LANES), lambda h, i: (0, h)),
                  pl.BlockSpec((S, LANES), lambda h, i: (0, h))],
        out_specs=[pl.BlockSpec((T, LANES), lambda h, i: (i, h)), pl.BlockSpec((2, T, LANES), lambda h, i: (h, i, 0))],
        out_shape=[jax.ShapeDtypeStruct((S, D), BF16), jax.ShapeDtypeStruct((SBH, S, LANES), F32)],
        scratch_shapes=[pltpu.VMEM((T, LANES), F32), pltpu.VMEM((T, 1), F32)],
        compiler_params=_cp("parallel", "arbitrary"))(qs, kn, v)


def sb_bwd(qs, kn, v, do, rsave):
    S = qs.shape[0]
    T = min(SB_T, S)
    nq = S // T

    def body(q_ref, k_ref, v_ref, do_ref, r_ref, dq_ref, dk_ref, dv_ref, crun):
        i = pl.program_id(1)

        @pl.when(i == 0)
        def _():
            dk_ref[...] = jnp.zeros_like(dk_ref)
            dv_ref[...] = jnp.zeros_like(dv_ref)

        qb, dob = q_ref[...], do_ref[...]
        lo = _iota((1, LANES), 1) < SBD
        row, col = _iota((T, T), 0), _iota((T, T), 1)
        u = (row > col).astype(BF16)
        u2 = (row < col).astype(BF16)
        lane_blk = _iota((T, LANES), 1)
        dq_ref[...] = jnp.zeros_like(dq_ref)
        for hh in range(2):
            hm = lo if hh == 0 else jnp.logical_not(lo)
            qh = jnp.where(hm, qb, jnp.zeros_like(qb))
            doh = jnp.where(hm, dob, jnp.zeros_like(dob))
            crun[...] = jnp.zeros_like(crun)

            def step(j, carry, hm=hm, qh=qh, doh=doh, hh=hh):
                off = pl.multiple_of(j * T, T)
                kb = k_ref[pl.ds(off, T), :]
                vb = v_ref[pl.ds(off, T), :]
                vb = jnp.where(hm, vb, jnp.zeros_like(vb))
                valid = (j * T + col) < (i * T + row)
                z, e, lb, l1m = _sb_logits(qh, kb, valid)
                r = jnp.sum(jnp.where(lane_blk == j, r_ref[hh], 0.0), axis=-1, keepdims=True)
                aft = _split_dot(l1m, u) + r
                a = jnp.where(valid, jnp.exp(lb + aft), 0.0)
                w = a * _dot_nt(doh, vb)
                cprev = crun[...]
                cw = _split_dot(w, u2) + cprev
                inv = 1.0 / (1.0 + e)
                pos = z >= 0.0
                beta = jnp.where(pos, 1.0, e) * inv
                onem = jnp.where(pos, e, 1.0) * inv
                dz = jnp.where(valid, w * onem - beta * cw, 0.0).astype(BF16)
                dq_ref[...] += _dot(dz, jnp.where(hm, kb, jnp.zeros_like(kb)))
                dk_ref[pl.ds(off, T), :] += _dot_tn(dz, qh)
                dv_ref[pl.ds(off, T), :] += _dot_tn(a.astype(BF16), doh)
                crun[...] = cprev + jnp.sum(w, axis=-1, keepdims=True)
                return carry

            col_max = jnp.max(r_ref[hh], axis=0, keepdims=True)
            seen = jnp.logical_and(col_max > SB_DEAD, _iota((1, LANES), 1) <= i)
            n_live = jnp.sum(seen.astype(jnp.int32))
            lax.fori_loop(i + 1 - n_live, i + 1, step, 0)

    return _pcall(
        body, name="sb_bwd", grid=(D // LANES, nq),
        in_specs=[pl.BlockSpec((T, LANES), lambda h, i: (i, h)), pl.BlockSpec((S, LANES), lambda h, i: (0, h)),
                  pl.BlockSpec((S, LANES), lambda h, i: (0, h)), pl.BlockSpec((T, LANES), lambda h, i: (i, h)),
                  pl.BlockSpec((2, T, LANES), lambda h, i: (h, i, 0))],
        out_specs=[pl.BlockSpec((T, LANES), lambda h, i: (i, h)), pl.BlockSpec((S, LANES), lambda h, i: (0, h)),
                   pl.BlockSpec((S, LANES), lambda h, i: (0, h))],
        out_shape=[jax.ShapeDtypeStruct((S, D), F32)] * 3,
        scratch_shapes=[pltpu.VMEM((T, 1), F32)],
        compiler_params=_cp("parallel", "arbitrary"))(qs, kn, v, do, rsave)


def loss_head(y, target, tm=512):
    S = y.shape[0]
    tm = min(tm, S)

    def body(y_ref, t_ref, ls_ref, dy_ref):
        i = pl.program_id(0)
        err = y_ref[...] - t_ref[...]
        dy_ref[...] = err * (1.0 / D)
        part = jnp.sum(err * err, axis=0, keepdims=True)

        @pl.when(i == 0)
        def _():
            ls_ref[...] = part

        @pl.when(i > 0)
        def _():
            ls_ref[...] += part

    return _pcall(
        body, name="loss_head", grid=(S // tm,),
        in_specs=[pl.BlockSpec((tm, D), lambda i: (i, 0))] * 2,
        out_specs=[pl.BlockSpec((1, D), lambda i: (0, 0)), pl.BlockSpec((tm, D), lambda i: (i, 0))],
        out_shape=[jax.ShapeDtypeStruct((1, D), F32), jax.ShapeDtypeStruct((S, D), F32)],
        compiler_params=_cp("arbitrary"))(y, target)


def _row_tile(rows, cols):
    cap = max(8, (1 << 20) // (4 * cols))
    return max(t for t in range(8, min(rows, cap) + 1, 8) if rows % t == 0)


def _adamw_update(w, g, m, v):
    c1 = 1.0 / (1.0 - ADAM_B1 ** ADAM_STEP)
    c2 = 1.0 / (1.0 - ADAM_B2 ** ADAM_STEP)
    mn = ADAM_B1 * m + (1.0 - ADAM_B1) * g
    vn = ADAM_B2 * v + (1.0 - ADAM_B2) * (g * g)
    return -ADAM_LR * ((mn * c1) / (jnp.sqrt(vn * c2) + ADAM_EPS) + ADAM_WD * w), mn, vn


def adamw(w, g, m, v, name="adamw"):
    R, C = w.shape
    tr = _row_tile(R, C)

    def body(w_ref, g_ref, m_ref, v_ref, d_ref, mo_ref, vo_ref):
        d_ref[...], mo_ref[...], vo_ref[...] = _adamw_update(w_ref[...], g_ref[...], m_ref[...], v_ref[...])

    spec = pl.BlockSpec((tr, C), lambda i: (i, 0))
    return _pcall(
        body, name=name, grid=(R // tr,), in_specs=[spec] * 4, out_specs=[spec] * 3,
        out_shape=[jax.ShapeDtypeStruct((R, C), F32)] * 3,
        compiler_params=_cp("parallel"))(w, g, m, v)


def adamw_halves(w, g_mine, g_other, m, v, name="adamw_halves"):
    R, C = w.shape
    H = R // 2
    tr = _row_tile(H, C)
    n_i = H // tr
    where = lax.axis_index("c").astype(jnp.int32).reshape(1)

    def body(s_ref, w_ref, gm_ref, go_ref, m_ref, v_ref, g_ref, d_ref, mo_ref, vo_ref):
        g = jnp.where(pl.program_id(0) == s_ref[0], gm_ref[...], go_ref[...])
        g_ref[...] = g
        d_ref[...], mo_ref[...], vo_ref[...] = _adamw_update(w_ref[...], g, m_ref[...], v_ref[...])

    full = pl.BlockSpec((tr, C), lambda h, i, s: (h * n_i + i, 0))
    half = pl.BlockSpec((tr, C), lambda h, i, s: (i, 0))
    return _pcall(
        body, name=name,
        grid_spec=pltpu.PrefetchScalarGridSpec(
            num_scalar_prefetch=1, grid=(2, n_i), in_specs=[full, half, half, full, full], out_specs=[full] * 4),
        out_shape=[jax.ShapeDtypeStruct((R, C), F32)] * 4,
        compiler_params=_cp("parallel", "parallel"))(where, w, g_mine, g_other, m, v)


def pair_sum(gstacks, halves):
    c = lax.axis_index("c")
    me = 2 * lax.axis_index("x") + lax.axis_index("y")
    where = jnp.stack([c, me]).astype(jnp.int32)
    outs = []
    for g, xh in zip(gstacks, halves):
        _, H, C = xh.shape
        t = _row_tile(H, C)
        n_i = H // t

        def body(s_ref, g_ref, x_ref, qb_ref, own_ref):
            j = pl.program_id(1)
            q = g_ref[0] + x_ref[0]
            qb_ref[0] = q.astype(BF16)

            @pl.when(j == s_ref[1])
            def _():
                own_ref[...] = q

        outs.append(_pcall(
            body, name="pair_sum",
            grid_spec=pltpu.PrefetchScalarGridSpec(
                num_scalar_prefetch=1, grid=(n_i, N_CHIPS),
                in_specs=[pl.BlockSpec((1, t, C), lambda i, j, s, n_i=n_i: (j, s[0] * n_i + i, 0)),
                          pl.BlockSpec((1, t, C), lambda i, j, s: (j, i, 0))],
                out_specs=[pl.BlockSpec((1, t, C), lambda i, j, s: (j, i, 0)),
                           pl.BlockSpec((t, C), lambda i, j, s: (i, 0))]),
            out_shape=[jax.ShapeDtypeStruct((N_CHIPS, H, C), BF16), jax.ShapeDtypeStruct((H, C), F32)],
            compiler_params=_cp("parallel", "arbitrary"))(where, g, xh))
    return [o[0] for o in outs], [o[1] for o in outs]


def chip_sum(owns, recvs):
    outs = []
    for own, rc in zip(owns, recvs):
        H, C = own.shape
        t = _row_tile(H, C)

        def body(o_ref, r_ref, t_ref):
            t_ref[...] = ((o_ref[...] + r_ref[0].astype(F32)) + r_ref[1].astype(F32)) + r_ref[2].astype(F32)

        outs.append(_pcall(
            body, name="chip_sum", grid=(H // t,),
            in_specs=[pl.BlockSpec((t, C), lambda i: (i, 0)), pl.BlockSpec((3, t, C), lambda i: (0, i, 0))],
            out_specs=pl.BlockSpec((t, C), lambda i: (i, 0)),
            out_shape=jax.ShapeDtypeStruct((H, C), F32), compiler_params=_cp("parallel"))(own, rc))
    return outs


MESH = pl.DeviceIdType.MESH
ANY = pl.BlockSpec(memory_space=pl.ANY)
SPLIT_MIN_BYTES = 1 << 20


def _other_chips(x, y):
    return [(1 - x, y), (x, 1 - y), (1 - x, 1 - y)]


def _half_rows(rows, who):
    half = rows // 2
    return pl.ds(pl.multiple_of(who * half, 16), half)


def gather_all(shards):
    n = len(shards)
    rows = [s.shape[0] for s in shards]
    split = [r % 32 == 0 and s.size * s.dtype.itemsize >= SPLIT_MIN_BYTES for r, s in zip(rows, shards)]

    def body(*refs):
        ins, outs = refs[:n], refs[n:2 * n]
        ici_send, ici_recv, d2d_send, d2d_recv = refs[2 * n:]
        x, y, c = lax.axis_index("x"), lax.axis_index("y"), lax.axis_index("c")
        me, sib, chips = 2 * x + y, (x, y, 1 - c), _other_chips(x, y)

        def part(k, who):
            return _half_rows(rows[k], who) if split[k] else pl.ds(0, rows[k])

        def ici(k, r, block):
            px, py = chips[r]
            return pltpu.make_async_remote_copy(
                src_ref=ins[k].at[part(k, c)], dst_ref=outs[k].at[block, part(k, c)],
                send_sem=ici_send.at[3 * k + r], recv_sem=ici_recv.at[3 * k + r],
                device_id=(px, py, c), device_id_type=MESH)

        def d2d(k, r, who):
            px, py = chips[r]
            blk = outs[k].at[2 * px + py, part(k, who)]
            return pltpu.make_async_remote_copy(
                src_ref=blk, dst_ref=blk, send_sem=d2d_send.at[3 * k + r], recv_sem=d2d_recv.at[3 * k + r],
                device_id=sib, device_id_type=MESH)

        sends = [ici(k, r, me) for k in range(n) for r in range(3)]
        for cp in sends:
            cp.start()
        for r in range(3):
            px, py = chips[r]
            for k in range(n):
                ici(k, r, 2 * px + py).wait_recv()
                if split[k]:
                    fwd = d2d(k, r, c)
                    fwd.start()
                    sends.append(fwd)
        for r in range(3):
            for k in range(n):
                if split[k]:
                    d2d(k, r, 1 - c).wait_recv()
        for cp in sends:
            cp.wait_send()

    return _pcall(
        body, name="gather_all", in_specs=[ANY] * n, out_specs=[ANY] * n,
        out_shape=[jax.ShapeDtypeStruct((N_CHIPS,) + s.shape, s.dtype) for s in shards],
        scratch_shapes=[pltpu.SemaphoreType.DMA((3 * n,))] * 4)(*shards)


def swap_halves(gstacks):
    n = len(gstacks)

    def body(*refs):
        ins, outs, send_sems, recv_sems = refs[:n], refs[n:2 * n], refs[2 * n], refs[2 * n + 1]
        x, y, c = lax.axis_index("x"), lax.axis_index("y"), lax.axis_index("c")
        copies = [pltpu.make_async_remote_copy(
            src_ref=ins[k].at[:, _half_rows(ins[k].shape[1], 1 - c)], dst_ref=outs[k],
            send_sem=send_sems.at[k], recv_sem=recv_sems.at[k], device_id=(x, y, 1 - c), device_id_type=MESH)
            for k in range(n)]
        for cp in copies:
            cp.start()
        for cp in copies:
            cp.wait()

    return _pcall(
        body, name="swap_halves", in_specs=[ANY] * n, out_specs=[ANY] * n,
        out_shape=[jax.ShapeDtypeStruct((g.shape[0], g.shape[1] // 2, g.shape[2]), g.dtype) for g in gstacks],
        scratch_shapes=[pltpu.SemaphoreType.DMA((n,)), pltpu.SemaphoreType.DMA((n,))])(*gstacks)


def scatter_chips(stacks):
    n = len(stacks)

    def body(*refs):
        ins, outs, send_sems, recv_sems = refs[:n], refs[n:2 * n], refs[2 * n], refs[2 * n + 1]
        x, y, c = lax.axis_index("x"), lax.axis_index("y"), lax.axis_index("c")
        copies = [pltpu.make_async_remote_copy(
            src_ref=ins[k].at[2 * px + py], dst_ref=outs[k].at[r], send_sem=send_sems.at[3 * k + r],
            recv_sem=recv_sems.at[3 * k + r], device_id=(px, py, c), device_id_type=MESH)
            for k in range(n) for r, (px, py) in enumerate(_other_chips(x, y))]
        for cp in copies:
            cp.start()
        for cp in copies:
            cp.wait()

    return _pcall(
        body, name="scatter_chips", in_specs=[ANY] * n, out_specs=[ANY] * n,
        out_shape=[jax.ShapeDtypeStruct((3,) + s.shape[1:], s.dtype) for s in stacks],
        scratch_shapes=[pltpu.SemaphoreType.DMA((3 * n,)), pltpu.SemaphoreType.DMA((3 * n,))])(*stacks)


def swap_totals(totals):
    n = len(totals)

    def body(*refs):
        ins, outs, send_sems, recv_sems = refs[:n], refs[n:2 * n], refs[2 * n], refs[2 * n + 1]
        x, y, c = lax.axis_index("x"), lax.axis_index("y"), lax.axis_index("c")
        copies = [pltpu.make_async_remote_copy(
            src_ref=ins[k], dst_ref=outs[k], send_sem=send_sems.at[k], recv_sem=recv_sems.at[k],
            device_id=(x, y, 1 - c), device_id_type=MESH) for k in range(n)]
        for cp in copies:
            cp.start()
        for cp in copies:
            cp.wait()

    return _pcall(
        body, name="swap_totals", in_specs=[ANY] * n, out_specs=[ANY] * n,
        out_shape=[jax.ShapeDtypeStruct(t.shape, t.dtype) for t in totals],
        scratch_shapes=[pltpu.SemaphoreType.DMA((n,)), pltpu.SemaphoreType.DMA((n,))])(*totals)


def place_own(gathered, own):
    R, C = own.shape
    t = _row_tile(R, C)
    where = (2 * lax.axis_index("x") + lax.axis_index("y")).astype(jnp.int32).reshape(1)

    def body(s_ref, own_ref, g_ref, o_ref):
        o_ref[0] = own_ref[...]

    return _pcall(
        body, name="place_own",
        grid_spec=pltpu.PrefetchScalarGridSpec(
            num_scalar_prefetch=1, grid=(R // t,), in_specs=[pl.BlockSpec((t, C), lambda i, s: (i, 0)), ANY],
            out_specs=pl.BlockSpec((1, t, C), lambda i, s: (s[0], i, 0))),
        out_shape=jax.ShapeDtypeStruct(gathered.shape, gathered.dtype), input_output_aliases={2: 0},
        compiler_params=_cp("parallel"))(where, own, gathered)


def reduce_scatter(gstacks):
    halves = swap_halves(gstacks)
    payload, own = pair_sum(gstacks, halves)
    recv = scatter_chips(payload)
    mine = chip_sum(own, recv)
    return mine, swap_totals(mine)


def allreduce_small(vec):
    R = vec.shape[0]

    def body(in_ref, out_ref, buf, send_sems, recv_sems):
        x, y, c = lax.axis_index("x"), lax.axis_index("y"), lax.axis_index("c")
        me = 4 * x + 2 * y + c
        buf[me] = in_ref[...]
        copies = []
        for k in range(1, 8):
            peer = (x ^ (k >> 2), y ^ ((k >> 1) & 1), c ^ (k & 1))
            copies.append(pltpu.make_async_remote_copy(
                src_ref=in_ref, dst_ref=buf.at[me], send_sem=send_sems.at[k - 1], recv_sem=recv_sems.at[k - 1],
                device_id=peer, device_id_type=MESH))
        for cp in copies:
            cp.start()
        for cp in copies:
            cp.wait()
        acc = buf[0]
        for d in range(1, 8):
            acc = acc + buf[d]
        out_ref[...] = acc

    vm = pl.BlockSpec(memory_space=pltpu.VMEM)
    return _pcall(
        body, name="allreduce_small", in_specs=[vm], out_specs=vm,
        out_shape=jax.ShapeDtypeStruct((R, LANES), F32),
        scratch_shapes=[pltpu.VMEM((8, R, LANES), F32), pltpu.SemaphoreType.DMA((7,)), pltpu.SemaphoreType.DMA((7,))])(vec)


SHARDED = [("pool_in", 1, "bf16"), ("pool_group", 2, "bf16"), ("pool_scale", 1, "f32"), ("ssd_in", 2, "bf16"),
           ("ssd_conv_w", 2, "f32"), ("ssd_out", 1, "bf16"), ("sb_qkv", 2, "bf16"), ("sb_out", 1, "bf16"),
           ("ffn_gate", 2, "bf16"), ("ffn_up", 2, "bf16"), ("ffn_down", 1, "bf16")]
REPLICATED = ["mix_norm", "ssd_conv_b", "ssd_dt_bias", "ssd_a_log", "ssd_d", "ssd_out_norm", "sb_q_norm",
              "sb_k_norm", "ffn_norm"]
STACKED = ["ffn_gate", "ffn_up", "ffn_down"]
PACKED = [s for s in SHARDED if s[0] not in STACKED]
WEIGHT_ORDER = ["mix_norm", "pool_in", "pool_group", "pool_scale", "ssd_in", "ssd_conv_w", "ssd_conv_b",
                "ssd_dt_bias", "ssd_a_log", "ssd_d", "ssd_out_norm", "ssd_out", "sb_qkv", "sb_q_norm", "sb_k_norm",
                "sb_out", "ffn_norm", "ffn_gate", "ffn_up", "ffn_down"]
ROW_PAD = 1024


def _piece_rows(n, mult):
    rows = -(-n // LANES)
    return -(-rows // mult) * mult


def _as_rows(a, mult):
    flat = a.reshape(-1)
    rows = _piece_rows(flat.shape[0], mult)
    if rows * LANES != flat.shape[0]:
        flat = jnp.pad(flat, (0, rows * LANES - flat.shape[0]))
    return flat.reshape(rows, LANES)


def _pack(arrs, mult=8, row_pad=ROW_PAD):
    parts = [_as_rows(a, mult) for a in arrs]
    rows = sum(p.shape[0] for p in parts)
    pad = -rows % row_pad
    if pad:
        parts.append(jnp.zeros((pad, LANES), parts[0].dtype))
    return jnp.concatenate(parts, axis=0)


def _unpack(packed, shapes, mult=8, lead=()):
    out, off = [], 0
    for s in shapes:
        n = math.prod(s)
        rows = _piece_rows(n, mult)
        piece = packed[..., off:off + rows, :].reshape(lead + (rows * LANES,))
        out.append(piece[..., :n].reshape(lead + tuple(s)))
        off += rows
    return out


def _rows2d(a):
    return a.reshape(-1, a.shape[-1])


def _gather_weights(shards):
    kinds = (("bf16", BF16, 16, ROW_PAD), ("f32", F32, 8, 8))
    groups = [[(n, ax) for n, ax, k in PACKED if k == kind] for kind, _, _, _ in kinds]
    packs = [_pack([shards[n].astype(dtype) for n, _ in group], mult, row_pad)
             for group, (_, dtype, mult, row_pad) in zip(groups, kinds)]
    stacked = [_rows2d(shards[n].astype(BF16)) for n in STACKED]
    gathered = gather_all(packs + stacked)
    full = {}
    me = 2 * lax.axis_index("x") + lax.axis_index("y")
    for group, got, (_, _, mult, _) in zip(groups, gathered, kinds):
        pieces = _unpack(got, [shards[n].shape for n, _ in group], mult, lead=(N_CHIPS,))
        for (n, ax), p in zip(group, pieces):
            own = shards[n].astype(p.dtype)
            full[n] = jnp.concatenate([jnp.where(me == j, own, p[j]) for j in range(N_CHIPS)], axis=ax)
    for n, got, own in zip(STACKED, gathered[len(packs):], stacked):
        full[n] = place_own(got, own)
    return full


def _split_shards(full, axis):
    return jnp.stack(jnp.split(full, N_CHIPS, axis=axis))


def _ffn_fwd(x, gain, wg4, wu4, wd4, layer):
    h = rmsnorm_fwd(x, gain, name="ffn_norm_fwd")
    a4, b4, hid4 = ffn_up(h, wg4, wu4, layer)
    xo = ffn_down(hid4, wd4, layer, x)
    return xo, (x, h, a4, b4, hid4)


def _ffn_bwd(dout, saved, gain, wg4, wu4, wd4, layer):
    x, h, a4, b4, hid4 = saved
    da4, db4 = ffn_bwd_hidden(dout, wd4, layer, a4, b4)
    dwd4 = ffn_wgrad_out(hid4, dout)
    dwg4, dwu4 = ffn_wgrad_in(h, da4, db4)
    dh = ffn_dh(da4, db4, wg4, wu4, layer)
    dx, dgain = rmsnorm_bwd(x, gain, dh, dout, name="ffn_norm_bwd")
    return dx, dgain, dwg4, dwu4, dwd4


def _pool_layer_fwd(x, gain, w_in, wgrp, scale):
    h = rmsnorm_fwd(x, gain, name="pool_norm_fwd")
    u = linear([(h, w_in, "nn")], name="pool_in")
    xo, p = pool_fwd(u, wgrp, scale, x)
    return xo, (x, h, p)


def _pool_layer_bwd(dout, saved, gain, w_in, wgrp, scale):
    x, h, p = saved
    dp, dwgrp, dscale = pool_bwd_group(dout, p, wgrp, scale)
    du = pool_bwd_window(dp)
    (dw_in,) = wgrad(h, [du], name="pool_dwin")
    dh = linear([(du, w_in, "nt")], name="pool_dh")
    dx, dgain = rmsnorm_bwd(x, gain, dh, dout, name="pool_norm_bwd")
    return dx, dgain, dw_in, dwgrp, dscale


def _ssd_layer_fwd(x, gain, w_z, w_xbc, w_dt, conv_w, conv_b, dt_bias, a_log, d_full, out_norm, w_out):
    h = rmsnorm_fwd(x, gain, name="ssd_norm_fwd")
    z = linear([(h, w_z, "nn")], name="ssd_in_z")
    xbc = linear([(h, w_xbc, "nn")], tn=2048, name="ssd_in_xbc")
    dt_raw = linear([(h, w_dt, "nn")], name="ssd_in_dt")
    act = conv_fwd(xbc, conv_w, conv_b)
    y, states = ssd_scan_fwd(act, dt_raw, dt_bias, a_log, d_full)
    gn = gate_norm_fwd(y, z, out_norm)
    xo = linear([(gn, w_out, "nn")], res=x, name="ssd_out")
    return xo, (x, h, z, xbc, dt_raw, act, y, states, gn)


def _ssd_layer_bwd(dout, saved, gain, w_z, w_xbc, w_dt, conv_w, conv_b, dt_bias, a_log, d_full, out_norm, w_out):
    x, h, z, xbc, dt_raw, act, y, states, gn = saved
    dgn = linear([(dout, w_out, "nt")], name="ssd_dgn")
    (dw_out,) = wgrad(gn, [dout], name="ssd_dwout")
    dy, dz, dout_norm = gate_norm_bwd(dgn, y, z, out_norm)
    dact, ddt_raw, dbias, dalog, dd_full = ssd_scan_bwd(dy, act, dt_raw, dt_bias, a_log, d_full, states)
    dpre, dconv_w8, dconv_b = conv_bwd_pre(dact, xbc, conv_w, conv_b)
    dxbc = conv_bwd_input(dpre, conv_w)
    ddt_b = ddt_raw.astype(BF16)
    (dw_z,) = wgrad(h, [dz], name="ssd_dwz")
    (dw_xbc,) = wgrad(h, [dxbc], tn=2048, name="ssd_dwxbc")
    (dw_dt,) = wgrad(h, [ddt_b], name="ssd_dwdt")
    dh = linear([(dz, w_z, "nt"), (dxbc, w_xbc, "nt"), (ddt_b, w_dt, "nt")], tm=256, name="ssd_dh")
    dx, dgain = rmsnorm_bwd(x, gain, dh, dout, name="ssd_norm_bwd")
    dw_in = jnp.concatenate([dw_z, dw_xbc, dw_dt], axis=1)
    dd = dd_full.reshape(NH, HP).sum(axis=1).reshape(1, NH)
    return dx, dgain, dw_in, dconv_w8[:4], dconv_b, dbias, dalog, dd, dout_norm, dw_out


def _sb_layer_fwd(x, gain, w_qkv, qg, kg, w_out):
    h = rmsnorm_fwd(x, gain, name="sb_norm_fwd")
    qkv = linear([(h, w_qkv, "nn")], tn=1024, name="sb_qkv")
    qs, kn, v = sb_prep_fwd(qkv, qg, kg)
    o, rsave = sb_fwd(qs, kn, v)
    xo = linear([(o, w_out, "nn")], res=x, name="sb_out")
    return xo, (x, h, qkv, qs, kn, v, o, rsave)


def _sb_layer_bwd(dout, saved, gain, w_qkv, qg, kg, w_out):
    x, h, qkv, qs, kn, v, o, rsave = saved
    do = linear([(dout, w_out, "nt")], out_dtype=BF16, name="sb_do")
    (dw_out,) = wgrad(o, [dout], name="sb_dwout")
    dqs, dkn, dv = sb_bwd(qs, kn, v, do, rsave)
    dqkv, dqg, dkg = sb_prep_bwd(dqs, dkn, dv, qkv, qg, kg)
    (dw_qkv,) = wgrad(h, [dqkv], tn=1024, name="sb_dwqkv")
    dh = linear([(dqkv, w_qkv, "nt")], name="sb_dh")
    dx, dgain = rmsnorm_bwd(x, gain, dh, dout, name="sb_norm_bwd")
    dqg = dqg.reshape(SBH, SBD).sum(axis=0).reshape(1, SBD)
    dkg = dkg.reshape(SBH, SBD).sum(axis=0).reshape(1, SBD)
    return dx, dgain, dw_qkv, dqg, dkg, dw_out


def _local_step(x, target, full, rep):
    S = x.shape[0]
    d_full = jnp.repeat(rep["ssd_d"][0], HP).reshape(1, DI)
    qg = jnp.tile(rep["sb_q_norm"][0], 2).reshape(1, LANES)
    kg = jnp.tile(rep["sb_k_norm"][0], 2).reshape(1, LANES)
    ssd_in = full["ssd_in"][0]
    w_z, w_xbc, w_dt = ssd_in[:, :DI], ssd_in[:, DI:DI + CONV_CH], ssd_in[:, DI + CONV_CH:]
    conv_w = full["ssd_conv_w"][0]
    conv_b = rep["ssd_conv_b"]
    pool_scale = full["pool_scale"]

    def mixer_args(i):
        kind, j = i % 3, i // 3
        if kind == 0:
            return (full["pool_in"][j], full["pool_group"][j], pool_scale[j:j + 1])
        if kind == 1:
            return (w_z, w_xbc, w_dt, conv_w, conv_b, rep["ssd_dt_bias"], rep["ssd_a_log"], d_full,
                    rep["ssd_out_norm"], full["ssd_out"][0])
        return (full["sb_qkv"][0], qg, kg, full["sb_out"][0])

    fwd = (_pool_layer_fwd, _ssd_layer_fwd, _sb_layer_fwd)
    bwd = (_pool_layer_bwd, _ssd_layer_bwd, _sb_layer_bwd)
    saved = []
    for i in range(DEPTH):
        x, sm = fwd[i % 3](x, rep["mix_norm"][i], *mixer_args(i))
        x, sf = _ffn_fwd(x, rep["ffn_norm"][i], full["ffn_gate"], full["ffn_up"], full["ffn_down"], i)
        saved.append((sm, sf))

    colsq, dx = loss_head(x, target)
    loss = 0.5 * jnp.sum(colsq) / D

    g = {n: [None] * DEPTH for n in ("mix_norm", "ffn_norm", "ffn_gate", "ffn_up", "ffn_down")}
    g["pool_in"], g["pool_group"], g["pool_scale"] = [None] * 2, [None] * 2, [None] * 2
    for i in reversed(range(DEPTH)):
        sm, sf = saved[i]
        dx, g["ffn_norm"][i], g["ffn_gate"][i], g["ffn_up"][i], g["ffn_down"][i] = _ffn_bwd(
            dx, sf, rep["ffn_norm"][i], full["ffn_gate"], full["ffn_up"], full["ffn_down"], i)
        kind, j = i % 3, i // 3
        res = bwd[kind](dx, sm, rep["mix_norm"][i], *mixer_args(i))
        dx, g["mix_norm"][i] = res[0], res[1]
        if kind == 0:
            g["pool_in"][j], g["pool_group"][j], g["pool_scale"][j] = res[2:]
        elif kind == 1:
            dw_in, dconv_w, dconv_b, dbias, dalog, dd, don, dw_out = res[2:]
            g.update(ssd_in=dw_in[None], ssd_conv_w=dconv_w[None], ssd_conv_b=dconv_b, ssd_dt_bias=dbias,
                     ssd_a_log=dalog, ssd_d=dd, ssd_out_norm=don, ssd_out=dw_out[None])
        else:
            dw_qkv, dqg, dkg, dw_out = res[2:]
            g.update(sb_qkv=dw_qkv[None], sb_q_norm=dqg, sb_k_norm=dkg, sb_out=dw_out[None])
    for n in ("mix_norm", "ffn_norm", "pool_scale"):
        g[n] = jnp.concatenate(g[n], axis=0)
    for n in ("pool_in", "pool_group"):
        g[n] = jnp.stack(g[n])
    for n in ("ffn_gate", "ffn_up", "ffn_down"):
        g[n] = jnp.concatenate(g[n], axis=1)
    return loss, dx, g


def kernel(x, mix_norm, pool_in, pool_group, pool_scale, ssd_in, ssd_conv_w, ssd_conv_b, ssd_dt_bias, ssd_a_log, ssd_d, ssd_out_norm, ssd_out, sb_qkv, sb_q_norm, sb_k_norm, sb_out, ffn_norm, ffn_gate, ffn_up, ffn_down, loss_target, m_mix_norm, m_pool_in, m_pool_group, m_pool_scale, m_ssd_in, m_ssd_conv_w, m_ssd_conv_b, m_ssd_dt_bias, m_ssd_a_log, m_ssd_d, m_ssd_out_norm, m_ssd_out, m_sb_qkv, m_sb_q_norm, m_sb_k_norm, m_sb_out, m_ffn_norm, m_ffn_gate, m_ffn_up, m_ffn_down, v_mix_norm, v_pool_in, v_pool_group, v_pool_scale, v_ssd_in, v_ssd_conv_w, v_ssd_conv_b, v_ssd_dt_bias, v_ssd_a_log, v_ssd_d, v_ssd_out_norm, v_ssd_out, v_sb_qkv, v_sb_q_norm, v_sb_k_norm, v_sb_out, v_ffn_norm, v_ffn_gate, v_ffn_up, v_ffn_down):
    given = dict(locals())
    w = {n: given[n] for n in WEIGHT_ORDER}
    m = {n: given["m_" + n] for n in WEIGHT_ORDER}
    v = {n: given["v_" + n] for n in WEIGHT_ORDER}
    packed_names = [s[0] for s in PACKED]

    full = _gather_weights(w)
    rep = {n: w[n] for n in REPLICATED}

    loss, dx, g = _local_step(x[0], loss_target[0], full, rep)
    loss = lax.psum(loss, ("x", "y", "c"))

    gstack = jnp.stack([_pack([_split_shards(g[n], ax)[j] for n, ax, _ in PACKED]) for j in range(N_CHIPS)])
    mine, other = reduce_scatter([gstack] + [g[n] for n in STACKED])
    out = {}
    gs, ds, ms, vs = adamw_halves(_pack([w[n] for n in packed_names]), mine[0], other[0],
                                  _pack([m[n] for n in packed_names]), _pack([v[n] for n in packed_names]),
                                  name="adamw_packed")
    for key, flat in (("g", gs), ("d", ds), ("m", ms), ("v", vs)):
        for n, a in zip(packed_names, _unpack(flat, [w[n].shape for n in packed_names])):
            out[key, n] = a
    for n, g_mine, g_other in zip(STACKED, mine[1:], other[1:]):
        res = adamw_halves(_rows2d(w[n]), g_mine, g_other, _rows2d(m[n]), _rows2d(v[n]), name="adamw_" + n)
        for key, a in zip("gdmv", res):
            out[key, n] = a.reshape(w[n].shape)

    rep_shapes = [w[n].shape for n in REPLICATED]
    pack_small = functools.partial(_pack, row_pad=8)
    gsum = allreduce_small(pack_small([g[n] for n in REPLICATED]))
    gs = gsum
    ds, ms, vs = adamw(pack_small([w[n] for n in REPLICATED]), gsum, pack_small([m[n] for n in REPLICATED]),
                       pack_small([v[n] for n in REPLICATED]), name="adamw_replicated")
    for key, flat in (("g", gs), ("d", ds), ("m", ms), ("v", vs)):
        for n, a in zip(REPLICATED, _unpack(flat, rep_shapes)):
            out[key, n] = a

    return (loss, dx[None], *[out["g", n] for n in WEIGHT_ORDER], *[out["d", n] for n in WEIGHT_ORDER],
            *[out["m", n] for n in WEIGHT_ORDER], *[out["v", n] for n in WEIGHT_ORDER])
```

```python
import math

import jax
import jax.numpy as jnp
from jax import lax
from jax.experimental import pallas as pl
from jax.experimental.pallas import tpu as pltpu

F32 = jnp.float32
BF16 = jnp.bfloat16

D = 1024
DEPTH = 4
EPS = 1e-6
POOL_WINDOWS = (2, 4, 8, 16)
PG = 256
DI = 2048
NH = 32
HP = 64
NG = 8
NS = 128
GW = 256
CH = 256
CONV_CH = 4096
SSD_IN = 6176
SBH = 16
SBD = 64
FH = 2816
N_CHIPS = 4
LANES = 128

ADAM_LR = 0.001
ADAM_B1 = 0.9
ADAM_B2 = 0.999
ADAM_EPS = 1e-08
ADAM_WD = 0.01
ADAM_STEP = 10

VMEM_LIMIT = 56 * 1024 * 1024


def _pcall(body, **kw):
    return pl.pallas_call(body, **kw)


def _cp(*sem):
    return pltpu.CompilerParams(dimension_semantics=sem, vmem_limit_bytes=VMEM_LIMIT)


def _dot(a, b, prec=None):
    return lax.dot_general(a, b, (((1,), (0,)), ((), ())), precision=prec, preferred_element_type=F32)


def _dot_nt(a, b, prec=None):
    return lax.dot_general(a, b, (((1,), (1,)), ((), ())), precision=prec, preferred_element_type=F32)


def _dot_tn(a, b, prec=None):
    return lax.dot_general(a, b, (((0,), (0,)), ((), ())), precision=prec, preferred_element_type=F32)


def _split3(x):
    x1 = x.astype(BF16)
    r = x - x1.astype(F32)
    x2 = r.astype(BF16)
    return x1, x2, (r - x2.astype(F32)).astype(BF16)


def _sel(dot, x, mask, x_first=True):
    mb = mask.astype(BF16)
    p = [dot(xi, mb) if x_first else dot(mb, xi) for xi in _split3(x)]
    return (p[0] + p[1]) + p[2]


def _sigmoid(x):
    return 1.0 / (1.0 + jnp.exp(-x))


def _iota(shape, axis):
    return lax.broadcasted_iota(jnp.int32, shape, axis)


def linear(pairs, res=None, out_dtype=F32, tm=512, tn=None, name="linear"):
    M = pairs[0][0].shape[0]
    N = pairs[0][1].shape[1] if pairs[0][2] == "nn" else pairs[0][1].shape[0]
    tm = min(tm, M)
    tn = N if tn is None else min(tn, N)
    n_pairs = len(pairs)
    modes = [p[2] for p in pairs]

    def body(*refs):
        acc = None
        for k in range(n_pairs):
            a = refs[2 * k][...].astype(BF16)
            w = refs[2 * k + 1][...]
            t = _dot(a, w) if modes[k] == "nn" else _dot_nt(a, w)
            acc = t if acc is None else acc + t
        if res is not None:
            acc = acc + refs[2 * n_pairs][...]
        refs[-1][...] = acc.astype(out_dtype)

    in_specs, args = [], []
    for a, w, mode in pairs:
        K = a.shape[1]
        in_specs.append(pl.BlockSpec((tm, K), lambda j, i: (i, 0)))
        if mode == "nn":
            in_specs.append(pl.BlockSpec((K, tn), lambda j, i: (0, j)))
        else:
            in_specs.append(pl.BlockSpec((tn, K), lambda j, i: (j, 0)))
        args += [a, w]
    if res is not None:
        in_specs.append(pl.BlockSpec((tm, tn), lambda j, i: (i, j)))
        args.append(res)
    return _pcall(
        body, name=name, grid=(N // tn, M // tm), in_specs=in_specs,
        out_specs=pl.BlockSpec((tm, tn), lambda j, i: (i, j)),
        out_shape=jax.ShapeDtypeStruct((M, N), out_dtype),
        compiler_params=_cp("parallel", "arbitrary"))(*args)


def wgrad(a, gs, tk=1024, tn=None, tm=1024, name="wgrad"):
    M, Ka = a.shape
    N = gs[0].shape[1]
    tk, tm = min(tk, Ka), min(tm, M)
    tn = N if tn is None else min(tn, N)
    n_g = len(gs)

    def body(*refs):
        a_ref, g_refs, o_refs = refs[0], refs[1:1 + n_g], refs[1 + n_g:]
        m = pl.program_id(2)
        at = a_ref[...].astype(BF16)
        for g_ref, o_ref in zip(g_refs, o_refs):
            t = _dot_tn(at, g_ref[...].astype(BF16))

            @pl.when(m == 0)
            def _():
                o_ref[...] = t

            @pl.when(m > 0)
            def _():
                o_ref[...] += t

    out = _pcall(
        body, name=name, grid=(Ka // tk, N // tn, M // tm),
        in_specs=[pl.BlockSpec((tm, tk), lambda k, j, m: (m, k))]
        + [pl.BlockSpec((tm, tn), lambda k, j, m: (m, j))] * n_g,
        out_specs=[pl.BlockSpec((tk, tn), lambda k, j, m: (k, j))] * n_g,
        out_shape=[jax.ShapeDtypeStruct((Ka, N), F32)] * n_g,
        compiler_params=_cp("parallel", "parallel", "arbitrary"))(a, *gs)
    return out


def rmsnorm_fwd(x, gain, tm=512, name="rmsnorm_fwd"):
    S, Dm = x.shape
    tm = min(tm, S)

    def body(x_ref, g_ref, o_ref):
        xv = x_ref[...]
        r = lax.rsqrt(jnp.mean(xv * xv, axis=-1, keepdims=True) + EPS)
        o_ref[...] = (xv * r * g_ref[...]).astype(BF16)

    return _pcall(
        body, name=name, grid=(S // tm,),
        in_specs=[pl.BlockSpec((tm, Dm), lambda i: (i, 0)), pl.BlockSpec((1, Dm), lambda i: (0, 0))],
        out_specs=pl.BlockSpec((tm, Dm), lambda i: (i, 0)),
        out_shape=jax.ShapeDtypeStruct((S, Dm), BF16),
        compiler_params=_cp("parallel"))(x, gain.reshape(1, Dm))


def rmsnorm_bwd(x, gain, dh, dres, tm=512, name="rmsnorm_bwd"):
    S, Dm = x.shape
    tm = min(tm, S)

    def body(x_ref, g_ref, dh_ref, dr_ref, dx_ref, dg_ref):
        i = pl.program_id(0)
        xv = x_ref[...]
        r = lax.rsqrt(jnp.mean(xv * xv, axis=-1, keepdims=True) + EPS)
        y = xv * r
        dhv = dh_ref[...]
        dy = dhv * g_ref[...]
        dx_ref[...] = dr_ref[...] + r * (dy - y * jnp.mean(dy * y, axis=-1, keepdims=True))
        part = jnp.sum(dhv * y, axis=0, keepdims=True)

        @pl.when(i == 0)
        def _():
            dg_ref[...] = part

        @pl.when(i > 0)
        def _():
            dg_ref[...] += part

    return _pcall(
        body, name=name, grid=(S // tm,),
        in_specs=[pl.BlockSpec((tm, Dm), lambda i: (i, 0)), pl.BlockSpec((1, Dm), lambda i: (0, 0)),
                  pl.BlockSpec((tm, Dm), lambda i: (i, 0)), pl.BlockSpec((tm, Dm), lambda i: (i, 0))],
        out_specs=[pl.BlockSpec((tm, Dm), lambda i: (i, 0)), pl.BlockSpec((1, Dm), lambda i: (0, 0))],
        out_shape=[jax.ShapeDtypeStruct((S, Dm), F32), jax.ShapeDtypeStruct((1, Dm), F32)],
        compiler_params=_cp("arbitrary"))(x, gain.reshape(1, Dm), dh, dres)


FS = FH // N_CHIPS


def ffn_up(h, wg4, wu4, layer, tm=512):
    S = h.shape[0]
    tm = min(tm, S)

    def body(h_ref, wg_ref, wu_ref, a_ref, b_ref, hid_ref):
        hv = h_ref[...]
        a = _dot(hv, wg_ref[0])
        b = _dot(hv, wu_ref[0])
        a_ref[0] = a
        b_ref[0] = b
        hid_ref[0] = (a * _sigmoid(a) * b).astype(BF16)

    wspec = pl.BlockSpec((1, D, FS), lambda j, i: (j, layer, 0))
    aspec = pl.BlockSpec((1, tm, FS), lambda j, i: (j, i, 0))
    return _pcall(
        body, name="ffn_up", grid=(N_CHIPS, S // tm),
        in_specs=[pl.BlockSpec((tm, D), lambda j, i: (i, 0)), wspec, wspec], out_specs=[aspec] * 3,
        out_shape=[jax.ShapeDtypeStruct((N_CHIPS, S, FS), F32), jax.ShapeDtypeStruct((N_CHIPS, S, FS), F32),
                   jax.ShapeDtypeStruct((N_CHIPS, S, FS), BF16)],
        compiler_params=_cp("parallel", "arbitrary"))(h, wg4, wu4)


def ffn_down(hid4, wd4, layer, x, tm=512):
    S = x.shape[0]
    tm = min(tm, S)

    def body(hid_ref, wd_ref, x_ref, o_ref):
        acc = x_ref[...]
        for j in range(N_CHIPS):
            acc = acc + _dot(hid_ref[j], wd_ref[j])
        o_ref[...] = acc

    return _pcall(
        body, name="ffn_down", grid=(S // tm,),
        in_specs=[pl.BlockSpec((N_CHIPS, tm, FS), lambda i: (0, i, 0)),
                  pl.BlockSpec((N_CHIPS, FS, D), lambda i: (0, layer, 0)), pl.BlockSpec((tm, D), lambda i: (i, 0))],
        out_specs=pl.BlockSpec((tm, D), lambda i: (i, 0)),
        out_shape=jax.ShapeDtypeStruct((S, D), F32), compiler_params=_cp("parallel"))(hid4, wd4, x)


def ffn_bwd_hidden(dout, wd4, layer, a4, b4, tm=1024):
    S = dout.shape[0]
    tm = min(tm, S)

    def body(do_ref, wd_ref, a_ref, b_ref, da_ref, db_ref):
        dhid = _dot_nt(do_ref[...].astype(BF16), wd_ref[0])
        av, bv = a_ref[0], b_ref[0]
        s = _sigmoid(av)
        da_ref[0] = (dhid * bv * (s * (1.0 + av * (1.0 - s)))).astype(BF16)
        db_ref[0] = (dhid * (av * s)).astype(BF16)

    aspec = pl.BlockSpec((1, tm, FS), lambda i, j: (j, i, 0))
    return _pcall(
        body, name="ffn_bwd_hidden", grid=(S // tm, N_CHIPS),
        in_specs=[pl.BlockSpec((tm, D), lambda i, j: (i, 0)), pl.BlockSpec((1, FS, D), lambda i, j: (j, layer, 0)),
                  aspec, aspec],
        out_specs=[aspec] * 2, out_shape=[jax.ShapeDtypeStruct((N_CHIPS, S, FS), BF16)] * 2,
        compiler_params=_cp("parallel", "arbitrary"))(dout, wd4, a4, b4)


def ffn_wgrad_in(h, da4, db4, tm=2048):
    S = h.shape[0]
    tm = min(tm, S)

    def body(h_ref, da_ref, db_ref, dg_ref, du_ref):
        m = pl.program_id(1)
        hv = h_ref[...]
        for g_ref, o_ref in ((da_ref, dg_ref), (db_ref, du_ref)):
            t = _dot_tn(hv, g_ref[0])

            @pl.when(m == 0)
            def _():
                o_ref[0] = t

            @pl.when(m > 0)
            def _():
                o_ref[0] += t

    aspec = pl.BlockSpec((1, tm, FS), lambda j, m: (j, m, 0))
    ospec = pl.BlockSpec((1, D, FS), lambda j, m: (j, 0, 0))
    return _pcall(
        body, name="ffn_wgrad_in", grid=(N_CHIPS, S // tm),
        in_specs=[pl.BlockSpec((tm, D), lambda j, m: (m, 0)), aspec, aspec], out_specs=[ospec] * 2,
        out_shape=[jax.ShapeDtypeStruct((N_CHIPS, D, FS), F32)] * 2,
        compiler_params=_cp("parallel", "arbitrary"))(h, da4, db4)


def ffn_wgrad_out(hid4, dout, tm=2048):
    S = dout.shape[0]
    tm = min(tm, S)

    def body(hid_ref, do_ref, o_ref):
        m = pl.program_id(1)
        t = _dot_tn(hid_ref[0], do_ref[...].astype(BF16))

        @pl.when(m == 0)
        def _():
            o_ref[0] = t

        @pl.when(m > 0)
        def _():
            o_ref[0] += t

    return _pcall(
        body, name="ffn_wgrad_out", grid=(N_CHIPS, S // tm),
        in_specs=[pl.BlockSpec((1, tm, FS), lambda j, m: (j, m, 0)), pl.BlockSpec((tm, D), lambda j, m: (m, 0))],
        out_specs=pl.BlockSpec((1, FS, D), lambda j, m: (j, 0, 0)),
        out_shape=jax.ShapeDtypeStruct((N_CHIPS, FS, D), F32),
        compiler_params=_cp("parallel", "arbitrary"))(hid4, dout)


def ffn_dh(da4, db4, wg4, wu4, layer, tm=256):
    S = da4.shape[1]
    tm = min(tm, S)

    def body(da_ref, db_ref, wg_ref, wu_ref, o_ref):
        acc = _dot_nt(da_ref[0], wg_ref[0]) + _dot_nt(db_ref[0], wu_ref[0])
        for j in range(1, N_CHIPS):
            acc = acc + _dot_nt(da_ref[j], wg_ref[j]) + _dot_nt(db_ref[j], wu_ref[j])
        o_ref[...] = acc

    aspec = pl.BlockSpec((N_CHIPS, tm, FS), lambda i: (0, i, 0))
    wspec = pl.BlockSpec((N_CHIPS, D, FS), lambda i: (0, layer, 0))
    return _pcall(
        body, name="ffn_dh", grid=(S // tm,), in_specs=[aspec, aspec, wspec, wspec],
        out_specs=pl.BlockSpec((tm, D), lambda i: (i, 0)),
        out_shape=jax.ShapeDtypeStruct((S, D), F32), compiler_params=_cp("parallel"))(da4, db4, wg4, wu4)


POOL_T = 128
POOL_HALO = 16


def pool_fwd(u, wgrp, scale, x_res):
    S = u.shape[0]
    T, HB = min(POOL_T, S), POOL_HALO
    per = T // HB

    def body(u_ref, tail_ref, wg_ref, sc_ref, x_ref, xo_ref, p_ref):
        i = pl.program_id(0)
        uc = u_ref[...]
        tail = jnp.where(i > 0, tail_ref[...], 0.0)
        d_cur = _iota((T, T), 0) - _iota((T, T), 1)
        d_tail = _iota((T, HB), 0) - _iota((T, HB), 1) + HB
        tg = i * T + _iota((T, 1), 0)
        for g, w in enumerate(POOL_WINDOWS):
            gs = slice(g * PG, (g + 1) * PG)
            band = (d_cur >= 0) & (d_cur < w)
            band_t = (d_tail >= 0) & (d_tail < w)
            ug = uc[:, gs]
            ws = _sel(_dot, ug, band, False) + _sel(_dot, tail[:, gs], band_t, False)
            cnt = jnp.minimum(tg + 1, w).astype(F32)
            pb = (ws / cnt - ug).astype(BF16)
            p_ref[:, gs] = pb
            xo_ref[:, gs] = x_ref[:, gs] + _dot(pb, wg_ref[g]) * sc_ref[:, gs]

    return _pcall(
        body, name="pool_fwd", grid=(S // T,),
        in_specs=[pl.BlockSpec((T, D), lambda i: (i, 0)),
                  pl.BlockSpec((HB, D), lambda i: (jnp.maximum(i * per - 1, 0), 0)),
                  pl.BlockSpec((4, PG, PG), lambda i: (0, 0, 0)), pl.BlockSpec((1, D), lambda i: (0, 0)),
                  pl.BlockSpec((T, D), lambda i: (i, 0))],
        out_specs=[pl.BlockSpec((T, D), lambda i: (i, 0))] * 2,
        out_shape=[jax.ShapeDtypeStruct((S, D), F32), jax.ShapeDtypeStruct((S, D), BF16)],
        compiler_params=_cp("parallel"))(u, u, wgrp, scale, x_res)


def pool_bwd_group(dm, p, wgrp, scale, tm=512):
    S = dm.shape[0]
    tm = min(tm, S)

    def body(dm_ref, p_ref, wg_ref, sc_ref, dp_ref, dwg_ref, dsc_ref):
        i = pl.program_id(0)

        @pl.when(i == 0)
        def _():
            dwg_ref[...] = jnp.zeros_like(dwg_ref)
            dsc_ref[...] = jnp.zeros_like(dsc_ref)

        for g in range(4):
            gs = slice(g * PG, (g + 1) * PG)
            dmg, pg, wg = dm_ref[:, gs], p_ref[:, gs], wg_ref[g]
            dsc_ref[:, gs] += jnp.sum(dmg * _dot(pg, wg), axis=0, keepdims=True)
            dy = (dmg * sc_ref[:, gs]).astype(BF16)
            dp_ref[:, gs] = _dot_nt(dy, wg)
            dwg_ref[g] += _dot_tn(pg, dy)

    return _pcall(
        body, name="pool_bwd_group", grid=(S // tm,),
        in_specs=[pl.BlockSpec((tm, D), lambda i: (i, 0)), pl.BlockSpec((tm, D), lambda i: (i, 0)),
                  pl.BlockSpec((4, PG, PG), lambda i: (0, 0, 0)), pl.BlockSpec((1, D), lambda i: (0, 0))],
        out_specs=[pl.BlockSpec((tm, D), lambda i: (i, 0)), pl.BlockSpec((4, PG, PG), lambda i: (0, 0, 0)),
                   pl.BlockSpec((1, D), lambda i: (0, 0))],
        out_shape=[jax.ShapeDtypeStruct((S, D), F32), jax.ShapeDtypeStruct((4, PG, PG), F32),
                   jax.ShapeDtypeStruct((1, D), F32)],
        compiler_params=_cp("arbitrary"))(dm, p, wgrp, scale)


def pool_bwd_window(dp):
    S = dp.shape[0]
    T, HB = min(POOL_T, S), POOL_HALO
    per = T // HB
    nt = S // T

    def body(dp_ref, nxt_ref, du_ref):
        i = pl.program_id(0)
        dc = dp_ref[...]
        nxt = jnp.where(i < nt - 1, nxt_ref[...], 0.0)
        d_cur = _iota((T, T), 1) - _iota((T, T), 0)
        d_nxt = _iota((T, HB), 1) - _iota((T, HB), 0) + T
        tg = i * T + _iota((T, 1), 0)
        tn_ = (i + 1) * T + _iota((HB, 1), 0)
        for g, w in enumerate(POOL_WINDOWS):
            gs = slice(g * PG, (g + 1) * PG)
            band = (d_cur >= 0) & (d_cur < w)
            band_n = (d_nxt >= 0) & (d_nxt < w)
            dcg = dc[:, gs]
            cur = dcg / jnp.minimum(tg + 1, w).astype(F32)
            nx = nxt[:, gs] / jnp.minimum(tn_ + 1, w).astype(F32)
            du_ref[:, gs] = (_sel(_dot, cur, band, False) + _sel(_dot, nx, band_n, False) - dcg).astype(BF16)

    return _pcall(
        body, name="pool_bwd_window", grid=(nt,),
        in_specs=[pl.BlockSpec((T, D), lambda i: (i, 0)),
                  pl.BlockSpec((HB, D), lambda i: (jnp.minimum((i + 1) * per, S // HB - 1), 0))],
        out_specs=pl.BlockSpec((T, D), lambda i: (i, 0)),
        out_shape=jax.ShapeDtypeStruct((S, D), BF16),
        compiler_params=_cp("parallel"))(dp, dp)


CONV_T = 256


def _shift_down(xc, prev8, j):
    if j == 0:
        return xc
    T = xc.shape[0]
    body = pltpu.roll(xc, j, 0)
    first = jnp.where(_iota((8, 1), 0) < j, pltpu.roll(prev8, j, 0), body[0:8])
    return jnp.concatenate([first, body[8:T]], axis=0)


def _shift_up(dc, next8, j):
    if j == 0:
        return dc
    T = dc.shape[0]
    body = pltpu.roll(dc, T - j, 0)
    last = jnp.where(_iota((8, 1), 0) + j < 8, body[T - 8:T], pltpu.roll(next8, 8 - j, 0))
    return jnp.concatenate([body[0:T - 8], last], axis=0)


def conv_fwd(xbc, conv_w, conv_b):
    S = xbc.shape[0]
    T = min(CONV_T, S)
    CB = 1024

    def body(x_ref, prev_ref, w_ref, b_ref, o_ref):
        i = pl.program_id(1)
        xc = x_ref[...]
        prev8 = jnp.where(i > 0, prev_ref[...], 0.0)
        pre = b_ref[...] + w_ref[3:4, :] * xc
        for j in range(1, 4):
            pre = pre + w_ref[3 - j:4 - j, :] * _shift_down(xc, prev8, j)
        o_ref[...] = pre * _sigmoid(pre)

    return _pcall(
        body, name="conv_fwd", grid=(CONV_CH // CB, S // T),
        in_specs=[pl.BlockSpec((T, CB), lambda c, i: (i, c)),
                  pl.BlockSpec((8, CB), lambda c, i: (jnp.maximum(i * (T // 8) - 1, 0), c)),
                  pl.BlockSpec((4, CB), lambda c, i: (0, c)), pl.BlockSpec((1, CB), lambda c, i: (0, c))],
        out_specs=pl.BlockSpec((T, CB), lambda c, i: (i, c)),
        out_shape=jax.ShapeDtypeStruct((S, CONV_CH), F32),
        compiler_params=_cp("parallel", "parallel"))(xbc, xbc, conv_w, conv_b)


def conv_bwd_pre(dact, xbc, conv_w, conv_b):
    S = xbc.shape[0]
    T = min(CONV_T, S)
    CB = 1024

    def body(da_ref, x_ref, prev_ref, w_ref, b_ref, dpre_ref, dw_ref, db_ref):
        i = pl.program_id(1)
        xc = x_ref[...]
        prev8 = jnp.where(i > 0, prev_ref[...], 0.0)
        sh = [_shift_down(xc, prev8, j) for j in range(4)]
        pre = b_ref[...] + w_ref[3:4, :] * sh[0]
        for j in range(1, 4):
            pre = pre + w_ref[3 - j:4 - j, :] * sh[j]
        s = _sigmoid(pre)
        dpre = da_ref[...] * (s * (1.0 + pre * (1.0 - s)))
        dpre_ref[...] = dpre
        rows = [jnp.sum(dpre * sh[3 - k], axis=0, keepdims=True) for k in range(4)]
        dw = jnp.concatenate(rows + [jnp.zeros((4, CB), F32)], axis=0)
        db = jnp.sum(dpre, axis=0, keepdims=True)

        @pl.when(i == 0)
        def _():
            dw_ref[...] = dw
            db_ref[...] = db

        @pl.when(i > 0)
        def _():
            dw_ref[...] += dw
            db_ref[...] += db

    return _pcall(
        body, name="conv_bwd_pre", grid=(CONV_CH // CB, S // T),
        in_specs=[pl.BlockSpec((T, CB), lambda c, i: (i, c)), pl.BlockSpec((T, CB), lambda c, i: (i, c)),
                  pl.BlockSpec((8, CB), lambda c, i: (jnp.maximum(i * (T // 8) - 1, 0), c)),
                  pl.BlockSpec((4, CB), lambda c, i: (0, c)), pl.BlockSpec((1, CB), lambda c, i: (0, c))],
        out_specs=[pl.BlockSpec((T, CB), lambda c, i: (i, c)), pl.BlockSpec((8, CB), lambda c, i: (0, c)),
                   pl.BlockSpec((1, CB), lambda c, i: (0, c))],
        out_shape=[jax.ShapeDtypeStruct((S, CONV_CH), F32), jax.ShapeDtypeStruct((8, CONV_CH), F32),
                   jax.ShapeDtypeStruct((1, CONV_CH), F32)],
        compiler_params=_cp("parallel", "arbitrary"))(dact, xbc, xbc, conv_w, conv_b)


def conv_bwd_input(dpre, conv_w):
    S = dpre.shape[0]
    T = min(CONV_T, S)
    CB = 1024
    nt = S // T

    def body(d_ref, nxt_ref, w_ref, o_ref):
        i = pl.program_id(1)
        dc = d_ref[...]
        next8 = jnp.where(i < nt - 1, nxt_ref[...], 0.0)
        acc = w_ref[3:4, :] * dc
        for j in range(1, 4):
            acc = acc + w_ref[3 - j:4 - j, :] * _shift_up(dc, next8, j)
        o_ref[...] = acc.astype(BF16)

    return _pcall(
        body, name="conv_bwd_input", grid=(CONV_CH // CB, nt),
        in_specs=[pl.BlockSpec((T, CB), lambda c, i: (i, c)),
                  pl.BlockSpec((8, CB), lambda c, i: (jnp.minimum((i + 1) * (T // 8), S // 8 - 1), c)),
                  pl.BlockSpec((4, CB), lambda c, i: (0, c))],
        out_specs=pl.BlockSpec((T, CB), lambda c, i: (i, c)),
        out_shape=jax.ShapeDtypeStruct((S, CONV_CH), BF16),
        compiler_params=_cp("parallel", "parallel"))(dpre, dpre, conv_w)


def _ssd_chunk_terms(dt_ref, bias_ref, alog_ref):
    L = CH
    dtp = dt_ref[...] + bias_ref[...]
    dt = jnp.maximum(dtp, 0.0) + jnp.log(1.0 + jnp.exp(-jnp.abs(dtp)))
    a = -jnp.exp(alog_ref[...])
    da = dt * a
    tri = _iota((L, L), 0) >= _iota((L, L), 1)
    acum = _sel(_dot, da, tri, False)
    triu = _iota((L, L), 0) <= _iota((L, L), 1)
    acum_row = _sel(_dot_tn, da, triu)
    expand = _iota((NH, DI), 1) // HP == _iota((NH, DI), 0)
    acum_full = _sel(_dot, acum, expand)
    e_full = jnp.exp(acum_full)
    w_full = jnp.exp(acum_full[L - 1:L, :] - acum_full)
    dt_full = _sel(_dot, dt, expand)
    return dtp, dt, a, acum, acum_row, expand, triu, e_full, w_full, dt_full


def ssd_scan_fwd(xbc_act, dt_raw, dt_bias, a_log, d_full):
    S = xbc_act.shape[0]
    L = CH
    nc = S // L

    def body(xs_ref, b_ref, c_ref, dt_ref, bias_ref, alog_ref, d_ref, y_ref, st_ref, state):
        c = pl.program_id(0)

        @pl.when(c == 0)
        def _():
            state[...] = jnp.zeros_like(state)

        st_ref[0] = state[...]
        _, _, _, acum, acum_row, _, _, e_full, w_full, dt_full = _ssd_chunk_terms(dt_ref, bias_ref, alog_ref)
        causal = _iota((L, L), 0) >= _iota((L, L), 1)
        lane_head = _iota((1, GW), 1) // HP
        for g in range(NG):
            gs = slice(g * GW, (g + 1) * GW)
            ns = slice(g * NS, (g + 1) * NS)
            xs_g = xs_ref[:, gs]
            xdt_g = xs_g * dt_full[:, gs]
            cg = c_ref[:, ns].astype(BF16)
            bg = b_ref[:, ns].astype(BF16)
            gmat = _dot_nt(cg, bg)
            yg = jnp.zeros((L, GW), F32)
            for hh in range(4):
                h = 4 * g + hh
                diff = acum[:, h:h + 1] - acum_row[h:h + 1, :]
                dk = jnp.exp(jnp.where(causal, diff, -1e30))
                xm = jnp.where(lane_head == hh, xdt_g, 0.0).astype(BF16)
                yg = yg + _dot((gmat * dk).astype(BF16), xm)
            sg = state[g]
            yoff = _dot(cg, sg.astype(BF16)) * e_full[:, gs]
            y_ref[:, gs] = yg + yoff + d_ref[:, gs] * xs_g
            state[g] = sg * e_full[L - 1:L, gs] + _dot_tn(bg, (w_full[:, gs] * xdt_g).astype(BF16))

    return _pcall(
        body, name="ssd_scan_fwd", grid=(nc,),
        in_specs=[pl.BlockSpec((L, DI), lambda c: (c, 0)), pl.BlockSpec((L, 1024), lambda c: (c, 2)),
                  pl.BlockSpec((L, 1024), lambda c: (c, 3)), pl.BlockSpec((L, NH), lambda c: (c, 0)),
                  pl.BlockSpec((1, NH), lambda c: (0, 0)), pl.BlockSpec((1, NH), lambda c: (0, 0)),
                  pl.BlockSpec((1, DI), lambda c: (0, 0))],
        out_specs=[pl.BlockSpec((L, DI), lambda c: (c, 0)), pl.BlockSpec((1, NG, NS, GW), lambda c: (c, 0, 0, 0))],
        out_shape=[jax.ShapeDtypeStruct((S, DI), F32), jax.ShapeDtypeStruct((nc, NG, NS, GW), F32)],
        scratch_shapes=[pltpu.VMEM((NG, NS, GW), F32)],
        compiler_params=_cp("arbitrary"))(xbc_act, xbc_act, xbc_act, dt_raw, dt_bias, a_log, d_full)


def ssd_scan_bwd(dy, xbc_act, dt_raw, dt_bias, a_log, d_full, states):
    S = xbc_act.shape[0]
    L = CH
    nc = S // L

    def body(dy_ref, xs_ref, b_ref, c_ref, dt_ref, bias_ref, alog_ref, d_ref, st_ref,
             dxbc_ref, ddt_ref, dbias_ref, dalog_ref, dd_ref, dstate):
        c = pl.program_id(0)

        @pl.when(c == 0)
        def _():
            dstate[...] = jnp.zeros_like(dstate)
            dbias_ref[...] = jnp.zeros_like(dbias_ref)
            dalog_ref[...] = jnp.zeros_like(dalog_ref)
            dd_ref[...] = jnp.zeros_like(dd_ref)

        dtp, dt, a, acum, acum_row, expand, triu, e_full, w_full, dt_full = _ssd_chunk_terms(
            dt_ref, bias_ref, alog_ref)
        causal = _iota((L, L), 0) >= _iota((L, L), 1)
        lane_head = _iota((1, GW), 1) // HP
        head_id = _iota((1, NH), 1)
        head_row = _iota((NH, 1), 0)
        dacum = jnp.zeros((L, NH), F32)
        dacum_t = jnp.zeros((NH, L), F32)
        red_parts = []
        dxdt_parts = []
        alast_parts = []
        for g in range(NG):
            gs = slice(g * GW, (g + 1) * GW)
            ns = slice(g * NS, (g + 1) * NS)
            xs_g = xs_ref[:, gs]
            xdt_g = xs_g * dt_full[:, gs]
            dy_g = dy_ref[:, gs]
            cg = c_ref[:, ns].astype(BF16)
            bg = b_ref[:, ns].astype(BF16)
            gmat = _dot_nt(cg, bg)
            sg = st_ref[0, g]
            dsg = dstate[g]
            sgb, dsgb = sg.astype(BF16), dsg.astype(BF16)
            cs = _dot(cg, sgb)
            bds = _dot(bg, dsgb)
            e_g, w_g = e_full[:, gs], w_full[:, gs]
            dxdt = w_g * bds
            dgsum = jnp.zeros((L, L), F32)
            for hh in range(4):
                h = 4 * g + hh
                hm = lane_head == hh
                diff = acum[:, h:h + 1] - acum_row[h:h + 1, :]
                dk = jnp.exp(jnp.where(causal, diff, -1e30))
                m = gmat * dk
                dym = jnp.where(hm, dy_g, 0.0).astype(BF16)
                xm = jnp.where(hm, xdt_g, 0.0).astype(BF16)
                dm = _dot_nt(dym, xm)
                dxdt = dxdt + _dot_tn(m.astype(BF16), dym)
                dgsum = dgsum + dm * dk
                em = dm * m
                dacum = dacum + jnp.sum(em, axis=1, keepdims=True) * (head_id == h).astype(F32)
                dacum_t = dacum_t + (head_row == h).astype(F32) * jnp.sum(em, axis=0, keepdims=True)
            dgb = dgsum.astype(BF16)
            edy = (e_g * dy_g).astype(BF16)
            wx = (w_g * xdt_g).astype(BF16)
            dc_g = _dot(dgb, bg) + _dot_nt(edy, sgb)
            db_g = _dot_tn(dgb, cg) + _dot_nt(wx, dsgb)
            dxbc_ref[:, DI + g * NS:DI + (g + 1) * NS] = db_g
            dxbc_ref[:, DI + 1024 + g * NS:DI + 1024 + (g + 1) * NS] = dc_g
            p2w = bds * xdt_g * w_g
            red_parts.append(dy_g * cs * e_g - p2w)
            alast_parts.append(jnp.sum(p2w, axis=0, keepdims=True)
                               + e_full[L - 1:L, gs] * jnp.sum(dsg * sg, axis=0, keepdims=True))
            dxdt_parts.append(dxdt)
            dstate[g] = e_full[L - 1:L, gs] * dsg + _dot_tn(cg, edy)
            dxbc_ref[:, gs] = dxdt * dt_full[:, gs] + dy_g * d_ref[:, gs]
            dd_ref[:, gs] += jnp.sum(dy_g * xs_g, axis=0, keepdims=True)
        red = jnp.concatenate(red_parts, axis=1)
        dxdt_all = jnp.concatenate(dxdt_parts, axis=1)
        alast = jnp.concatenate(alast_parts, axis=1)
        eye = _iota((NH, NH), 0) == _iota((NH, NH), 1)
        dacum = dacum - _sel(_dot_tn, dacum_t, eye) + _sel(_dot_nt, red, expand)
        dalast = _sel(_dot_nt, jnp.broadcast_to(alast, (8, DI)), expand)[0:1, :]
        dacum = dacum + jnp.where(_iota((L, 1), 0) == L - 1, dalast, 0.0)
        dda = _sel(_dot, dacum, triu, False)
        ddt = _sel(_dot_nt, dxdt_all * xs_ref[...], expand) + dda * a
        dalog_ref[...] += jnp.sum(dda * dt, axis=0, keepdims=True) * a
        ddt_raw = ddt * _sigmoid(dtp)
        ddt_ref[...] = ddt_raw
        dbias_ref[...] += jnp.sum(ddt_raw, axis=0, keepdims=True)

    rev = lambda c: (nc - 1 - c, 0)
    return _pcall(
        body, name="ssd_scan_bwd", grid=(nc,),
        in_specs=[pl.BlockSpec((L, DI), rev), pl.BlockSpec((L, DI), rev),
                  pl.BlockSpec((L, 1024), lambda c: (nc - 1 - c, 2)), pl.BlockSpec((L, 1024), lambda c: (nc - 1 - c, 3)),
                  pl.BlockSpec((L, NH), rev), pl.BlockSpec((1, NH), lambda c: (0, 0)),
                  pl.BlockSpec((1, NH), lambda c: (0, 0)), pl.BlockSpec((1, DI), lambda c: (0, 0)),
                  pl.BlockSpec((1, NG, NS, GW), lambda c: (nc - 1 - c, 0, 0, 0))],
        out_specs=[pl.BlockSpec((L, CONV_CH), rev), pl.BlockSpec((L, NH), rev),
                   pl.BlockSpec((1, NH), lambda c: (0, 0)), pl.BlockSpec((1, NH), lambda c: (0, 0)),
                   pl.BlockSpec((1, DI), lambda c: (0, 0))],
        out_shape=[jax.ShapeDtypeStruct((S, CONV_CH), F32), jax.ShapeDtypeStruct((S, NH), F32),
                   jax.ShapeDtypeStruct((1, NH), F32), jax.ShapeDtypeStruct((1, NH), F32),
                   jax.ShapeDtypeStruct((1, DI), F32)],
        scratch_shapes=[pltpu.VMEM((NG, NS, GW), F32)],
        compiler_params=_cp("arbitrary"))(dy, xbc_act, xbc_act, xbc_act, dt_raw, dt_bias, a_log, d_full, states)


def gate_norm_fwd(y, z, out_norm, tm=256):
    S = y.shape[0]
    tm = min(tm, S)

    def body(y_ref, z_ref, on_ref, o_ref):
        zv = z_ref[...]
        gin = y_ref[...] * (zv * _sigmoid(zv))
        for g in range(NG):
            gs = slice(g * GW, (g + 1) * GW)
            blk = gin[:, gs]
            r = lax.rsqrt(jnp.mean(blk * blk, axis=-1, keepdims=True) + EPS)
            o_ref[:, gs] = (blk * r * on_ref[:, gs]).astype(BF16)

    return _pcall(
        body, name="gate_norm_fwd", grid=(S // tm,),
        in_specs=[pl.BlockSpec((tm, DI), lambda i: (i, 0)), pl.BlockSpec((tm, DI), lambda i: (i, 0)),
                  pl.BlockSpec((1, DI), lambda i: (0, 0))],
        out_specs=pl.BlockSpec((tm, DI), lambda i: (i, 0)),
        out_shape=jax.ShapeDtypeStruct((S, DI), BF16),
        compiler_params=_cp("parallel"))(y, z, out_norm)


def gate_norm_bwd(dgn, y, z, out_norm, tm=256):
    S = y.shape[0]
    tm = min(tm, S)

    def body(dg_ref, y_ref, z_ref, on_ref, dy_ref, dz_ref, don_ref):
        i = pl.program_id(0)

        @pl.when(i == 0)
        def _():
            don_ref[...] = jnp.zeros_like(don_ref)

        zv, yv = z_ref[...], y_ref[...]
        s = _sigmoid(zv)
        sz = zv * s
        gin = yv * sz
        for g in range(NG):
            gs = slice(g * GW, (g + 1) * GW)
            blk = gin[:, gs]
            r = lax.rsqrt(jnp.mean(blk * blk, axis=-1, keepdims=True) + EPS)
            n = blk * r
            dg = dg_ref[:, gs]
            don_ref[:, gs] += jnp.sum(dg * n, axis=0, keepdims=True)
            dn = dg * on_ref[:, gs]
            dgin = r * (dn - n * jnp.mean(dn * n, axis=-1, keepdims=True))
            dy_ref[:, gs] = dgin * sz[:, gs]
            dz_ref[:, gs] = (dgin * yv[:, gs] * (s[:, gs] * (1.0 + zv[:, gs] * (1.0 - s[:, gs])))).astype(BF16)

    return _pcall(
        body, name="gate_norm_bwd", grid=(S // tm,),
        in_specs=[pl.BlockSpec((tm, DI), lambda i: (i, 0))] * 3 + [pl.BlockSpec((1, DI), lambda i: (0, 0))],
        out_specs=[pl.BlockSpec((tm, DI), lambda i: (i, 0)), pl.BlockSpec((tm, DI), lambda i: (i, 0)),
                   pl.BlockSpec((1, DI), lambda i: (0, 0))],
        out_shape=[jax.ShapeDtypeStruct((S, DI), F32), jax.ShapeDtypeStruct((S, DI), BF16),
                   jax.ShapeDtypeStruct((1, DI), F32)],
        compiler_params=_cp("arbitrary"))(dgn, y, z, out_norm)


SB_T = 256
SB_QSCALE = 0.125
SB_DEAD = -110.0
SB_UNSEEN = -1e30


def _head_norm(xv, lo):
    sq = xv * xv
    s0 = jnp.sum(jnp.where(lo, sq, 0.0), axis=-1, keepdims=True)
    s1 = jnp.sum(jnp.where(lo, 0.0, sq), axis=-1, keepdims=True)
    return jnp.where(lo, lax.rsqrt(s0 / SBD + EPS), lax.rsqrt(s1 / SBD + EPS))


def sb_prep_fwd(qkv, qg, kg, tm=256):
    S = qkv.shape[0]
    tm = min(tm, S)

    def body(x_ref, qg_ref, kg_ref, q_ref, k_ref, v_ref):
        lo = _iota((1, LANES), 1) < SBD
        for sl in range(D // LANES):
            cs = slice(sl * LANES, (sl + 1) * LANES)
            xq = x_ref[:, cs]
            q_ref[:, cs] = ((xq * _head_norm(xq, lo) * qg_ref[...]).astype(BF16).astype(F32) * SB_QSCALE).astype(BF16)
            xk = x_ref[:, D + sl * LANES:D + (sl + 1) * LANES]
            k_ref[:, cs] = (xk * _head_norm(xk, lo) * kg_ref[...]).astype(BF16)
        v_ref[...] = x_ref[:, 2 * D:3 * D].astype(BF16)

    return _pcall(
        body, name="sb_prep_fwd", grid=(S // tm,),
        in_specs=[pl.BlockSpec((tm, 3 * D), lambda i: (i, 0)), pl.BlockSpec((1, LANES), lambda i: (0, 0)),
                  pl.BlockSpec((1, LANES), lambda i: (0, 0))],
        out_specs=[pl.BlockSpec((tm, D), lambda i: (i, 0))] * 3,
        out_shape=[jax.ShapeDtypeStruct((S, D), BF16)] * 3,
        compiler_params=_cp("parallel"))(qkv, qg, kg)


def sb_prep_bwd(dqs, dkn, dv, qkv, qg, kg, tm=256):
    S = qkv.shape[0]
    tm = min(tm, S)

    def body(dq_ref, dk_ref, dv_ref, x_ref, qg_ref, kg_ref, dx_ref, dqg_ref, dkg_ref):
        i = pl.program_id(0)

        @pl.when(i == 0)
        def _():
            dqg_ref[...] = jnp.zeros_like(dqg_ref)
            dkg_ref[...] = jnp.zeros_like(dkg_ref)

        lo = _iota((1, LANES), 1) < SBD

        def one(xv, dh, gain):
            r = _head_norm(xv, lo)
            y = xv * r
            dy = dh * gain
            t = dy * y
            m0 = jnp.sum(jnp.where(lo, t, 0.0), axis=-1, keepdims=True)
            m1 = jnp.sum(jnp.where(lo, 0.0, t), axis=-1, keepdims=True)
            dx = r * (dy - y * (jnp.where(lo, m0, m1) / SBD))
            return dx, jnp.sum(dh * y, axis=0, keepdims=True)

        for sl in range(D // LANES):
            cs = slice(sl * LANES, (sl + 1) * LANES)
            dx, dg = one(x_ref[:, cs], dq_ref[:, cs] * SB_QSCALE, qg_ref[...])
            dx_ref[:, cs] = dx.astype(BF16)
            dqg_ref[:, cs] += dg
            ks = slice(D + sl * LANES, D + (sl + 1) * LANES)
            dx, dg = one(x_ref[:, ks], dk_ref[:, cs], kg_ref[...])
            dx_ref[:, ks] = dx.astype(BF16)
            dkg_ref[:, cs] += dg
        dx_ref[:, 2 * D:3 * D] = dv_ref[...].astype(BF16)

    return _pcall(
        body, name="sb_prep_bwd", grid=(S // tm,),
        in_specs=[pl.BlockSpec((tm, D), lambda i: (i, 0))] * 3
        + [pl.BlockSpec((tm, 3 * D), lambda i: (i, 0)), pl.BlockSpec((1, LANES), lambda i: (0, 0)),
           pl.BlockSpec((1, LANES), lambda i: (0, 0))],
        out_specs=[pl.BlockSpec((tm, 3 * D), lambda i: (i, 0)), pl.BlockSpec((1, D), lambda i: (0, 0)),
                   pl.BlockSpec((1, D), lambda i: (0, 0))],
        out_shape=[jax.ShapeDtypeStruct((S, 3 * D), BF16), jax.ShapeDtypeStruct((1, D), F32),
                   jax.ShapeDtypeStruct((1, D), F32)],
        compiler_params=_cp("arbitrary"))(dqs, dkn, dv, qkv, qg, kg)


def _split_dot(x, u):
    hi = x.astype(BF16)
    lo = (x - hi.astype(F32)).astype(BF16)
    return _dot(hi, u) + _dot(lo, u)


def _sb_logits(qh, kb, valid):
    z = _dot_nt(qh, kb)
    e = jnp.exp(-jnp.abs(z))
    lp = jnp.log(1.0 + e)
    lb = jnp.minimum(z, 0.0) - lp
    l1m = jnp.where(valid, lb - z, 0.0)
    return z, e, lb, l1m


def sb_fwd(qs, kn, v):
    S = qs.shape[0]
    T = min(SB_T, S)
    nq = S // T

    def body(q_ref, k_ref, v_ref, o_ref, r_ref, oacc, rrun):
        i = pl.program_id(1)
        qb = q_ref[...]
        lo = _iota((1, LANES), 1) < SBD
        row, col = _iota((T, T), 0), _iota((T, T), 1)
        u = (row > col).astype(BF16)
        lane_blk = _iota((T, LANES), 1)
        oacc[...] = jnp.zeros_like(oacc)
        for hh in range(2):
            hm = lo if hh == 0 else jnp.logical_not(lo)
            qh = jnp.where(hm, qb, jnp.zeros_like(qb))
            rrun[...] = jnp.zeros_like(rrun)
            r_ref[hh] = jnp.full((T, LANES), SB_UNSEEN, F32)

            def live(carry):
                s, rmax = carry
                return jnp.logical_and(s <= i, rmax > SB_DEAD)

            def step(carry, hm=hm, qh=qh, hh=hh):
                s, _ = carry
                j = i - s
                off = pl.multiple_of(j * T, T)
                kb = k_ref[pl.ds(off, T), :]
                vb = v_ref[pl.ds(off, T), :]
                vb = jnp.where(hm, vb, jnp.zeros_like(vb))
                valid = (j * T + col) < (i * T + row)
                _, _, lb, l1m = _sb_logits(qh, kb, valid)
                r = rrun[...]
                aft = _split_dot(l1m, u) + r
                a = jnp.where(valid, jnp.exp(lb + aft), 0.0)
                oacc[...] += _dot(a.astype(BF16), vb)
                r_ref[hh] = jnp.where(lane_blk == j, r, r_ref[hh])
                rnew = r + jnp.sum(l1m, axis=-1, keepdims=True)
                rrun[...] = rnew
                return s + 1, jnp.max(rnew)

            lax.while_loop(live, step, (jnp.int32(0), jnp.float32(0.0)))
        o_ref[...] = oacc[...].astype(BF16)

    return _pcall(
        body, name="sb_fwd", grid=(D // LANES, nq),
        in_specs=[pl.BlockSpec((T, LANES), lambda h, i: (i, h)), pl.BlockSpec((S, LANES), lambda h, i: (0, h)),
                  pl.BlockSpec((S, LANES), lambda h, i: (0, h))],
        out_specs=[pl.BlockSpec((T, LANES), lambda h, i: (i, h)), pl.BlockSpec((2, T, LANES), lambda h, i: (h, i, 0))],
        out_shape=[jax.ShapeDtypeStruct((S, D), BF16), jax.ShapeDtypeStruct((SBH, S, LANES), F32)],
        scratch_shapes=[pltpu.VMEM((T, LANES), F32), pltpu.VMEM((T, 1), F32)],
        compiler_params=_cp("parallel", "arbitrary"))(qs, kn, v)


def sb_bwd(qs, kn, v, do, rsave):
    S = qs.shape[0]
    T = min(SB_T, S)
    nq = S // T

    def body(q_ref, k_ref, v_ref, do_ref, r_ref, dq_ref, dk_ref, dv_ref, crun):
        i = pl.program_id(1)

        @pl.when(i == 0)
        def _():
            dk_ref[...] = jnp.zeros_like(dk_ref)
            dv_ref[...] = jnp.zeros_like(dv_ref)

        qb, dob = q_ref[...], do_ref[...]
        lo = _iota((1, LANES), 1) < SBD
        row, col = _iota((T, T), 0), _iota((T, T), 1)
        u = (row > col).astype(BF16)
        u2 = (row < col).astype(BF16)
        lane_blk = _iota((T, LANES), 1)
        dq_ref[...] = jnp.zeros_like(dq_ref)
        for hh in range(2):
            hm = lo if hh == 0 else jnp.logical_not(lo)
            qh = jnp.where(hm, qb, jnp.zeros_like(qb))
            doh = jnp.where(hm, dob, jnp.zeros_like(dob))
            crun[...] = jnp.zeros_like(crun)

            def step(j, carry, hm=hm, qh=qh, doh=doh, hh=hh):
                off = pl.multiple_of(j * T, T)
                kb = k_ref[pl.ds(off, T), :]
                vb = v_ref[pl.ds(off, T), :]
                vb = jnp.where(hm, vb, jnp.zeros_like(vb))
                valid = (j * T + col) < (i * T + row)
                z, e, lb, l1m = _sb_logits(qh, kb, valid)
                r = jnp.sum(jnp.where(lane_blk == j, r_ref[hh], 0.0), axis=-1, keepdims=True)
                aft = _split_dot(l1m, u) + r
                a = jnp.where(valid, jnp.exp(lb + aft), 0.0)
                w = a * _dot_nt(doh, vb)
                cprev = crun[...]
                cw = _split_dot(w, u2) + cprev
                inv = 1.0 / (1.0 + e)
                pos = z >= 0.0
                beta = jnp.where(pos, 1.0, e) * inv
                onem = jnp.where(pos, e, 1.0) * inv
                dz = jnp.where(valid, w * onem - beta * cw, 0.0).astype(BF16)
                dq_ref[...] += _dot(dz, jnp.where(hm, kb, jnp.zeros_like(kb)))
                dk_ref[pl.ds(off, T), :] += _dot_tn(dz, qh)
                dv_ref[pl.ds(off, T), :] += _dot_tn(a.astype(BF16), doh)
                crun[...] = cprev + jnp.sum(w, axis=-1, keepdims=True)
                return carry

            col_max = jnp.max(r_ref[hh], axis=0, keepdims=True)
            seen = jnp.logical_and(col_max > SB_DEAD, _iota((1, LANES), 1) <= i)
            n_live = jnp.sum(seen.astype(jnp.int32))
            lax.fori_loop(i + 1 - n_live, i + 1, step, 0)

    return _pcall(
        body, name="sb_bwd", grid=(D // LANES, nq),
        in_specs=[pl.BlockSpec((T, LANES), lambda h, i: (i, h)), pl.BlockSpec((S, LANES), lambda h, i: (0, h)),
                  pl.BlockSpec((S, LANES), lambda h, i: (0, h)), pl.BlockSpec((T, LANES), lambda h, i: (i, h)),
                  pl.BlockSpec((2, T, LANES), lambda h, i: (h, i, 0))],
        out_specs=[pl.BlockSpec((T, LANES), lambda h, i: (i, h)), pl.BlockSpec((S, LANES), lambda h, i: (0, h)),
                   pl.BlockSpec((S, LANES), lambda h, i: (0, h))],
        out_shape=[jax.ShapeDtypeStruct((S, D), F32)] * 3,
        scratch_shapes=[pltpu.VMEM((T, 1), F32)],
        compiler_params=_cp("parallel", "arbitrary"))(qs, kn, v, do, rsave)


def loss_head(y, target, tm=512):
    S = y.shape[0]
    tm = min(tm, S)

    def body(y_ref, t_ref, ls_ref, dy_ref):
        i = pl.program_id(0)
        err = y_ref[...] - t_ref[...]
        dy_ref[...] = err * (1.0 / D)
        part = jnp.sum(err * err, axis=0, keepdims=True)

        @pl.when(i == 0)
        def _():
            ls_ref[...] = part

        @pl.when(i > 0)
        def _():
            ls_ref[...] += part

    return _pcall(
        body, name="loss_head", grid=(S // tm,),
        in_specs=[pl.BlockSpec((tm, D), lambda i: (i, 0))] * 2,
        out_specs=[pl.BlockSpec((1, D), lambda i: (0, 0)), pl.BlockSpec((tm, D), lambda i: (i, 0))],
        out_shape=[jax.ShapeDtypeStruct((1, D), F32), jax.ShapeDtypeStruct((S, D), F32)],
        compiler_params=_cp("arbitrary"))(y, target)


def _row_tile(rows, cols):
    cap = max(8, (1 << 20) // (4 * cols))
    return max(t for t in range(8, min(rows, cap) + 1, 8) if rows % t == 0)


def _adamw_update(w, g, m, v):
    c1 = 1.0 / (1.0 - ADAM_B1 ** ADAM_STEP)
    c2 = 1.0 / (1.0 - ADAM_B2 ** ADAM_STEP)
    mn = ADAM_B1 * m + (1.0 - ADAM_B1) * g
    vn = ADAM_B2 * v + (1.0 - ADAM_B2) * (g * g)
    return -ADAM_LR * ((mn * c1) / (jnp.sqrt(vn * c2) + ADAM_EPS) + ADAM_WD * w), mn, vn


def adamw(w, g, m, v, name="adamw"):
    R, C = w.shape
    tr = _row_tile(R, C)

    def body(w_ref, g_ref, m_ref, v_ref, d_ref, mo_ref, vo_ref):
        d_ref[...], mo_ref[...], vo_ref[...] = _adamw_update(w_ref[...], g_ref[...], m_ref[...], v_ref[...])

    spec = pl.BlockSpec((tr, C), lambda i: (i, 0))
    return _pcall(
        body, name=name, grid=(R // tr,), in_specs=[spec] * 4, out_specs=[spec] * 3,
        out_shape=[jax.ShapeDtypeStruct((R, C), F32)] * 3,
        compiler_params=_cp("parallel"))(w, g, m, v)


def adamw_halves(w, g_mine, g_other, m, v, name="adamw_halves"):
    R, C = w.shape
    H = R // 2
    tr = _row_tile(H, C)
    n_i = H // tr
    where = lax.axis_index("c").astype(jnp.int32).reshape(1)

    def body(s_ref, w_ref, gm_ref, go_ref, m_ref, v_ref, g_ref, d_ref, mo_ref, vo_ref):
        g = jnp.where(pl.program_id(0) == s_ref[0], gm_ref[...], go_ref[...])
        g_ref[...] = g
        d_ref[...], mo_ref[...], vo_ref[...] = _adamw_update(w_ref[...], g, m_ref[...], v_ref[...])

    full = pl.BlockSpec((tr, C), lambda h, i, s: (h * n_i + i, 0))
    half = pl.BlockSpec((tr, C), lambda h, i, s: (i, 0))
    return _pcall(
        body, name=name,
        grid_spec=pltpu.PrefetchScalarGridSpec(
            num_scalar_prefetch=1, grid=(2, n_i), in_specs=[full, half, half, full, full], out_specs=[full] * 4),
        out_shape=[jax.ShapeDtypeStruct((R, C), F32)] * 4,
        compiler_params=_cp("parallel", "parallel"))(where, w, g_mine, g_other, m, v)


def pair_sum(gstacks, halves):
    c = lax.axis_index("c")
    me = 2 * lax.axis_index("x") + lax.axis_index("y")
    where = jnp.stack([c, me]).astype(jnp.int32)
    outs = []
    for g, xh in zip(gstacks, halves):
        _, H, C = xh.shape
        t = _row_tile(H, C)
        n_i = H // t

        def body(s_ref, g_ref, x_ref, qb_ref, own_ref):
            j = pl.program_id(1)
            q = g_ref[0] + x_ref[0]
            qb_ref[0] = q.astype(BF16)

            @pl.when(j == s_ref[1])
            def _():
                own_ref[...] = q

        outs.append(_pcall(
            body, name="pair_sum",
            grid_spec=pltpu.PrefetchScalarGridSpec(
                num_scalar_prefetch=1, grid=(n_i, N_CHIPS),
                in_specs=[pl.BlockSpec((1, t, C), lambda i, j, s, n_i=n_i: (j, s[0] * n_i + i, 0)),
                          pl.BlockSpec((1, t, C), lambda i, j, s: (j, i, 0))],
                out_specs=[pl.BlockSpec((1, t, C), lambda i, j, s: (j, i, 0)),
                           pl.BlockSpec((t, C), lambda i, j, s: (i, 0))]),
            out_shape=[jax.ShapeDtypeStruct((N_CHIPS, H, C), BF16), jax.ShapeDtypeStruct((H, C), F32)],
            compiler_params=_cp("parallel", "arbitrary"))(where, g, xh))
    return [o[0] for o in outs], [o[1] for o in outs]


def chip_sum(owns, recvs):
    outs = []
    for own, rc in zip(owns, recvs):
        H, C = own.shape
        t = _row_tile(H, C)

        def body(o_ref, r_ref, t_ref):
            t_ref[...] = ((o_ref[...] + r_ref[0].astype(F32)) + r_ref[1].astype(F32)) + r_ref[2].astype(F32)

        outs.append(_pcall(
            body, name="chip_sum", grid=(H // t,),
            in_specs=[pl.BlockSpec((t, C), lambda i: (i, 0)), pl.BlockSpec((3, t, C), lambda i: (0, i, 0))],
            out_specs=pl.BlockSpec((t, C), lambda i: (i, 0)),
            out_shape=jax.ShapeDtypeStruct((H, C), F32), compiler_params=_cp("parallel"))(own, rc))
    return outs


MESH = pl.DeviceIdType.MESH
ANY = pl.BlockSpec(memory_space=pl.ANY)
SPLIT_MIN_BYTES = 1 << 20


def _other_chips(x, y):
    return [(1 - x, y), (x, 1 - y), (1 - x, 1 - y)]


def _half_rows(rows, who):
    half = rows // 2
    return pl.ds(pl.multiple_of(who * half, 16), half)


def gather_all(shards):
    n = len(shards)
    rows = [s.shape[0] for s in shards]
    split = [r % 32 == 0 and s.size * s.dtype.itemsize >= SPLIT_MIN_BYTES for r, s in zip(rows, shards)]

    def body(*refs):
        ins, outs = refs[:n], refs[n:2 * n]
        ici_send, ici_recv, d2d_send, d2d_recv = refs[2 * n:]
        x, y, c = lax.axis_index("x"), lax.axis_index("y"), lax.axis_index("c")
        me, sib, chips = 2 * x + y, (x, y, 1 - c), _other_chips(x, y)

        def part(k, who):
            return _half_rows(rows[k], who) if split[k] else pl.ds(0, rows[k])

        def ici(k, r, block):
            px, py = chips[r]
            return pltpu.make_async_remote_copy(
                src_ref=ins[k].at[part(k, c)], dst_ref=outs[k].at[block, part(k, c)],
                send_sem=ici_send.at[3 * k + r], recv_sem=ici_recv.at[3 * k + r],
                device_id=(px, py, c), device_id_type=MESH)

        def d2d(k, r, who):
            px, py = chips[r]
            blk = outs[k].at[2 * px + py, part(k, who)]
            return pltpu.make_async_remote_copy(
                src_ref=blk, dst_ref=blk, send_sem=d2d_send.at[3 * k + r], recv_sem=d2d_recv.at[3 * k + r],
                device_id=sib, device_id_type=MESH)

        sends = [ici(k, r, me) for k in range(n) for r in range(3)]
        for cp in sends:
            cp.start()
        for r in range(3):
            px, py = chips[r]
            for k in range(n):
                ici(k, r, 2 * px + py).wait_recv()
                if split[k]:
                    fwd = d2d(k, r, c)
                    fwd.start()
                    sends.append(fwd)
        for r in range(3):
            for k in range(n):
                if split[k]:
                    d2d(k, r, 1 - c).wait_recv()
        for cp in sends:
            cp.wait_send()

    return _pcall(
        body, name="gather_all", in_specs=[ANY] * n, out_specs=[ANY] * n,
        out_shape=[jax.ShapeDtypeStruct((N_CHIPS,) + s.shape, s.dtype) for s in shards],
        scratch_shapes=[pltpu.SemaphoreType.DMA((3 * n,))] * 4)(*shards)


def swap_halves(gstacks):
    n = len(gstacks)

    def body(*refs):
        ins, outs, send_sems, recv_sems = refs[:n], refs[n:2 * n], refs[2 * n], refs[2 * n + 1]
        x, y, c = lax.axis_index("x"), lax.axis_index("y"), lax.axis_index("c")
        copies = [pltpu.make_async_remote_copy(
            src_ref=ins[k].at[:, _half_rows(ins[k].shape[1], 1 - c)], dst_ref=outs[k],
            send_sem=send_sems.at[k], recv_sem=recv_sems.at[k], device_id=(x, y, 1 - c), device_id_type=MESH)
            for k in range(n)]
        for cp in copies:
            cp.start()
        for cp in copies:
            cp.wait()

    return _pcall(
        body, name="swap_halves", in_specs=[ANY] * n, out_specs=[ANY] * n,
        out_shape=[jax.ShapeDtypeStruct((g.shape[0], g.shape[1] // 2, g.shape[2]), g.dtype) for g in gstacks],
        scratch_shapes=[pltpu.SemaphoreType.DMA((n,)), pltpu.SemaphoreType.DMA((n,))])(*gstacks)


def scatter_chips(stacks):
    n = len(stacks)

    def body(*refs):
        ins, outs, send_sems, recv_sems = refs[:n], refs[n:2 * n], refs[2 * n], refs[2 * n + 1]
        x, y, c = lax.axis_index("x"), lax.axis_index("y"), lax.axis_index("c")
        copies = [pltpu.make_async_remote_copy(
            src_ref=ins[k].at[2 * px + py], dst_ref=outs[k].at[r], send_sem=send_sems.at[3 * k + r],
            recv_sem=recv_sems.at[3 * k + r], device_id=(px, py, c), device_id_type=MESH)
            for k in range(n) for r, (px, py) in enumerate(_other_chips(x, y))]
        for cp in copies:
            cp.start()
        for cp in copies:
            cp.wait()

    return _pcall(
        body, name="scatter_chips", in_specs=[ANY] * n, out_specs=[ANY] * n,
        out_shape=[jax.ShapeDtypeStruct((3,) + s.shape[1:], s.dtype) for s in stacks],
        scratch_shapes=[pltpu.SemaphoreType.DMA((3 * n,)), pltpu.SemaphoreType.DMA((3 * n,))])(*stacks)


def swap_totals(totals):
    n = len(totals)

    def body(*refs):
        ins, outs, send_sems, recv_sems = refs[:n], refs[n:2 * n], refs[2 * n], refs[2 * n + 1]
        x, y, c = lax.axis_index("x"), lax.axis_index("y"), lax.axis_index("c")
        copies = [pltpu.make_async_remote_copy(
            src_ref=ins[k], dst_ref=outs[k], send_sem=send_sems.at[k], recv_sem=recv_sems.at[k],
            device_id=(x, y, 1 - c), device_id_type=MESH) for k in range(n)]
        for cp in copies:
            cp.start()
        for cp in copies:
            cp.wait()

    return _pcall(
        body, name="swap_totals", in_specs=[ANY] * n, out_specs=[ANY] * n,
        out_shape=[jax.ShapeDtypeStruct(t.shape, t.dtype) for t in totals],
        scratch_shapes=[pltpu.SemaphoreType.DMA((n,)), pltpu.SemaphoreType.DMA((n,))])(*totals)


def place_own(gathered, own):
    R, C = own.shape
    t = _row_tile(R, C)
    where = (2 * lax.axis_index("x") + lax.axis_index("y")).astype(jnp.int32).reshape(1)

    def body(s_ref, own_ref, g_ref, o_ref):
        o_ref[0] = own_ref[...]

    return _pcall(
        body, name="place_own",
        grid_spec=pltpu.PrefetchScalarGridSpec(
            num_scalar_prefetch=1, grid=(R // t,), in_specs=[pl.BlockSpec((t, C), lambda i, s: (i, 0)), ANY],
            out_specs=pl.BlockSpec((1, t, C), lambda i, s: (s[0], i, 0))),
        out_shape=jax.ShapeDtypeStruct(gathered.shape, gathered.dtype), input_output_aliases={2: 0},
        compiler_params=_cp("parallel"))(where, own, gathered)


def reduce_scatter(gstacks):
    halves = swap_halves(gstacks)
    payload, own = pair_sum(gstacks, halves)
    recv = scatter_chips(payload)
    mine = chip_sum(own, recv)
    return mine, swap_totals(mine)


def allreduce_small(vec):
    R = vec.shape[0]

    def body(in_ref, out_ref, buf, send_sems, recv_sems):
        x, y, c = lax.axis_index("x"), lax.axis_index("y"), lax.axis_index("c")
        me = 4 * x + 2 * y + c
        buf[me] = in_ref[...]
        copies = []
        for k in range(1, 8):
            peer = (x ^ (k >> 2), y ^ ((k >> 1) & 1), c ^ (k & 1))
            copies.append(pltpu.make_async_remote_copy(
                src_ref=in_ref, dst_ref=buf.at[me], send_sem=send_sems.at[k - 1], recv_sem=recv_sems.at[k - 1],
                device_id=peer, device_id_type=MESH))
        for cp in copies:
            cp.start()
        for cp in copies:
            cp.wait()
        acc = buf[0]
        for d in range(1, 8):
            acc = acc + buf[d]
        out_ref[...] = acc

    vm = pl.BlockSpec(memory_space=pltpu.VMEM)
    return _pcall(
        body, name="allreduce_small", in_specs=[vm], out_specs=vm,
        out_shape=jax.ShapeDtypeStruct((R, LANES), F32),
        scratch_shapes=[pltpu.VMEM((8, R, LANES), F32), pltpu.SemaphoreType.DMA((7,)), pltpu.SemaphoreType.DMA((7,))])(vec)


MATMUL_SHARDED = [("pool_in", 1), ("pool_group", 2), ("ssd_in", 2), ("ssd_out", 1), ("sb_qkv", 2), ("sb_out", 1),
                  ("ffn_gate", 2), ("ffn_up", 2), ("ffn_down", 1)]
STACKED = ["ffn_gate", "ffn_up", "ffn_down"]
SMALL_SHARDED = [("pool_scale", 1), ("ssd_conv_w", 2)]
REPLICATED = ["mix_norm", "ssd_conv_b", "ssd_dt_bias", "ssd_a_log", "ssd_d", "ssd_out_norm", "sb_q_norm",
              "sb_k_norm", "ffn_norm"]
WEIGHT_ORDER = ["mix_norm", "pool_in", "pool_group", "pool_scale", "ssd_in", "ssd_conv_w", "ssd_conv_b",
                "ssd_dt_bias", "ssd_a_log", "ssd_d", "ssd_out_norm", "ssd_out", "sb_qkv", "sb_q_norm", "sb_k_norm",
                "sb_out", "ffn_norm", "ffn_gate", "ffn_up", "ffn_down"]


def _piece_rows(n, mult):
    rows = -(-n // LANES)
    return -(-rows // mult) * mult


def _as_rows(a, mult):
    flat = a.reshape(-1)
    rows = _piece_rows(flat.shape[0], mult)
    if rows * LANES != flat.shape[0]:
        flat = jnp.pad(flat, (0, rows * LANES - flat.shape[0]))
    return flat.reshape(rows, LANES)


def _pack(arrs, mult=8, row_pad=8):
    parts = [_as_rows(a, mult) for a in arrs]
    rows = sum(p.shape[0] for p in parts)
    pad = -rows % row_pad
    if pad:
        parts.append(jnp.zeros((pad, LANES), parts[0].dtype))
    return jnp.concatenate(parts, axis=0)


def _unpack(packed, shapes, mult=8, lead=()):
    out, off = [], 0
    for s in shapes:
        n = math.prod(s)
        rows = _piece_rows(n, mult)
        piece = packed[..., off:off + rows, :].reshape(lead + (rows * LANES,))
        out.append(piece[..., :n].reshape(lead + tuple(s)))
        off += rows
    return out


def _rows2d(a):
    return a.reshape(-1, a.shape[-1])


def _gather_weights(shards):
    own = [_rows2d(shards[n].astype(BF16)) for n, _ in MATMUL_SHARDED]
    small = _pack([shards[n] for n, _ in SMALL_SHARDED])
    gathered = gather_all(own + [small])
    me = 2 * lax.axis_index("x") + lax.axis_index("y")

    def whole(got, mine, ax):
        return jnp.concatenate([jnp.where(me == j, mine, got[j]) for j in range(N_CHIPS)], axis=ax)

    full = {}
    for (n, ax), got, mine in zip(MATMUL_SHARDED, gathered, own):
        if n in STACKED:
            full[n] = place_own(got, mine)
        else:
            shp = shards[n].shape
            full[n] = whole(got.reshape((N_CHIPS,) + shp), mine.reshape(shp), ax)
    pieces = _unpack(gathered[-1], [shards[n].shape for n, _ in SMALL_SHARDED], lead=(N_CHIPS,))
    for (n, ax), got in zip(SMALL_SHARDED, pieces):
        full[n] = whole(got, shards[n], ax)
    return full


def _split_shards(full, axis):
    return jnp.stack(jnp.split(full, N_CHIPS, axis=axis))


def _ffn_fwd(x, gain, wg4, wu4, wd4, layer):
    h = rmsnorm_fwd(x, gain, name="ffn_norm_fwd")
    a4, b4, hid4 = ffn_up(h, wg4, wu4, layer)
    xo = ffn_down(hid4, wd4, layer, x)
    return xo, (x, h, a4, b4, hid4)


def _ffn_bwd(dout, saved, gain, wg4, wu4, wd4, layer):
    x, h, a4, b4, hid4 = saved
    da4, db4 = ffn_bwd_hidden(dout, wd4, layer, a4, b4)
    dwd4 = ffn_wgrad_out(hid4, dout)
    dwg4, dwu4 = ffn_wgrad_in(h, da4, db4)
    dh = ffn_dh(da4, db4, wg4, wu4, layer)
    dx, dgain = rmsnorm_bwd(x, gain, dh, dout, name="ffn_norm_bwd")
    return dx, dgain, dwg4, dwu4, dwd4


def _pool_layer_fwd(x, gain, w_in, wgrp, scale):
    h = rmsnorm_fwd(x, gain, name="pool_norm_fwd")
    u = linear([(h, w_in, "nn")], name="pool_in")
    xo, p = pool_fwd(u, wgrp, scale, x)
    return xo, (x, h, p)


def _pool_layer_bwd(dout, saved, gain, w_in, wgrp, scale):
    x, h, p = saved
    dp, dwgrp, dscale = pool_bwd_group(dout, p, wgrp, scale)
    du = pool_bwd_window(dp)
    (dw_in,) = wgrad(h, [du], name="pool_dwin")
    dh = linear([(du, w_in, "nt")], name="pool_dh")
    dx, dgain = rmsnorm_bwd(x, gain, dh, dout, name="pool_norm_bwd")
    return dx, dgain, dw_in, dwgrp, dscale


def _ssd_layer_fwd(x, gain, w_z, w_xbc, w_dt, conv_w, conv_b, dt_bias, a_log, d_full, out_norm, w_out):
    h = rmsnorm_fwd(x, gain, name="ssd_norm_fwd")
    z = linear([(h, w_z, "nn")], name="ssd_in_z")
    xbc = linear([(h, w_xbc, "nn")], tn=2048, name="ssd_in_xbc")
    dt_raw = linear([(h, w_dt, "nn")], name="ssd_in_dt")
    act = conv_fwd(xbc, conv_w, conv_b)
    y, states = ssd_scan_fwd(act, dt_raw, dt_bias, a_log, d_full)
    gn = gate_norm_fwd(y, z, out_norm)
    xo = linear([(gn, w_out, "nn")], res=x, name="ssd_out")
    return xo, (x, h, z, xbc, dt_raw, act, y, states, gn)


def _ssd_layer_bwd(dout, saved, gain, w_z, w_xbc, w_dt, conv_w, conv_b, dt_bias, a_log, d_full, out_norm, w_out):
    x, h, z, xbc, dt_raw, act, y, states, gn = saved
    dgn = linear([(dout, w_out, "nt")], name="ssd_dgn")
    (dw_out,) = wgrad(gn, [dout], name="ssd_dwout")
    dy, dz, dout_norm = gate_norm_bwd(dgn, y, z, out_norm)
    dact, ddt_raw, dbias, dalog, dd_full = ssd_scan_bwd(dy, act, dt_raw, dt_bias, a_log, d_full, states)
    dpre, dconv_w8, dconv_b = conv_bwd_pre(dact, xbc, conv_w, conv_b)
    dxbc = conv_bwd_input(dpre, conv_w)
    ddt_b = ddt_raw.astype(BF16)
    (dw_z,) = wgrad(h, [dz], name="ssd_dwz")
    (dw_xbc,) = wgrad(h, [dxbc], tn=2048, name="ssd_dwxbc")
    (dw_dt,) = wgrad(h, [ddt_b], name="ssd_dwdt")
    dh = linear([(dz, w_z, "nt"), (dxbc, w_xbc, "nt"), (ddt_b, w_dt, "nt")], tm=256, name="ssd_dh")
    dx, dgain = rmsnorm_bwd(x, gain, dh, dout, name="ssd_norm_bwd")
    dw_in = jnp.concatenate([dw_z, dw_xbc, dw_dt], axis=1)
    dd = dd_full.reshape(NH, HP).sum(axis=1).reshape(1, NH)
    return dx, dgain, dw_in, dconv_w8[:4], dconv_b, dbias, dalog, dd, dout_norm, dw_out


def _sb_layer_fwd(x, gain, w_qkv, qg, kg, w_out):
    h = rmsnorm_fwd(x, gain, name="sb_norm_fwd")
    qkv = linear([(h, w_qkv, "nn")], tn=1024, name="sb_qkv")
    qs, kn, v = sb_prep_fwd(qkv, qg, kg)
    o, rsave = sb_fwd(qs, kn, v)
    xo = linear([(o, w_out, "nn")], res=x, name="sb_out")
    return xo, (x, h, qkv, qs, kn, v, o, rsave)


def _sb_layer_bwd(dout, saved, gain, w_qkv, qg, kg, w_out):
    x, h, qkv, qs, kn, v, o, rsave = saved
    do = linear([(dout, w_out, "nt")], out_dtype=BF16, name="sb_do")
    (dw_out,) = wgrad(o, [dout], name="sb_dwout")
    dqs, dkn, dv = sb_bwd(qs, kn, v, do, rsave)
    dqkv, dqg, dkg = sb_prep_bwd(dqs, dkn, dv, qkv, qg, kg)
    (dw_qkv,) = wgrad(h, [dqkv], tn=1024, name="sb_dwqkv")
    dh = linear([(dqkv, w_qkv, "nt")], name="sb_dh")
    dx, dgain = rmsnorm_bwd(x, gain, dh, dout, name="sb_norm_bwd")
    dqg = dqg.reshape(SBH, SBD).sum(axis=0).reshape(1, SBD)
    dkg = dkg.reshape(SBH, SBD).sum(axis=0).reshape(1, SBD)
    return dx, dgain, dw_qkv, dqg, dkg, dw_out


def _local_step(x, target, full, rep):
    S = x.shape[0]
    d_full = jnp.repeat(rep["ssd_d"][0], HP).reshape(1, DI)
    qg = jnp.tile(rep["sb_q_norm"][0], 2).reshape(1, LANES)
    kg = jnp.tile(rep["sb_k_norm"][0], 2).reshape(1, LANES)
    ssd_in = full["ssd_in"][0]
    w_z, w_xbc, w_dt = ssd_in[:, :DI], ssd_in[:, DI:DI + CONV_CH], ssd_in[:, DI + CONV_CH:]
    conv_w = full["ssd_conv_w"][0]
    conv_b = rep["ssd_conv_b"]
    pool_scale = full["pool_scale"]

    def mixer_args(i):
        kind, j = i % 3, i // 3
        if kind == 0:
            return (full["pool_in"][j], full["pool_group"][j], pool_scale[j:j + 1])
        if kind == 1:
            return (w_z, w_xbc, w_dt, conv_w, conv_b, rep["ssd_dt_bias"], rep["ssd_a_log"], d_full,
                    rep["ssd_out_norm"], full["ssd_out"][0])
        return (full["sb_qkv"][0], qg, kg, full["sb_out"][0])

    fwd = (_pool_layer_fwd, _ssd_layer_fwd, _sb_layer_fwd)
    bwd = (_pool_layer_bwd, _ssd_layer_bwd, _sb_layer_bwd)
    saved = []
    for i in range(DEPTH):
        x, sm = fwd[i % 3](x, rep["mix_norm"][i], *mixer_args(i))
        x, sf = _ffn_fwd(x, rep["ffn_norm"][i], full["ffn_gate"], full["ffn_up"], full["ffn_down"], i)
        saved.append((sm, sf))

    colsq, dx = loss_head(x, target)
    loss = 0.5 * jnp.sum(colsq) / D

    g = {n: [None] * DEPTH for n in ("mix_norm", "ffn_norm", "ffn_gate", "ffn_up", "ffn_down")}
    g["pool_in"], g["pool_group"], g["pool_scale"] = [None] * 2, [None] * 2, [None] * 2
    for i in reversed(range(DEPTH)):
        sm, sf = saved[i]
        dx, g["ffn_norm"][i], g["ffn_gate"][i], g["ffn_up"][i], g["ffn_down"][i] = _ffn_bwd(
            dx, sf, rep["ffn_norm"][i], full["ffn_gate"], full["ffn_up"], full["ffn_down"], i)
        kind, j = i % 3, i // 3
        res = bwd[kind](dx, sm, rep["mix_norm"][i], *mixer_args(i))
        dx, g["mix_norm"][i] = res[0], res[1]
        if kind == 0:
            g["pool_in"][j], g["pool_group"][j], g["pool_scale"][j] = res[2:]
        elif kind == 1:
            dw_in, dconv_w, dconv_b, dbias, dalog, dd, don, dw_out = res[2:]
            g.update(ssd_in=dw_in[None], ssd_conv_w=dconv_w[None], ssd_conv_b=dconv_b, ssd_dt_bias=dbias,
                     ssd_a_log=dalog, ssd_d=dd, ssd_out_norm=don, ssd_out=dw_out[None])
        else:
            dw_qkv, dqg, dkg, dw_out = res[2:]
            g.update(sb_qkv=dw_qkv[None], sb_q_norm=dqg, sb_k_norm=dkg, sb_out=dw_out[None])
    for n in ("mix_norm", "ffn_norm", "pool_scale"):
        g[n] = jnp.concatenate(g[n], axis=0)
    for n in ("pool_in", "pool_group"):
        g[n] = jnp.stack(g[n])
    for n in ("ffn_gate", "ffn_up", "ffn_down"):
        g[n] = jnp.concatenate(g[n], axis=1)
    return loss, dx, g


def kernel(x, mix_norm, pool_in, pool_group, pool_scale, ssd_in, ssd_conv_w, ssd_conv_b, ssd_dt_bias, ssd_a_log, ssd_d, ssd_out_norm, ssd_out, sb_qkv, sb_q_norm, sb_k_norm, sb_out, ffn_norm, ffn_gate, ffn_up, ffn_down, loss_target, m_mix_norm, m_pool_in, m_pool_group, m_pool_scale, m_ssd_in, m_ssd_conv_w, m_ssd_conv_b, m_ssd_dt_bias, m_ssd_a_log, m_ssd_d, m_ssd_out_norm, m_ssd_out, m_sb_qkv, m_sb_q_norm, m_sb_k_norm, m_sb_out, m_ffn_norm, m_ffn_gate, m_ffn_up, m_ffn_down, v_mix_norm, v_pool_in, v_pool_group, v_pool_scale, v_ssd_in, v_ssd_conv_w, v_ssd_conv_b, v_ssd_dt_bias, v_ssd_a_log, v_ssd_d, v_ssd_out_norm, v_ssd_out, v_sb_qkv, v_sb_q_norm, v_sb_k_norm, v_sb_out, v_ffn_norm, v_ffn_gate, v_ffn_up, v_ffn_down):
    given = dict(locals())
    w = {n: given[n] for n in WEIGHT_ORDER}
    m = {n: given["m_" + n] for n in WEIGHT_ORDER}
    v = {n: given["v_" + n] for n in WEIGHT_ORDER}
    full = _gather_weights(w)
    rep = {n: w[n] for n in REPLICATED}

    loss, dx, g = _local_step(x[0], loss_target[0], full, rep)
    loss = lax.psum(loss, ("x", "y", "c"))
    out = {}

    gstacks = [g[n] if n in STACKED else _split_shards(g[n], ax).reshape((N_CHIPS,) + _rows2d(w[n]).shape)
               for n, ax in MATMUL_SHARDED]
    mine, other = reduce_scatter(gstacks)
    for (n, _), g_mine, g_other in zip(MATMUL_SHARDED, mine, other):
        res = adamw_halves(_rows2d(w[n]), g_mine, g_other, _rows2d(m[n]), _rows2d(v[n]), name="adamw_" + n)
        for key, a in zip("gdmv", res):
            out[key, n] = a.reshape(w[n].shape)

    small = REPLICATED + [n for n, _ in SMALL_SHARDED]
    gfull = _unpack(allreduce_small(_pack([g[n] for n in small])), [g[n].shape for n in small])
    me = 2 * lax.axis_index("x") + lax.axis_index("y")
    gsum = dict(zip(small, gfull))
    for n, ax in SMALL_SHARDED:
        gsum[n] = lax.dynamic_slice_in_dim(gsum[n], me * w[n].shape[ax], w[n].shape[ax], axis=ax)
    res = adamw(*[_pack([t[n] for n in small]) for t in (w, gsum, m, v)], name="adamw_small")
    for key, flat in zip("dmv", res):
        for n, a in zip(small, _unpack(flat, [w[n].shape for n in small])):
            out[key, n] = a
    for n in small:
        out["g", n] = gsum[n]

    return (loss, dx[None], *[out["g", n] for n in WEIGHT_ORDER], *[out["d", n] for n in WEIGHT_ORDER],
            *[out["m", n] for n in WEIGHT_ORDER], *[out["v", n] for n in WEIGHT_ORDER])
```

```python
import math

import jax
import jax.numpy as jnp
from jax import lax
from jax.experimental import pallas as pl
from jax.experimental.pallas import tpu as pltpu

F32 = jnp.float32
BF16 = jnp.bfloat16

D = 1024
DEPTH = 4
EPS = 1e-6
POOL_WINDOWS = (2, 4, 8, 16)
PG = 256
DI = 2048
NH = 32
HP = 64
NG = 8
NS = 128
GW = 256
CH = 256
CONV_CH = 4096
SSD_IN = 6176
SBH = 16
SBD = 64
FH = 2816
N_CHIPS = 4
LANES = 128

ADAM_LR = 0.001
ADAM_B1 = 0.9
ADAM_B2 = 0.999
ADAM_EPS = 1e-08
ADAM_WD = 0.01
ADAM_STEP = 10

VMEM_LIMIT = 56 * 1024 * 1024


def _pcall(body, **kw):
    return pl.pallas_call(body, **kw)


def _cp(*sem):
    return pltpu.CompilerParams(dimension_semantics=sem, vmem_limit_bytes=VMEM_LIMIT)


def _dot(a, b, prec=None):
    return lax.dot_general(a, b, (((1,), (0,)), ((), ())), precision=prec, preferred_element_type=F32)


def _dot_nt(a, b, prec=None):
    return lax.dot_general(a, b, (((1,), (1,)), ((), ())), precision=prec, preferred_element_type=F32)


def _dot_tn(a, b, prec=None):
    return lax.dot_general(a, b, (((0,), (0,)), ((), ())), precision=prec, preferred_element_type=F32)


def _split3(x):
    x1 = x.astype(BF16)
    r = x - x1.astype(F32)
    x2 = r.astype(BF16)
    return x1, x2, (r - x2.astype(F32)).astype(BF16)


def _sel(dot, x, mask, x_first=True):
    mb = mask.astype(BF16)
    p = [dot(xi, mb) if x_first else dot(mb, xi) for xi in _split3(x)]
    return (p[0] + p[1]) + p[2]


def _sigmoid(x):
    return 1.0 / (1.0 + jnp.exp(-x))


def _iota(shape, axis):
    return lax.broadcasted_iota(jnp.int32, shape, axis)


def linear(pairs, res=None, out_dtype=F32, tm=512, tn=None, name="linear"):
    M = pairs[0][0].shape[0]
    N = pairs[0][1].shape[1] if pairs[0][2] == "nn" else pairs[0][1].shape[0]
    tm = min(tm, M)
    tn = N if tn is None else min(tn, N)
    n_pairs = len(pairs)
    modes = [p[2] for p in pairs]

    def body(*refs):
        acc = None
        for k in range(n_pairs):
            a = refs[2 * k][...].astype(BF16)
            w = refs[2 * k + 1][...]
            t = _dot(a, w) if modes[k] == "nn" else _dot_nt(a, w)
            acc = t if acc is None else acc + t
        if res is not None:
            acc = acc + refs[2 * n_pairs][...]
        refs[-1][...] = acc.astype(out_dtype)

    in_specs, args = [], []
    for a, w, mode in pairs:
        K = a.shape[1]
        in_specs.append(pl.BlockSpec((tm, K), lambda j, i: (i, 0)))
        if mode == "nn":
            in_specs.append(pl.BlockSpec((K, tn), lambda j, i: (0, j)))
        else:
            in_specs.append(pl.BlockSpec((tn, K), lambda j, i: (j, 0)))
        args += [a, w]
    if res is not None:
        in_specs.append(pl.BlockSpec((tm, tn), lambda j, i: (i, j)))
        args.append(res)
    return _pcall(
        body, name=name, grid=(N // tn, M // tm), in_specs=in_specs,
        out_specs=pl.BlockSpec((tm, tn), lambda j, i: (i, j)),
        out_shape=jax.ShapeDtypeStruct((M, N), out_dtype),
        compiler_params=_cp("parallel", "arbitrary"))(*args)


def wgrad(a, gs, tk=1024, tn=None, tm=1024, name="wgrad"):
    M, Ka = a.shape
    N = gs[0].shape[1]
    tk, tm = min(tk, Ka), min(tm, M)
    tn = N if tn is None else min(tn, N)
    n_g = len(gs)

    def body(*refs):
        a_ref, g_refs, o_refs = refs[0], refs[1:1 + n_g], refs[1 + n_g:]
        m = pl.program_id(2)
        at = a_ref[...].astype(BF16)
        for g_ref, o_ref in zip(g_refs, o_refs):
            t = _dot_tn(at, g_ref[...].astype(BF16))

            @pl.when(m == 0)
            def _():
                o_ref[...] = t

            @pl.when(m > 0)
            def _():
                o_ref[...] += t

    out = _pcall(
        body, name=name, grid=(Ka // tk, N // tn, M // tm),
        in_specs=[pl.BlockSpec((tm, tk), lambda k, j, m: (m, k))]
        + [pl.BlockSpec((tm, tn), lambda k, j, m: (m, j))] * n_g,
        out_specs=[pl.BlockSpec((tk, tn), lambda k, j, m: (k, j))] * n_g,
        out_shape=[jax.ShapeDtypeStruct((Ka, N), F32)] * n_g,
        compiler_params=_cp("parallel", "parallel", "arbitrary"))(a, *gs)
    return out


def rmsnorm_fwd(x, gain, tm=512, name="rmsnorm_fwd"):
    S, Dm = x.shape
    tm = min(tm, S)

    def body(x_ref, g_ref, o_ref):
        xv = x_ref[...]
        r = lax.rsqrt(jnp.mean(xv * xv, axis=-1, keepdims=True) + EPS)
        o_ref[...] = (xv * r * g_ref[...]).astype(BF16)

    return _pcall(
        body, name=name, grid=(S // tm,),
        in_specs=[pl.BlockSpec((tm, Dm), lambda i: (i, 0)), pl.BlockSpec((1, Dm), lambda i: (0, 0))],
        out_specs=pl.BlockSpec((tm, Dm), lambda i: (i, 0)),
        out_shape=jax.ShapeDtypeStruct((S, Dm), BF16),
        compiler_params=_cp("parallel"))(x, gain.reshape(1, Dm))


def rmsnorm_bwd(x, gain, dh, dres, tm=512, name="rmsnorm_bwd"):
    S, Dm = x.shape
    tm = min(tm, S)

    def body(x_ref, g_ref, dh_ref, dr_ref, dx_ref, dg_ref):
        i = pl.program_id(0)
        xv = x_ref[...]
        r = lax.rsqrt(jnp.mean(xv * xv, axis=-1, keepdims=True) + EPS)
        y = xv * r
        dhv = dh_ref[...]
        dy = dhv * g_ref[...]
        dx_ref[...] = dr_ref[...] + r * (dy - y * jnp.mean(dy * y, axis=-1, keepdims=True))
        part = jnp.sum(dhv * y, axis=0, keepdims=True)

        @pl.when(i == 0)
        def _():
            dg_ref[...] = part

        @pl.when(i > 0)
        def _():
            dg_ref[...] += part

    return _pcall(
        body, name=name, grid=(S // tm,),
        in_specs=[pl.BlockSpec((tm, Dm), lambda i: (i, 0)), pl.BlockSpec((1, Dm), lambda i: (0, 0)),
                  pl.BlockSpec((tm, Dm), lambda i: (i, 0)), pl.BlockSpec((tm, Dm), lambda i: (i, 0))],
        out_specs=[pl.BlockSpec((tm, Dm), lambda i: (i, 0)), pl.BlockSpec((1, Dm), lambda i: (0, 0))],
        out_shape=[jax.ShapeDtypeStruct((S, Dm), F32), jax.ShapeDtypeStruct((1, Dm), F32)],
        compiler_params=_cp("arbitrary"))(x, gain.reshape(1, Dm), dh, dres)


FS = FH // N_CHIPS


def ffn_up(h, wg4, wu4, layer, tm=1024):
    S = h.shape[0]
    tm = min(tm, S)

    def body(h_ref, wg_ref, wu_ref, a_ref, b_ref, hid_ref):
        hv = h_ref[...]
        a = _dot(hv, wg_ref[0])
        b = _dot(hv, wu_ref[0])
        a_ref[0] = a
        b_ref[0] = b
        hid_ref[0] = (a * _sigmoid(a) * b).astype(BF16)

    wspec = pl.BlockSpec((1, D, FS), lambda j, i: (j, layer, 0))
    aspec = pl.BlockSpec((1, tm, FS), lambda j, i: (j, i, 0))
    return _pcall(
        body, name="ffn_up", grid=(N_CHIPS, S // tm),
        in_specs=[pl.BlockSpec((tm, D), lambda j, i: (i, 0)), wspec, wspec], out_specs=[aspec] * 3,
        out_shape=[jax.ShapeDtypeStruct((N_CHIPS, S, FS), F32), jax.ShapeDtypeStruct((N_CHIPS, S, FS), F32),
                   jax.ShapeDtypeStruct((N_CHIPS, S, FS), BF16)],
        compiler_params=_cp("parallel", "arbitrary"))(h, wg4, wu4)


def ffn_down(hid4, wd4, layer, x, tm=1024):
    S = x.shape[0]
    tm = min(tm, S)

    def body(hid_ref, wd_ref, x_ref, o_ref):
        acc = x_ref[...]
        for j in range(N_CHIPS):
            acc = acc + _dot(hid_ref[j], wd_ref[j])
        o_ref[...] = acc

    return _pcall(
        body, name="ffn_down", grid=(S // tm,),
        in_specs=[pl.BlockSpec((N_CHIPS, tm, FS), lambda i: (0, i, 0)),
                  pl.BlockSpec((N_CHIPS, FS, D), lambda i: (0, layer, 0)), pl.BlockSpec((tm, D), lambda i: (i, 0))],
        out_specs=pl.BlockSpec((tm, D), lambda i: (i, 0)),
        out_shape=jax.ShapeDtypeStruct((S, D), F32), compiler_params=_cp("parallel"))(hid4, wd4, x)


def ffn_bwd_hidden(dout, wd4, layer, a4, b4, tm=1024):
    S = dout.shape[0]
    tm = min(tm, S)

    def body(do_ref, wd_ref, a_ref, b_ref, da_ref, db_ref):
        dhid = _dot_nt(do_ref[...].astype(BF16), wd_ref[0])
        av, bv = a_ref[0], b_ref[0]
        s = _sigmoid(av)
        da_ref[0] = (dhid * bv * (s * (1.0 + av * (1.0 - s)))).astype(BF16)
        db_ref[0] = (dhid * (av * s)).astype(BF16)

    aspec = pl.BlockSpec((1, tm, FS), lambda i, j: (j, i, 0))
    return _pcall(
        body, name="ffn_bwd_hidden", grid=(S // tm, N_CHIPS),
        in_specs=[pl.BlockSpec((tm, D), lambda i, j: (i, 0)), pl.BlockSpec((1, FS, D), lambda i, j: (j, layer, 0)),
                  aspec, aspec],
        out_specs=[aspec] * 2, out_shape=[jax.ShapeDtypeStruct((N_CHIPS, S, FS), BF16)] * 2,
        compiler_params=_cp("parallel", "arbitrary"))(dout, wd4, a4, b4)


def ffn_wgrad_in(h, da4, db4, tm=2048):
    S = h.shape[0]
    tm = min(tm, S)

    def body(h_ref, da_ref, db_ref, dg_ref, du_ref):
        m = pl.program_id(1)
        hv = h_ref[...]
        for g_ref, o_ref in ((da_ref, dg_ref), (db_ref, du_ref)):
            t = _dot_tn(hv, g_ref[0])

            @pl.when(m == 0)
            def _():
                o_ref[0] = t

            @pl.when(m > 0)
            def _():
                o_ref[0] += t

    aspec = pl.BlockSpec((1, tm, FS), lambda j, m: (j, m, 0))
    ospec = pl.BlockSpec((1, D, FS), lambda j, m: (j, 0, 0))
    return _pcall(
        body, name="ffn_wgrad_in", grid=(N_CHIPS, S // tm),
        in_specs=[pl.BlockSpec((tm, D), lambda j, m: (m, 0)), aspec, aspec], out_specs=[ospec] * 2,
        out_shape=[jax.ShapeDtypeStruct((N_CHIPS, D, FS), F32)] * 2,
        compiler_params=_cp("parallel", "arbitrary"))(h, da4, db4)


def ffn_wgrad_out(hid4, dout, tm=2048):
    S = dout.shape[0]
    tm = min(tm, S)

    def body(hid_ref, do_ref, o_ref):
        m = pl.program_id(1)
        t = _dot_tn(hid_ref[0], do_ref[...].astype(BF16))

        @pl.when(m == 0)
        def _():
            o_ref[0] = t

        @pl.when(m > 0)
        def _():
            o_ref[0] += t

    return _pcall(
        body, name="ffn_wgrad_out", grid=(N_CHIPS, S // tm),
        in_specs=[pl.BlockSpec((1, tm, FS), lambda j, m: (j, m, 0)), pl.BlockSpec((tm, D), lambda j, m: (m, 0))],
        out_specs=pl.BlockSpec((1, FS, D), lambda j, m: (j, 0, 0)),
        out_shape=jax.ShapeDtypeStruct((N_CHIPS, FS, D), F32),
        compiler_params=_cp("parallel", "arbitrary"))(hid4, dout)


def ffn_dh(da4, db4, wg4, wu4, layer, tm=512):
    S = da4.shape[1]
    tm = min(tm, S)

    def body(da_ref, db_ref, wg_ref, wu_ref, o_ref):
        acc = _dot_nt(da_ref[0], wg_ref[0]) + _dot_nt(db_ref[0], wu_ref[0])
        for j in range(1, N_CHIPS):
            acc = acc + _dot_nt(da_ref[j], wg_ref[j]) + _dot_nt(db_ref[j], wu_ref[j])
        o_ref[...] = acc

    aspec = pl.BlockSpec((N_CHIPS, tm, FS), lambda i: (0, i, 0))
    wspec = pl.BlockSpec((N_CHIPS, D, FS), lambda i: (0, layer, 0))
    return _pcall(
        body, name="ffn_dh", grid=(S // tm,), in_specs=[aspec, aspec, wspec, wspec],
        out_specs=pl.BlockSpec((tm, D), lambda i: (i, 0)),
        out_shape=jax.ShapeDtypeStruct((S, D), F32), compiler_params=_cp("parallel"))(da4, db4, wg4, wu4)


POOL_T = 128
POOL_HALO = 16


def pool_fwd(u, wgrp, scale, x_res):
    S = u.shape[0]
    T, HB = min(POOL_T, S), POOL_HALO
    per = T // HB

    def body(u_ref, tail_ref, wg_ref, sc_ref, x_ref, xo_ref, p_ref):
        i = pl.program_id(0)
        uc = u_ref[...]
        tail = jnp.where(i > 0, tail_ref[...], 0.0)
        d_cur = _iota((T, T), 0) - _iota((T, T), 1)
        d_tail = _iota((T, HB), 0) - _iota((T, HB), 1) + HB
        tg = i * T + _iota((T, 1), 0)
        for g, w in enumerate(POOL_WINDOWS):
            gs = slice(g * PG, (g + 1) * PG)
            band = (d_cur >= 0) & (d_cur < w)
            band_t = (d_tail >= 0) & (d_tail < w)
            ug = uc[:, gs]
            ws = _sel(_dot, ug, band, False) + _sel(_dot, tail[:, gs], band_t, False)
            cnt = jnp.minimum(tg + 1, w).astype(F32)
            pb = (ws / cnt - ug).astype(BF16)
            p_ref[:, gs] = pb
            xo_ref[:, gs] = x_ref[:, gs] + _dot(pb, wg_ref[g]) * sc_ref[:, gs]

    return _pcall(
        body, name="pool_fwd", grid=(S // T,),
        in_specs=[pl.BlockSpec((T, D), lambda i: (i, 0)),
                  pl.BlockSpec((HB, D), lambda i: (jnp.maximum(i * per - 1, 0), 0)),
                  pl.BlockSpec((4, PG, PG), lambda i: (0, 0, 0)), pl.BlockSpec((1, D), lambda i: (0, 0)),
                  pl.BlockSpec((T, D), lambda i: (i, 0))],
        out_specs=[pl.BlockSpec((T, D), lambda i: (i, 0))] * 2,
        out_shape=[jax.ShapeDtypeStruct((S, D), F32), jax.ShapeDtypeStruct((S, D), BF16)],
        compiler_params=_cp("parallel"))(u, u, wgrp, scale, x_res)


def pool_bwd_group(dm, p, wgrp, scale, tm=512):
    S = dm.shape[0]
    tm = min(tm, S)

    def body(dm_ref, p_ref, wg_ref, sc_ref, dp_ref, dwg_ref, dsc_ref):
        i = pl.program_id(0)

        @pl.when(i == 0)
        def _():
            dwg_ref[...] = jnp.zeros_like(dwg_ref)
            dsc_ref[...] = jnp.zeros_like(dsc_ref)

        for g in range(4):
            gs = slice(g * PG, (g + 1) * PG)
            dmg, pg, wg = dm_ref[:, gs], p_ref[:, gs], wg_ref[g]
            dsc_ref[:, gs] += jnp.sum(dmg * _dot(pg, wg), axis=0, keepdims=True)
            dy = (dmg * sc_ref[:, gs]).astype(BF16)
            dp_ref[:, gs] = _dot_nt(dy, wg)
            dwg_ref[g] += _dot_tn(pg, dy)

    return _pcall(
        body, name="pool_bwd_group", grid=(S // tm,),
        in_specs=[pl.BlockSpec((tm, D), lambda i: (i, 0)), pl.BlockSpec((tm, D), lambda i: (i, 0)),
                  pl.BlockSpec((4, PG, PG), lambda i: (0, 0, 0)), pl.BlockSpec((1, D), lambda i: (0, 0))],
        out_specs=[pl.BlockSpec((tm, D), lambda i: (i, 0)), pl.BlockSpec((4, PG, PG), lambda i: (0, 0, 0)),
                   pl.BlockSpec((1, D), lambda i: (0, 0))],
        out_shape=[jax.ShapeDtypeStruct((S, D), F32), jax.ShapeDtypeStruct((4, PG, PG), F32),
                   jax.ShapeDtypeStruct((1, D), F32)],
        compiler_params=_cp("arbitrary"))(dm, p, wgrp, scale)


def pool_bwd_window(dp):
    S = dp.shape[0]
    T, HB = min(POOL_T, S), POOL_HALO
    per = T // HB
    nt = S // T

    def body(dp_ref, nxt_ref, du_ref):
        i = pl.program_id(0)
        dc = dp_ref[...]
        nxt = jnp.where(i < nt - 1, nxt_ref[...], 0.0)
        d_cur = _iota((T, T), 1) - _iota((T, T), 0)
        d_nxt = _iota((T, HB), 1) - _iota((T, HB), 0) + T
        tg = i * T + _iota((T, 1), 0)
        tn_ = (i + 1) * T + _iota((HB, 1), 0)
        for g, w in enumerate(POOL_WINDOWS):
            gs = slice(g * PG, (g + 1) * PG)
            band = (d_cur >= 0) & (d_cur < w)
            band_n = (d_nxt >= 0) & (d_nxt < w)
            dcg = dc[:, gs]
            cur = dcg / jnp.minimum(tg + 1, w).astype(F32)
            nx = nxt[:, gs] / jnp.minimum(tn_ + 1, w).astype(F32)
            du_ref[:, gs] = (_sel(_dot, cur, band, False) + _sel(_dot, nx, band_n, False) - dcg).astype(BF16)

    return _pcall(
        body, name="pool_bwd_window", grid=(nt,),
        in_specs=[pl.BlockSpec((T, D), lambda i: (i, 0)),
                  pl.BlockSpec((HB, D), lambda i: (jnp.minimum((i + 1) * per, S // HB - 1), 0))],
        out_specs=pl.BlockSpec((T, D), lambda i: (i, 0)),
        out_shape=jax.ShapeDtypeStruct((S, D), BF16),
        compiler_params=_cp("parallel"))(dp, dp)


CONV_T = 256


def _shift_down(xc, prev8, j):
    if j == 0:
        return xc
    T = xc.shape[0]
    body = pltpu.roll(xc, j, 0)
    first = jnp.where(_iota((8, 1), 0) < j, pltpu.roll(prev8, j, 0), body[0:8])
    return jnp.concatenate([first, body[8:T]], axis=0)


def _shift_up(dc, next8, j):
    if j == 0:
        return dc
    T = dc.shape[0]
    body = pltpu.roll(dc, T - j, 0)
    last = jnp.where(_iota((8, 1), 0) + j < 8, body[T - 8:T], pltpu.roll(next8, 8 - j, 0))
    return jnp.concatenate([body[0:T - 8], last], axis=0)


def conv_fwd(xbc, conv_w, conv_b):
    S = xbc.shape[0]
    T = min(CONV_T, S)
    CB = 1024

    def body(x_ref, prev_ref, w_ref, b_ref, o_ref):
        i = pl.program_id(1)
        xc = x_ref[...]
        prev8 = jnp.where(i > 0, prev_ref[...], 0.0)
        pre = b_ref[...] + w_ref[3:4, :] * xc
        for j in range(1, 4):
            pre = pre + w_ref[3 - j:4 - j, :] * _shift_down(xc, prev8, j)
        o_ref[...] = pre * _sigmoid(pre)

    return _pcall(
        body, name="conv_fwd", grid=(CONV_CH // CB, S // T),
        in_specs=[pl.BlockSpec((T, CB), lambda c, i: (i, c)),
                  pl.BlockSpec((8, CB), lambda c, i: (jnp.maximum(i * (T // 8) - 1, 0), c)),
                  pl.BlockSpec((4, CB), lambda c, i: (0, c)), pl.BlockSpec((1, CB), lambda c, i: (0, c))],
        out_specs=pl.BlockSpec((T, CB), lambda c, i: (i, c)),
        out_shape=jax.ShapeDtypeStruct((S, CONV_CH), F32),
        compiler_params=_cp("parallel", "parallel"))(xbc, xbc, conv_w, conv_b)


def conv_bwd_pre(dact, xbc, conv_w, conv_b):
    S = xbc.shape[0]
    T = min(CONV_T, S)
    CB = 1024

    def body(da_ref, x_ref, prev_ref, w_ref, b_ref, dpre_ref, dw_ref, db_ref):
        i = pl.program_id(1)
        xc = x_ref[...]
        prev8 = jnp.where(i > 0, prev_ref[...], 0.0)
        sh = [_shift_down(xc, prev8, j) for j in range(4)]
        pre = b_ref[...] + w_ref[3:4, :] * sh[0]
        for j in range(1, 4):
            pre = pre + w_ref[3 - j:4 - j, :] * sh[j]
        s = _sigmoid(pre)
        dpre = da_ref[...] * (s * (1.0 + pre * (1.0 - s)))
        dpre_ref[...] = dpre
        rows = [jnp.sum(dpre * sh[3 - k], axis=0, keepdims=True) for k in range(4)]
        dw = jnp.concatenate(rows + [jnp.zeros((4, CB), F32)], axis=0)
        db = jnp.sum(dpre, axis=0, keepdims=True)

        @pl.when(i == 0)
        def _():
            dw_ref[...] = dw
            db_ref[...] = db

        @pl.when(i > 0)
        def _():
            dw_ref[...] += dw
            db_ref[...] += db

    return _pcall(
        body, name="conv_bwd_pre", grid=(CONV_CH // CB, S // T),
        in_specs=[pl.BlockSpec((T, CB), lambda c, i: (i, c)), pl.BlockSpec((T, CB), lambda c, i: (i, c)),
                  pl.BlockSpec((8, CB), lambda c, i: (jnp.maximum(i * (T // 8) - 1, 0), c)),
                  pl.BlockSpec((4, CB), lambda c, i: (0, c)), pl.BlockSpec((1, CB), lambda c, i: (0, c))],
        out_specs=[pl.BlockSpec((T, CB), lambda c, i: (i, c)), pl.BlockSpec((8, CB), lambda c, i: (0, c)),
                   pl.BlockSpec((1, CB), lambda c, i: (0, c))],
        out_shape=[jax.ShapeDtypeStruct((S, CONV_CH), F32), jax.ShapeDtypeStruct((8, CONV_CH), F32),
                   jax.ShapeDtypeStruct((1, CONV_CH), F32)],
        compiler_params=_cp("parallel", "arbitrary"))(dact, xbc, xbc, conv_w, conv_b)


def conv_bwd_input(dpre, conv_w):
    S = dpre.shape[0]
    T = min(CONV_T, S)
    CB = 1024
    nt = S // T

    def body(d_ref, nxt_ref, w_ref, o_ref):
        i = pl.program_id(1)
        dc = d_ref[...]
        next8 = jnp.where(i < nt - 1, nxt_ref[...], 0.0)
        acc = w_ref[3:4, :] * dc
        for j in range(1, 4):
            acc = acc + w_ref[3 - j:4 - j, :] * _shift_up(dc, next8, j)
        o_ref[...] = acc.astype(BF16)

    return _pcall(
        body, name="conv_bwd_input", grid=(CONV_CH // CB, nt),
        in_specs=[pl.BlockSpec((T, CB), lambda c, i: (i, c)),
                  pl.BlockSpec((8, CB), lambda c, i: (jnp.minimum((i + 1) * (T // 8), S // 8 - 1), c)),
                  pl.BlockSpec((4, CB), lambda c, i: (0, c))],
        out_specs=pl.BlockSpec((T, CB), lambda c, i: (i, c)),
        out_shape=jax.ShapeDtypeStruct((S, CONV_CH), BF16),
        compiler_params=_cp("parallel", "parallel"))(dpre, dpre, conv_w)


def _ssd_chunk_terms(dt_ref, bias_ref, alog_ref):
    L = CH
    dtp = dt_ref[...] + bias_ref[...]
    dt = jnp.maximum(dtp, 0.0) + jnp.log(1.0 + jnp.exp(-jnp.abs(dtp)))
    a = -jnp.exp(alog_ref[...])
    da = dt * a
    tri = _iota((L, L), 0) >= _iota((L, L), 1)
    acum = _sel(_dot, da, tri, False)
    triu = _iota((L, L), 0) <= _iota((L, L), 1)
    acum_row = _sel(_dot_tn, da, triu)
    expand = _iota((NH, DI), 1) // HP == _iota((NH, DI), 0)
    acum_full = _sel(_dot, acum, expand)
    e_full = jnp.exp(acum_full)
    w_full = jnp.exp(acum_full[L - 1:L, :] - acum_full)
    dt_full = _sel(_dot, dt, expand)
    return dtp, dt, a, acum, acum_row, expand, triu, e_full, w_full, dt_full


def ssd_scan_fwd(xbc_act, dt_raw, dt_bias, a_log, d_full):
    S = xbc_act.shape[0]
    L = CH
    nc = S // L

    def body(xs_ref, b_ref, c_ref, dt_ref, bias_ref, alog_ref, d_ref, y_ref, st_ref, state):
        c = pl.program_id(0)

        @pl.when(c == 0)
        def _():
            state[...] = jnp.zeros_like(state)

        st_ref[0] = state[...]
        _, _, _, acum, acum_row, _, _, e_full, w_full, dt_full = _ssd_chunk_terms(dt_ref, bias_ref, alog_ref)
        causal = _iota((L, L), 0) >= _iota((L, L), 1)
        lane_head = _iota((1, GW), 1) // HP
        for g in range(NG):
            gs = slice(g * GW, (g + 1) * GW)
            ns = slice(g * NS, (g + 1) * NS)
            xs_g = xs_ref[:, gs]
            xdt_g = xs_g * dt_full[:, gs]
            cg = c_ref[:, ns].astype(BF16)
            bg = b_ref[:, ns].astype(BF16)
            gmat = _dot_nt(cg, bg)
            yg = jnp.zeros((L, GW), F32)
            for hh in range(4):
                h = 4 * g + hh
                diff = acum[:, h:h + 1] - acum_row[h:h + 1, :]
                dk = jnp.exp(jnp.where(causal, diff, -1e30))
                xm = jnp.where(lane_head == hh, xdt_g, 0.0).astype(BF16)
                yg = yg + _dot((gmat * dk).astype(BF16), xm)
            sg = state[g]
            yoff = _dot(cg, sg.astype(BF16)) * e_full[:, gs]
            y_ref[:, gs] = yg + yoff + d_ref[:, gs] * xs_g
            state[g] = sg * e_full[L - 1:L, gs] + _dot_tn(bg, (w_full[:, gs] * xdt_g).astype(BF16))

    return _pcall(
        body, name="ssd_scan_fwd", grid=(nc,),
        in_specs=[pl.BlockSpec((L, DI), lambda c: (c, 0)), pl.BlockSpec((L, 1024), lambda c: (c, 2)),
                  pl.BlockSpec((L, 1024), lambda c: (c, 3)), pl.BlockSpec((L, NH), lambda c: (c, 0)),
                  pl.BlockSpec((1, NH), lambda c: (0, 0)), pl.BlockSpec((1, NH), lambda c: (0, 0)),
                  pl.BlockSpec((1, DI), lambda c: (0, 0))],
        out_specs=[pl.BlockSpec((L, DI), lambda c: (c, 0)), pl.BlockSpec((1, NG, NS, GW), lambda c: (c, 0, 0, 0))],
        out_shape=[jax.ShapeDtypeStruct((S, DI), F32), jax.ShapeDtypeStruct((nc, NG, NS, GW), F32)],
        scratch_shapes=[pltpu.VMEM((NG, NS, GW), F32)],
        compiler_params=_cp("arbitrary"))(xbc_act, xbc_act, xbc_act, dt_raw, dt_bias, a_log, d_full)


def ssd_scan_bwd(dy, xbc_act, dt_raw, dt_bias, a_log, d_full, states):
    S = xbc_act.shape[0]
    L = CH
    nc = S // L

    def body(dy_ref, xs_ref, b_ref, c_ref, dt_ref, bias_ref, alog_ref, d_ref, st_ref,
             dxbc_ref, ddt_ref, dbias_ref, dalog_ref, dd_ref, dstate):
        c = pl.program_id(0)

        @pl.when(c == 0)
        def _():
            dstate[...] = jnp.zeros_like(dstate)
            dbias_ref[...] = jnp.zeros_like(dbias_ref)
            dalog_ref[...] = jnp.zeros_like(dalog_ref)
            dd_ref[...] = jnp.zeros_like(dd_ref)

        dtp, dt, a, acum, acum_row, expand, triu, e_full, w_full, dt_full = _ssd_chunk_terms(
            dt_ref, bias_ref, alog_ref)
        causal = _iota((L, L), 0) >= _iota((L, L), 1)
        lane_head = _iota((1, GW), 1) // HP
        head_id = _iota((1, NH), 1)
        head_row = _iota((NH, 1), 0)
        dacum = jnp.zeros((L, NH), F32)
        dacum_t = jnp.zeros((NH, L), F32)
        red_parts = []
        dxdt_parts = []
        alast_parts = []
        for g in range(NG):
            gs = slice(g * GW, (g + 1) * GW)
            ns = slice(g * NS, (g + 1) * NS)
            xs_g = xs_ref[:, gs]
            xdt_g = xs_g * dt_full[:, gs]
            dy_g = dy_ref[:, gs]
            cg = c_ref[:, ns].astype(BF16)
            bg = b_ref[:, ns].astype(BF16)
            gmat = _dot_nt(cg, bg)
            sg = st_ref[0, g]
            dsg = dstate[g]
            sgb, dsgb = sg.astype(BF16), dsg.astype(BF16)
            cs = _dot(cg, sgb)
            bds = _dot(bg, dsgb)
            e_g, w_g = e_full[:, gs], w_full[:, gs]
            dxdt = w_g * bds
            dgsum = jnp.zeros((L, L), F32)
            for hh in range(4):
                h = 4 * g + hh
                hm = lane_head == hh
                diff = acum[:, h:h + 1] - acum_row[h:h + 1, :]
                dk = jnp.exp(jnp.where(causal, diff, -1e30))
                m = gmat * dk
                dym = jnp.where(hm, dy_g, 0.0).astype(BF16)
                xm = jnp.where(hm, xdt_g, 0.0).astype(BF16)
                dm = _dot_nt(dym, xm)
                dxdt = dxdt + _dot_tn(m.astype(BF16), dym)
                dgsum = dgsum + dm * dk
                em = dm * m
                dacum = dacum + jnp.sum(em, axis=1, keepdims=True) * (head_id == h).astype(F32)
                dacum_t = dacum_t + (head_row == h).astype(F32) * jnp.sum(em, axis=0, keepdims=True)
            dgb = dgsum.astype(BF16)
            edy = (e_g * dy_g).astype(BF16)
            wx = (w_g * xdt_g).astype(BF16)
            dc_g = _dot(dgb, bg) + _dot_nt(edy, sgb)
            db_g = _dot_tn(dgb, cg) + _dot_nt(wx, dsgb)
            dxbc_ref[:, DI + g * NS:DI + (g + 1) * NS] = db_g
            dxbc_ref[:, DI + 1024 + g * NS:DI + 1024 + (g + 1) * NS] = dc_g
            p2w = bds * xdt_g * w_g
            red_parts.append(dy_g * cs * e_g - p2w)
            alast_parts.append(jnp.sum(p2w, axis=0, keepdims=True)
                               + e_full[L - 1:L, gs] * jnp.sum(dsg * sg, axis=0, keepdims=True))
            dxdt_parts.append(dxdt)
            dstate[g] = e_full[L - 1:L, gs] * dsg + _dot_tn(cg, edy)
            dxbc_ref[:, gs] = dxdt * dt_full[:, gs] + dy_g * d_ref[:, gs]
            dd_ref[:, gs] += jnp.sum(dy_g * xs_g, axis=0, keepdims=True)
        red = jnp.concatenate(red_parts, axis=1)
        dxdt_all = jnp.concatenate(dxdt_parts, axis=1)
        alast = jnp.concatenate(alast_parts, axis=1)
        eye = _iota((NH, NH), 0) == _iota((NH, NH), 1)
        dacum = dacum - _sel(_dot_tn, dacum_t, eye) + _sel(_dot_nt, red, expand)
        dalast = _sel(_dot_nt, jnp.broadcast_to(alast, (8, DI)), expand)[0:1, :]
        dacum = dacum + jnp.where(_iota((L, 1), 0) == L - 1, dalast, 0.0)
        dda = _sel(_dot, dacum, triu, False)
        ddt = _sel(_dot_nt, dxdt_all * xs_ref[...], expand) + dda * a
        dalog_ref[...] += jnp.sum(dda * dt, axis=0, keepdims=True) * a
        ddt_raw = ddt * _sigmoid(dtp)
        ddt_ref[...] = ddt_raw
        dbias_ref[...] += jnp.sum(ddt_raw, axis=0, keepdims=True)

    rev = lambda c: (nc - 1 - c, 0)
    return _pcall(
        body, name="ssd_scan_bwd", grid=(nc,),
        in_specs=[pl.BlockSpec((L, DI), rev), pl.BlockSpec((L, DI), rev),
                  pl.BlockSpec((L, 1024), lambda c: (nc - 1 - c, 2)), pl.BlockSpec((L, 1024), lambda c: (nc - 1 - c, 3)),
                  pl.BlockSpec((L, NH), rev), pl.BlockSpec((1, NH), lambda c: (0, 0)),
                  pl.BlockSpec((1, NH), lambda c: (0, 0)), pl.BlockSpec((1, DI), lambda c: (0, 0)),
                  pl.BlockSpec((1, NG, NS, GW), lambda c: (nc - 1 - c, 0, 0, 0))],
        out_specs=[pl.BlockSpec((L, CONV_CH), rev), pl.BlockSpec((L, NH), rev),
                   pl.BlockSpec((1, NH), lambda c: (0, 0)), pl.BlockSpec((1, NH), lambda c: (0, 0)),
                   pl.BlockSpec((1, DI), lambda c: (0, 0))],
        out_shape=[jax.ShapeDtypeStruct((S, CONV_CH), F32), jax.ShapeDtypeStruct((S, NH), F32),
                   jax.ShapeDtypeStruct((1, NH), F32), jax.ShapeDtypeStruct((1, NH), F32),
                   jax.ShapeDtypeStruct((1, DI), F32)],
        scratch_shapes=[pltpu.VMEM((NG, NS, GW), F32)],
        compiler_params=_cp("arbitrary"))(dy, xbc_act, xbc_act, xbc_act, dt_raw, dt_bias, a_log, d_full, states)


def gate_norm_fwd(y, z, out_norm, tm=256):
    S = y.shape[0]
    tm = min(tm, S)

    def body(y_ref, z_ref, on_ref, o_ref):
        zv = z_ref[...]
        gin = y_ref[...] * (zv * _sigmoid(zv))
        for g in range(NG):
            gs = slice(g * GW, (g + 1) * GW)
            blk = gin[:, gs]
            r = lax.rsqrt(jnp.mean(blk * blk, axis=-1, keepdims=True) + EPS)
            o_ref[:, gs] = (blk * r * on_ref[:, gs]).astype(BF16)

    return _pcall(
        body, name="gate_norm_fwd", grid=(S // tm,),
        in_specs=[pl.BlockSpec((tm, DI), lambda i: (i, 0)), pl.BlockSpec((tm, DI), lambda i: (i, 0)),
                  pl.BlockSpec((1, DI), lambda i: (0, 0))],
        out_specs=pl.BlockSpec((tm, DI), lambda i: (i, 0)),
        out_shape=jax.ShapeDtypeStruct((S, DI), BF16),
        compiler_params=_cp("parallel"))(y, z, out_norm)


def gate_norm_bwd(dgn, y, z, out_norm, tm=256):
    S = y.shape[0]
    tm = min(tm, S)

    def body(dg_ref, y_ref, z_ref, on_ref, dy_ref, dz_ref, don_ref):
        i = pl.program_id(0)

        @pl.when(i == 0)
        def _():
            don_ref[...] = jnp.zeros_like(don_ref)

        zv, yv = z_ref[...], y_ref[...]
        s = _sigmoid(zv)
        sz = zv * s
        gin = yv * sz
        for g in range(NG):
            gs = slice(g * GW, (g + 1) * GW)
            blk = gin[:, gs]
            r = lax.rsqrt(jnp.mean(blk * blk, axis=-1, keepdims=True) + EPS)
            n = blk * r
            dg = dg_ref[:, gs]
            don_ref[:, gs] += jnp.sum(dg * n, axis=0, keepdims=True)
            dn = dg * on_ref[:, gs]
            dgin = r * (dn - n * jnp.mean(dn * n, axis=-1, keepdims=True))
            dy_ref[:, gs] = dgin * sz[:, gs]
            dz_ref[:, gs] = (dgin * yv[:, gs] * (s[:, gs] * (1.0 + zv[:, gs] * (1.0 - s[:, gs])))).astype(BF16)

    return _pcall(
        body, name="gate_norm_bwd", grid=(S // tm,),
        in_specs=[pl.BlockSpec((tm, DI), lambda i: (i, 0))] * 3 + [pl.BlockSpec((1, DI), lambda i: (0, 0))],
        out_specs=[pl.BlockSpec((tm, DI), lambda i: (i, 0)), pl.BlockSpec((tm, DI), lambda i: (i, 0)),
                   pl.BlockSpec((1, DI), lambda i: (0, 0))],
        out_shape=[jax.ShapeDtypeStruct((S, DI), F32), jax.ShapeDtypeStruct((S, DI), BF16),
                   jax.ShapeDtypeStruct((1, DI), F32)],
        compiler_params=_cp("arbitrary"))(dgn, y, z, out_norm)


SB_T = 256
SB_QSCALE = 0.125
SB_DEAD = -110.0
SB_UNSEEN = -1e30


def _head_norm(xv, lo):
    sq = xv * xv
    s0 = jnp.sum(jnp.where(lo, sq, 0.0), axis=-1, keepdims=True)
    s1 = jnp.sum(jnp.where(lo, 0.0, sq), axis=-1, keepdims=True)
    return jnp.where(lo, lax.rsqrt(s0 / SBD + EPS), lax.rsqrt(s1 / SBD + EPS))


def sb_prep_fwd(qkv, qg, kg, tm=256):
    S = qkv.shape[0]
    tm = min(tm, S)

    def body(x_ref, qg_ref, kg_ref, q_ref, k_ref, v_ref):
        lo = _iota((1, LANES), 1) < SBD
        for sl in range(D // LANES):
            cs = slice(sl * LANES, (sl + 1) * LANES)
            xq = x_ref[:, cs]
            q_ref[:, cs] = ((xq * _head_norm(xq, lo) * qg_ref[...]).astype(BF16).astype(F32) * SB_QSCALE).astype(BF16)
            xk = x_ref[:, D + sl * LANES:D + (sl + 1) * LANES]
            k_ref[:, cs] = (xk * _head_norm(xk, lo) * kg_ref[...]).astype(BF16)
        v_ref[...] = x_ref[:, 2 * D:3 * D].astype(BF16)

    return _pcall(
        body, name="sb_prep_fwd", grid=(S // tm,),
        in_specs=[pl.BlockSpec((tm, 3 * D), lambda i: (i, 0)), pl.BlockSpec((1, LANES), lambda i: (0, 0)),
                  pl.BlockSpec((1, LANES), lambda i: (0, 0))],
        out_specs=[pl.BlockSpec((tm, D), lambda i: (i, 0))] * 3,
        out_shape=[jax.ShapeDtypeStruct((S, D), BF16)] * 3,
        compiler_params=_cp("parallel"))(qkv, qg, kg)


def sb_prep_bwd(dqs, dkn, dv, qkv, qg, kg, tm=256):
    S = qkv.shape[0]
    tm = min(tm, S)

    def body(dq_ref, dk_ref, dv_ref, x_ref, qg_ref, kg_ref, dx_ref, dqg_ref, dkg_ref):
        i = pl.program_id(0)

        @pl.when(i == 0)
        def _():
            dqg_ref[...] = jnp.zeros_like(dqg_ref)
            dkg_ref[...] = jnp.zeros_like(dkg_ref)

        lo = _iota((1, LANES), 1) < SBD

        def one(xv, dh, gain):
            r = _head_norm(xv, lo)
            y = xv * r
            dy = dh * gain
            t = dy * y
            m0 = jnp.sum(jnp.where(lo, t, 0.0), axis=-1, keepdims=True)
            m1 = jnp.sum(jnp.where(lo, 0.0, t), axis=-1, keepdims=True)
            dx = r * (dy - y * (jnp.where(lo, m0, m1) / SBD))
            return dx, jnp.sum(dh * y, axis=0, keepdims=True)

        for sl in range(D // LANES):
            cs = slice(sl * LANES, (sl + 1) * LANES)
            dx, dg = one(x_ref[:, cs], dq_ref[:, cs] * SB_QSCALE, qg_ref[...])
            dx_ref[:, cs] = dx.astype(BF16)
            dqg_ref[:, cs] += dg
            ks = slice(D + sl * LANES, D + (sl + 1) * LANES)
            dx, dg = one(x_ref[:, ks], dk_ref[:, cs], kg_ref[...])
            dx_ref[:, ks] = dx.astype(BF16)
            dkg_ref[:, cs] += dg
        dx_ref[:, 2 * D:3 * D] = dv_ref[...].astype(BF16)

    return _pcall(
        body, name="sb_prep_bwd", grid=(S // tm,),
        in_specs=[pl.BlockSpec((tm, D), lambda i: (i, 0))] * 3
        + [pl.BlockSpec((tm, 3 * D), lambda i: (i, 0)), pl.BlockSpec((1, LANES), lambda i: (0, 0)),
           pl.BlockSpec((1, LANES), lambda i: (0, 0))],
        out_specs=[pl.BlockSpec((tm, 3 * D), lambda i: (i, 0)), pl.BlockSpec((1, D), lambda i: (0, 0)),
                   pl.BlockSpec((1, D), lambda i: (0, 0))],
        out_shape=[jax.ShapeDtypeStruct((S, 3 * D), BF16), jax.ShapeDtypeStruct((1, D), F32),
                   jax.ShapeDtypeStruct((1, D), F32)],
        compiler_params=_cp("arbitrary"))(dqs, dkn, dv, qkv, qg, kg)


def _split_dot(x, u):
    hi = x.astype(BF16)
    lo = (x - hi.astype(F32)).astype(BF16)
    return _dot(hi, u) + _dot(lo, u)


def _sb_logits(qh, kb, valid):
    z = _dot_nt(qh, kb)
    e = jnp.exp(-jnp.abs(z))
    lp = jnp.log(1.0 + e)
    lb = jnp.minimum(z, 0.0) - lp
    l1m = jnp.where(valid, lb - z, 0.0)
    return z, e, lb, l1m


def sb_fwd(qs, kn, v):
    S = qs.shape[0]
    T = min(SB_T, S)
    nq = S // T

    def body(q_ref, k_ref, v_ref, o_ref, r_ref, oacc, rrun):
        i = pl.program_id(1)
        qb = q_ref[...]
        lo = _iota((1, LANES), 1) < SBD
        masks = (lo, jnp.logical_not(lo))
        qhs = [jnp.where(hm, qb, jnp.zeros_like(qb)) for hm in masks]
        row, col = _iota((T, T), 0), _iota((T, T), 1)
        u = (row > col).astype(BF16)
        lane_blk = _iota((T, LANES), 1)
        oacc[...] = jnp.zeros_like(oacc)
        rrun[...] = jnp.zeros_like(rrun)
        r_ref[...] = jnp.full((2, T, LANES), SB_UNSEEN, F32)

        def live(carry):
            s, rmax = carry
            return jnp.logical_and(s <= i, rmax > SB_DEAD)

        def step(carry):
            s, _ = carry
            j = i - s
            off = pl.multiple_of(j * T, T)
            kb = k_ref[pl.ds(off, T), :]
            vb = v_ref[pl.ds(off, T), :]
            valid = (j * T + col) < (i * T + row)
            acc, rmax = None, None
            for hh in range(2):
                _, _, lb, l1m = _sb_logits(qhs[hh], kb, valid)
                r = rrun[hh]
                aft = _split_dot(l1m, u) + r
                a = jnp.where(valid, jnp.exp(lb + aft), 0.0)
                t = _dot(a.astype(BF16), jnp.where(masks[hh], vb, jnp.zeros_like(vb)))
                acc = t if acc is None else acc + t
                r_ref[hh] = jnp.where(lane_blk == j, r, r_ref[hh])
                rnew = r + jnp.sum(l1m, axis=-1, keepdims=True)
                rrun[hh] = rnew
                top = jnp.max(rnew)
                rmax = top if rmax is None else jnp.maximum(rmax, top)
            oacc[...] += acc
            return s + 1, rmax

        lax.while_loop(live, step, (jnp.int32(0), jnp.float32(0.0)))
        o_ref[...] = oacc[...].astype(BF16)

    return _pcall(
        body, name="sb_fwd", grid=(D // LANES, nq),
        in_specs=[pl.BlockSpec((T, LANES), lambda h, i: (i, h)), pl.BlockSpec((S, LANES), lambda h, i: (0, h)),
                  pl.BlockSpec((S, LANES), lambda h, i: (0, h))],
        out_specs=[pl.BlockSpec((T, LANES), lambda h, i: (i, h)), pl.BlockSpec((2, T, LANES), lambda h, i: (h, i, 0))],
        out_shape=[jax.ShapeDtypeStruct((S, D), BF16), jax.ShapeDtypeStruct((SBH, S, LANES), F32)],
        scratch_shapes=[pltpu.VMEM((T, LANES), F32), pltpu.VMEM((2, T, 1), F32)],
        compiler_params=_cp("parallel", "arbitrary"))(qs, kn, v)


def sb_bwd(qs, kn, v, do, rsave):
    S = qs.shape[0]
    T = min(SB_T, S)
    nq = S // T

    def body(q_ref, k_ref, v_ref, do_ref, r_ref, dq_ref, dk_ref, dv_ref, crun):
        i = pl.program_id(1)

        @pl.when(i == 0)
        def _():
            dk_ref[...] = jnp.zeros_like(dk_ref)
            dv_ref[...] = jnp.zeros_like(dv_ref)

        qb, dob = q_ref[...], do_ref[...]
        lo = _iota((1, LANES), 1) < SBD
        masks = (lo, jnp.logical_not(lo))
        qhs = [jnp.where(hm, qb, jnp.zeros_like(qb)) for hm in masks]
        dohs = [jnp.where(hm, dob, jnp.zeros_like(dob)) for hm in masks]
        row, col = _iota((T, T), 0), _iota((T, T), 1)
        u = (row > col).astype(BF16)
        u2 = (row < col).astype(BF16)
        lane_blk = _iota((T, LANES), 1)
        dq_ref[...] = jnp.zeros_like(dq_ref)
        crun[...] = jnp.zeros_like(crun)

        def step(j, carry):
            off = pl.multiple_of(j * T, T)
            kb = k_ref[pl.ds(off, T), :]
            vb = v_ref[pl.ds(off, T), :]
            valid = (j * T + col) < (i * T + row)
            dq_t, dk_t, dv_t = None, None, None
            for hh in range(2):
                hm, qh, doh = masks[hh], qhs[hh], dohs[hh]
                z, e, lb, l1m = _sb_logits(qh, kb, valid)
                r = jnp.sum(jnp.where(lane_blk == j, r_ref[hh], 0.0), axis=-1, keepdims=True)
                aft = _split_dot(l1m, u) + r
                a = jnp.where(valid, jnp.exp(lb + aft), 0.0)
                w = a * _dot_nt(doh, jnp.where(hm, vb, jnp.zeros_like(vb)))
                cprev = crun[hh]
                cw = _split_dot(w, u2) + cprev
                inv = 1.0 / (1.0 + e)
                pos = z >= 0.0
                beta = jnp.where(pos, 1.0, e) * inv
                onem = jnp.where(pos, e, 1.0) * inv
                dz = jnp.where(valid, w * onem - beta * cw, 0.0).astype(BF16)
                tq = _dot(dz, jnp.where(hm, kb, jnp.zeros_like(kb)))
                tk = _dot_tn(dz, qh)
                tv = _dot_tn(a.astype(BF16), doh)
                dq_t, dk_t, dv_t = (tq, tk, tv) if dq_t is None else (dq_t + tq, dk_t + tk, dv_t + tv)
                crun[hh] = cprev + jnp.sum(w, axis=-1, keepdims=True)
            dq_ref[...] += dq_t
            dk_ref[pl.ds(off, T), :] += dk_t
            dv_ref[pl.ds(off, T), :] += dv_t
            return carry

        n_live = None
        for hh in range(2):
            col_max = jnp.max(r_ref[hh], axis=0, keepdims=True)
            seen = jnp.logical_and(col_max > SB_DEAD, _iota((1, LANES), 1) <= i)
            n = jnp.sum(seen.astype(jnp.int32))
            n_live = n if n_live is None else jnp.maximum(n_live, n)
        lax.fori_loop(i + 1 - n_live, i + 1, step, 0)

    return _pcall(
        body, name="sb_bwd", grid=(D // LANES, nq),
        in_specs=[pl.BlockSpec((T, LANES), lambda h, i: (i, h)), pl.BlockSpec((S, LANES), lambda h, i: (0, h)),
                  pl.BlockSpec((S, LANES), lambda h, i: (0, h)), pl.BlockSpec((T, LANES), lambda h, i: (i, h)),
                  pl.BlockSpec((2, T, LANES), lambda h, i: (h, i, 0))],
        out_specs=[pl.BlockSpec((T, LANES), lambda h, i: (i, h)), pl.BlockSpec((S, LANES), lambda h, i: (0, h)),
                   pl.BlockSpec((S, LANES), lambda h, i: (0, h))],
        out_shape=[jax.ShapeDtypeStruct((S, D), F32)] * 3,
        scratch_shapes=[pltpu.VMEM((2, T, 1), F32)],
        compiler_params=_cp("parallel", "arbitrary"))(qs, kn, v, do, rsave)


def loss_head(y, target, tm=512):
    S = y.shape[0]
    tm = min(tm, S)

    def body(y_ref, t_ref, ls_ref, dy_ref):
        i = pl.program_id(0)
        err = y_ref[...] - t_ref[...]
        dy_ref[...] = err * (1.0 / D)
        part = jnp.sum(err * err, axis=0, keepdims=True)

        @pl.when(i == 0)
        def _():
            ls_ref[...] = part

        @pl.when(i > 0)
        def _():
            ls_ref[...] += part

    return _pcall(
        body, name="loss_head", grid=(S // tm,),
        in_specs=[pl.BlockSpec((tm, D), lambda i: (i, 0))] * 2,
        out_specs=[pl.BlockSpec((1, D), lambda i: (0, 0)), pl.BlockSpec((tm, D), lambda i: (i, 0))],
        out_shape=[jax.ShapeDtypeStruct((1, D), F32), jax.ShapeDtypeStruct((S, D), F32)],
        compiler_params=_cp("arbitrary"))(y, target)


def _row_tile(rows, cols):
    cap = max(8, (1 << 20) // (4 * cols))
    return max(t for t in range(8, min(rows, cap) + 1, 8) if rows % t == 0)


def _adamw_update(w, g, m, v):
    c1 = 1.0 / (1.0 - ADAM_B1 ** ADAM_STEP)
    c2 = 1.0 / (1.0 - ADAM_B2 ** ADAM_STEP)
    mn = ADAM_B1 * m + (1.0 - ADAM_B1) * g
    vn = ADAM_B2 * v + (1.0 - ADAM_B2) * (g * g)
    return -ADAM_LR * ((mn * c1) / (jnp.sqrt(vn * c2) + ADAM_EPS) + ADAM_WD * w), mn, vn


def adamw(w, g, m, v, name="adamw"):
    R, C = w.shape
    tr = _row_tile(R, C)

    def body(w_ref, g_ref, m_ref, v_ref, d_ref, mo_ref, vo_ref):
        d_ref[...], mo_ref[...], vo_ref[...] = _adamw_update(w_ref[...], g_ref[...], m_ref[...], v_ref[...])

    spec = pl.BlockSpec((tr, C), lambda i: (i, 0))
    return _pcall(
        body, name=name, grid=(R // tr,), in_specs=[spec] * 4, out_specs=[spec] * 3,
        out_shape=[jax.ShapeDtypeStruct((R, C), F32)] * 3,
        compiler_params=_cp("parallel"))(w, g, m, v)


def adamw_halves(w, g_mine, g_other, m, v, name="adamw_halves"):
    R, C = w.shape
    H = R // 2
    tr = _row_tile(H, C)
    n_i = H // tr
    where = lax.axis_index("c").astype(jnp.int32).reshape(1)

    def body(s_ref, w_ref, gm_ref, go_ref, m_ref, v_ref, g_ref, d_ref, mo_ref, vo_ref):
        g = jnp.where(pl.program_id(0) == s_ref[0], gm_ref[...], go_ref[...])
        g_ref[...] = g
        d_ref[...], mo_ref[...], vo_ref[...] = _adamw_update(w_ref[...], g, m_ref[...], v_ref[...])

    full = pl.BlockSpec((tr, C), lambda h, i, s: (h * n_i + i, 0))
    half = pl.BlockSpec((tr, C), lambda h, i, s: (i, 0))
    return _pcall(
        body, name=name,
        grid_spec=pltpu.PrefetchScalarGridSpec(
            num_scalar_prefetch=1, grid=(2, n_i), in_specs=[full, half, half, full, full], out_specs=[full] * 4),
        out_shape=[jax.ShapeDtypeStruct((R, C), F32)] * 4,
        compiler_params=_cp("parallel", "parallel"))(where, w, g_mine, g_other, m, v)


def pair_sum(gstacks, halves):
    c = lax.axis_index("c")
    me = 2 * lax.axis_index("x") + lax.axis_index("y")
    where = jnp.stack([c, me]).astype(jnp.int32)
    outs = []
    for g, xh in zip(gstacks, halves):
        _, H, C = xh.shape
        t = _row_tile(H, C)
        n_i = H // t

        def body(s_ref, g_ref, x_ref, qb_ref, own_ref):
            j = pl.program_id(1)
            q = g_ref[0] + x_ref[0]
            qb_ref[0] = q.astype(BF16)

            @pl.when(j == s_ref[1])
            def _():
                own_ref[...] = q

        outs.append(_pcall(
            body, name="pair_sum",
            grid_spec=pltpu.PrefetchScalarGridSpec(
                num_scalar_prefetch=1, grid=(n_i, N_CHIPS),
                in_specs=[pl.BlockSpec((1, t, C), lambda i, j, s, n_i=n_i: (j, s[0] * n_i + i, 0)),
                          pl.BlockSpec((1, t, C), lambda i, j, s: (j, i, 0))],
                out_specs=[pl.BlockSpec((1, t, C), lambda i, j, s: (j, i, 0)),
                           pl.BlockSpec((t, C), lambda i, j, s: (i, 0))]),
            out_shape=[jax.ShapeDtypeStruct((N_CHIPS, H, C), BF16), jax.ShapeDtypeStruct((H, C), F32)],
            compiler_params=_cp("parallel", "arbitrary"))(where, g, xh))
    return [o[0] for o in outs], [o[1] for o in outs]


def chip_sum(owns, recvs):
    outs = []
    for own, rc in zip(owns, recvs):
        H, C = own.shape
        t = _row_tile(H, C)

        def body(o_ref, r_ref, t_ref):
            t_ref[...] = ((o_ref[...] + r_ref[0].astype(F32)) + r_ref[1].astype(F32)) + r_ref[2].astype(F32)

        outs.append(_pcall(
            body, name="chip_sum", grid=(H // t,),
            in_specs=[pl.BlockSpec((t, C), lambda i: (i, 0)), pl.BlockSpec((3, t, C), lambda i: (0, i, 0))],
            out_specs=pl.BlockSpec((t, C), lambda i: (i, 0)),
            out_shape=jax.ShapeDtypeStruct((H, C), F32), compiler_params=_cp("parallel"))(own, rc))
    return outs


MESH = pl.DeviceIdType.MESH
ANY = pl.BlockSpec(memory_space=pl.ANY)
SPLIT_MIN_BYTES = 1 << 20


def _other_chips(x, y):
    return [(1 - x, y), (x, 1 - y), (1 - x, 1 - y)]


def _half_rows(rows, who):
    half = rows // 2
    return pl.ds(pl.multiple_of(who * half, 16), half)


def gather_all(shards):
    n = len(shards)
    rows = [s.shape[0] for s in shards]
    split = [r % 32 == 0 and s.size * s.dtype.itemsize >= SPLIT_MIN_BYTES for r, s in zip(rows, shards)]

    def body(*refs):
        ins, outs = refs[:n], refs[n:2 * n]
        ici_send, ici_recv, d2d_send, d2d_recv = refs[2 * n:]
        x, y, c = lax.axis_index("x"), lax.axis_index("y"), lax.axis_index("c")
        me, sib, chips = 2 * x + y, (x, y, 1 - c), _other_chips(x, y)

        def part(k, who):
            return _half_rows(rows[k], who) if split[k] else pl.ds(0, rows[k])

        def ici(k, r, block):
            px, py = chips[r]
            return pltpu.make_async_remote_copy(
                src_ref=ins[k].at[part(k, c)], dst_ref=outs[k].at[block, part(k, c)],
                send_sem=ici_send.at[3 * k + r], recv_sem=ici_recv.at[3 * k + r],
                device_id=(px, py, c), device_id_type=MESH)

        def d2d(k, r, who):
            px, py = chips[r]
            blk = outs[k].at[2 * px + py, part(k, who)]
            return pltpu.make_async_remote_copy(
                src_ref=blk, dst_ref=blk, send_sem=d2d_send.at[3 * k + r], recv_sem=d2d_recv.at[3 * k + r],
                device_id=sib, device_id_type=MESH)

        sends = [ici(k, r, me) for k in range(n) for r in range(3)]
        for cp in sends:
            cp.start()
        for r in range(3):
            px, py = chips[r]
            for k in range(n):
                ici(k, r, 2 * px + py).wait_recv()
                if split[k]:
                    fwd = d2d(k, r, c)
                    fwd.start()
                    sends.append(fwd)
        for r in range(3):
            for k in range(n):
                if split[k]:
                    d2d(k, r, 1 - c).wait_recv()
        for cp in sends:
            cp.wait_send()

    return _pcall(
        body, name="gather_all", in_specs=[ANY] * n, out_specs=[ANY] * n,
        out_shape=[jax.ShapeDtypeStruct((N_CHIPS,) + s.shape, s.dtype) for s in shards],
        scratch_shapes=[pltpu.SemaphoreType.DMA((3 * n,))] * 4)(*shards)


def swap_halves(gstacks):
    n = len(gstacks)

    def body(*refs):
        ins, outs, send_sems, recv_sems = refs[:n], refs[n:2 * n], refs[2 * n], refs[2 * n + 1]
        x, y, c = lax.axis_index("x"), lax.axis_index("y"), lax.axis_index("c")
        copies = [pltpu.make_async_remote_copy(
            src_ref=ins[k].at[:, _half_rows(ins[k].shape[1], 1 - c)], dst_ref=outs[k],
            send_sem=send_sems.at[k], recv_sem=recv_sems.at[k], device_id=(x, y, 1 - c), device_id_type=MESH)
            for k in range(n)]
        for cp in copies:
            cp.start()
        for cp in copies:
            cp.wait()

    return _pcall(
        body, name="swap_halves", in_specs=[ANY] * n, out_specs=[ANY] * n,
        out_shape=[jax.ShapeDtypeStruct((g.shape[0], g.shape[1] // 2, g.shape[2]), g.dtype) for g in gstacks],
        scratch_shapes=[pltpu.SemaphoreType.DMA((n,)), pltpu.SemaphoreType.DMA((n,))])(*gstacks)


def scatter_chips(stacks):
    n = len(stacks)

    def body(*refs):
        ins, outs, send_sems, recv_sems = refs[:n], refs[n:2 * n], refs[2 * n], refs[2 * n + 1]
        x, y, c = lax.axis_index("x"), lax.axis_index("y"), lax.axis_index("c")
        copies = [pltpu.make_async_remote_copy(
            src_ref=ins[k].at[2 * px + py], dst_ref=outs[k].at[r], send_sem=send_sems.at[3 * k + r],
            recv_sem=recv_sems.at[3 * k + r], device_id=(px, py, c), device_id_type=MESH)
            for k in range(n) for r, (px, py) in enumerate(_other_chips(x, y))]
        for cp in copies:
            cp.start()
        for cp in copies:
            cp.wait()

    return _pcall(
        body, name="scatter_chips", in_specs=[ANY] * n, out_specs=[ANY] * n,
        out_shape=[jax.ShapeDtypeStruct((3,) + s.shape[1:], s.dtype) for s in stacks],
        scratch_shapes=[pltpu.SemaphoreType.DMA((3 * n,)), pltpu.SemaphoreType.DMA((3 * n,))])(*stacks)


def swap_totals(totals):
    n = len(totals)

    def body(*refs):
        ins, outs, send_sems, recv_sems = refs[:n], refs[n:2 * n], refs[2 * n], refs[2 * n + 1]
        x, y, c = lax.axis_index("x"), lax.axis_index("y"), lax.axis_index("c")
        copies = [pltpu.make_async_remote_copy(
            src_ref=ins[k], dst_ref=outs[k], send_sem=send_sems.at[k], recv_sem=recv_sems.at[k],
            device_id=(x, y, 1 - c), device_id_type=MESH) for k in range(n)]
        for cp in copies:
            cp.start()
        for cp in copies:
            cp.wait()

    return _pcall(
        body, name="swap_totals", in_specs=[ANY] * n, out_specs=[ANY] * n,
        out_shape=[jax.ShapeDtypeStruct(t.shape, t.dtype) for t in totals],
        scratch_shapes=[pltpu.SemaphoreType.DMA((n,)), pltpu.SemaphoreType.DMA((n,))])(*totals)


def place_own(gathered, own):
    R, C = own.shape
    t = _row_tile(R, C)
    where = (2 * lax.axis_index("x") + lax.axis_index("y")).astype(jnp.int32).reshape(1)

    def body(s_ref, own_ref, g_ref, o_ref):
        o_ref[0] = own_ref[...]

    return _pcall(
        body, name="place_own",
        grid_spec=pltpu.PrefetchScalarGridSpec(
            num_scalar_prefetch=1, grid=(R // t,), in_specs=[pl.BlockSpec((t, C), lambda i, s: (i, 0)), ANY],
            out_specs=pl.BlockSpec((1, t, C), lambda i, s: (s[0], i, 0))),
        out_shape=jax.ShapeDtypeStruct(gathered.shape, gathered.dtype), input_output_aliases={2: 0},
        compiler_params=_cp("parallel"))(where, own, gathered)


def reduce_scatter(gstacks):
    halves = swap_halves(gstacks)
    payload, own = pair_sum(gstacks, halves)
    recv = scatter_chips(payload)
    mine = chip_sum(own, recv)
    return mine, swap_totals(mine)


def allreduce_small(vec):
    R = vec.shape[0]

    def body(in_ref, out_ref, buf, send_sems, recv_sems):
        x, y, c = lax.axis_index("x"), lax.axis_index("y"), lax.axis_index("c")
        me = 4 * x + 2 * y + c
        buf[me] = in_ref[...]
        copies = []
        for k in range(1, 8):
            peer = (x ^ (k >> 2), y ^ ((k >> 1) & 1), c ^ (k & 1))
            copies.append(pltpu.make_async_remote_copy(
                src_ref=in_ref, dst_ref=buf.at[me], send_sem=send_sems.at[k - 1], recv_sem=recv_sems.at[k - 1],
                device_id=peer, device_id_type=MESH))
        for cp in copies:
            cp.start()
        for cp in copies:
            cp.wait()
        acc = buf[0]
        for d in range(1, 8):
            acc = acc + buf[d]
        out_ref[...] = acc

    vm = pl.BlockSpec(memory_space=pltpu.VMEM)
    return _pcall(
        body, name="allreduce_small", in_specs=[vm], out_specs=vm,
        out_shape=jax.ShapeDtypeStruct((R, LANES), F32),
        scratch_shapes=[pltpu.VMEM((8, R, LANES), F32), pltpu.SemaphoreType.DMA((7,)), pltpu.SemaphoreType.DMA((7,))])(vec)


MATMUL_SHARDED = [("pool_in", 1), ("pool_group", 2), ("ssd_in", 2), ("ssd_out", 1), ("sb_qkv", 2), ("sb_out", 1),
                  ("ffn_gate", 2), ("ffn_up", 2), ("ffn_down", 1)]
STACKED = ["ffn_gate", "ffn_up", "ffn_down"]
SMALL_SHARDED = [("pool_scale", 1), ("ssd_conv_w", 2)]
REPLICATED = ["mix_norm", "ssd_conv_b", "ssd_dt_bias", "ssd_a_log", "ssd_d", "ssd_out_norm", "sb_q_norm",
              "sb_k_norm", "ffn_norm"]
WEIGHT_ORDER = ["mix_norm", "pool_in", "pool_group", "pool_scale", "ssd_in", "ssd_conv_w", "ssd_conv_b",
                "ssd_dt_bias", "ssd_a_log", "ssd_d", "ssd_out_norm", "ssd_out", "sb_qkv", "sb_q_norm", "sb_k_norm",
                "sb_out", "ffn_norm", "ffn_gate", "ffn_up", "ffn_down"]


def _piece_rows(n, mult):
    rows = -(-n // LANES)
    return -(-rows // mult) * mult


def _as_rows(a, mult):
    flat = a.reshape(-1)
    rows = _piece_rows(flat.shape[0], mult)
    if rows * LANES != flat.shape[0]:
        flat = jnp.pad(flat, (0, rows * LANES - flat.shape[0]))
    return flat.reshape(rows, LANES)


def _pack(arrs, mult=8, row_pad=8):
    parts = [_as_rows(a, mult) for a in arrs]
    rows = sum(p.shape[0] for p in parts)
    pad = -rows % row_pad
    if pad:
        parts.append(jnp.zeros((pad, LANES), parts[0].dtype))
    return jnp.concatenate(parts, axis=0)


def _unpack(packed, shapes, mult=8, lead=()):
    out, off = [], 0
    for s in shapes:
        n = math.prod(s)
        rows = _piece_rows(n, mult)
        piece = packed[..., off:off + rows, :].reshape(lead + (rows * LANES,))
        out.append(piece[..., :n].reshape(lead + tuple(s)))
        off += rows
    return out


def _rows2d(a):
    return a.reshape(-1, a.shape[-1])


def _gather_weights(shards):
    own = [_rows2d(shards[n].astype(BF16)) for n, _ in MATMUL_SHARDED]
    small = _pack([shards[n] for n, _ in SMALL_SHARDED])
    gathered = gather_all(own + [small])
    me = 2 * lax.axis_index("x") + lax.axis_index("y")

    def whole(got, mine, ax):
        return jnp.concatenate([jnp.where(me == j, mine, got[j]) for j in range(N_CHIPS)], axis=ax)

    full = {}
    for (n, ax), got, mine in zip(MATMUL_SHARDED, gathered, own):
        if n in STACKED:
            full[n] = place_own(got, mine)
        else:
            shp = shards[n].shape
            full[n] = whole(got.reshape((N_CHIPS,) + shp), mine.reshape(shp), ax)
    pieces = _unpack(gathered[-1], [shards[n].shape for n, _ in SMALL_SHARDED], lead=(N_CHIPS,))
    for (n, ax), got in zip(SMALL_SHARDED, pieces):
        full[n] = whole(got, shards[n], ax)
    return full


def _split_shards(full, axis):
    return jnp.stack(jnp.split(full, N_CHIPS, axis=axis))


def _ffn_fwd(x, gain, wg4, wu4, wd4, layer):
    h = rmsnorm_fwd(x, gain, name="ffn_norm_fwd")
    a4, b4, hid4 = ffn_up(h, wg4, wu4, layer)
    xo = ffn_down(hid4, wd4, layer, x)
    return xo, (x, h, a4, b4, hid4)


def _ffn_bwd(dout, saved, gain, wg4, wu4, wd4, layer):
    x, h, a4, b4, hid4 = saved
    da4, db4 = ffn_bwd_hidden(dout, wd4, layer, a4, b4)
    dwd4 = ffn_wgrad_out(hid4, dout)
    dwg4, dwu4 = ffn_wgrad_in(h, da4, db4)
    dh = ffn_dh(da4, db4, wg4, wu4, layer)
    dx, dgain = rmsnorm_bwd(x, gain, dh, dout, name="ffn_norm_bwd")
    return dx, dgain, dwg4, dwu4, dwd4


def _pool_layer_fwd(x, gain, w_in, wgrp, scale):
    h = rmsnorm_fwd(x, gain, name="pool_norm_fwd")
    u = linear([(h, w_in, "nn")], name="pool_in")
    xo, p = pool_fwd(u, wgrp, scale, x)
    return xo, (x, h, p)


def _pool_layer_bwd(dout, saved, gain, w_in, wgrp, scale):
    x, h, p = saved
    dp, dwgrp, dscale = pool_bwd_group(dout, p, wgrp, scale)
    du = pool_bwd_window(dp)
    (dw_in,) = wgrad(h, [du], name="pool_dwin")
    dh = linear([(du, w_in, "nt")], name="pool_dh")
    dx, dgain = rmsnorm_bwd(x, gain, dh, dout, name="pool_norm_bwd")
    return dx, dgain, dw_in, dwgrp, dscale


def _ssd_layer_fwd(x, gain, w_z, w_xbc, w_dt, conv_w, conv_b, dt_bias, a_log, d_full, out_norm, w_out):
    h = rmsnorm_fwd(x, gain, name="ssd_norm_fwd")
    z = linear([(h, w_z, "nn")], name="ssd_in_z")
    xbc = linear([(h, w_xbc, "nn")], tn=2048, name="ssd_in_xbc")
    dt_raw = linear([(h, w_dt, "nn")], name="ssd_in_dt")
    act = conv_fwd(xbc, conv_w, conv_b)
    y, states = ssd_scan_fwd(act, dt_raw, dt_bias, a_log, d_full)
    gn = gate_norm_fwd(y, z, out_norm)
    xo = linear([(gn, w_out, "nn")], res=x, name="ssd_out")
    return xo, (x, h, z, xbc, dt_raw, act, y, states, gn)


def _ssd_layer_bwd(dout, saved, gain, w_z, w_xbc, w_dt, conv_w, conv_b, dt_bias, a_log, d_full, out_norm, w_out):
    x, h, z, xbc, dt_raw, act, y, states, gn = saved
    dgn = linear([(dout, w_out, "nt")], name="ssd_dgn")
    (dw_out,) = wgrad(gn, [dout], name="ssd_dwout")
    dy, dz, dout_norm = gate_norm_bwd(dgn, y, z, out_norm)
    dact, ddt_raw, dbias, dalog, dd_full = ssd_scan_bwd(dy, act, dt_raw, dt_bias, a_log, d_full, states)
    dpre, dconv_w8, dconv_b = conv_bwd_pre(dact, xbc, conv_w, conv_b)
    dxbc = conv_bwd_input(dpre, conv_w)
    ddt_b = ddt_raw.astype(BF16)
    (dw_z,) = wgrad(h, [dz], name="ssd_dwz")
    (dw_xbc,) = wgrad(h, [dxbc], tn=2048, name="ssd_dwxbc")
    (dw_dt,) = wgrad(h, [ddt_b], name="ssd_dwdt")
    dh = linear([(dz, w_z, "nt"), (dxbc, w_xbc, "nt"), (ddt_b, w_dt, "nt")], tm=256, name="ssd_dh")
    dx, dgain = rmsnorm_bwd(x, gain, dh, dout, name="ssd_norm_bwd")
    dw_in = jnp.concatenate([dw_z, dw_xbc, dw_dt], axis=1)
    dd = dd_full.reshape(NH, HP).sum(axis=1).reshape(1, NH)
    return dx, dgain, dw_in, dconv_w8[:4], dconv_b, dbias, dalog, dd, dout_norm, dw_out


def _sb_layer_fwd(x, gain, w_qkv, qg, kg, w_out):
    h = rmsnorm_fwd(x, gain, name="sb_norm_fwd")
    qkv = linear([(h, w_qkv, "nn")], tn=1024, name="sb_qkv")
    qs, kn, v = sb_prep_fwd(qkv, qg, kg)
    o, rsave = sb_fwd(qs, kn, v)
    xo = linear([(o, w_out, "nn")], res=x, name="sb_out")
    return xo, (x, h, qkv, qs, kn, v, o, rsave)


def _sb_layer_bwd(dout, saved, gain, w_qkv, qg, kg, w_out):
    x, h, qkv, qs, kn, v, o, rsave = saved
    do = linear([(dout, w_out, "nt")], out_dtype=BF16, name="sb_do")
    (dw_out,) = wgrad(o, [dout], name="sb_dwout")
    dqs, dkn, dv = sb_bwd(qs, kn, v, do, rsave)
    dqkv, dqg, dkg = sb_prep_bwd(dqs, dkn, dv, qkv, qg, kg)
    (dw_qkv,) = wgrad(h, [dqkv], tn=1024, name="sb_dwqkv")
    dh = linear([(dqkv, w_qkv, "nt")], name="sb_dh")
    dx, dgain = rmsnorm_bwd(x, gain, dh, dout, name="sb_norm_bwd")
    dqg = dqg.reshape(SBH, SBD).sum(axis=0).reshape(1, SBD)
    dkg = dkg.reshape(SBH, SBD).sum(axis=0).reshape(1, SBD)
    return dx, dgain, dw_qkv, dqg, dkg, dw_out


def _local_step(x, target, full, rep):
    S = x.shape[0]
    d_full = jnp.repeat(rep["ssd_d"][0], HP).reshape(1, DI)
    qg = jnp.tile(rep["sb_q_norm"][0], 2).reshape(1, LANES)
    kg = jnp.tile(rep["sb_k_norm"][0], 2).reshape(1, LANES)
    ssd_in = full["ssd_in"][0]
    w_z, w_xbc, w_dt = ssd_in[:, :DI], ssd_in[:, DI:DI + CONV_CH], ssd_in[:, DI + CONV_CH:]
    conv_w = full["ssd_conv_w"][0]
    conv_b = rep["ssd_conv_b"]
    pool_scale = full["pool_scale"]

    def mixer_args(i):
        kind, j = i % 3, i // 3
        if kind == 0:
            return (full["pool_in"][j], full["pool_group"][j], pool_scale[j:j + 1])
        if kind == 1:
            return (w_z, w_xbc, w_dt, conv_w, conv_b, rep["ssd_dt_bias"], rep["ssd_a_log"], d_full,
                    rep["ssd_out_norm"], full["ssd_out"][0])
        return (full["sb_qkv"][0], qg, kg, full["sb_out"][0])

    fwd = (_pool_layer_fwd, _ssd_layer_fwd, _sb_layer_fwd)
    bwd = (_pool_layer_bwd, _ssd_layer_bwd, _sb_layer_bwd)
    saved = []
    for i in range(DEPTH):
        x, sm = fwd[i % 3](x, rep["mix_norm"][i], *mixer_args(i))
        x, sf = _ffn_fwd(x, rep["ffn_norm"][i], full["ffn_gate"], full["ffn_up"], full["ffn_down"], i)
        saved.append((sm, sf))

    colsq, dx = loss_head(x, target)
    loss = 0.5 * jnp.sum(colsq) / D

    g = {n: [None] * DEPTH for n in ("mix_norm", "ffn_norm", "ffn_gate", "ffn_up", "ffn_down")}
    g["pool_in"], g["pool_group"], g["pool_scale"] = [None] * 2, [None] * 2, [None] * 2
    for i in reversed(range(DEPTH)):
        sm, sf = saved[i]
        dx, g["ffn_norm"][i], g["ffn_gate"][i], g["ffn_up"][i], g["ffn_down"][i] = _ffn_bwd(
            dx, sf, rep["ffn_norm"][i], full["ffn_gate"], full["ffn_up"], full["ffn_down"], i)
        kind, j = i % 3, i // 3
        res = bwd[kind](dx, sm, rep["mix_norm"][i], *mixer_args(i))
        dx, g["mix_norm"][i] = res[0], res[1]
        if kind == 0:
            g["pool_in"][j], g["pool_group"][j], g["pool_scale"][j] = res[2:]
        elif kind == 1:
            dw_in, dconv_w, dconv_b, dbias, dalog, dd, don, dw_out = res[2:]
            g.update(ssd_in=dw_in[None], ssd_conv_w=dconv_w[None], ssd_conv_b=dconv_b, ssd_dt_bias=dbias,
                     ssd_a_log=dalog, ssd_d=dd, ssd_out_norm=don, ssd_out=dw_out[None])
        else:
            dw_qkv, dqg, dkg, dw_out = res[2:]
            g.update(sb_qkv=dw_qkv[None], sb_q_norm=dqg, sb_k_norm=dkg, sb_out=dw_out[None])
    for n in ("mix_norm", "ffn_norm", "pool_scale"):
        g[n] = jnp.concatenate(g[n], axis=0)
    for n in ("pool_in", "pool_group"):
        g[n] = jnp.stack(g[n])
    for n in ("ffn_gate", "ffn_up", "ffn_down"):
        g[n] = jnp.concatenate(g[n], axis=1)
    return loss, dx, g


def kernel(x, mix_norm, pool_in, pool_group, pool_scale, ssd_in, ssd_conv_w, ssd_conv_b, ssd_dt_bias, ssd_a_log, ssd_d, ssd_out_norm, ssd_out, sb_qkv, sb_q_norm, sb_k_norm, sb_out, ffn_norm, ffn_gate, ffn_up, ffn_down, loss_target, m_mix_norm, m_pool_in, m_pool_group, m_pool_scale, m_ssd_in, m_ssd_conv_w, m_ssd_conv_b, m_ssd_dt_bias, m_ssd_a_log, m_ssd_d, m_ssd_out_norm, m_ssd_out, m_sb_qkv, m_sb_q_norm, m_sb_k_norm, m_sb_out, m_ffn_norm, m_ffn_gate, m_ffn_up, m_ffn_down, v_mix_norm, v_pool_in, v_pool_group, v_pool_scale, v_ssd_in, v_ssd_conv_w, v_ssd_conv_b, v_ssd_dt_bias, v_ssd_a_log, v_ssd_d, v_ssd_out_norm, v_ssd_out, v_sb_qkv, v_sb_q_norm, v_sb_k_norm, v_sb_out, v_ffn_norm, v_ffn_gate, v_ffn_up, v_ffn_down):
    given = dict(locals())
    w = {n: given[n] for n in WEIGHT_ORDER}
    m = {n: given["m_" + n] for n in WEIGHT_ORDER}
    v = {n: given["v_" + n] for n in WEIGHT_ORDER}
    full = _gather_weights(w)
    rep = {n: w[n] for n in REPLICATED}

    loss, dx, g = _local_step(x[0], loss_target[0], full, rep)
    loss = lax.psum(loss, ("x", "y", "c"))
    out = {}

    gstacks = [g[n] if n in STACKED else _split_shards(g[n], ax).reshape((N_CHIPS,) + _rows2d(w[n]).shape)
               for n, ax in MATMUL_SHARDED]
    mine, other = reduce_scatter(gstacks)
    for (n, _), g_mine, g_other in zip(MATMUL_SHARDED, mine, other):
        res = adamw_halves(_rows2d(w[n]), g_mine, g_other, _rows2d(m[n]), _rows2d(v[n]), name="adamw_" + n)
        for key, a in zip("gdmv", res):
            out[key, n] = a.reshape(w[n].shape)

    small = REPLICATED + [n for n, _ in SMALL_SHARDED]
    gfull = _unpack(allreduce_small(_pack([g[n] for n in small])), [g[n].shape for n in small])
    me = 2 * lax.axis_index("x") + lax.axis_index("y")
    gsum = dict(zip(small, gfull))
    for n, ax in SMALL_SHARDED:
        gsum[n] = lax.dynamic_slice_in_dim(gsum[n], me * w[n].shape[ax], w[n].shape[ax], axis=ax)
    res = adamw(*[_pack([t[n] for n in small]) for t in (w, gsum, m, v)], name="adamw_small")
    for key, flat in zip("dmv", res):
        for n, a in zip(small, _unpack(flat, [w[n].shape for n in small])):
            out[key, n] = a
    for n in small:
        out["g", n] = gsum[n]

    return (loss, dx[None], *[out["g", n] for n in WEIGHT_ORDER], *[out["d", n] for n in WEIGHT_ORDER],
            *[out["m", n] for n in WEIGHT_ORDER], *[out["v", n] for n in WEIGHT_ORDER])
```

```python
import math

import jax
import jax.numpy as jnp
from jax import lax
from jax.experimental import pallas as pl
from jax.experimental.pallas import tpu as pltpu

F32 = jnp.float32
BF16 = jnp.bfloat16

D = 1024
DEPTH = 4
EPS = 1e-6
POOL_WINDOWS = (2, 4, 8, 16)
PG = 256
DI = 2048
NH = 32
HP = 64
NG = 8
NS = 128
GW = 256
CH = 256
CONV_CH = 4096
SSD_IN = 6176
SBH = 16
SBD = 64
FH = 2816
N_CHIPS = 4
LANES = 128

ADAM_LR = 0.001
ADAM_B1 = 0.9
ADAM_B2 = 0.999
ADAM_EPS = 1e-08
ADAM_WD = 0.01
ADAM_STEP = 10

VMEM_LIMIT = 56 * 1024 * 1024


def _pcall(body, **kw):
    return pl.pallas_call(body, **kw)


def _cp(*sem):
    return pltpu.CompilerParams(dimension_semantics=sem, vmem_limit_bytes=VMEM_LIMIT)


def _dot(a, b, prec=None):
    return lax.dot_general(a, b, (((1,), (0,)), ((), ())), precision=prec, preferred_element_type=F32)


def _dot_nt(a, b, prec=None):
    return lax.dot_general(a, b, (((1,), (1,)), ((), ())), precision=prec, preferred_element_type=F32)


def _dot_tn(a, b, prec=None):
    return lax.dot_general(a, b, (((0,), (0,)), ((), ())), precision=prec, preferred_element_type=F32)


def _split3(x):
    x1 = x.astype(BF16)
    r = x - x1.astype(F32)
    x2 = r.astype(BF16)
    return x1, x2, (r - x2.astype(F32)).astype(BF16)


def _sel(dot, x, mask, x_first=True):
    mb = mask.astype(BF16)
    p = [dot(xi, mb) if x_first else dot(mb, xi) for xi in _split3(x)]
    return (p[0] + p[1]) + p[2]


def _sigmoid(x):
    return 1.0 / (1.0 + jnp.exp(-x))


def _iota(shape, axis):
    return lax.broadcasted_iota(jnp.int32, shape, axis)


def linear(pairs, res=None, out_dtype=F32, tm=512, tn=None, name="linear"):
    M = pairs[0][0].shape[0]
    N = pairs[0][1].shape[1] if pairs[0][2] == "nn" else pairs[0][1].shape[0]
    tm = min(tm, M)
    tn = N if tn is None else min(tn, N)
    n_pairs = len(pairs)
    modes = [p[2] for p in pairs]

    def body(*refs):
        acc = None
        for k in range(n_pairs):
            a = refs[2 * k][...].astype(BF16)
            w = refs[2 * k + 1][...]
            t = _dot(a, w) if modes[k] == "nn" else _dot_nt(a, w)
            acc = t if acc is None else acc + t
        if res is not None:
            acc = acc + refs[2 * n_pairs][...]
        refs[-1][...] = acc.astype(out_dtype)

    in_specs, args = [], []
    for a, w, mode in pairs:
        K = a.shape[1]
        in_specs.append(pl.BlockSpec((tm, K), lambda j, i: (i, 0)))
        if mode == "nn":
            in_specs.append(pl.BlockSpec((K, tn), lambda j, i: (0, j)))
        else:
            in_specs.append(pl.BlockSpec((tn, K), lambda j, i: (j, 0)))
        args += [a, w]
    if res is not None:
        in_specs.append(pl.BlockSpec((tm, tn), lambda j, i: (i, j)))
        args.append(res)
    return _pcall(
        body, name=name, grid=(N // tn, M // tm), in_specs=in_specs,
        out_specs=pl.BlockSpec((tm, tn), lambda j, i: (i, j)),
        out_shape=jax.ShapeDtypeStruct((M, N), out_dtype),
        compiler_params=_cp("parallel", "arbitrary"))(*args)


def wgrad(a, gs, tk=1024, tn=None, tm=1024, name="wgrad"):
    M, Ka = a.shape
    N = gs[0].shape[1]
    tk, tm = min(tk, Ka), min(tm, M)
    tn = N if tn is None else min(tn, N)
    n_g = len(gs)

    def body(*refs):
        a_ref, g_refs, o_refs = refs[0], refs[1:1 + n_g], refs[1 + n_g:]
        m = pl.program_id(2)
        at = a_ref[...].astype(BF16)
        for g_ref, o_ref in zip(g_refs, o_refs):
            t = _dot_tn(at, g_ref[...].astype(BF16))

            @pl.when(m == 0)
            def _():
                o_ref[...] = t

            @pl.when(m > 0)
            def _():
                o_ref[...] += t

    out = _pcall(
        body, name=name, grid=(Ka // tk, N // tn, M // tm),
        in_specs=[pl.BlockSpec((tm, tk), lambda k, j, m: (m, k))]
        + [pl.BlockSpec((tm, tn), lambda k, j, m: (m, j))] * n_g,
        out_specs=[pl.BlockSpec((tk, tn), lambda k, j, m: (k, j))] * n_g,
        out_shape=[jax.ShapeDtypeStruct((Ka, N), F32)] * n_g,
        compiler_params=_cp("parallel", "parallel", "arbitrary"))(a, *gs)
    return out


def rmsnorm_fwd(x, gain, tm=512, name="rmsnorm_fwd"):
    S, Dm = x.shape
    tm = min(tm, S)

    def body(x_ref, g_ref, o_ref):
        xv = x_ref[...]
        r = lax.rsqrt(jnp.mean(xv * xv, axis=-1, keepdims=True) + EPS)
        o_ref[...] = (xv * r * g_ref[...]).astype(BF16)

    return _pcall(
        body, name=name, grid=(S // tm,),
        in_specs=[pl.BlockSpec((tm, Dm), lambda i: (i, 0)), pl.BlockSpec((1, Dm), lambda i: (0, 0))],
        out_specs=pl.BlockSpec((tm, Dm), lambda i: (i, 0)),
        out_shape=jax.ShapeDtypeStruct((S, Dm), BF16),
        compiler_params=_cp("parallel"))(x, gain.reshape(1, Dm))


def rmsnorm_bwd(x, gain, dh, dres, tm=512, name="rmsnorm_bwd"):
    S, Dm = x.shape
    tm = min(tm, S)

    def body(x_ref, g_ref, dh_ref, dr_ref, dx_ref, dg_ref):
        i = pl.program_id(0)
        xv = x_ref[...]
        r = lax.rsqrt(jnp.mean(xv * xv, axis=-1, keepdims=True) + EPS)
        y = xv * r
        dhv = dh_ref[...]
        dy = dhv * g_ref[...]
        dx_ref[...] = dr_ref[...] + r * (dy - y * jnp.mean(dy * y, axis=-1, keepdims=True))
        part = jnp.sum(dhv * y, axis=0, keepdims=True)

        @pl.when(i == 0)
        def _():
            dg_ref[...] = part

        @pl.when(i > 0)
        def _():
            dg_ref[...] += part

    return _pcall(
        body, name=name, grid=(S // tm,),
        in_specs=[pl.BlockSpec((tm, Dm), lambda i: (i, 0)), pl.BlockSpec((1, Dm), lambda i: (0, 0)),
                  pl.BlockSpec((tm, Dm), lambda i: (i, 0)), pl.BlockSpec((tm, Dm), lambda i: (i, 0))],
        out_specs=[pl.BlockSpec((tm, Dm), lambda i: (i, 0)), pl.BlockSpec((1, Dm), lambda i: (0, 0))],
        out_shape=[jax.ShapeDtypeStruct((S, Dm), F32), jax.ShapeDtypeStruct((1, Dm), F32)],
        compiler_params=_cp("arbitrary"))(x, gain.reshape(1, Dm), dh, dres)


FS = FH // N_CHIPS


def ffn_up(h, wg4, wu4, layer, tm=1024):
    S = h.shape[0]
    tm = min(tm, S)

    def body(h_ref, wg_ref, wu_ref, a_ref, b_ref, hid_ref):
        hv = h_ref[...]
        a = _dot(hv, wg_ref[0])
        b = _dot(hv, wu_ref[0])
        a_ref[0] = a.astype(BF16)
        b_ref[0] = b.astype(BF16)
        hid_ref[0] = (a * _sigmoid(a) * b).astype(BF16)

    wspec = pl.BlockSpec((1, D, FS), lambda j, i: (j, layer, 0))
    aspec = pl.BlockSpec((1, tm, FS), lambda j, i: (j, i, 0))
    return _pcall(
        body, name="ffn_up", grid=(N_CHIPS, S // tm),
        in_specs=[pl.BlockSpec((tm, D), lambda j, i: (i, 0)), wspec, wspec], out_specs=[aspec] * 3,
        out_shape=[jax.ShapeDtypeStruct((N_CHIPS, S, FS), BF16)] * 3,
        compiler_params=_cp("parallel", "arbitrary"))(h, wg4, wu4)


def ffn_down(hid4, wd4, layer, x, tm=1024):
    S = x.shape[0]
    tm = min(tm, S)

    def body(hid_ref, wd_ref, x_ref, o_ref):
        acc = x_ref[...]
        for j in range(N_CHIPS):
            acc = acc + _dot(hid_ref[j], wd_ref[j])
        o_ref[...] = acc

    return _pcall(
        body, name="ffn_down", grid=(S // tm,),
        in_specs=[pl.BlockSpec((N_CHIPS, tm, FS), lambda i: (0, i, 0)),
                  pl.BlockSpec((N_CHIPS, FS, D), lambda i: (0, layer, 0)), pl.BlockSpec((tm, D), lambda i: (i, 0))],
        out_specs=pl.BlockSpec((tm, D), lambda i: (i, 0)),
        out_shape=jax.ShapeDtypeStruct((S, D), F32), compiler_params=_cp("parallel"))(hid4, wd4, x)


def ffn_bwd_hidden(dout, wd4, layer, a4, b4, tm=1024):
    S = dout.shape[0]
    tm = min(tm, S)

    def body(do_ref, wd_ref, a_ref, b_ref, da_ref, db_ref):
        dhid = _dot_nt(do_ref[...].astype(BF16), wd_ref[0])
        av, bv = a_ref[0].astype(F32), b_ref[0].astype(F32)
        s = _sigmoid(av)
        da_ref[0] = (dhid * bv * (s * (1.0 + av * (1.0 - s)))).astype(BF16)
        db_ref[0] = (dhid * (av * s)).astype(BF16)

    aspec = pl.BlockSpec((1, tm, FS), lambda i, j: (j, i, 0))
    return _pcall(
        body, name="ffn_bwd_hidden", grid=(S // tm, N_CHIPS),
        in_specs=[pl.BlockSpec((tm, D), lambda i, j: (i, 0)), pl.BlockSpec((1, FS, D), lambda i, j: (j, layer, 0)),
                  aspec, aspec],
        out_specs=[aspec] * 2, out_shape=[jax.ShapeDtypeStruct((N_CHIPS, S, FS), BF16)] * 2,
        compiler_params=_cp("parallel", "arbitrary"))(dout, wd4, a4, b4)


def ffn_wgrad_in(h, da4, db4, layer, bufs, tm=2048):
    S = h.shape[0]
    tm = min(tm, S)

    def body(h_ref, da_ref, db_ref, *rest):
        dg_ref, du_ref = rest[-2:]
        m = pl.program_id(1)
        hv = h_ref[...]
        for g_ref, o_ref in ((da_ref, dg_ref), (db_ref, du_ref)):
            t = _dot_tn(hv, g_ref[0])

            @pl.when(m == 0)
            def _():
                o_ref[0] = t

            @pl.when(m > 0)
            def _():
                o_ref[0] += t

    aspec = pl.BlockSpec((1, tm, FS), lambda j, m: (j, m, 0))
    ospec = pl.BlockSpec((1, D, FS), lambda j, m: (j, layer, 0))
    kept = {} if bufs is None else dict(input_output_aliases={3: 0, 4: 1})
    return _pcall(
        body, name="ffn_wgrad_in", grid=(N_CHIPS, S // tm),
        in_specs=[pl.BlockSpec((tm, D), lambda j, m: (m, 0)), aspec, aspec] + ([] if bufs is None else [ANY, ANY]),
        out_specs=[ospec] * 2, out_shape=[jax.ShapeDtypeStruct((N_CHIPS, DEPTH * D, FS), F32)] * 2,
        compiler_params=_cp("parallel", "arbitrary"), **kept)(h, da4, db4, *(bufs or ()))


def ffn_wgrad_out(hid4, dout, layer, buf, tm=2048):
    S = dout.shape[0]
    tm = min(tm, S)

    def body(hid_ref, do_ref, *rest):
        o_ref = rest[-1]
        m = pl.program_id(1)
        t = _dot_tn(hid_ref[0], do_ref[...].astype(BF16))

        @pl.when(m == 0)
        def _():
            o_ref[0] = t

        @pl.when(m > 0)
        def _():
            o_ref[0] += t

    kept = {} if buf is None else dict(input_output_aliases={2: 0})
    return _pcall(
        body, name="ffn_wgrad_out", grid=(N_CHIPS, S // tm),
        in_specs=[pl.BlockSpec((1, tm, FS), lambda j, m: (j, m, 0)), pl.BlockSpec((tm, D), lambda j, m: (m, 0))]
        + ([] if buf is None else [ANY]),
        out_specs=pl.BlockSpec((1, FS, D), lambda j, m: (j, layer, 0)),
        out_shape=jax.ShapeDtypeStruct((N_CHIPS, DEPTH * FS, D), F32),
        compiler_params=_cp("parallel", "arbitrary"), **kept)(hid4, dout, *(() if buf is None else (buf,)))


def ffn_dh(da4, db4, wg4, wu4, layer, tm=512):
    S = da4.shape[1]
    tm = min(tm, S)

    def body(da_ref, db_ref, wg_ref, wu_ref, o_ref):
        acc = _dot_nt(da_ref[0], wg_ref[0]) + _dot_nt(db_ref[0], wu_ref[0])
        for j in range(1, N_CHIPS):
            acc = acc + _dot_nt(da_ref[j], wg_ref[j]) + _dot_nt(db_ref[j], wu_ref[j])
        o_ref[...] = acc

    aspec = pl.BlockSpec((N_CHIPS, tm, FS), lambda i: (0, i, 0))
    wspec = pl.BlockSpec((N_CHIPS, D, FS), lambda i: (0, layer, 0))
    return _pcall(
        body, name="ffn_dh", grid=(S // tm,), in_specs=[aspec, aspec, wspec, wspec],
        out_specs=pl.BlockSpec((tm, D), lambda i: (i, 0)),
        out_shape=jax.ShapeDtypeStruct((S, D), F32), compiler_params=_cp("parallel"))(da4, db4, wg4, wu4)


POOL_T = 128
POOL_HALO = 16


def pool_fwd(u, wgrp, scale, x_res):
    S = u.shape[0]
    T, HB = min(POOL_T, S), POOL_HALO
    per = T // HB

    def body(u_ref, tail_ref, wg_ref, sc_ref, x_ref, xo_ref, p_ref):
        i = pl.program_id(0)
        uc = u_ref[...]
        tail = jnp.where(i > 0, tail_ref[...], 0.0)
        d_cur = _iota((T, T), 0) - _iota((T, T), 1)
        d_tail = _iota((T, HB), 0) - _iota((T, HB), 1) + HB
        tg = i * T + _iota((T, 1), 0)
        for g, w in enumerate(POOL_WINDOWS):
            gs = slice(g * PG, (g + 1) * PG)
            band = (d_cur >= 0) & (d_cur < w)
            band_t = (d_tail >= 0) & (d_tail < w)
            ug = uc[:, gs]
            ws = _sel(_dot, ug, band, False) + _sel(_dot, tail[:, gs], band_t, False)
            cnt = jnp.minimum(tg + 1, w).astype(F32)
            pb = (ws / cnt - ug).astype(BF16)
            p_ref[:, gs] = pb
            xo_ref[:, gs] = x_ref[:, gs] + _dot(pb, wg_ref[g]) * sc_ref[:, gs]

    return _pcall(
        body, name="pool_fwd", grid=(S // T,),
        in_specs=[pl.BlockSpec((T, D), lambda i: (i, 0)),
                  pl.BlockSpec((HB, D), lambda i: (jnp.maximum(i * per - 1, 0), 0)),
                  pl.BlockSpec((4, PG, PG), lambda i: (0, 0, 0)), pl.BlockSpec((1, D), lambda i: (0, 0)),
                  pl.BlockSpec((T, D), lambda i: (i, 0))],
        out_specs=[pl.BlockSpec((T, D), lambda i: (i, 0))] * 2,
        out_shape=[jax.ShapeDtypeStruct((S, D), F32), jax.ShapeDtypeStruct((S, D), BF16)],
        compiler_params=_cp("parallel"))(u, u, wgrp, scale, x_res)


def pool_bwd_group(dm, p, wgrp, scale, tm=512):
    S = dm.shape[0]
    tm = min(tm, S)

    def body(dm_ref, p_ref, wg_ref, sc_ref, dp_ref, dwg_ref, dsc_ref):
        i = pl.program_id(0)

        @pl.when(i == 0)
        def _():
            dwg_ref[...] = jnp.zeros_like(dwg_ref)
            dsc_ref[...] = jnp.zeros_like(dsc_ref)

        for g in range(4):
            gs = slice(g * PG, (g + 1) * PG)
            dmg, pg, wg = dm_ref[:, gs], p_ref[:, gs], wg_ref[g]
            dsc_ref[:, gs] += jnp.sum(dmg * _dot(pg, wg), axis=0, keepdims=True)
            dy = (dmg * sc_ref[:, gs]).astype(BF16)
            dp_ref[:, gs] = _dot_nt(dy, wg)
            dwg_ref[g] += _dot_tn(pg, dy)

    return _pcall(
        body, name="pool_bwd_group", grid=(S // tm,),
        in_specs=[pl.BlockSpec((tm, D), lambda i: (i, 0)), pl.BlockSpec((tm, D), lambda i: (i, 0)),
                  pl.BlockSpec((4, PG, PG), lambda i: (0, 0, 0)), pl.BlockSpec((1, D), lambda i: (0, 0))],
        out_specs=[pl.BlockSpec((tm, D), lambda i: (i, 0)), pl.BlockSpec((4, PG, PG), lambda i: (0, 0, 0)),
                   pl.BlockSpec((1, D), lambda i: (0, 0))],
        out_shape=[jax.ShapeDtypeStruct((S, D), F32), jax.ShapeDtypeStruct((4, PG, PG), F32),
                   jax.ShapeDtypeStruct((1, D), F32)],
        compiler_params=_cp("arbitrary"))(dm, p, wgrp, scale)


def pool_bwd_window(dp):
    S = dp.shape[0]
    T, HB = min(POOL_T, S), POOL_HALO
    per = T // HB
    nt = S // T

    def body(dp_ref, nxt_ref, du_ref):
        i = pl.program_id(0)
        dc = dp_ref[...]
        nxt = jnp.where(i < nt - 1, nxt_ref[...], 0.0)
        d_cur = _iota((T, T), 1) - _iota((T, T), 0)
        d_nxt = _iota((T, HB), 1) - _iota((T, HB), 0) + T
        tg = i * T + _iota((T, 1), 0)
        tn_ = (i + 1) * T + _iota((HB, 1), 0)
        for g, w in enumerate(POOL_WINDOWS):
            gs = slice(g * PG, (g + 1) * PG)
            band = (d_cur >= 0) & (d_cur < w)
            band_n = (d_nxt >= 0) & (d_nxt < w)
            dcg = dc[:, gs]
            cur = dcg / jnp.minimum(tg + 1, w).astype(F32)
            nx = nxt[:, gs] / jnp.minimum(tn_ + 1, w).astype(F32)
            du_ref[:, gs] = (_sel(_dot, cur, band, False) + _sel(_dot, nx, band_n, False) - dcg).astype(BF16)

    return _pcall(
        body, name="pool_bwd_window", grid=(nt,),
        in_specs=[pl.BlockSpec((T, D), lambda i: (i, 0)),
                  pl.BlockSpec((HB, D), lambda i: (jnp.minimum((i + 1) * per, S // HB - 1), 0))],
        out_specs=pl.BlockSpec((T, D), lambda i: (i, 0)),
        out_shape=jax.ShapeDtypeStruct((S, D), BF16),
        compiler_params=_cp("parallel"))(dp, dp)


CONV_T = 256


def _shift_down(xc, prev8, j):
    if j == 0:
        return xc
    T = xc.shape[0]
    body = pltpu.roll(xc, j, 0)
    first = jnp.where(_iota((8, 1), 0) < j, pltpu.roll(prev8, j, 0), body[0:8])
    return jnp.concatenate([first, body[8:T]], axis=0)


def _shift_up(dc, next8, j):
    if j == 0:
        return dc
    T = dc.shape[0]
    body = pltpu.roll(dc, T - j, 0)
    last = jnp.where(_iota((8, 1), 0) + j < 8, body[T - 8:T], pltpu.roll(next8, 8 - j, 0))
    return jnp.concatenate([body[0:T - 8], last], axis=0)


def conv_fwd(xbc, conv_w, conv_b):
    S = xbc.shape[0]
    T = min(CONV_T, S)
    CB = 1024

    def body(x_ref, prev_ref, w_ref, b_ref, o_ref):
        i = pl.program_id(1)
        xc = x_ref[...]
        prev8 = jnp.where(i > 0, prev_ref[...], 0.0)
        pre = b_ref[...] + w_ref[3:4, :] * xc
        for j in range(1, 4):
            pre = pre + w_ref[3 - j:4 - j, :] * _shift_down(xc, prev8, j)
        o_ref[...] = pre * _sigmoid(pre)

    return _pcall(
        body, name="conv_fwd", grid=(CONV_CH // CB, S // T),
        in_specs=[pl.BlockSpec((T, CB), lambda c, i: (i, c)),
                  pl.BlockSpec((8, CB), lambda c, i: (jnp.maximum(i * (T // 8) - 1, 0), c)),
                  pl.BlockSpec((4, CB), lambda c, i: (0, c)), pl.BlockSpec((1, CB), lambda c, i: (0, c))],
        out_specs=pl.BlockSpec((T, CB), lambda c, i: (i, c)),
        out_shape=jax.ShapeDtypeStruct((S, CONV_CH), F32),
        compiler_params=_cp("parallel", "parallel"))(xbc, xbc, conv_w, conv_b)


def conv_bwd_pre(dact, xbc, conv_w, conv_b):
    S = xbc.shape[0]
    T = min(CONV_T, S)
    CB = 1024

    def body(da_ref, x_ref, prev_ref, w_ref, b_ref, dpre_ref, dw_ref, db_ref):
        i = pl.program_id(1)
        xc = x_ref[...]
        prev8 = jnp.where(i > 0, prev_ref[...], 0.0)
        sh = [_shift_down(xc, prev8, j) for j in range(4)]
        pre = b_ref[...] + w_ref[3:4, :] * sh[0]
        for j in range(1, 4):
            pre = pre + w_ref[3 - j:4 - j, :] * sh[j]
        s = _sigmoid(pre)
        dpre = da_ref[...] * (s * (1.0 + pre * (1.0 - s)))
        dpre_ref[...] = dpre
        rows = [jnp.sum(dpre * sh[3 - k], axis=0, keepdims=True) for k in range(4)]
        dw = jnp.concatenate(rows + [jnp.zeros((4, CB), F32)], axis=0)
        db = jnp.sum(dpre, axis=0, keepdims=True)

        @pl.when(i == 0)
        def _():
            dw_ref[...] = dw
            db_ref[...] = db

        @pl.when(i > 0)
        def _():
            dw_ref[...] += dw
            db_ref[...] += db

    return _pcall(
        body, name="conv_bwd_pre", grid=(CONV_CH // CB, S // T),
        in_specs=[pl.BlockSpec((T, CB), lambda c, i: (i, c)), pl.BlockSpec((T, CB), lambda c, i: (i, c)),
                  pl.BlockSpec((8, CB), lambda c, i: (jnp.maximum(i * (T // 8) - 1, 0), c)),
                  pl.BlockSpec((4, CB), lambda c, i: (0, c)), pl.BlockSpec((1, CB), lambda c, i: (0, c))],
        out_specs=[pl.BlockSpec((T, CB), lambda c, i: (i, c)), pl.BlockSpec((8, CB), lambda c, i: (0, c)),
                   pl.BlockSpec((1, CB), lambda c, i: (0, c))],
        out_shape=[jax.ShapeDtypeStruct((S, CONV_CH), F32), jax.ShapeDtypeStruct((8, CONV_CH), F32),
                   jax.ShapeDtypeStruct((1, CONV_CH), F32)],
        compiler_params=_cp("parallel", "arbitrary"))(dact, xbc, xbc, conv_w, conv_b)


def conv_bwd_input(dpre, conv_w):
    S = dpre.shape[0]
    T = min(CONV_T, S)
    CB = 1024
    nt = S // T

    def body(d_ref, nxt_ref, w_ref, o_ref):
        i = pl.program_id(1)
        dc = d_ref[...]
        next8 = jnp.where(i < nt - 1, nxt_ref[...], 0.0)
        acc = w_ref[3:4, :] * dc
        for j in range(1, 4):
            acc = acc + w_ref[3 - j:4 - j, :] * _shift_up(dc, next8, j)
        o_ref[...] = acc.astype(BF16)

    return _pcall(
        body, name="conv_bwd_input", grid=(CONV_CH // CB, nt),
        in_specs=[pl.BlockSpec((T, CB), lambda c, i: (i, c)),
                  pl.BlockSpec((8, CB), lambda c, i: (jnp.minimum((i + 1) * (T // 8), S // 8 - 1), c)),
                  pl.BlockSpec((4, CB), lambda c, i: (0, c))],
        out_specs=pl.BlockSpec((T, CB), lambda c, i: (i, c)),
        out_shape=jax.ShapeDtypeStruct((S, CONV_CH), BF16),
        compiler_params=_cp("parallel", "parallel"))(dpre, dpre, conv_w)


def _ssd_chunk_terms(dt_ref, bias_ref, alog_ref):
    L = CH
    dtp = dt_ref[...] + bias_ref[...]
    dt = jnp.maximum(dtp, 0.0) + jnp.log(1.0 + jnp.exp(-jnp.abs(dtp)))
    a = -jnp.exp(alog_ref[...])
    da = dt * a
    tri = _iota((L, L), 0) >= _iota((L, L), 1)
    acum = _sel(_dot, da, tri, False)
    triu = _iota((L, L), 0) <= _iota((L, L), 1)
    acum_row = _sel(_dot_tn, da, triu)
    expand = _iota((NH, DI), 1) // HP == _iota((NH, DI), 0)
    acum_full = _sel(_dot, acum, expand)
    e_full = jnp.exp(acum_full)
    w_full = jnp.exp(acum_full[L - 1:L, :] - acum_full)
    dt_full = _sel(_dot, dt, expand)
    return dtp, dt, a, acum, acum_row, expand, triu, e_full, w_full, dt_full


def ssd_scan_fwd(xbc_act, dt_raw, dt_bias, a_log, d_full):
    S = xbc_act.shape[0]
    L = CH
    nc = S // L

    def body(xs_ref, b_ref, c_ref, dt_ref, bias_ref, alog_ref, d_ref, y_ref, st_ref, state):
        c = pl.program_id(0)

        @pl.when(c == 0)
        def _():
            state[...] = jnp.zeros_like(state)

        st_ref[0] = state[...]
        _, _, _, acum, acum_row, _, _, e_full, w_full, dt_full = _ssd_chunk_terms(dt_ref, bias_ref, alog_ref)
        causal = _iota((L, L), 0) >= _iota((L, L), 1)
        lane_head = _iota((1, GW), 1) // HP
        for g in range(NG):
            gs = slice(g * GW, (g + 1) * GW)
            ns = slice(g * NS, (g + 1) * NS)
            xs_g = xs_ref[:, gs]
            xdt_g = xs_g * dt_full[:, gs]
            cg = c_ref[:, ns].astype(BF16)
            bg = b_ref[:, ns].astype(BF16)
            gmat = _dot_nt(cg, bg)
            yg = jnp.zeros((L, GW), F32)
            for hh in range(4):
                h = 4 * g + hh
                diff = acum[:, h:h + 1] - acum_row[h:h + 1, :]
                dk = jnp.exp(jnp.where(causal, diff, -1e30))
                xm = jnp.where(lane_head == hh, xdt_g, 0.0).astype(BF16)
                yg = yg + _dot((gmat * dk).astype(BF16), xm)
            sg = state[g]
            yoff = _dot(cg, sg.astype(BF16)) * e_full[:, gs]
            y_ref[:, gs] = yg + yoff + d_ref[:, gs] * xs_g
            state[g] = sg * e_full[L - 1:L, gs] + _dot_tn(bg, (w_full[:, gs] * xdt_g).astype(BF16))

    return _pcall(
        body, name="ssd_scan_fwd", grid=(nc,),
        in_specs=[pl.BlockSpec((L, DI), lambda c: (c, 0)), pl.BlockSpec((L, 1024), lambda c: (c, 2)),
                  pl.BlockSpec((L, 1024), lambda c: (c, 3)), pl.BlockSpec((L, NH), lambda c: (c, 0)),
                  pl.BlockSpec((1, NH), lambda c: (0, 0)), pl.BlockSpec((1, NH), lambda c: (0, 0)),
                  pl.BlockSpec((1, DI), lambda c: (0, 0))],
        out_specs=[pl.BlockSpec((L, DI), lambda c: (c, 0)), pl.BlockSpec((1, NG, NS, GW), lambda c: (c, 0, 0, 0))],
        out_shape=[jax.ShapeDtypeStruct((S, DI), F32), jax.ShapeDtypeStruct((nc, NG, NS, GW), F32)],
        scratch_shapes=[pltpu.VMEM((NG, NS, GW), F32)],
        compiler_params=_cp("arbitrary"))(xbc_act, xbc_act, xbc_act, dt_raw, dt_bias, a_log, d_full)


def ssd_scan_bwd(dy, xbc_act, dt_raw, dt_bias, a_log, d_full, states):
    S = xbc_act.shape[0]
    L = CH
    nc = S // L

    def body(dy_ref, xs_ref, b_ref, c_ref, dt_ref, bias_ref, alog_ref, d_ref, st_ref,
             dxbc_ref, ddt_ref, dbias_ref, dalog_ref, dd_ref, dstate):
        c = pl.program_id(0)

        @pl.when(c == 0)
        def _():
            dstate[...] = jnp.zeros_like(dstate)
            dbias_ref[...] = jnp.zeros_like(dbias_ref)
            dalog_ref[...] = jnp.zeros_like(dalog_ref)
            dd_ref[...] = jnp.zeros_like(dd_ref)

        dtp, dt, a, acum, acum_row, expand, triu, e_full, w_full, dt_full = _ssd_chunk_terms(
            dt_ref, bias_ref, alog_ref)
        causal = _iota((L, L), 0) >= _iota((L, L), 1)
        lane_head = _iota((1, GW), 1) // HP
        head_id = _iota((1, NH), 1)
        head_row = _iota((NH, 1), 0)
        dacum = jnp.zeros((L, NH), F32)
        dacum_t = jnp.zeros((NH, L), F32)
        red_parts = []
        dxdt_parts = []
        alast_parts = []
        for g in range(NG):
            gs = slice(g * GW, (g + 1) * GW)
            ns = slice(g * NS, (g + 1) * NS)
            xs_g = xs_ref[:, gs]
            xdt_g = xs_g * dt_full[:, gs]
            dy_g = dy_ref[:, gs]
            cg = c_ref[:, ns].astype(BF16)
            bg = b_ref[:, ns].astype(BF16)
            gmat = _dot_nt(cg, bg)
            sg = st_ref[0, g]
            dsg = dstate[g]
            sgb, dsgb = sg.astype(BF16), dsg.astype(BF16)
            cs = _dot(cg, sgb)
            bds = _dot(bg, dsgb)
            e_g, w_g = e_full[:, gs], w_full[:, gs]
            dxdt = w_g * bds
            dgsum = jnp.zeros((L, L), F32)
            for hh in range(4):
                h = 4 * g + hh
                hm = lane_head == hh
                diff = acum[:, h:h + 1] - acum_row[h:h + 1, :]
                dk = jnp.exp(jnp.where(causal, diff, -1e30))
                m = gmat * dk
                dym = jnp.where(hm, dy_g, 0.0).astype(BF16)
                xm = jnp.where(hm, xdt_g, 0.0).astype(BF16)
                dm = _dot_nt(dym, xm)
                dxdt = dxdt + _dot_tn(m.astype(BF16), dym)
                dgsum = dgsum + dm * dk
                em = dm * m
                dacum = dacum + jnp.sum(em, axis=1, keepdims=True) * (head_id == h).astype(F32)
                dacum_t = dacum_t + (head_row == h).astype(F32) * jnp.sum(em, axis=0, keepdims=True)
            dgb = dgsum.astype(BF16)
            edy = (e_g * dy_g).astype(BF16)
            wx = (w_g * xdt_g).astype(BF16)
            dc_g = _dot(dgb, bg) + _dot_nt(edy, sgb)
            db_g = _dot_tn(dgb, cg) + _dot_nt(wx, dsgb)
            dxbc_ref[:, DI + g * NS:DI + (g + 1) * NS] = db_g
            dxbc_ref[:, DI + 1024 + g * NS:DI + 1024 + (g + 1) * NS] = dc_g
            p2w = bds * xdt_g * w_g
            red_parts.append(dy_g * cs * e_g - p2w)
            alast_parts.append(jnp.sum(p2w, axis=0, keepdims=True)
                               + e_full[L - 1:L, gs] * jnp.sum(dsg * sg, axis=0, keepdims=True))
            dxdt_parts.append(dxdt)
            dstate[g] = e_full[L - 1:L, gs] * dsg + _dot_tn(cg, edy)
            dxbc_ref[:, gs] = dxdt * dt_full[:, gs] + dy_g * d_ref[:, gs]
            dd_ref[:, gs] += jnp.sum(dy_g * xs_g, axis=0, keepdims=True)
        red = jnp.concatenate(red_parts, axis=1)
        dxdt_all = jnp.concatenate(dxdt_parts, axis=1)
        alast = jnp.concatenate(alast_parts, axis=1)
        eye = _iota((NH, NH), 0) == _iota((NH, NH), 1)
        dacum = dacum - _sel(_dot_tn, dacum_t, eye) + _sel(_dot_nt, red, expand)
        dalast = _sel(_dot_nt, jnp.broadcast_to(alast, (8, DI)), expand)[0:1, :]
        dacum = dacum + jnp.where(_iota((L, 1), 0) == L - 1, dalast, 0.0)
        dda = _sel(_dot, dacum, triu, False)
        ddt = _sel(_dot_nt, dxdt_all * xs_ref[...], expand) + dda * a
        dalog_ref[...] += jnp.sum(dda * dt, axis=0, keepdims=True) * a
        ddt_raw = ddt * _sigmoid(dtp)
        ddt_ref[...] = ddt_raw
        dbias_ref[...] += jnp.sum(ddt_raw, axis=0, keepdims=True)

    rev = lambda c: (nc - 1 - c, 0)
    return _pcall(
        body, name="ssd_scan_bwd", grid=(nc,),
        in_specs=[pl.BlockSpec((L, DI), rev), pl.BlockSpec((L, DI), rev),
                  pl.BlockSpec((L, 1024), lambda c: (nc - 1 - c, 2)), pl.BlockSpec((L, 1024), lambda c: (nc - 1 - c, 3)),
                  pl.BlockSpec((L, NH), rev), pl.BlockSpec((1, NH), lambda c: (0, 0)),
                  pl.BlockSpec((1, NH), lambda c: (0, 0)), pl.BlockSpec((1, DI), lambda c: (0, 0)),
                  pl.BlockSpec((1, NG, NS, GW), lambda c: (nc - 1 - c, 0, 0, 0))],
        out_specs=[pl.BlockSpec((L, CONV_CH), rev), pl.BlockSpec((L, NH), rev),
                   pl.BlockSpec((1, NH), lambda c: (0, 0)), pl.BlockSpec((1, NH), lambda c: (0, 0)),
                   pl.BlockSpec((1, DI), lambda c: (0, 0))],
        out_shape=[jax.ShapeDtypeStruct((S, CONV_CH), F32), jax.ShapeDtypeStruct((S, NH), F32),
                   jax.ShapeDtypeStruct((1, NH), F32), jax.ShapeDtypeStruct((1, NH), F32),
                   jax.ShapeDtypeStruct((1, DI), F32)],
        scratch_shapes=[pltpu.VMEM((NG, NS, GW), F32)],
        compiler_params=_cp("arbitrary"))(dy, xbc_act, xbc_act, xbc_act, dt_raw, dt_bias, a_log, d_full, states)


def gate_norm_fwd(y, z, out_norm, tm=256):
    S = y.shape[0]
    tm = min(tm, S)

    def body(y_ref, z_ref, on_ref, o_ref):
        zv = z_ref[...]
        gin = y_ref[...] * (zv * _sigmoid(zv))
        for g in range(NG):
            gs = slice(g * GW, (g + 1) * GW)
            blk = gin[:, gs]
            r = lax.rsqrt(jnp.mean(blk * blk, axis=-1, keepdims=True) + EPS)
            o_ref[:, gs] = (blk * r * on_ref[:, gs]).astype(BF16)

    return _pcall(
        body, name="gate_norm_fwd", grid=(S // tm,),
        in_specs=[pl.BlockSpec((tm, DI), lambda i: (i, 0)), pl.BlockSpec((tm, DI), lambda i: (i, 0)),
                  pl.BlockSpec((1, DI), lambda i: (0, 0))],
        out_specs=pl.BlockSpec((tm, DI), lambda i: (i, 0)),
        out_shape=jax.ShapeDtypeStruct((S, DI), BF16),
        compiler_params=_cp("parallel"))(y, z, out_norm)


def gate_norm_bwd(dgn, y, z, out_norm, tm=256):
    S = y.shape[0]
    tm = min(tm, S)

    def body(dg_ref, y_ref, z_ref, on_ref, dy_ref, dz_ref, don_ref):
        i = pl.program_id(0)

        @pl.when(i == 0)
        def _():
            don_ref[...] = jnp.zeros_like(don_ref)

        zv, yv = z_ref[...], y_ref[...]
        s = _sigmoid(zv)
        sz = zv * s
        gin = yv * sz
        for g in range(NG):
            gs = slice(g * GW, (g + 1) * GW)
            blk = gin[:, gs]
            r = lax.rsqrt(jnp.mean(blk * blk, axis=-1, keepdims=True) + EPS)
            n = blk * r
            dg = dg_ref[:, gs]
            don_ref[:, gs] += jnp.sum(dg * n, axis=0, keepdims=True)
            dn = dg * on_ref[:, gs]
            dgin = r * (dn - n * jnp.mean(dn * n, axis=-1, keepdims=True))
            dy_ref[:, gs] = dgin * sz[:, gs]
            dz_ref[:, gs] = (dgin * yv[:, gs] * (s[:, gs] * (1.0 + zv[:, gs] * (1.0 - s[:, gs])))).astype(BF16)

    return _pcall(
        body, name="gate_norm_bwd", grid=(S // tm,),
        in_specs=[pl.BlockSpec((tm, DI), lambda i: (i, 0))] * 3 + [pl.BlockSpec((1, DI), lambda i: (0, 0))],
        out_specs=[pl.BlockSpec((tm, DI), lambda i: (i, 0)), pl.BlockSpec((tm, DI), lambda i: (i, 0)),
                   pl.BlockSpec((1, DI), lambda i: (0, 0))],
        out_shape=[jax.ShapeDtypeStruct((S, DI), F32), jax.ShapeDtypeStruct((S, DI), BF16),
                   jax.ShapeDtypeStruct((1, DI), F32)],
        compiler_params=_cp("arbitrary"))(dgn, y, z, out_norm)


SB_T = 256
SB_QSCALE = 0.125
SB_DEAD = -110.0
SB_UNSEEN = -1e30


def _head_norm(xv, lo):
    sq = xv * xv
    s0 = jnp.sum(jnp.where(lo, sq, 0.0), axis=-1, keepdims=True)
    s1 = jnp.sum(jnp.where(lo, 0.0, sq), axis=-1, keepdims=True)
    return jnp.where(lo, lax.rsqrt(s0 / SBD + EPS), lax.rsqrt(s1 / SBD + EPS))


def sb_prep_fwd(qkv, qg, kg, tm=256):
    S = qkv.shape[0]
    tm = min(tm, S)

    def body(x_ref, qg_ref, kg_ref, q_ref, k_ref, v_ref):
        lo = _iota((1, LANES), 1) < SBD
        for sl in range(D // LANES):
            cs = slice(sl * LANES, (sl + 1) * LANES)
            xq = x_ref[:, cs]
            q_ref[:, cs] = ((xq * _head_norm(xq, lo) * qg_ref[...]).astype(BF16).astype(F32) * SB_QSCALE).astype(BF16)
            xk = x_ref[:, D + sl * LANES:D + (sl + 1) * LANES]
            k_ref[:, cs] = (xk * _head_norm(xk, lo) * kg_ref[...]).astype(BF16)
        v_ref[...] = x_ref[:, 2 * D:3 * D].astype(BF16)

    return _pcall(
        body, name="sb_prep_fwd", grid=(S // tm,),
        in_specs=[pl.BlockSpec((tm, 3 * D), lambda i: (i, 0)), pl.BlockSpec((1, LANES), lambda i: (0, 0)),
                  pl.BlockSpec((1, LANES), lambda i: (0, 0))],
        out_specs=[pl.BlockSpec((tm, D), lambda i: (i, 0))] * 3,
        out_shape=[jax.ShapeDtypeStruct((S, D), BF16)] * 3,
        compiler_params=_cp("parallel"))(qkv, qg, kg)


def sb_prep_bwd(dqs, dkn, dv, qkv, qg, kg, tm=256):
    S = qkv.shape[0]
    tm = min(tm, S)

    def body(dq_ref, dk_ref, dv_ref, x_ref, qg_ref, kg_ref, dx_ref, dqg_ref, dkg_ref):
        i = pl.program_id(0)

        @pl.when(i == 0)
        def _():
            dqg_ref[...] = jnp.zeros_like(dqg_ref)
            dkg_ref[...] = jnp.zeros_like(dkg_ref)

        lo = _iota((1, LANES), 1) < SBD

        def one(xv, dh, gain):
            r = _head_norm(xv, lo)
            y = xv * r
            dy = dh * gain
            t = dy * y
            m0 = jnp.sum(jnp.where(lo, t, 0.0), axis=-1, keepdims=True)
            m1 = jnp.sum(jnp.where(lo, 0.0, t), axis=-1, keepdims=True)
            dx = r * (dy - y * (jnp.where(lo, m0, m1) / SBD))
            return dx, jnp.sum(dh * y, axis=0, keepdims=True)

        for sl in range(D // LANES):
            cs = slice(sl * LANES, (sl + 1) * LANES)
            dx, dg = one(x_ref[:, cs], dq_ref[:, cs] * SB_QSCALE, qg_ref[...])
            dx_ref[:, cs] = dx.astype(BF16)
            dqg_ref[:, cs] += dg
            ks = slice(D + sl * LANES, D + (sl + 1) * LANES)
            dx, dg = one(x_ref[:, ks], dk_ref[:, cs], kg_ref[...])
            dx_ref[:, ks] = dx.astype(BF16)
            dkg_ref[:, cs] += dg
        dx_ref[:, 2 * D:3 * D] = dv_ref[...].astype(BF16)

    return _pcall(
        body, name="sb_prep_bwd", grid=(S // tm,),
        in_specs=[pl.BlockSpec((tm, D), lambda i: (i, 0))] * 3
        + [pl.BlockSpec((tm, 3 * D), lambda i: (i, 0)), pl.BlockSpec((1, LANES), lambda i: (0, 0)),
           pl.BlockSpec((1, LANES), lambda i: (0, 0))],
        out_specs=[pl.BlockSpec((tm, 3 * D), lambda i: (i, 0)), pl.BlockSpec((1, D), lambda i: (0, 0)),
                   pl.BlockSpec((1, D), lambda i: (0, 0))],
        out_shape=[jax.ShapeDtypeStruct((S, 3 * D), BF16), jax.ShapeDtypeStruct((1, D), F32),
                   jax.ShapeDtypeStruct((1, D), F32)],
        compiler_params=_cp("arbitrary"))(dqs, dkn, dv, qkv, qg, kg)


def _split_dot(x, u):
    hi = x.astype(BF16)
    lo = (x - hi.astype(F32)).astype(BF16)
    return _dot(hi, u) + _dot(lo, u)


def _sb_logits(qh, kb, valid):
    z = _dot_nt(qh, kb)
    e = jnp.exp(-jnp.abs(z))
    lp = jnp.log(1.0 + e)
    lb = jnp.minimum(z, 0.0) - lp
    l1m = jnp.where(valid, lb - z, 0.0)
    return z, e, lb, l1m


def sb_fwd(qs, kn, v):
    S = qs.shape[0]
    T = min(SB_T, S)
    nq = S // T

    def body(q_ref, k_ref, v_ref, o_ref, r_ref, oacc, rrun):
        i = pl.program_id(1)
        qb = q_ref[...]
        lo = _iota((1, LANES), 1) < SBD
        masks = (lo, jnp.logical_not(lo))
        qhs = [jnp.where(hm, qb, jnp.zeros_like(qb)) for hm in masks]
        row, col = _iota((T, T), 0), _iota((T, T), 1)
        u = (row > col).astype(BF16)
        lane_blk = _iota((T, LANES), 1)
        oacc[...] = jnp.zeros_like(oacc)
        rrun[...] = jnp.zeros_like(rrun)
        r_ref[...] = jnp.full((2, T, LANES), SB_UNSEEN, F32)

        def live(carry):
            s, rmax = carry
            return jnp.logical_and(s <= i, rmax > SB_DEAD)

        def step(carry):
            s, _ = carry
            j = i - s
            off = pl.multiple_of(j * T, T)
            kb = k_ref[pl.ds(off, T), :]
            vb = v_ref[pl.ds(off, T), :]
            valid = (j * T + col) < (i * T + row)
            acc, rmax = None, None
            for hh in range(2):
                _, _, lb, l1m = _sb_logits(qhs[hh], kb, valid)
                r = rrun[hh]
                aft = _split_dot(l1m, u) + r
                a = jnp.where(valid, jnp.exp(lb + aft), 0.0)
                t = _dot(a.astype(BF16), jnp.where(masks[hh], vb, jnp.zeros_like(vb)))
                acc = t if acc is None else acc + t
                r_ref[hh] = jnp.where(lane_blk == j, r, r_ref[hh])
                rnew = r + jnp.sum(l1m, axis=-1, keepdims=True)
                rrun[hh] = rnew
                top = jnp.max(rnew)
                rmax = top if rmax is None else jnp.maximum(rmax, top)
            oacc[...] += acc
            return s + 1, rmax

        lax.while_loop(live, step, (jnp.int32(0), jnp.float32(0.0)))
        o_ref[...] = oacc[...].astype(BF16)

    return _pcall(
        body, name="sb_fwd", grid=(D // LANES, nq),
        in_specs=[pl.BlockSpec((T, LANES), lambda h, i: (i, h)), pl.BlockSpec((S, LANES), lambda h, i: (0, h)),
                  pl.BlockSpec((S, LANES), lambda h, i: (0, h))],
        out_specs=[pl.BlockSpec((T, LANES), lambda h, i: (i, h)), pl.BlockSpec((2, T, LANES), lambda h, i: (h, i, 0))],
        out_shape=[jax.ShapeDtypeStruct((S, D), BF16), jax.ShapeDtypeStruct((SBH, S, LANES), F32)],
        scratch_shapes=[pltpu.VMEM((T, LANES), F32), pltpu.VMEM((2, T, 1), F32)],
        compiler_params=_cp("parallel", "arbitrary"))(qs, kn, v)


def sb_bwd(qs, kn, v, do, rsave):
    S = qs.shape[0]
    T = min(SB_T, S)
    nq = S // T

    def body(q_ref, k_ref, v_ref, do_ref, r_ref, dq_ref, dk_ref, dv_ref, crun):
        i = pl.program_id(1)

        @pl.when(i == 0)
        def _():
            dk_ref[...] = jnp.zeros_like(dk_ref)
            dv_ref[...] = jnp.zeros_like(dv_ref)

        qb, dob = q_ref[...], do_ref[...]
        lo = _iota((1, LANES), 1) < SBD
        masks = (lo, jnp.logical_not(lo))
        qhs = [jnp.where(hm, qb, jnp.zeros_like(qb)) for hm in masks]
        dohs = [jnp.where(hm, dob, jnp.zeros_like(dob)) for hm in masks]
        row, col = _iota((T, T), 0), _iota((T, T), 1)
        u = (row > col).astype(BF16)
        u2 = (row < col).astype(BF16)
        lane_blk = _iota((T, LANES), 1)
        dq_ref[...] = jnp.zeros_like(dq_ref)
        crun[...] = jnp.zeros_like(crun)

        def step(j, carry):
            off = pl.multiple_of(j * T, T)
            kb = k_ref[pl.ds(off, T), :]
            vb = v_ref[pl.ds(off, T), :]
            valid = (j * T + col) < (i * T + row)
            dq_t, dk_t, dv_t = None, None, None
            for hh in range(2):
                hm, qh, doh = masks[hh], qhs[hh], dohs[hh]
                z, e, lb, l1m = _sb_logits(qh, kb, valid)
                r = jnp.sum(jnp.where(lane_blk == j, r_ref[hh], 0.0), axis=-1, keepdims=True)
                aft = _split_dot(l1m, u) + r
                a = jnp.where(valid, jnp.exp(lb + aft), 0.0)
                w = a * _dot_nt(doh, jnp.where(hm, vb, jnp.zeros_like(vb)))
                cprev = crun[hh]
                cw = _split_dot(w, u2) + cprev
                inv = 1.0 / (1.0 + e)
                pos = z >= 0.0
                beta = jnp.where(pos, 1.0, e) * inv
                onem = jnp.where(pos, e, 1.0) * inv
                dz = jnp.where(valid, w * onem - beta * cw, 0.0).astype(BF16)
                tq = _dot(dz, jnp.where(hm, kb, jnp.zeros_like(kb)))
                tk = _dot_tn(dz, qh)
                tv = _dot_tn(a.astype(BF16), doh)
                dq_t, dk_t, dv_t = (tq, tk, tv) if dq_t is None else (dq_t + tq, dk_t + tk, dv_t + tv)
                crun[hh] = cprev + jnp.sum(w, axis=-1, keepdims=True)
            dq_ref[...] += dq_t
            dk_ref[pl.ds(off, T), :] += dk_t
            dv_ref[pl.ds(off, T), :] += dv_t
            return carry

        n_live = None
        for hh in range(2):
            col_max = jnp.max(r_ref[hh], axis=0, keepdims=True)
            seen = jnp.logical_and(col_max > SB_DEAD, _iota((1, LANES), 1) <= i)
            n = jnp.sum(seen.astype(jnp.int32))
            n_live = n if n_live is None else jnp.maximum(n_live, n)
        lax.fori_loop(i + 1 - n_live, i + 1, step, 0)

    return _pcall(
        body, name="sb_bwd", grid=(D // LANES, nq),
        in_specs=[pl.BlockSpec((T, LANES), lambda h, i: (i, h)), pl.BlockSpec((S, LANES), lambda h, i: (0, h)),
                  pl.BlockSpec((S, LANES), lambda h, i: (0, h)), pl.BlockSpec((T, LANES), lambda h, i: (i, h)),
                  pl.BlockSpec((2, T, LANES), lambda h, i: (h, i, 0))],
        out_specs=[pl.BlockSpec((T, LANES), lambda h, i: (i, h)), pl.BlockSpec((S, LANES), lambda h, i: (0, h)),
                   pl.BlockSpec((S, LANES), lambda h, i: (0, h))],
        out_shape=[jax.ShapeDtypeStruct((S, D), F32)] * 3,
        scratch_shapes=[pltpu.VMEM((2, T, 1), F32)],
        compiler_params=_cp("parallel", "arbitrary"))(qs, kn, v, do, rsave)


def loss_head(y, target, tm=512):
    S = y.shape[0]
    tm = min(tm, S)

    def body(y_ref, t_ref, ls_ref, dy_ref):
        i = pl.program_id(0)
        err = y_ref[...] - t_ref[...]
        dy_ref[...] = err * (1.0 / D)
        part = jnp.sum(err * err, axis=0, keepdims=True)

        @pl.when(i == 0)
        def _():
            ls_ref[...] = part

        @pl.when(i > 0)
        def _():
            ls_ref[...] += part

    return _pcall(
        body, name="loss_head", grid=(S // tm,),
        in_specs=[pl.BlockSpec((tm, D), lambda i: (i, 0))] * 2,
        out_specs=[pl.BlockSpec((1, D), lambda i: (0, 0)), pl.BlockSpec((tm, D), lambda i: (i, 0))],
        out_shape=[jax.ShapeDtypeStruct((1, D), F32), jax.ShapeDtypeStruct((S, D), F32)],
        compiler_params=_cp("arbitrary"))(y, target)


def _row_tile(rows, cols):
    cap = max(8, (1 << 20) // (4 * cols))
    return max(t for t in range(8, min(rows, cap) + 1, 8) if rows % t == 0)


def _adamw_update(w, g, m, v):
    c1 = 1.0 / (1.0 - ADAM_B1 ** ADAM_STEP)
    c2 = 1.0 / (1.0 - ADAM_B2 ** ADAM_STEP)
    mn = ADAM_B1 * m + (1.0 - ADAM_B1) * g
    vn = ADAM_B2 * v + (1.0 - ADAM_B2) * (g * g)
    return -ADAM_LR * ((mn * c1) / (jnp.sqrt(vn * c2) + ADAM_EPS) + ADAM_WD * w), mn, vn


def adamw(w, g, m, v, name="adamw"):
    R, C = w.shape
    tr = _row_tile(R, C)

    def body(w_ref, g_ref, m_ref, v_ref, d_ref, mo_ref, vo_ref):
        d_ref[...], mo_ref[...], vo_ref[...] = _adamw_update(w_ref[...], g_ref[...], m_ref[...], v_ref[...])

    spec = pl.BlockSpec((tr, C), lambda i: (i, 0))
    return _pcall(
        body, name=name, grid=(R // tr,), in_specs=[spec] * 4, out_specs=[spec] * 3,
        out_shape=[jax.ShapeDtypeStruct((R, C), F32)] * 3,
        compiler_params=_cp("parallel"))(w, g, m, v)


def adamw_halves(w, g_mine, g_other, m, v, name="adamw_halves"):
    R, C = w.shape
    H = R // 2
    tr = _row_tile(H, C)
    n_i = H // tr
    where = lax.axis_index("c").astype(jnp.int32).reshape(1)

    def body(s_ref, w_ref, gm_ref, go_ref, m_ref, v_ref, g_ref, d_ref, mo_ref, vo_ref):
        g = jnp.where(pl.program_id(0) == s_ref[0], gm_ref[...], go_ref[...])
        g_ref[...] = g
        d_ref[...], mo_ref[...], vo_ref[...] = _adamw_update(w_ref[...], g, m_ref[...], v_ref[...])

    full = pl.BlockSpec((tr, C), lambda h, i, s: (h * n_i + i, 0))
    half = pl.BlockSpec((tr, C), lambda h, i, s: (i, 0))
    return _pcall(
        body, name=name,
        grid_spec=pltpu.PrefetchScalarGridSpec(
            num_scalar_prefetch=1, grid=(2, n_i), in_specs=[full, half, half, full, full], out_specs=[full] * 4),
        out_shape=[jax.ShapeDtypeStruct((R, C), F32)] * 4,
        compiler_params=_cp("parallel", "parallel"))(where, w, g_mine, g_other, m, v)


def pair_sum(gstacks, halves):
    c = lax.axis_index("c")
    me = 2 * lax.axis_index("x") + lax.axis_index("y")
    where = jnp.stack([c, me]).astype(jnp.int32)
    outs = []
    for g, xh in zip(gstacks, halves):
        _, H, C = xh.shape
        t = _row_tile(H, C)
        n_i = H // t

        def body(s_ref, g_ref, x_ref, qb_ref, own_ref):
            j = pl.program_id(1)
            q = g_ref[0] + x_ref[0]
            qb_ref[0] = q.astype(BF16)

            @pl.when(j == s_ref[1])
            def _():
                own_ref[...] = q

        outs.append(_pcall(
            body, name="pair_sum",
            grid_spec=pltpu.PrefetchScalarGridSpec(
                num_scalar_prefetch=1, grid=(n_i, N_CHIPS),
                in_specs=[pl.BlockSpec((1, t, C), lambda i, j, s, n_i=n_i: (j, s[0] * n_i + i, 0)),
                          pl.BlockSpec((1, t, C), lambda i, j, s: (j, i, 0))],
                out_specs=[pl.BlockSpec((1, t, C), lambda i, j, s: (j, i, 0)),
                           pl.BlockSpec((t, C), lambda i, j, s: (i, 0))]),
            out_shape=[jax.ShapeDtypeStruct((N_CHIPS, H, C), BF16), jax.ShapeDtypeStruct((H, C), F32)],
            compiler_params=_cp("parallel", "arbitrary"))(where, g, xh))
    return [o[0] for o in outs], [o[1] for o in outs]


def chip_sum(owns, recvs):
    outs = []
    for own, rc in zip(owns, recvs):
        H, C = own.shape
        t = _row_tile(H, C)

        def body(o_ref, r_ref, t_ref):
            t_ref[...] = ((o_ref[...] + r_ref[0].astype(F32)) + r_ref[1].astype(F32)) + r_ref[2].astype(F32)

        outs.append(_pcall(
            body, name="chip_sum", grid=(H // t,),
            in_specs=[pl.BlockSpec((t, C), lambda i: (i, 0)), pl.BlockSpec((3, t, C), lambda i: (0, i, 0))],
            out_specs=pl.BlockSpec((t, C), lambda i: (i, 0)),
            out_shape=jax.ShapeDtypeStruct((H, C), F32), compiler_params=_cp("parallel"))(own, rc))
    return outs


MESH = pl.DeviceIdType.MESH
ANY = pl.BlockSpec(memory_space=pl.ANY)
SPLIT_MIN_BYTES = 1 << 20


def _other_chips(x, y):
    return [(1 - x, y), (x, 1 - y), (1 - x, 1 - y)]


def _half_rows(rows, who):
    half = rows // 2
    return pl.ds(pl.multiple_of(who * half, 16), half)


def gather_all(shards):
    n = len(shards)
    rows = [s.shape[0] for s in shards]
    split = [r % 32 == 0 and s.size * s.dtype.itemsize >= SPLIT_MIN_BYTES for r, s in zip(rows, shards)]

    def body(*refs):
        ins, outs = refs[:n], refs[n:2 * n]
        ici_send, ici_recv, d2d_send, d2d_recv = refs[2 * n:]
        x, y, c = lax.axis_index("x"), lax.axis_index("y"), lax.axis_index("c")
        me, sib, chips = 2 * x + y, (x, y, 1 - c), _other_chips(x, y)

        def part(k, who):
            return _half_rows(rows[k], who) if split[k] else pl.ds(0, rows[k])

        def ici(k, r, block):
            px, py = chips[r]
            return pltpu.make_async_remote_copy(
                src_ref=ins[k].at[part(k, c)], dst_ref=outs[k].at[block, part(k, c)],
                send_sem=ici_send.at[3 * k + r], recv_sem=ici_recv.at[3 * k + r],
                device_id=(px, py, c), device_id_type=MESH)

        def d2d(k, r, who):
            px, py = chips[r]
            blk = outs[k].at[2 * px + py, part(k, who)]
            return pltpu.make_async_remote_copy(
                src_ref=blk, dst_ref=blk, send_sem=d2d_send.at[3 * k + r], recv_sem=d2d_recv.at[3 * k + r],
                device_id=sib, device_id_type=MESH)

        sends = [ici(k, r, me) for k in range(n) for r in range(3)]
        for cp in sends:
            cp.start()
        for r in range(3):
            px, py = chips[r]
            for k in range(n):
                ici(k, r, 2 * px + py).wait_recv()
                if split[k]:
                    fwd = d2d(k, r, c)
                    fwd.start()
                    sends.append(fwd)
        for r in range(3):
            for k in range(n):
                if split[k]:
                    d2d(k, r, 1 - c).wait_recv()
        for cp in sends:
            cp.wait_send()

    return _pcall(
        body, name="gather_all", in_specs=[ANY] * n, out_specs=[ANY] * n,
        out_shape=[jax.ShapeDtypeStruct((N_CHIPS,) + s.shape, s.dtype) for s in shards],
        scratch_shapes=[pltpu.SemaphoreType.DMA((3 * n,))] * 4)(*shards)


def swap_halves(gstacks):
    n = len(gstacks)

    def body(*refs):
        ins, outs, send_sems, recv_sems = refs[:n], refs[n:2 * n], refs[2 * n], refs[2 * n + 1]
        x, y, c = lax.axis_index("x"), lax.axis_index("y"), lax.axis_index("c")
        copies = [pltpu.make_async_remote_copy(
            src_ref=ins[k].at[:, _half_rows(ins[k].shape[1], 1 - c)], dst_ref=outs[k],
            send_sem=send_sems.at[k], recv_sem=recv_sems.at[k], device_id=(x, y, 1 - c), device_id_type=MESH)
            for k in range(n)]
        for cp in copies:
            cp.start()
        for cp in copies:
            cp.wait()

    return _pcall(
        body, name="swap_halves", in_specs=[ANY] * n, out_specs=[ANY] * n,
        out_shape=[jax.ShapeDtypeStruct((g.shape[0], g.shape[1] // 2, g.shape[2]), g.dtype) for g in gstacks],
        scratch_shapes=[pltpu.SemaphoreType.DMA((n,)), pltpu.SemaphoreType.DMA((n,))])(*gstacks)


def scatter_chips(stacks):
    n = len(stacks)

    def body(*refs):
        ins, outs, send_sems, recv_sems = refs[:n], refs[n:2 * n], refs[2 * n], refs[2 * n + 1]
        x, y, c = lax.axis_index("x"), lax.axis_index("y"), lax.axis_index("c")
        copies = [pltpu.make_async_remote_copy(
            src_ref=ins[k].at[2 * px + py], dst_ref=outs[k].at[r], send_sem=send_sems.at[3 * k + r],
            recv_sem=recv_sems.at[3 * k + r], device_id=(px, py, c), device_id_type=MESH)
            for k in range(n) for r, (px, py) in enumerate(_other_chips(x, y))]
        for cp in copies:
            cp.start()
        for cp in copies:
            cp.wait()

    return _pcall(
        body, name="scatter_chips", in_specs=[ANY] * n, out_specs=[ANY] * n,
        out_shape=[jax.ShapeDtypeStruct((3,) + s.shape[1:], s.dtype) for s in stacks],
        scratch_shapes=[pltpu.SemaphoreType.DMA((3 * n,)), pltpu.SemaphoreType.DMA((3 * n,))])(*stacks)


def swap_totals(totals):
    n = len(totals)

    def body(*refs):
        ins, outs, send_sems, recv_sems = refs[:n], refs[n:2 * n], refs[2 * n], refs[2 * n + 1]
        x, y, c = lax.axis_index("x"), lax.axis_index("y"), lax.axis_index("c")
        copies = [pltpu.make_async_remote_copy(
            src_ref=ins[k], dst_ref=outs[k], send_sem=send_sems.at[k], recv_sem=recv_sems.at[k],
            device_id=(x, y, 1 - c), device_id_type=MESH) for k in range(n)]
        for cp in copies:
            cp.start()
        for cp in copies:
            cp.wait()

    return _pcall(
        body, name="swap_totals", in_specs=[ANY] * n, out_specs=[ANY] * n,
        out_shape=[jax.ShapeDtypeStruct(t.shape, t.dtype) for t in totals],
        scratch_shapes=[pltpu.SemaphoreType.DMA((n,)), pltpu.SemaphoreType.DMA((n,))])(*totals)


def place_own(gathered, own):
    R, C = own.shape
    t = _row_tile(R, C)
    where = (2 * lax.axis_index("x") + lax.axis_index("y")).astype(jnp.int32).reshape(1)

    def body(s_ref, own_ref, g_ref, o_ref):
        o_ref[0] = own_ref[...]

    return _pcall(
        body, name="place_own",
        grid_spec=pltpu.PrefetchScalarGridSpec(
            num_scalar_prefetch=1, grid=(R // t,), in_specs=[pl.BlockSpec((t, C), lambda i, s: (i, 0)), ANY],
            out_specs=pl.BlockSpec((1, t, C), lambda i, s: (s[0], i, 0))),
        out_shape=jax.ShapeDtypeStruct(gathered.shape, gathered.dtype), input_output_aliases={2: 0},
        compiler_params=_cp("parallel"))(where, own, gathered)


def reduce_scatter(gstacks):
    halves = swap_halves(gstacks)
    payload, own = pair_sum(gstacks, halves)
    recv = scatter_chips(payload)
    mine = chip_sum(own, recv)
    return mine, swap_totals(mine)


def allreduce_small(vec):
    R = vec.shape[0]

    def body(in_ref, out_ref, buf, send_sems, recv_sems):
        x, y, c = lax.axis_index("x"), lax.axis_index("y"), lax.axis_index("c")
        me = 4 * x + 2 * y + c
        buf[me] = in_ref[...]
        copies = []
        for k in range(1, 8):
            peer = (x ^ (k >> 2), y ^ ((k >> 1) & 1), c ^ (k & 1))
            copies.append(pltpu.make_async_remote_copy(
                src_ref=in_ref, dst_ref=buf.at[me], send_sem=send_sems.at[k - 1], recv_sem=recv_sems.at[k - 1],
                device_id=peer, device_id_type=MESH))
        for cp in copies:
            cp.start()
        for cp in copies:
            cp.wait()
        acc = buf[0]
        for d in range(1, 8):
            acc = acc + buf[d]
        out_ref[...] = acc

    vm = pl.BlockSpec(memory_space=pltpu.VMEM)
    return _pcall(
        body, name="allreduce_small", in_specs=[vm], out_specs=vm,
        out_shape=jax.ShapeDtypeStruct((R, LANES), F32),
        scratch_shapes=[pltpu.VMEM((8, R, LANES), F32), pltpu.SemaphoreType.DMA((7,)), pltpu.SemaphoreType.DMA((7,))])(vec)


MATMUL_SHARDED = [("pool_in", 1), ("pool_group", 2), ("ssd_in", 2), ("ssd_out", 1), ("sb_qkv", 2), ("sb_out", 1),
                  ("ffn_gate", 2), ("ffn_up", 2), ("ffn_down", 1)]
STACKED = ["ffn_gate", "ffn_up", "ffn_down"]
SMALL_SHARDED = [("pool_scale", 1), ("ssd_conv_w", 2)]
REPLICATED = ["mix_norm", "ssd_conv_b", "ssd_dt_bias", "ssd_a_log", "ssd_d", "ssd_out_norm", "sb_q_norm",
              "sb_k_norm", "ffn_norm"]
WEIGHT_ORDER = ["mix_norm", "pool_in", "pool_group", "pool_scale", "ssd_in", "ssd_conv_w", "ssd_conv_b",
                "ssd_dt_bias", "ssd_a_log", "ssd_d", "ssd_out_norm", "ssd_out", "sb_qkv", "sb_q_norm", "sb_k_norm",
                "sb_out", "ffn_norm", "ffn_gate", "ffn_up", "ffn_down"]


def _piece_rows(n, mult):
    rows = -(-n // LANES)
    return -(-rows // mult) * mult


def _as_rows(a, mult):
    flat = a.reshape(-1)
    rows = _piece_rows(flat.shape[0], mult)
    if rows * LANES != flat.shape[0]:
        flat = jnp.pad(flat, (0, rows * LANES - flat.shape[0]))
    return flat.reshape(rows, LANES)


def _pack(arrs, mult=8, row_pad=8):
    parts = [_as_rows(a, mult) for a in arrs]
    rows = sum(p.shape[0] for p in parts)
    pad = -rows % row_pad
    if pad:
        parts.append(jnp.zeros((pad, LANES), parts[0].dtype))
    return jnp.concatenate(parts, axis=0)


def _unpack(packed, shapes, mult=8, lead=()):
    out, off = [], 0
    for s in shapes:
        n = math.prod(s)
        rows = _piece_rows(n, mult)
        piece = packed[..., off:off + rows, :].reshape(lead + (rows * LANES,))
        out.append(piece[..., :n].reshape(lead + tuple(s)))
        off += rows
    return out


def _rows2d(a):
    return a.reshape(-1, a.shape[-1])


def _gather_weights(shards):
    own = [_rows2d(shards[n].astype(BF16)) for n, _ in MATMUL_SHARDED]
    small = _pack([shards[n] for n, _ in SMALL_SHARDED])
    gathered = gather_all(own + [small])
    me = 2 * lax.axis_index("x") + lax.axis_index("y")

    def whole(got, mine, ax):
        return jnp.concatenate([jnp.where(me == j, mine, got[j]) for j in range(N_CHIPS)], axis=ax)

    full = {}
    for (n, ax), got, mine in zip(MATMUL_SHARDED, gathered, own):
        if n in STACKED:
            full[n] = place_own(got, mine)
        else:
            shp = shards[n].shape
            full[n] = whole(got.reshape((N_CHIPS,) + shp), mine.reshape(shp), ax)
    pieces = _unpack(gathered[-1], [shards[n].shape for n, _ in SMALL_SHARDED], lead=(N_CHIPS,))
    for (n, ax), got in zip(SMALL_SHARDED, pieces):
        full[n] = whole(got, shards[n], ax)
    return full


def _split_shards(full, axis):
    return jnp.stack(jnp.split(full, N_CHIPS, axis=axis))


def _ffn_fwd(x, gain, wg4, wu4, wd4, layer):
    h = rmsnorm_fwd(x, gain, name="ffn_norm_fwd")
    a4, b4, hid4 = ffn_up(h, wg4, wu4, layer)
    xo = ffn_down(hid4, wd4, layer, x)
    return xo, (x, h, a4, b4, hid4)


def _ffn_bwd(dout, saved, gain, wg4, wu4, wd4, layer, gbufs):
    x, h, a4, b4, hid4 = saved
    da4, db4 = ffn_bwd_hidden(dout, wd4, layer, a4, b4)
    dwd4 = ffn_wgrad_out(hid4, dout, layer, None if gbufs is None else gbufs[2])
    dwg4, dwu4 = ffn_wgrad_in(h, da4, db4, layer, None if gbufs is None else gbufs[:2])
    dh = ffn_dh(da4, db4, wg4, wu4, layer)
    dx, dgain = rmsnorm_bwd(x, gain, dh, dout, name="ffn_norm_bwd")
    return dx, dgain, (dwg4, dwu4, dwd4)


def _pool_layer_fwd(x, gain, w_in, wgrp, scale):
    h = rmsnorm_fwd(x, gain, name="pool_norm_fwd")
    u = linear([(h, w_in, "nn")], name="pool_in")
    xo, p = pool_fwd(u, wgrp, scale, x)
    return xo, (x, h, p)


def _pool_layer_bwd(dout, saved, gain, w_in, wgrp, scale):
    x, h, p = saved
    dp, dwgrp, dscale = pool_bwd_group(dout, p, wgrp, scale)
    du = pool_bwd_window(dp)
    (dw_in,) = wgrad(h, [du], name="pool_dwin")
    dh = linear([(du, w_in, "nt")], name="pool_dh")
    dx, dgain = rmsnorm_bwd(x, gain, dh, dout, name="pool_norm_bwd")
    return dx, dgain, dw_in, dwgrp, dscale


def _ssd_layer_fwd(x, gain, w_z, w_xbc, w_dt, conv_w, conv_b, dt_bias, a_log, d_full, out_norm, w_out):
    h = rmsnorm_fwd(x, gain, name="ssd_norm_fwd")
    z = linear([(h, w_z, "nn")], name="ssd_in_z")
    xbc = linear([(h, w_xbc, "nn")], tn=2048, name="ssd_in_xbc")
    dt_raw = linear([(h, w_dt, "nn")], name="ssd_in_dt")
    act = conv_fwd(xbc, conv_w, conv_b)
    y, states = ssd_scan_fwd(act, dt_raw, dt_bias, a_log, d_full)
    gn = gate_norm_fwd(y, z, out_norm)
    xo = linear([(gn, w_out, "nn")], res=x, name="ssd_out")
    return xo, (x, h, z, xbc, dt_raw, act, y, states, gn)


def _ssd_layer_bwd(dout, saved, gain, w_z, w_xbc, w_dt, conv_w, conv_b, dt_bias, a_log, d_full, out_norm, w_out):
    x, h, z, xbc, dt_raw, act, y, states, gn = saved
    dgn = linear([(dout, w_out, "nt")], name="ssd_dgn")
    (dw_out,) = wgrad(gn, [dout], name="ssd_dwout")
    dy, dz, dout_norm = gate_norm_bwd(dgn, y, z, out_norm)
    dact, ddt_raw, dbias, dalog, dd_full = ssd_scan_bwd(dy, act, dt_raw, dt_bias, a_log, d_full, states)
    dpre, dconv_w8, dconv_b = conv_bwd_pre(dact, xbc, conv_w, conv_b)
    dxbc = conv_bwd_input(dpre, conv_w)
    ddt_b = ddt_raw.astype(BF16)
    (dw_z,) = wgrad(h, [dz], name="ssd_dwz")
    (dw_xbc,) = wgrad(h, [dxbc], tn=2048, name="ssd_dwxbc")
    (dw_dt,) = wgrad(h, [ddt_b], name="ssd_dwdt")
    dh = linear([(dz, w_z, "nt"), (dxbc, w_xbc, "nt"), (ddt_b, w_dt, "nt")], tm=256, name="ssd_dh")
    dx, dgain = rmsnorm_bwd(x, gain, dh, dout, name="ssd_norm_bwd")
    dw_in = jnp.concatenate([dw_z, dw_xbc, dw_dt], axis=1)
    dd = dd_full.reshape(NH, HP).sum(axis=1).reshape(1, NH)
    return dx, dgain, dw_in, dconv_w8[:4], dconv_b, dbias, dalog, dd, dout_norm, dw_out


def _sb_layer_fwd(x, gain, w_qkv, qg, kg, w_out):
    h = rmsnorm_fwd(x, gain, name="sb_norm_fwd")
    qkv = linear([(h, w_qkv, "nn")], tn=1024, name="sb_qkv")
    qs, kn, v = sb_prep_fwd(qkv, qg, kg)
    o, rsave = sb_fwd(qs, kn, v)
    xo = linear([(o, w_out, "nn")], res=x, name="sb_out")
    return xo, (x, h, qkv, qs, kn, v, o, rsave)


def _sb_layer_bwd(dout, saved, gain, w_qkv, qg, kg, w_out):
    x, h, qkv, qs, kn, v, o, rsave = saved
    do = linear([(dout, w_out, "nt")], out_dtype=BF16, name="sb_do")
    (dw_out,) = wgrad(o, [dout], name="sb_dwout")
    dqs, dkn, dv = sb_bwd(qs, kn, v, do, rsave)
    dqkv, dqg, dkg = sb_prep_bwd(dqs, dkn, dv, qkv, qg, kg)
    (dw_qkv,) = wgrad(h, [dqkv], tn=1024, name="sb_dwqkv")
    dh = linear([(dqkv, w_qkv, "nt")], name="sb_dh")
    dx, dgain = rmsnorm_bwd(x, gain, dh, dout, name="sb_norm_bwd")
    dqg = dqg.reshape(SBH, SBD).sum(axis=0).reshape(1, SBD)
    dkg = dkg.reshape(SBH, SBD).sum(axis=0).reshape(1, SBD)
    return dx, dgain, dw_qkv, dqg, dkg, dw_out


def _local_step(x, target, full, rep):
    S = x.shape[0]
    d_full = jnp.repeat(rep["ssd_d"][0], HP).reshape(1, DI)
    qg = jnp.tile(rep["sb_q_norm"][0], 2).reshape(1, LANES)
    kg = jnp.tile(rep["sb_k_norm"][0], 2).reshape(1, LANES)
    ssd_in = full["ssd_in"][0]
    w_z, w_xbc, w_dt = ssd_in[:, :DI], ssd_in[:, DI:DI + CONV_CH], ssd_in[:, DI + CONV_CH:]
    conv_w = full["ssd_conv_w"][0]
    conv_b = rep["ssd_conv_b"]
    pool_scale = full["pool_scale"]

    def mixer_args(i):
        kind, j = i % 3, i // 3
        if kind == 0:
            return (full["pool_in"][j], full["pool_group"][j], pool_scale[j:j + 1])
        if kind == 1:
            return (w_z, w_xbc, w_dt, conv_w, conv_b, rep["ssd_dt_bias"], rep["ssd_a_log"], d_full,
                    rep["ssd_out_norm"], full["ssd_out"][0])
        return (full["sb_qkv"][0], qg, kg, full["sb_out"][0])

    fwd = (_pool_layer_fwd, _ssd_layer_fwd, _sb_layer_fwd)
    bwd = (_pool_layer_bwd, _ssd_layer_bwd, _sb_layer_bwd)
    saved = []
    for i in range(DEPTH):
        x, sm = fwd[i % 3](x, rep["mix_norm"][i], *mixer_args(i))
        x, sf = _ffn_fwd(x, rep["ffn_norm"][i], full["ffn_gate"], full["ffn_up"], full["ffn_down"], i)
        saved.append((sm, sf))

    colsq, dx = loss_head(x, target)
    loss = 0.5 * jnp.sum(colsq) / D

    g = {n: [None] * DEPTH for n in ("mix_norm", "ffn_norm")}
    ffn_g = None
    g["pool_in"], g["pool_group"], g["pool_scale"] = [None] * 2, [None] * 2, [None] * 2
    for i in reversed(range(DEPTH)):
        sm, sf = saved[i]
        dx, g["ffn_norm"][i], ffn_g = _ffn_bwd(
            dx, sf, rep["ffn_norm"][i], full["ffn_gate"], full["ffn_up"], full["ffn_down"], i, ffn_g)
        kind, j = i % 3, i // 3
        res = bwd[kind](dx, sm, rep["mix_norm"][i], *mixer_args(i))
        dx, g["mix_norm"][i] = res[0], res[1]
        if kind == 0:
            g["pool_in"][j], g["pool_group"][j], g["pool_scale"][j] = res[2:]
        elif kind == 1:
            dw_in, dconv_w, dconv_b, dbias, dalog, dd, don, dw_out = res[2:]
            g.update(ssd_in=dw_in[None], ssd_conv_w=dconv_w[None], ssd_conv_b=dconv_b, ssd_dt_bias=dbias,
                     ssd_a_log=dalog, ssd_d=dd, ssd_out_norm=don, ssd_out=dw_out[None])
        else:
            dw_qkv, dqg, dkg, dw_out = res[2:]
            g.update(sb_qkv=dw_qkv[None], sb_q_norm=dqg, sb_k_norm=dkg, sb_out=dw_out[None])
    for n in ("mix_norm", "ffn_norm", "pool_scale"):
        g[n] = jnp.concatenate(g[n], axis=0)
    for n in ("pool_in", "pool_group"):
        g[n] = jnp.stack(g[n])
    g["ffn_gate"], g["ffn_up"], g["ffn_down"] = ffn_g
    return loss, dx, g


def kernel(x, mix_norm, pool_in, pool_group, pool_scale, ssd_in, ssd_conv_w, ssd_conv_b, ssd_dt_bias, ssd_a_log, ssd_d, ssd_out_norm, ssd_out, sb_qkv, sb_q_norm, sb_k_norm, sb_out, ffn_norm, ffn_gate, ffn_up, ffn_down, loss_target, m_mix_norm, m_pool_in, m_pool_group, m_pool_scale, m_ssd_in, m_ssd_conv_w, m_ssd_conv_b, m_ssd_dt_bias, m_ssd_a_log, m_ssd_d, m_ssd_out_norm, m_ssd_out, m_sb_qkv, m_sb_q_norm, m_sb_k_norm, m_sb_out, m_ffn_norm, m_ffn_gate, m_ffn_up, m_ffn_down, v_mix_norm, v_pool_in, v_pool_group, v_pool_scale, v_ssd_in, v_ssd_conv_w, v_ssd_conv_b, v_ssd_dt_bias, v_ssd_a_log, v_ssd_d, v_ssd_out_norm, v_ssd_out, v_sb_qkv, v_sb_q_norm, v_sb_k_norm, v_sb_out, v_ffn_norm, v_ffn_gate, v_ffn_up, v_ffn_down):
    given = dict(locals())
    w = {n: given[n] for n in WEIGHT_ORDER}
    m = {n: given["m_" + n] for n in WEIGHT_ORDER}
    v = {n: given["v_" + n] for n in WEIGHT_ORDER}
    full = _gather_weights(w)
    rep = {n: w[n] for n in REPLICATED}

    loss, dx, g = _local_step(x[0], loss_target[0], full, rep)
    loss = lax.psum(loss, ("x", "y", "c"))
    out = {}

    gstacks = [g[n] if n in STACKED else _split_shards(g[n], ax).reshape((N_CHIPS,) + _rows2d(w[n]).shape)
               for n, ax in MATMUL_SHARDED]
    mine, other = reduce_scatter(gstacks)
    for (n, _), g_mine, g_other in zip(MATMUL_SHARDED, mine, other):
        res = adamw_halves(_rows2d(w[n]), g_mine, g_other, _rows2d(m[n]), _rows2d(v[n]), name="adamw_" + n)
        for key, a in zip("gdmv", res):
            out[key, n] = a.reshape(w[n].shape)

    small = REPLICATED + [n for n, _ in SMALL_SHARDED]
    gfull = _unpack(allreduce_small(_pack([g[n] for n in small])), [g[n].shape for n in small])
    me = 2 * lax.axis_index("x") + lax.axis_index("y")
    gsum = dict(zip(small, gfull))
    for n, ax in SMALL_SHARDED:
        gsum[n] = lax.dynamic_slice_in_dim(gsum[n], me * w[n].shape[ax], w[n].shape[ax], axis=ax)
    res = adamw(*[_pack([t[n] for n in small]) for t in (w, gsum, m, v)], name="adamw_small")
    for key, flat in zip("dmv", res):
        for n, a in zip(small, _unpack(flat, [w[n].shape for n in small])):
            out[key, n] = a
    for n in small:
        out["g", n] = gsum[n]

    return (loss, dx[None], *[out["g", n] for n in WEIGHT_ORDER], *[out["d", n] for n in WEIGHT_ORDER],
            *[out["m", n] for n in WEIGHT_ORDER], *[out["v", n] for n in WEIGHT_ORDER])
```

```python
import math

import jax
import jax.numpy as jnp
from jax import lax
from jax.experimental import pallas as pl
from jax.experimental.pallas import tpu as pltpu

F32 = jnp.float32
BF16 = jnp.bfloat16

D = 1024
DEPTH = 4
EPS = 1e-6
POOL_WINDOWS = (2, 4, 8, 16)
PG = 256
DI = 2048
NH = 32
HP = 64
NG = 8
NS = 128
GW = 256
CH = 256
CONV_CH = 4096
SSD_IN = 6176
SBH = 16
SBD = 64
FH = 2816
N_CHIPS = 4
LANES = 128

ADAM_LR = 0.001
ADAM_B1 = 0.9
ADAM_B2 = 0.999
ADAM_EPS = 1e-08
ADAM_WD = 0.01
ADAM_STEP = 10

VMEM_LIMIT = 56 * 1024 * 1024


def _pcall(body, **kw):
    return pl.pallas_call(body, **kw)


def _cp(*sem):
    return pltpu.CompilerParams(dimension_semantics=sem, vmem_limit_bytes=VMEM_LIMIT)


def _dot(a, b, prec=None):
    return lax.dot_general(a, b, (((1,), (0,)), ((), ())), precision=prec, preferred_element_type=F32)


def _dot_nt(a, b, prec=None):
    return lax.dot_general(a, b, (((1,), (1,)), ((), ())), precision=prec, preferred_element_type=F32)


def _dot_tn(a, b, prec=None):
    return lax.dot_general(a, b, (((0,), (0,)), ((), ())), precision=prec, preferred_element_type=F32)


def _split3(x):
    x1 = x.astype(BF16)
    r = x - x1.astype(F32)
    x2 = r.astype(BF16)
    return x1, x2, (r - x2.astype(F32)).astype(BF16)


def _sel(dot, x, mask, x_first=True):
    mb = mask.astype(BF16)
    p = [dot(xi, mb) if x_first else dot(mb, xi) for xi in _split3(x)]
    return (p[0] + p[1]) + p[2]


def _sigmoid(x):
    return 1.0 / (1.0 + jnp.exp(-x))


def _iota(shape, axis):
    return lax.broadcasted_iota(jnp.int32, shape, axis)


def linear(pairs, res=None, out_dtype=F32, tm=512, tn=None, name="linear"):
    M = pairs[0][0].shape[0]
    N = pairs[0][1].shape[1] if pairs[0][2] == "nn" else pairs[0][1].shape[0]
    tm = min(tm, M)
    tn = N if tn is None else min(tn, N)
    n_pairs = len(pairs)
    modes = [p[2] for p in pairs]

    def body(*refs):
        acc = None
        for k in range(n_pairs):
            a = refs[2 * k][...].astype(BF16)
            w = refs[2 * k + 1][...]
            t = _dot(a, w) if modes[k] == "nn" else _dot_nt(a, w)
            acc = t if acc is None else acc + t
        if res is not None:
            acc = acc + refs[2 * n_pairs][...]
        refs[-1][...] = acc.astype(out_dtype)

    in_specs, args = [], []
    for a, w, mode in pairs:
        K = a.shape[1]
        in_specs.append(pl.BlockSpec((tm, K), lambda j, i: (i, 0)))
        if mode == "nn":
            in_specs.append(pl.BlockSpec((K, tn), lambda j, i: (0, j)))
        else:
            in_specs.append(pl.BlockSpec((tn, K), lambda j, i: (j, 0)))
        args += [a, w]
    if res is not None:
        in_specs.append(pl.BlockSpec((tm, tn), lambda j, i: (i, j)))
        args.append(res)
    return _pcall(
        body, name=name, grid=(N // tn, M // tm), in_specs=in_specs,
        out_specs=pl.BlockSpec((tm, tn), lambda j, i: (i, j)),
        out_shape=jax.ShapeDtypeStruct((M, N), out_dtype),
        compiler_params=_cp("parallel", "arbitrary"))(*args)


def wgrad(a, gs, tk=1024, tn=None, tm=1024, name="wgrad"):
    M, Ka = a.shape
    N = gs[0].shape[1]
    tk, tm = min(tk, Ka), min(tm, M)
    tn = N if tn is None else min(tn, N)
    n_g = len(gs)

    def body(*refs):
        a_ref, g_refs, o_refs = refs[0], refs[1:1 + n_g], refs[1 + n_g:]
        m = pl.program_id(2)
        at = a_ref[...].astype(BF16)
        for g_ref, o_ref in zip(g_refs, o_refs):
            t = _dot_tn(at, g_ref[...].astype(BF16))

            @pl.when(m == 0)
            def _():
                o_ref[...] = t

            @pl.when(m > 0)
            def _():
                o_ref[...] += t

    out = _pcall(
        body, name=name, grid=(Ka // tk, N // tn, M // tm),
        in_specs=[pl.BlockSpec((tm, tk), lambda k, j, m: (m, k))]
        + [pl.BlockSpec((tm, tn), lambda k, j, m: (m, j))] * n_g,
        out_specs=[pl.BlockSpec((tk, tn), lambda k, j, m: (k, j))] * n_g,
        out_shape=[jax.ShapeDtypeStruct((Ka, N), F32)] * n_g,
        compiler_params=_cp("parallel", "parallel", "arbitrary"))(a, *gs)
    return out


def rmsnorm_fwd(x, gain, tm=512, name="rmsnorm_fwd"):
    S, Dm = x.shape
    tm = min(tm, S)

    def body(x_ref, g_ref, o_ref):
        xv = x_ref[...]
        r = lax.rsqrt(jnp.mean(xv * xv, axis=-1, keepdims=True) + EPS)
        o_ref[...] = (xv * r * g_ref[...]).astype(BF16)

    return _pcall(
        body, name=name, grid=(S // tm,),
        in_specs=[pl.BlockSpec((tm, Dm), lambda i: (i, 0)), pl.BlockSpec((1, Dm), lambda i: (0, 0))],
        out_specs=pl.BlockSpec((tm, Dm), lambda i: (i, 0)),
        out_shape=jax.ShapeDtypeStruct((S, Dm), BF16),
        compiler_params=_cp("parallel"))(x, gain.reshape(1, Dm))


def rmsnorm_bwd(x, gain, dh, dres, tm=512, name="rmsnorm_bwd"):
    S, Dm = x.shape
    tm = min(tm, S)

    def body(x_ref, g_ref, dh_ref, dr_ref, dx_ref, dg_ref):
        i = pl.program_id(0)
        xv = x_ref[...]
        r = lax.rsqrt(jnp.mean(xv * xv, axis=-1, keepdims=True) + EPS)
        y = xv * r
        dhv = dh_ref[...]
        dy = dhv * g_ref[...]
        dx_ref[...] = dr_ref[...] + r * (dy - y * jnp.mean(dy * y, axis=-1, keepdims=True))
        part = jnp.sum(dhv * y, axis=0, keepdims=True)

        @pl.when(i == 0)
        def _():
            dg_ref[...] = part

        @pl.when(i > 0)
        def _():
            dg_ref[...] += part

    return _pcall(
        body, name=name, grid=(S // tm,),
        in_specs=[pl.BlockSpec((tm, Dm), lambda i: (i, 0)), pl.BlockSpec((1, Dm), lambda i: (0, 0)),
                  pl.BlockSpec((tm, Dm), lambda i: (i, 0)), pl.BlockSpec((tm, Dm), lambda i: (i, 0))],
        out_specs=[pl.BlockSpec((tm, Dm), lambda i: (i, 0)), pl.BlockSpec((1, Dm), lambda i: (0, 0))],
        out_shape=[jax.ShapeDtypeStruct((S, Dm), F32), jax.ShapeDtypeStruct((1, Dm), F32)],
        compiler_params=_cp("arbitrary"))(x, gain.reshape(1, Dm), dh, dres)


FS = FH // N_CHIPS


def ffn_up(h, wg4, wu4, layer, tm=1024):
    S = h.shape[0]
    tm = min(tm, S)

    def body(h_ref, wg_ref, wu_ref, a_ref, b_ref, hid_ref):
        hv = h_ref[...]
        a = _dot(hv, wg_ref[0])
        b = _dot(hv, wu_ref[0])
        a_ref[0] = a.astype(BF16)
        b_ref[0] = b.astype(BF16)
        hid_ref[0] = (a * _sigmoid(a) * b).astype(BF16)

    wspec = pl.BlockSpec((1, D, FS), lambda j, i: (j, layer, 0))
    aspec = pl.BlockSpec((1, tm, FS), lambda j, i: (j, i, 0))
    return _pcall(
        body, name="ffn_up", grid=(N_CHIPS, S // tm),
        in_specs=[pl.BlockSpec((tm, D), lambda j, i: (i, 0)), wspec, wspec], out_specs=[aspec] * 3,
        out_shape=[jax.ShapeDtypeStruct((N_CHIPS, S, FS), BF16)] * 3,
        compiler_params=_cp("parallel", "arbitrary"))(h, wg4, wu4)


def ffn_down(hid4, wd4, layer, x, tm=1024):
    S = x.shape[0]
    tm = min(tm, S)

    def body(hid_ref, wd_ref, x_ref, o_ref):
        acc = x_ref[...]
        for j in range(N_CHIPS):
            acc = acc + _dot(hid_ref[j], wd_ref[j])
        o_ref[...] = acc

    return _pcall(
        body, name="ffn_down", grid=(S // tm,),
        in_specs=[pl.BlockSpec((N_CHIPS, tm, FS), lambda i: (0, i, 0)),
                  pl.BlockSpec((N_CHIPS, FS, D), lambda i: (0, layer, 0)), pl.BlockSpec((tm, D), lambda i: (i, 0))],
        out_specs=pl.BlockSpec((tm, D), lambda i: (i, 0)),
        out_shape=jax.ShapeDtypeStruct((S, D), F32), compiler_params=_cp("parallel"))(hid4, wd4, x)


def ffn_bwd_hidden(dout, wd4, layer, a4, b4, tm=1024):
    S = dout.shape[0]
    tm = min(tm, S)

    def body(do_ref, wd_ref, a_ref, b_ref, da_ref, db_ref):
        dhid = _dot_nt(do_ref[...].astype(BF16), wd_ref[0])
        av, bv = a_ref[0].astype(F32), b_ref[0].astype(F32)
        s = _sigmoid(av)
        da_ref[0] = (dhid * bv * (s * (1.0 + av * (1.0 - s)))).astype(BF16)
        db_ref[0] = (dhid * (av * s)).astype(BF16)

    aspec = pl.BlockSpec((1, tm, FS), lambda i, j: (j, i, 0))
    return _pcall(
        body, name="ffn_bwd_hidden", grid=(S // tm, N_CHIPS),
        in_specs=[pl.BlockSpec((tm, D), lambda i, j: (i, 0)), pl.BlockSpec((1, FS, D), lambda i, j: (j, layer, 0)),
                  aspec, aspec],
        out_specs=[aspec] * 2, out_shape=[jax.ShapeDtypeStruct((N_CHIPS, S, FS), BF16)] * 2,
        compiler_params=_cp("parallel", "arbitrary"))(dout, wd4, a4, b4)


def ffn_wgrad_in(h, da4, db4, layer, bufs, tm=2048):
    S = h.shape[0]
    tm = min(tm, S)

    def body(h_ref, da_ref, db_ref, *rest):
        dg_ref, du_ref = rest[-2:]
        m = pl.program_id(1)
        hv = h_ref[...]
        for g_ref, o_ref in ((da_ref, dg_ref), (db_ref, du_ref)):
            t = _dot_tn(hv, g_ref[0])

            @pl.when(m == 0)
            def _():
                o_ref[0] = t

            @pl.when(m > 0)
            def _():
                o_ref[0] += t

    aspec = pl.BlockSpec((1, tm, FS), lambda j, m: (j, m, 0))
    ospec = pl.BlockSpec((1, D, FS), lambda j, m: (j, layer, 0))
    kept = {} if bufs is None else dict(input_output_aliases={3: 0, 4: 1})
    return _pcall(
        body, name="ffn_wgrad_in", grid=(N_CHIPS, S // tm),
        in_specs=[pl.BlockSpec((tm, D), lambda j, m: (m, 0)), aspec, aspec] + ([] if bufs is None else [ANY, ANY]),
        out_specs=[ospec] * 2, out_shape=[jax.ShapeDtypeStruct((N_CHIPS, DEPTH * D, FS), F32)] * 2,
        compiler_params=_cp("parallel", "arbitrary"), **kept)(h, da4, db4, *(bufs or ()))


def ffn_wgrad_out(hid4, dout, layer, buf, tm=2048):
    S = dout.shape[0]
    tm = min(tm, S)

    def body(hid_ref, do_ref, *rest):
        o_ref = rest[-1]
        m = pl.program_id(1)
        t = _dot_tn(hid_ref[0], do_ref[...].astype(BF16))

        @pl.when(m == 0)
        def _():
            o_ref[0] = t

        @pl.when(m > 0)
        def _():
            o_ref[0] += t

    kept = {} if buf is None else dict(input_output_aliases={2: 0})
    return _pcall(
        body, name="ffn_wgrad_out", grid=(N_CHIPS, S // tm),
        in_specs=[pl.BlockSpec((1, tm, FS), lambda j, m: (j, m, 0)), pl.BlockSpec((tm, D), lambda j, m: (m, 0))]
        + ([] if buf is None else [ANY]),
        out_specs=pl.BlockSpec((1, FS, D), lambda j, m: (j, layer, 0)),
        out_shape=jax.ShapeDtypeStruct((N_CHIPS, DEPTH * FS, D), F32),
        compiler_params=_cp("parallel", "arbitrary"), **kept)(hid4, dout, *(() if buf is None else (buf,)))


def ffn_dh(da4, db4, wg4, wu4, layer, tm=512):
    S = da4.shape[1]
    tm = min(tm, S)

    def body(da_ref, db_ref, wg_ref, wu_ref, o_ref):
        acc = _dot_nt(da_ref[0], wg_ref[0]) + _dot_nt(db_ref[0], wu_ref[0])
        for j in range(1, N_CHIPS):
            acc = acc + _dot_nt(da_ref[j], wg_ref[j]) + _dot_nt(db_ref[j], wu_ref[j])
        o_ref[...] = acc

    aspec = pl.BlockSpec((N_CHIPS, tm, FS), lambda i: (0, i, 0))
    wspec = pl.BlockSpec((N_CHIPS, D, FS), lambda i: (0, layer, 0))
    return _pcall(
        body, name="ffn_dh", grid=(S // tm,), in_specs=[aspec, aspec, wspec, wspec],
        out_specs=pl.BlockSpec((tm, D), lambda i: (i, 0)),
        out_shape=jax.ShapeDtypeStruct((S, D), F32), compiler_params=_cp("parallel"))(da4, db4, wg4, wu4)


POOL_T = 128
POOL_HALO = 16


def pool_fwd(u, wgrp, scale, x_res):
    S = u.shape[0]
    T, HB = min(POOL_T, S), POOL_HALO
    per = T // HB

    def body(u_ref, tail_ref, wg_ref, sc_ref, x_ref, xo_ref, p_ref):
        i = pl.program_id(0)
        uc = u_ref[...]
        tail = jnp.where(i > 0, tail_ref[...], 0.0)
        d_cur = _iota((T, T), 0) - _iota((T, T), 1)
        d_tail = _iota((T, HB), 0) - _iota((T, HB), 1) + HB
        tg = i * T + _iota((T, 1), 0)
        for g, w in enumerate(POOL_WINDOWS):
            gs = slice(g * PG, (g + 1) * PG)
            band = (d_cur >= 0) & (d_cur < w)
            band_t = (d_tail >= 0) & (d_tail < w)
            ug = uc[:, gs]
            ws = _sel(_dot, ug, band, False) + _sel(_dot, tail[:, gs], band_t, False)
            cnt = jnp.minimum(tg + 1, w).astype(F32)
            pb = (ws / cnt - ug).astype(BF16)
            p_ref[:, gs] = pb
            xo_ref[:, gs] = x_ref[:, gs] + _dot(pb, wg_ref[g]) * sc_ref[:, gs]

    return _pcall(
        body, name="pool_fwd", grid=(S // T,),
        in_specs=[pl.BlockSpec((T, D), lambda i: (i, 0)),
                  pl.BlockSpec((HB, D), lambda i: (jnp.maximum(i * per - 1, 0), 0)),
                  pl.BlockSpec((4, PG, PG), lambda i: (0, 0, 0)), pl.BlockSpec((1, D), lambda i: (0, 0)),
                  pl.BlockSpec((T, D), lambda i: (i, 0))],
        out_specs=[pl.BlockSpec((T, D), lambda i: (i, 0))] * 2,
        out_shape=[jax.ShapeDtypeStruct((S, D), F32), jax.ShapeDtypeStruct((S, D), BF16)],
        compiler_params=_cp("parallel"))(u, u, wgrp, scale, x_res)


def pool_bwd_group(dm, p, wgrp, scale, tm=512):
    S = dm.shape[0]
    tm = min(tm, S)

    def body(dm_ref, p_ref, wg_ref, sc_ref, dp_ref, dwg_ref, dsc_ref):
        i = pl.program_id(0)

        @pl.when(i == 0)
        def _():
            dwg_ref[...] = jnp.zeros_like(dwg_ref)
            dsc_ref[...] = jnp.zeros_like(dsc_ref)

        for g in range(4):
            gs = slice(g * PG, (g + 1) * PG)
            dmg, pg, wg = dm_ref[:, gs], p_ref[:, gs], wg_ref[g]
            dsc_ref[:, gs] += jnp.sum(dmg * _dot(pg, wg), axis=0, keepdims=True)
            dy = (dmg * sc_ref[:, gs]).astype(BF16)
            dp_ref[:, gs] = _dot_nt(dy, wg)
            dwg_ref[g] += _dot_tn(pg, dy)

    return _pcall(
        body, name="pool_bwd_group", grid=(S // tm,),
        in_specs=[pl.BlockSpec((tm, D), lambda i: (i, 0)), pl.BlockSpec((tm, D), lambda i: (i, 0)),
                  pl.BlockSpec((4, PG, PG), lambda i: (0, 0, 0)), pl.BlockSpec((1, D), lambda i: (0, 0))],
        out_specs=[pl.BlockSpec((tm, D), lambda i: (i, 0)), pl.BlockSpec((4, PG, PG), lambda i: (0, 0, 0)),
                   pl.BlockSpec((1, D), lambda i: (0, 0))],
        out_shape=[jax.ShapeDtypeStruct((S, D), F32), jax.ShapeDtypeStruct((4, PG, PG), F32),
                   jax.ShapeDtypeStruct((1, D), F32)],
        compiler_params=_cp("arbitrary"))(dm, p, wgrp, scale)


def pool_bwd_window(dp):
    S = dp.shape[0]
    T, HB = min(POOL_T, S), POOL_HALO
    per = T // HB
    nt = S // T

    def body(dp_ref, nxt_ref, du_ref):
        i = pl.program_id(0)
        dc = dp_ref[...]
        nxt = jnp.where(i < nt - 1, nxt_ref[...], 0.0)
        d_cur = _iota((T, T), 1) - _iota((T, T), 0)
        d_nxt = _iota((T, HB), 1) - _iota((T, HB), 0) + T
        tg = i * T + _iota((T, 1), 0)
        tn_ = (i + 1) * T + _iota((HB, 1), 0)
        for g, w in enumerate(POOL_WINDOWS):
            gs = slice(g * PG, (g + 1) * PG)
            band = (d_cur >= 0) & (d_cur < w)
            band_n = (d_nxt >= 0) & (d_nxt < w)
            dcg = dc[:, gs]
            cur = dcg / jnp.minimum(tg + 1, w).astype(F32)
            nx = nxt[:, gs] / jnp.minimum(tn_ + 1, w).astype(F32)
            du_ref[:, gs] = (_sel(_dot, cur, band, False) + _sel(_dot, nx, band_n, False) - dcg).astype(BF16)

    return _pcall(
        body, name="pool_bwd_window", grid=(nt,),
        in_specs=[pl.BlockSpec((T, D), lambda i: (i, 0)),
                  pl.BlockSpec((HB, D), lambda i: (jnp.minimum((i + 1) * per, S // HB - 1), 0))],
        out_specs=pl.BlockSpec((T, D), lambda i: (i, 0)),
        out_shape=jax.ShapeDtypeStruct((S, D), BF16),
        compiler_params=_cp("parallel"))(dp, dp)


CONV_T = 256


def _shift_down(xc, prev8, j):
    if j == 0:
        return xc
    T = xc.shape[0]
    body = pltpu.roll(xc, j, 0)
    first = jnp.where(_iota((8, 1), 0) < j, pltpu.roll(prev8, j, 0), body[0:8])
    return jnp.concatenate([first, body[8:T]], axis=0)


def _shift_up(dc, next8, j):
    if j == 0:
        return dc
    T = dc.shape[0]
    body = pltpu.roll(dc, T - j, 0)
    last = jnp.where(_iota((8, 1), 0) + j < 8, body[T - 8:T], pltpu.roll(next8, 8 - j, 0))
    return jnp.concatenate([body[0:T - 8], last], axis=0)


def conv_fwd(xbc, conv_w, conv_b):
    S = xbc.shape[0]
    T = min(CONV_T, S)
    CB = 1024

    def body(x_ref, prev_ref, w_ref, b_ref, o_ref):
        i = pl.program_id(1)
        xc = x_ref[...]
        prev8 = jnp.where(i > 0, prev_ref[...], 0.0)
        pre = b_ref[...] + w_ref[3:4, :] * xc
        for j in range(1, 4):
            pre = pre + w_ref[3 - j:4 - j, :] * _shift_down(xc, prev8, j)
        o_ref[...] = pre * _sigmoid(pre)

    return _pcall(
        body, name="conv_fwd", grid=(CONV_CH // CB, S // T),
        in_specs=[pl.BlockSpec((T, CB), lambda c, i: (i, c)),
                  pl.BlockSpec((8, CB), lambda c, i: (jnp.maximum(i * (T // 8) - 1, 0), c)),
                  pl.BlockSpec((4, CB), lambda c, i: (0, c)), pl.BlockSpec((1, CB), lambda c, i: (0, c))],
        out_specs=pl.BlockSpec((T, CB), lambda c, i: (i, c)),
        out_shape=jax.ShapeDtypeStruct((S, CONV_CH), F32),
        compiler_params=_cp("parallel", "parallel"))(xbc, xbc, conv_w, conv_b)


def conv_bwd_pre(dact, xbc, conv_w, conv_b):
    S = xbc.shape[0]
    T = min(CONV_T, S)
    CB = 1024

    def body(da_ref, x_ref, prev_ref, w_ref, b_ref, dpre_ref, dw_ref, db_ref):
        i = pl.program_id(1)
        xc = x_ref[...]
        prev8 = jnp.where(i > 0, prev_ref[...], 0.0)
        sh = [_shift_down(xc, prev8, j) for j in range(4)]
        pre = b_ref[...] + w_ref[3:4, :] * sh[0]
        for j in range(1, 4):
            pre = pre + w_ref[3 - j:4 - j, :] * sh[j]
        s = _sigmoid(pre)
        dpre = da_ref[...] * (s * (1.0 + pre * (1.0 - s)))
        dpre_ref[...] = dpre
        rows = [jnp.sum(dpre * sh[3 - k], axis=0, keepdims=True) for k in range(4)]
        dw = jnp.concatenate(rows + [jnp.zeros((4, CB), F32)], axis=0)
        db = jnp.sum(dpre, axis=0, keepdims=True)

        @pl.when(i == 0)
        def _():
            dw_ref[...] = dw
            db_ref[...] = db

        @pl.when(i > 0)
        def _():
            dw_ref[...] += dw
            db_ref[...] += db

    return _pcall(
        body, name="conv_bwd_pre", grid=(CONV_CH // CB, S // T),
        in_specs=[pl.BlockSpec((T, CB), lambda c, i: (i, c)), pl.BlockSpec((T, CB), lambda c, i: (i, c)),
                  pl.BlockSpec((8, CB), lambda c, i: (jnp.maximum(i * (T // 8) - 1, 0), c)),
                  pl.BlockSpec((4, CB), lambda c, i: (0, c)), pl.BlockSpec((1, CB), lambda c, i: (0, c))],
        out_specs=[pl.BlockSpec((T, CB), lambda c, i: (i, c)), pl.BlockSpec((8, CB), lambda c, i: (0, c)),
                   pl.BlockSpec((1, CB), lambda c, i: (0, c))],
        out_shape=[jax.ShapeDtypeStruct((S, CONV_CH), F32), jax.ShapeDtypeStruct((8, CONV_CH), F32),
                   jax.ShapeDtypeStruct((1, CONV_CH), F32)],
        compiler_params=_cp("parallel", "arbitrary"))(dact, xbc, xbc, conv_w, conv_b)


def conv_bwd_input(dpre, conv_w):
    S = dpre.shape[0]
    T = min(CONV_T, S)
    CB = 1024
    nt = S // T

    def body(d_ref, nxt_ref, w_ref, o_ref):
        i = pl.program_id(1)
        dc = d_ref[...]
        next8 = jnp.where(i < nt - 1, nxt_ref[...], 0.0)
        acc = w_ref[3:4, :] * dc
        for j in range(1, 4):
            acc = acc + w_ref[3 - j:4 - j, :] * _shift_up(dc, next8, j)
        o_ref[...] = acc.astype(BF16)

    return _pcall(
        body, name="conv_bwd_input", grid=(CONV_CH // CB, nt),
        in_specs=[pl.BlockSpec((T, CB), lambda c, i: (i, c)),
                  pl.BlockSpec((8, CB), lambda c, i: (jnp.minimum((i + 1) * (T // 8), S // 8 - 1), c)),
                  pl.BlockSpec((4, CB), lambda c, i: (0, c))],
        out_specs=pl.BlockSpec((T, CB), lambda c, i: (i, c)),
        out_shape=jax.ShapeDtypeStruct((S, CONV_CH), BF16),
        compiler_params=_cp("parallel", "parallel"))(dpre, dpre, conv_w)


def _ssd_chunk_terms(dt_ref, bias_ref, alog_ref):
    L = CH
    dtp = dt_ref[...] + bias_ref[...]
    dt = jnp.maximum(dtp, 0.0) + jnp.log(1.0 + jnp.exp(-jnp.abs(dtp)))
    a = -jnp.exp(alog_ref[...])
    da = dt * a
    tri = _iota((L, L), 0) >= _iota((L, L), 1)
    acum = _sel(_dot, da, tri, False)
    triu = _iota((L, L), 0) <= _iota((L, L), 1)
    acum_row = _sel(_dot_tn, da, triu)
    expand = _iota((NH, DI), 1) // HP == _iota((NH, DI), 0)
    acum_full = _sel(_dot, acum, expand)
    e_full = jnp.exp(acum_full)
    w_full = jnp.exp(acum_full[L - 1:L, :] - acum_full)
    dt_full = _sel(_dot, dt, expand)
    return dtp, dt, a, acum, acum_row, expand, triu, e_full, w_full, dt_full


def ssd_scan_fwd(xbc_act, dt_raw, dt_bias, a_log, d_full):
    S = xbc_act.shape[0]
    L = CH
    nc = S // L

    def body(xs_ref, b_ref, c_ref, dt_ref, bias_ref, alog_ref, d_ref, y_ref, st_ref, state):
        c = pl.program_id(0)

        @pl.when(c == 0)
        def _():
            state[...] = jnp.zeros_like(state)

        st_ref[0] = state[...]
        _, _, _, acum, acum_row, _, _, e_full, w_full, dt_full = _ssd_chunk_terms(dt_ref, bias_ref, alog_ref)
        causal = _iota((L, L), 0) >= _iota((L, L), 1)
        lane_head = _iota((1, GW), 1) // HP
        for g in range(NG):
            gs = slice(g * GW, (g + 1) * GW)
            ns = slice(g * NS, (g + 1) * NS)
            xs_g = xs_ref[:, gs]
            xdt_g = xs_g * dt_full[:, gs]
            cg = c_ref[:, ns].astype(BF16)
            bg = b_ref[:, ns].astype(BF16)
            gmat = _dot_nt(cg, bg)
            yg = jnp.zeros((L, GW), F32)
            for hh in range(4):
                h = 4 * g + hh
                diff = acum[:, h:h + 1] - acum_row[h:h + 1, :]
                dk = jnp.exp(jnp.where(causal, diff, -1e30))
                xm = jnp.where(lane_head == hh, xdt_g, 0.0).astype(BF16)
                yg = yg + _dot((gmat * dk).astype(BF16), xm)
            sg = state[g]
            yoff = _dot(cg, sg.astype(BF16)) * e_full[:, gs]
            y_ref[:, gs] = yg + yoff + d_ref[:, gs] * xs_g
            state[g] = sg * e_full[L - 1:L, gs] + _dot_tn(bg, (w_full[:, gs] * xdt_g).astype(BF16))

    return _pcall(
        body, name="ssd_scan_fwd", grid=(nc,),
        in_specs=[pl.BlockSpec((L, DI), lambda c: (c, 0)), pl.BlockSpec((L, 1024), lambda c: (c, 2)),
                  pl.BlockSpec((L, 1024), lambda c: (c, 3)), pl.BlockSpec((L, NH), lambda c: (c, 0)),
                  pl.BlockSpec((1, NH), lambda c: (0, 0)), pl.BlockSpec((1, NH), lambda c: (0, 0)),
                  pl.BlockSpec((1, DI), lambda c: (0, 0))],
        out_specs=[pl.BlockSpec((L, DI), lambda c: (c, 0)), pl.BlockSpec((1, NG, NS, GW), lambda c: (c, 0, 0, 0))],
        out_shape=[jax.ShapeDtypeStruct((S, DI), F32), jax.ShapeDtypeStruct((nc, NG, NS, GW), F32)],
        scratch_shapes=[pltpu.VMEM((NG, NS, GW), F32)],
        compiler_params=_cp("arbitrary"))(xbc_act, xbc_act, xbc_act, dt_raw, dt_bias, a_log, d_full)


def ssd_scan_bwd(dy, xbc_act, dt_raw, dt_bias, a_log, d_full, states):
    S = xbc_act.shape[0]
    L = CH
    nc = S // L

    def body(dy_ref, xs_ref, b_ref, c_ref, dt_ref, bias_ref, alog_ref, d_ref, st_ref,
             dxbc_ref, ddt_ref, dbias_ref, dalog_ref, dd_ref, dstate):
        c = pl.program_id(0)

        @pl.when(c == 0)
        def _():
            dstate[...] = jnp.zeros_like(dstate)
            dbias_ref[...] = jnp.zeros_like(dbias_ref)
            dalog_ref[...] = jnp.zeros_like(dalog_ref)
            dd_ref[...] = jnp.zeros_like(dd_ref)

        dtp, dt, a, acum, acum_row, expand, triu, e_full, w_full, dt_full = _ssd_chunk_terms(
            dt_ref, bias_ref, alog_ref)
        causal = _iota((L, L), 0) >= _iota((L, L), 1)
        lane_head = _iota((1, GW), 1) // HP
        head_id = _iota((1, NH), 1)
        head_row = _iota((NH, 1), 0)
        dacum = jnp.zeros((L, NH), F32)
        dacum_t = jnp.zeros((NH, L), F32)
        red_parts = []
        dxdt_parts = []
        alast_parts = []
        for g in range(NG):
            gs = slice(g * GW, (g + 1) * GW)
            ns = slice(g * NS, (g + 1) * NS)
            xs_g = xs_ref[:, gs]
            xdt_g = xs_g * dt_full[:, gs]
            dy_g = dy_ref[:, gs]
            cg = c_ref[:, ns].astype(BF16)
            bg = b_ref[:, ns].astype(BF16)
            gmat = _dot_nt(cg, bg)
            sg = st_ref[0, g]
            dsg = dstate[g]
            sgb, dsgb = sg.astype(BF16), dsg.astype(BF16)
            cs = _dot(cg, sgb)
            bds = _dot(bg, dsgb)
            e_g, w_g = e_full[:, gs], w_full[:, gs]
            dxdt = w_g * bds
            dgsum = jnp.zeros((L, L), F32)
            for hh in range(4):
                h = 4 * g + hh
                hm = lane_head == hh
                diff = acum[:, h:h + 1] - acum_row[h:h + 1, :]
                dk = jnp.exp(jnp.where(causal, diff, -1e30))
                m = gmat * dk
                dym = jnp.where(hm, dy_g, 0.0).astype(BF16)
                xm = jnp.where(hm, xdt_g, 0.0).astype(BF16)
                dm = _dot_nt(dym, xm)
                dxdt = dxdt + _dot_tn(m.astype(BF16), dym)
                dgsum = dgsum + dm * dk
                em = dm * m
                dacum = dacum + jnp.sum(em, axis=1, keepdims=True) * (head_id == h).astype(F32)
                dacum_t = dacum_t + (head_row == h).astype(F32) * jnp.sum(em, axis=0, keepdims=True)
            dgb = dgsum.astype(BF16)
            edy = (e_g * dy_g).astype(BF16)
            wx = (w_g * xdt_g).astype(BF16)
            dc_g = _dot(dgb, bg) + _dot_nt(edy, sgb)
            db_g = _dot_tn(dgb, cg) + _dot_nt(wx, dsgb)
            dxbc_ref[:, DI + g * NS:DI + (g + 1) * NS] = db_g
            dxbc_ref[:, DI + 1024 + g * NS:DI + 1024 + (g + 1) * NS] = dc_g
            p2w = bds * xdt_g * w_g
            red_parts.append(dy_g * cs * e_g - p2w)
            alast_parts.append(jnp.sum(p2w, axis=0, keepdims=True)
                               + e_full[L - 1:L, gs] * jnp.sum(dsg * sg, axis=0, keepdims=True))
            dxdt_parts.append(dxdt)
            dstate[g] = e_full[L - 1:L, gs] * dsg + _dot_tn(cg, edy)
            dxbc_ref[:, gs] = dxdt * dt_full[:, gs] + dy_g * d_ref[:, gs]
            dd_ref[:, gs] += jnp.sum(dy_g * xs_g, axis=0, keepdims=True)
        red = jnp.concatenate(red_parts, axis=1)
        dxdt_all = jnp.concatenate(dxdt_parts, axis=1)
        alast = jnp.concatenate(alast_parts, axis=1)
        eye = _iota((NH, NH), 0) == _iota((NH, NH), 1)
        dacum = dacum - _sel(_dot_tn, dacum_t, eye) + _sel(_dot_nt, red, expand)
        dalast = _sel(_dot_nt, jnp.broadcast_to(alast, (8, DI)), expand)[0:1, :]
        dacum = dacum + jnp.where(_iota((L, 1), 0) == L - 1, dalast, 0.0)
        dda = _sel(_dot, dacum, triu, False)
        ddt = _sel(_dot_nt, dxdt_all * xs_ref[...], expand) + dda * a
        dalog_ref[...] += jnp.sum(dda * dt, axis=0, keepdims=True) * a
        ddt_raw = ddt * _sigmoid(dtp)
        ddt_ref[...] = ddt_raw
        dbias_ref[...] += jnp.sum(ddt_raw, axis=0, keepdims=True)

    rev = lambda c: (nc - 1 - c, 0)
    return _pcall(
        body, name="ssd_scan_bwd", grid=(nc,),
        in_specs=[pl.BlockSpec((L, DI), rev), pl.BlockSpec((L, DI), rev),
                  pl.BlockSpec((L, 1024), lambda c: (nc - 1 - c, 2)), pl.BlockSpec((L, 1024), lambda c: (nc - 1 - c, 3)),
                  pl.BlockSpec((L, NH), rev), pl.BlockSpec((1, NH), lambda c: (0, 0)),
                  pl.BlockSpec((1, NH), lambda c: (0, 0)), pl.BlockSpec((1, DI), lambda c: (0, 0)),
                  pl.BlockSpec((1, NG, NS, GW), lambda c: (nc - 1 - c, 0, 0, 0))],
        out_specs=[pl.BlockSpec((L, CONV_CH), rev), pl.BlockSpec((L, NH), rev),
                   pl.BlockSpec((1, NH), lambda c: (0, 0)), pl.BlockSpec((1, NH), lambda c: (0, 0)),
                   pl.BlockSpec((1, DI), lambda c: (0, 0))],
        out_shape=[jax.ShapeDtypeStruct((S, CONV_CH), F32), jax.ShapeDtypeStruct((S, NH), F32),
                   jax.ShapeDtypeStruct((1, NH), F32), jax.ShapeDtypeStruct((1, NH), F32),
                   jax.ShapeDtypeStruct((1, DI), F32)],
        scratch_shapes=[pltpu.VMEM((NG, NS, GW), F32)],
        compiler_params=_cp("arbitrary"))(dy, xbc_act, xbc_act, xbc_act, dt_raw, dt_bias, a_log, d_full, states)


def gate_norm_fwd(y, z, out_norm, tm=256):
    S = y.shape[0]
    tm = min(tm, S)

    def body(y_ref, z_ref, on_ref, o_ref):
        zv = z_ref[...]
        gin = y_ref[...] * (zv * _sigmoid(zv))
        for g in range(NG):
            gs = slice(g * GW, (g + 1) * GW)
            blk = gin[:, gs]
            r = lax.rsqrt(jnp.mean(blk * blk, axis=-1, keepdims=True) + EPS)
            o_ref[:, gs] = (blk * r * on_ref[:, gs]).astype(BF16)

    return _pcall(
        body, name="gate_norm_fwd", grid=(S // tm,),
        in_specs=[pl.BlockSpec((tm, DI), lambda i: (i, 0)), pl.BlockSpec((tm, DI), lambda i: (i, 0)),
                  pl.BlockSpec((1, DI), lambda i: (0, 0))],
        out_specs=pl.BlockSpec((tm, DI), lambda i: (i, 0)),
        out_shape=jax.ShapeDtypeStruct((S, DI), BF16),
        compiler_params=_cp("parallel"))(y, z, out_norm)


def gate_norm_bwd(dgn, y, z, out_norm, tm=256):
    S = y.shape[0]
    tm = min(tm, S)

    def body(dg_ref, y_ref, z_ref, on_ref, dy_ref, dz_ref, don_ref):
        i = pl.program_id(0)

        @pl.when(i == 0)
        def _():
            don_ref[...] = jnp.zeros_like(don_ref)

        zv, yv = z_ref[...], y_ref[...]
        s = _sigmoid(zv)
        sz = zv * s
        gin = yv * sz
        for g in range(NG):
            gs = slice(g * GW, (g + 1) * GW)
            blk = gin[:, gs]
            r = lax.rsqrt(jnp.mean(blk * blk, axis=-1, keepdims=True) + EPS)
            n = blk * r
            dg = dg_ref[:, gs]
            don_ref[:, gs] += jnp.sum(dg * n, axis=0, keepdims=True)
            dn = dg * on_ref[:, gs]
            dgin = r * (dn - n * jnp.mean(dn * n, axis=-1, keepdims=True))
            dy_ref[:, gs] = dgin * sz[:, gs]
            dz_ref[:, gs] = (dgin * yv[:, gs] * (s[:, gs] * (1.0 + zv[:, gs] * (1.0 - s[:, gs])))).astype(BF16)

    return _pcall(
        body, name="gate_norm_bwd", grid=(S // tm,),
        in_specs=[pl.BlockSpec((tm, DI), lambda i: (i, 0))] * 3 + [pl.BlockSpec((1, DI), lambda i: (0, 0))],
        out_specs=[pl.BlockSpec((tm, DI), lambda i: (i, 0)), pl.BlockSpec((tm, DI), lambda i: (i, 0)),
                   pl.BlockSpec((1, DI), lambda i: (0, 0))],
        out_shape=[jax.ShapeDtypeStruct((S, DI), F32), jax.ShapeDtypeStruct((S, DI), BF16),
                   jax.ShapeDtypeStruct((1, DI), F32)],
        compiler_params=_cp("arbitrary"))(dgn, y, z, out_norm)


SB_T = 256
SB_QSCALE = 0.125
SB_DEAD = -110.0
SB_UNSEEN = -1e30


def _head_norm(xv, lo):
    sq = xv * xv
    s0 = jnp.sum(jnp.where(lo, sq, 0.0), axis=-1, keepdims=True)
    s1 = jnp.sum(jnp.where(lo, 0.0, sq), axis=-1, keepdims=True)
    return jnp.where(lo, lax.rsqrt(s0 / SBD + EPS), lax.rsqrt(s1 / SBD + EPS))


def sb_prep_fwd(qkv, qg, kg, tm=256):
    S = qkv.shape[0]
    tm = min(tm, S)

    def body(x_ref, qg_ref, kg_ref, q_ref, k_ref, v_ref):
        lo = _iota((1, LANES), 1) < SBD
        for sl in range(D // LANES):
            cs = slice(sl * LANES, (sl + 1) * LANES)
            xq = x_ref[:, cs]
            q_ref[:, cs] = ((xq * _head_norm(xq, lo) * qg_ref[...]).astype(BF16).astype(F32) * SB_QSCALE).astype(BF16)
            xk = x_ref[:, D + sl * LANES:D + (sl + 1) * LANES]
            k_ref[:, cs] = (xk * _head_norm(xk, lo) * kg_ref[...]).astype(BF16)
        v_ref[...] = x_ref[:, 2 * D:3 * D].astype(BF16)

    return _pcall(
        body, name="sb_prep_fwd", grid=(S // tm,),
        in_specs=[pl.BlockSpec((tm, 3 * D), lambda i: (i, 0)), pl.BlockSpec((1, LANES), lambda i: (0, 0)),
                  pl.BlockSpec((1, LANES), lambda i: (0, 0))],
        out_specs=[pl.BlockSpec((tm, D), lambda i: (i, 0))] * 3,
        out_shape=[jax.ShapeDtypeStruct((S, D), BF16)] * 3,
        compiler_params=_cp("parallel"))(qkv, qg, kg)


def sb_prep_bwd(dqs, dkn, dv, qkv, qg, kg, tm=256):
    S = qkv.shape[0]
    tm = min(tm, S)

    def body(dq_ref, dk_ref, dv_ref, x_ref, qg_ref, kg_ref, dx_ref, dqg_ref, dkg_ref):
        i = pl.program_id(0)

        @pl.when(i == 0)
        def _():
            dqg_ref[...] = jnp.zeros_like(dqg_ref)
            dkg_ref[...] = jnp.zeros_like(dkg_ref)

        lo = _iota((1, LANES), 1) < SBD

        def one(xv, dh, gain):
            r = _head_norm(xv, lo)
            y = xv * r
            dy = dh * gain
            t = dy * y
            m0 = jnp.sum(jnp.where(lo, t, 0.0), axis=-1, keepdims=True)
            m1 = jnp.sum(jnp.where(lo, 0.0, t), axis=-1, keepdims=True)
            dx = r * (dy - y * (jnp.where(lo, m0, m1) / SBD))
            return dx, jnp.sum(dh * y, axis=0, keepdims=True)

        for sl in range(D // LANES):
            cs = slice(sl * LANES, (sl + 1) * LANES)
            dx, dg = one(x_ref[:, cs], dq_ref[:, cs] * SB_QSCALE, qg_ref[...])
            dx_ref[:, cs] = dx.astype(BF16)
            dqg_ref[:, cs] += dg
            ks = slice(D + sl * LANES, D + (sl + 1) * LANES)
            dx, dg = one(x_ref[:, ks], dk_ref[:, cs], kg_ref[...])
            dx_ref[:, ks] = dx.astype(BF16)
            dkg_ref[:, cs] += dg
        dx_ref[:, 2 * D:3 * D] = dv_ref[...].astype(BF16)

    return _pcall(
        body, name="sb_prep_bwd", grid=(S // tm,),
        in_specs=[pl.BlockSpec((tm, D), lambda i: (i, 0))] * 3
        + [pl.BlockSpec((tm, 3 * D), lambda i: (i, 0)), pl.BlockSpec((1, LANES), lambda i: (0, 0)),
           pl.BlockSpec((1, LANES), lambda i: (0, 0))],
        out_specs=[pl.BlockSpec((tm, 3 * D), lambda i: (i, 0)), pl.BlockSpec((1, D), lambda i: (0, 0)),
                   pl.BlockSpec((1, D), lambda i: (0, 0))],
        out_shape=[jax.ShapeDtypeStruct((S, 3 * D), BF16), jax.ShapeDtypeStruct((1, D), F32),
                   jax.ShapeDtypeStruct((1, D), F32)],
        compiler_params=_cp("arbitrary"))(dqs, dkn, dv, qkv, qg, kg)


def _split_dot(x, u):
    hi = x.astype(BF16)
    lo = (x - hi.astype(F32)).astype(BF16)
    return _dot(hi, u) + _dot(lo, u)


def _sb_logits(qh, kb, valid):
    z = _dot_nt(qh, kb)
    e = jnp.exp(-jnp.abs(z))
    lp = jnp.log(1.0 + e)
    lb = jnp.minimum(z, 0.0) - lp
    l1m = lb - z if valid is None else jnp.where(valid, lb - z, 0.0)
    return z, e, lb, l1m


def _keep(valid, x):
    return x if valid is None else jnp.where(valid, x, 0.0)


def sb_fwd(qs, kn, v):
    S = qs.shape[0]
    T = min(SB_T, S)
    nq = S // T

    def body(q_ref, k_ref, v_ref, o_ref, r_ref, oacc, rrun):
        i = pl.program_id(1)
        qb = q_ref[...]
        lo = _iota((1, LANES), 1) < SBD
        masks = (lo, jnp.logical_not(lo))
        qhs = [jnp.where(hm, qb, jnp.zeros_like(qb)) for hm in masks]
        row, col = _iota((T, T), 0), _iota((T, T), 1)
        u = (row > col).astype(BF16)
        lane_blk = _iota((T, LANES), 1)
        oacc[...] = jnp.zeros_like(oacc)
        rrun[...] = jnp.zeros_like(rrun)
        r_ref[...] = jnp.full((2, T, LANES), SB_UNSEEN, F32)

        def live(carry):
            s, rmax = carry
            return jnp.logical_and(s <= i, rmax > SB_DEAD)

        def block(j, valid):
            off = pl.multiple_of(j * T, T)
            kb = k_ref[pl.ds(off, T), :]
            vb = v_ref[pl.ds(off, T), :]
            acc, rmax = None, None
            for hh in range(2):
                _, _, lb, l1m = _sb_logits(qhs[hh], kb, valid)
                r = rrun[hh]
                aft = _split_dot(l1m, u) + r
                a = _keep(valid, jnp.exp(lb + aft))
                t = _dot(a.astype(BF16), jnp.where(masks[hh], vb, jnp.zeros_like(vb)))
                acc = t if acc is None else acc + t
                r_ref[hh] = jnp.where(lane_blk == j, r, r_ref[hh])
                rnew = r + jnp.sum(l1m, axis=-1, keepdims=True)
                rrun[hh] = rnew
                top = jnp.max(rnew)
                rmax = top if rmax is None else jnp.maximum(rmax, top)
            oacc[...] += acc
            return rmax

        first = block(i, col < row)
        lax.while_loop(live, lambda carry: (carry[0] + 1, block(i - carry[0], None)), (jnp.int32(1), first))
        o_ref[...] = oacc[...].astype(BF16)

    return _pcall(
        body, name="sb_fwd", grid=(D // LANES, nq),
        in_specs=[pl.BlockSpec((T, LANES), lambda h, i: (i, h)), pl.BlockSpec((S, LANES), lambda h, i: (0, h)),
                  pl.BlockSpec((S, LANES), lambda h, i: (0, h))],
        out_specs=[pl.BlockSpec((T, LANES), lambda h, i: (i, h)), pl.BlockSpec((2, T, LANES), lambda h, i: (h, i, 0))],
        out_shape=[jax.ShapeDtypeStruct((S, D), BF16), jax.ShapeDtypeStruct((SBH, S, LANES), F32)],
        scratch_shapes=[pltpu.VMEM((T, LANES), F32), pltpu.VMEM((2, T, 1), F32)],
        compiler_params=_cp("parallel", "arbitrary"))(qs, kn, v)


def sb_bwd(qs, kn, v, do, rsave):
    S = qs.shape[0]
    T = min(SB_T, S)
    nq = S // T

    def body(q_ref, k_ref, v_ref, do_ref, r_ref, dq_ref, dk_ref, dv_ref, crun):
        i = pl.program_id(1)

        @pl.when(i == 0)
        def _():
            dk_ref[...] = jnp.zeros_like(dk_ref)
            dv_ref[...] = jnp.zeros_like(dv_ref)

        qb, dob = q_ref[...], do_ref[...]
        lo = _iota((1, LANES), 1) < SBD
        masks = (lo, jnp.logical_not(lo))
        qhs = [jnp.where(hm, qb, jnp.zeros_like(qb)) for hm in masks]
        dohs = [jnp.where(hm, dob, jnp.zeros_like(dob)) for hm in masks]
        row, col = _iota((T, T), 0), _iota((T, T), 1)
        u = (row > col).astype(BF16)
        u2 = (row < col).astype(BF16)
        lane_blk = _iota((T, LANES), 1)
        dq_ref[...] = jnp.zeros_like(dq_ref)
        crun[...] = jnp.zeros_like(crun)

        def block(j, valid):
            off = pl.multiple_of(j * T, T)
            kb = k_ref[pl.ds(off, T), :]
            vb = v_ref[pl.ds(off, T), :]
            dq_t, dk_t, dv_t = None, None, None
            for hh in range(2):
                hm, qh, doh = masks[hh], qhs[hh], dohs[hh]
                z, e, lb, l1m = _sb_logits(qh, kb, valid)
                r = jnp.sum(jnp.where(lane_blk == j, r_ref[hh], 0.0), axis=-1, keepdims=True)
                aft = _split_dot(l1m, u) + r
                a = _keep(valid, jnp.exp(lb + aft))
                w = a * _dot_nt(doh, jnp.where(hm, vb, jnp.zeros_like(vb)))
                cprev = crun[hh]
                cw = _split_dot(w, u2) + cprev
                inv = 1.0 / (1.0 + e)
                pos = z >= 0.0
                beta = jnp.where(pos, 1.0, e) * inv
                onem = jnp.where(pos, e, 1.0) * inv
                dz = _keep(valid, w * onem - beta * cw).astype(BF16)
                tq = _dot(dz, jnp.where(hm, kb, jnp.zeros_like(kb)))
                tk = _dot_tn(dz, qh)
                tv = _dot_tn(a.astype(BF16), doh)
                dq_t, dk_t, dv_t = (tq, tk, tv) if dq_t is None else (dq_t + tq, dk_t + tk, dv_t + tv)
                crun[hh] = cprev + jnp.sum(w, axis=-1, keepdims=True)
            dq_ref[...] += dq_t
            dk_ref[pl.ds(off, T), :] += dk_t
            dv_ref[pl.ds(off, T), :] += dv_t

        n_live = None
        for hh in range(2):
            col_max = jnp.max(r_ref[hh], axis=0, keepdims=True)
            seen = jnp.logical_and(col_max > SB_DEAD, _iota((1, LANES), 1) <= i)
            n = jnp.sum(seen.astype(jnp.int32))
            n_live = n if n_live is None else jnp.maximum(n_live, n)
        lax.fori_loop(i + 1 - n_live, i, lambda j, carry: (block(j, None), carry)[1], 0)
        block(i, col < row)

    return _pcall(
        body, name="sb_bwd", grid=(D // LANES, nq),
        in_specs=[pl.BlockSpec((T, LANES), lambda h, i: (i, h)), pl.BlockSpec((S, LANES), lambda h, i: (0, h)),
                  pl.BlockSpec((S, LANES), lambda h, i: (0, h)), pl.BlockSpec((T, LANES), lambda h, i: (i, h)),
                  pl.BlockSpec((2, T, LANES), lambda h, i: (h, i, 0))],
        out_specs=[pl.BlockSpec((T, LANES), lambda h, i: (i, h)), pl.BlockSpec((S, LANES), lambda h, i: (0, h)),
                   pl.BlockSpec((S, LANES), lambda h, i: (0, h))],
        out_shape=[jax.ShapeDtypeStruct((S, D), F32)] * 3,
        scratch_shapes=[pltpu.VMEM((2, T, 1), F32)],
        compiler_params=_cp("parallel", "arbitrary"))(qs, kn, v, do, rsave)


def loss_head(y, target, tm=512):
    S = y.shape[0]
    tm = min(tm, S)

    def body(y_ref, t_ref, ls_ref, dy_ref):
        i = pl.program_id(0)
        err = y_ref[...] - t_ref[...]
        dy_ref[...] = err * (1.0 / D)
        part = jnp.sum(err * err, axis=0, keepdims=True)

        @pl.when(i == 0)
        def _():
            ls_ref[...] = part

        @pl.when(i > 0)
        def _():
            ls_ref[...] += part

    return _pcall(
        body, name="loss_head", grid=(S // tm,),
        in_specs=[pl.BlockSpec((tm, D), lambda i: (i, 0))] * 2,
        out_specs=[pl.BlockSpec((1, D), lambda i: (0, 0)), pl.BlockSpec((tm, D), lambda i: (i, 0))],
        out_shape=[jax.ShapeDtypeStruct((1, D), F32), jax.ShapeDtypeStruct((S, D), F32)],
        compiler_params=_cp("arbitrary"))(y, target)


def _row_tile(rows, cols):
    cap = max(8, (1 << 20) // (4 * cols))
    return max(t for t in range(8, min(rows, cap) + 1, 8) if rows % t == 0)


def _adamw_update(w, g, m, v):
    c1 = 1.0 / (1.0 - ADAM_B1 ** ADAM_STEP)
    c2 = 1.0 / (1.0 - ADAM_B2 ** ADAM_STEP)
    mn = ADAM_B1 * m + (1.0 - ADAM_B1) * g
    vn = ADAM_B2 * v + (1.0 - ADAM_B2) * (g * g)
    return -ADAM_LR * ((mn * c1) / (jnp.sqrt(vn * c2) + ADAM_EPS) + ADAM_WD * w), mn, vn


def adamw(w, g, m, v, name="adamw"):
    R, C = w.shape
    tr = _row_tile(R, C)

    def body(w_ref, g_ref, m_ref, v_ref, d_ref, mo_ref, vo_ref):
        d_ref[...], mo_ref[...], vo_ref[...] = _adamw_update(w_ref[...], g_ref[...], m_ref[...], v_ref[...])

    spec = pl.BlockSpec((tr, C), lambda i: (i, 0))
    return _pcall(
        body, name=name, grid=(R // tr,), in_specs=[spec] * 4, out_specs=[spec] * 3,
        out_shape=[jax.ShapeDtypeStruct((R, C), F32)] * 3,
        compiler_params=_cp("parallel"))(w, g, m, v)


def adamw_halves(w, g_mine, g_other, m, v, name="adamw_halves"):
    R, C = w.shape
    H = R // 2
    tr = _row_tile(H, C)
    n_i = H // tr
    where = lax.axis_index("c").astype(jnp.int32).reshape(1)

    def body(s_ref, w_ref, gm_ref, go_ref, m_ref, v_ref, g_ref, d_ref, mo_ref, vo_ref):
        g = jnp.where(pl.program_id(0) == s_ref[0], gm_ref[...], go_ref[...])
        g_ref[...] = g
        d_ref[...], mo_ref[...], vo_ref[...] = _adamw_update(w_ref[...], g, m_ref[...], v_ref[...])

    full = pl.BlockSpec((tr, C), lambda h, i, s: (h * n_i + i, 0))
    half = pl.BlockSpec((tr, C), lambda h, i, s: (i, 0))
    return _pcall(
        body, name=name,
        grid_spec=pltpu.PrefetchScalarGridSpec(
            num_scalar_prefetch=1, grid=(2, n_i), in_specs=[full, half, half, full, full], out_specs=[full] * 4),
        out_shape=[jax.ShapeDtypeStruct((R, C), F32)] * 4,
        compiler_params=_cp("parallel", "parallel"))(where, w, g_mine, g_other, m, v)


def pair_sum(gstacks, halves):
    c = lax.axis_index("c")
    me = 2 * lax.axis_index("x") + lax.axis_index("y")
    where = jnp.stack([c, me]).astype(jnp.int32)
    outs = []
    for g, xh in zip(gstacks, halves):
        _, H, C = xh.shape
        t = _row_tile(H, C)
        n_i = H // t

        def body(s_ref, g_ref, x_ref, qb_ref, own_ref):
            j = pl.program_id(1)
            q = g_ref[0] + x_ref[0]
            qb_ref[0] = q.astype(BF16)

            @pl.when(j == s_ref[1])
            def _():
                own_ref[...] = q

        outs.append(_pcall(
            body, name="pair_sum",
            grid_spec=pltpu.PrefetchScalarGridSpec(
                num_scalar_prefetch=1, grid=(n_i, N_CHIPS),
                in_specs=[pl.BlockSpec((1, t, C), lambda i, j, s, n_i=n_i: (j, s[0] * n_i + i, 0)),
                          pl.BlockSpec((1, t, C), lambda i, j, s: (j, i, 0))],
                out_specs=[pl.BlockSpec((1, t, C), lambda i, j, s: (j, i, 0)),
                           pl.BlockSpec((t, C), lambda i, j, s: (i, 0))]),
            out_shape=[jax.ShapeDtypeStruct((N_CHIPS, H, C), BF16), jax.ShapeDtypeStruct((H, C), F32)],
            compiler_params=_cp("parallel", "arbitrary"))(where, g, xh))
    return [o[0] for o in outs], [o[1] for o in outs]


def chip_sum(owns, recvs):
    outs = []
    for own, rc in zip(owns, recvs):
        H, C = own.shape
        t = _row_tile(H, C)

        def body(o_ref, r_ref, t_ref):
            t_ref[...] = ((o_ref[...] + r_ref[0].astype(F32)) + r_ref[1].astype(F32)) + r_ref[2].astype(F32)

        outs.append(_pcall(
            body, name="chip_sum", grid=(H // t,),
            in_specs=[pl.BlockSpec((t, C), lambda i: (i, 0)), pl.BlockSpec((3, t, C), lambda i: (0, i, 0))],
            out_specs=pl.BlockSpec((t, C), lambda i: (i, 0)),
            out_shape=jax.ShapeDtypeStruct((H, C), F32), compiler_params=_cp("parallel"))(own, rc))
    return outs


MESH = pl.DeviceIdType.MESH
ANY = pl.BlockSpec(memory_space=pl.ANY)
SPLIT_MIN_BYTES = 1 << 20


def _other_chips(x, y):
    return [(1 - x, y), (x, 1 - y), (1 - x, 1 - y)]


def _half_rows(rows, who):
    half = rows // 2
    return pl.ds(pl.multiple_of(who * half, 16), half)


def gather_all(shards):
    n = len(shards)
    rows = [s.shape[0] for s in shards]
    split = [r % 32 == 0 and s.size * s.dtype.itemsize >= SPLIT_MIN_BYTES for r, s in zip(rows, shards)]

    def body(*refs):
        ins, outs = refs[:n], refs[n:2 * n]
        ici_send, ici_recv, d2d_send, d2d_recv = refs[2 * n:]
        x, y, c = lax.axis_index("x"), lax.axis_index("y"), lax.axis_index("c")
        me, sib, chips = 2 * x + y, (x, y, 1 - c), _other_chips(x, y)

        def part(k, who):
            return _half_rows(rows[k], who) if split[k] else pl.ds(0, rows[k])

        def ici(k, r, block):
            px, py = chips[r]
            return pltpu.make_async_remote_copy(
                src_ref=ins[k].at[part(k, c)], dst_ref=outs[k].at[block, part(k, c)],
                send_sem=ici_send.at[3 * k + r], recv_sem=ici_recv.at[3 * k + r],
                device_id=(px, py, c), device_id_type=MESH)

        def d2d(k, r, who):
            px, py = chips[r]
            blk = outs[k].at[2 * px + py, part(k, who)]
            return pltpu.make_async_remote_copy(
                src_ref=blk, dst_ref=blk, send_sem=d2d_send.at[3 * k + r], recv_sem=d2d_recv.at[3 * k + r],
                device_id=sib, device_id_type=MESH)

        sends = [ici(k, r, me) for k in range(n) for r in range(3)]
        for cp in sends:
            cp.start()
        for k in range(n):
            for r in range(3):
                px, py = chips[r]
                ici(k, r, 2 * px + py).wait_recv()
                if split[k]:
                    fwd = d2d(k, r, c)
                    fwd.start()
                    sends.append(fwd)
        for k in range(n):
            for r in range(3):
                if split[k]:
                    d2d(k, r, 1 - c).wait_recv()
        for cp in sends:
            cp.wait_send()

    return _pcall(
        body, name="gather_all", in_specs=[ANY] * n, out_specs=[ANY] * n,
        out_shape=[jax.ShapeDtypeStruct((N_CHIPS,) + s.shape, s.dtype) for s in shards],
        scratch_shapes=[pltpu.SemaphoreType.DMA((3 * n,))] * 4)(*shards)


def swap_halves(gstacks):
    n = len(gstacks)

    def body(*refs):
        ins, outs, send_sems, recv_sems = refs[:n], refs[n:2 * n], refs[2 * n], refs[2 * n + 1]
        x, y, c = lax.axis_index("x"), lax.axis_index("y"), lax.axis_index("c")
        copies = [pltpu.make_async_remote_copy(
            src_ref=ins[k].at[:, _half_rows(ins[k].shape[1], 1 - c)], dst_ref=outs[k],
            send_sem=send_sems.at[k], recv_sem=recv_sems.at[k], device_id=(x, y, 1 - c), device_id_type=MESH)
            for k in range(n)]
        for cp in copies:
            cp.start()
        for cp in copies:
            cp.wait()

    return _pcall(
        body, name="swap_halves", in_specs=[ANY] * n, out_specs=[ANY] * n,
        out_shape=[jax.ShapeDtypeStruct((g.shape[0], g.shape[1] // 2, g.shape[2]), g.dtype) for g in gstacks],
        scratch_shapes=[pltpu.SemaphoreType.DMA((n,)), pltpu.SemaphoreType.DMA((n,))])(*gstacks)


def scatter_chips(stacks):
    n = len(stacks)

    def body(*refs):
        ins, outs, send_sems, recv_sems = refs[:n], refs[n:2 * n], refs[2 * n], refs[2 * n + 1]
        x, y, c = lax.axis_index("x"), lax.axis_index("y"), lax.axis_index("c")
        copies = [pltpu.make_async_remote_copy(
            src_ref=ins[k].at[2 * px + py], dst_ref=outs[k].at[r], send_sem=send_sems.at[3 * k + r],
            recv_sem=recv_sems.at[3 * k + r], device_id=(px, py, c), device_id_type=MESH)
            for k in range(n) for r, (px, py) in enumerate(_other_chips(x, y))]
        for cp in copies:
            cp.start()
        for cp in copies:
            cp.wait()

    return _pcall(
        body, name="scatter_chips", in_specs=[ANY] * n, out_specs=[ANY] * n,
        out_shape=[jax.ShapeDtypeStruct((3,) + s.shape[1:], s.dtype) for s in stacks],
        scratch_shapes=[pltpu.SemaphoreType.DMA((3 * n,)), pltpu.SemaphoreType.DMA((3 * n,))])(*stacks)


def swap_totals(totals):
    n = len(totals)

    def body(*refs):
        ins, outs, send_sems, recv_sems = refs[:n], refs[n:2 * n], refs[2 * n], refs[2 * n + 1]
        x, y, c = lax.axis_index("x"), lax.axis_index("y"), lax.axis_index("c")
        copies = [pltpu.make_async_remote_copy(
            src_ref=ins[k], dst_ref=outs[k], send_sem=send_sems.at[k], recv_sem=recv_sems.at[k],
            device_id=(x, y, 1 - c), device_id_type=MESH) for k in range(n)]
        for cp in copies:
            cp.start()
        for cp in copies:
            cp.wait()

    return _pcall(
        body, name="swap_totals", in_specs=[ANY] * n, out_specs=[ANY] * n,
        out_shape=[jax.ShapeDtypeStruct(t.shape, t.dtype) for t in totals],
        scratch_shapes=[pltpu.SemaphoreType.DMA((n,)), pltpu.SemaphoreType.DMA((n,))])(*totals)


def place_own(gathered, own):
    R, C = own.shape
    t = _row_tile(R, C)
    where = (2 * lax.axis_index("x") + lax.axis_index("y")).astype(jnp.int32).reshape(1)

    def body(s_ref, own_ref, g_ref, o_ref):
        o_ref[0] = own_ref[...]

    return _pcall(
        body, name="place_own",
        grid_spec=pltpu.PrefetchScalarGridSpec(
            num_scalar_prefetch=1, grid=(R // t,), in_specs=[pl.BlockSpec((t, C), lambda i, s: (i, 0)), ANY],
            out_specs=pl.BlockSpec((1, t, C), lambda i, s: (s[0], i, 0))),
        out_shape=jax.ShapeDtypeStruct(gathered.shape, gathered.dtype), input_output_aliases={2: 0},
        compiler_params=_cp("parallel"))(where, own, gathered)


def reduce_scatter(gstacks):
    halves = swap_halves(gstacks)
    payload, own = pair_sum(gstacks, halves)
    recv = scatter_chips(payload)
    mine = chip_sum(own, recv)
    return mine, swap_totals(mine)


def allreduce_small(vec):
    R = vec.shape[0]

    def body(in_ref, out_ref, buf, send_sems, recv_sems):
        x, y, c = lax.axis_index("x"), lax.axis_index("y"), lax.axis_index("c")
        me = 4 * x + 2 * y + c
        buf[me] = in_ref[...]
        copies = []
        for k in range(1, 8):
            peer = (x ^ (k >> 2), y ^ ((k >> 1) & 1), c ^ (k & 1))
            copies.append(pltpu.make_async_remote_copy(
                src_ref=in_ref, dst_ref=buf.at[me], send_sem=send_sems.at[k - 1], recv_sem=recv_sems.at[k - 1],
                device_id=peer, device_id_type=MESH))
        for cp in copies:
            cp.start()
        for cp in copies:
            cp.wait()
        acc = buf[0]
        for d in range(1, 8):
            acc = acc + buf[d]
        out_ref[...] = acc

    vm = pl.BlockSpec(memory_space=pltpu.VMEM)
    return _pcall(
        body, name="allreduce_small", in_specs=[vm], out_specs=vm,
        out_shape=jax.ShapeDtypeStruct((R, LANES), F32),
        scratch_shapes=[pltpu.VMEM((8, R, LANES), F32), pltpu.SemaphoreType.DMA((7,)), pltpu.SemaphoreType.DMA((7,))])(vec)


MATMUL_SHARDED = [("pool_in", 1), ("pool_group", 2), ("ssd_in", 2), ("ssd_out", 1), ("sb_qkv", 2), ("sb_out", 1),
                  ("ffn_gate", 2), ("ffn_up", 2), ("ffn_down", 1)]
STACKED = ["ffn_gate", "ffn_up", "ffn_down"]
SMALL_SHARDED = [("pool_scale", 1), ("ssd_conv_w", 2)]
REPLICATED = ["mix_norm", "ssd_conv_b", "ssd_dt_bias", "ssd_a_log", "ssd_d", "ssd_out_norm", "sb_q_norm",
              "sb_k_norm", "ffn_norm"]
WEIGHT_ORDER = ["mix_norm", "pool_in", "pool_group", "pool_scale", "ssd_in", "ssd_conv_w", "ssd_conv_b",
                "ssd_dt_bias", "ssd_a_log", "ssd_d", "ssd_out_norm", "ssd_out", "sb_qkv", "sb_q_norm", "sb_k_norm",
                "sb_out", "ffn_norm", "ffn_gate", "ffn_up", "ffn_down"]


def _piece_rows(n, mult):
    rows = -(-n // LANES)
    return -(-rows // mult) * mult


def _as_rows(a, mult):
    flat = a.reshape(-1)
    rows = _piece_rows(flat.shape[0], mult)
    if rows * LANES != flat.shape[0]:
        flat = jnp.pad(flat, (0, rows * LANES - flat.shape[0]))
    return flat.reshape(rows, LANES)


def _pack(arrs, mult=8, row_pad=8):
    parts = [_as_rows(a, mult) for a in arrs]
    rows = sum(p.shape[0] for p in parts)
    pad = -rows % row_pad
    if pad:
        parts.append(jnp.zeros((pad, LANES), parts[0].dtype))
    return jnp.concatenate(parts, axis=0)


def _unpack(packed, shapes, mult=8, lead=()):
    out, off = [], 0
    for s in shapes:
        n = math.prod(s)
        rows = _piece_rows(n, mult)
        piece = packed[..., off:off + rows, :].reshape(lead + (rows * LANES,))
        out.append(piece[..., :n].reshape(lead + tuple(s)))
        off += rows
    return out


def _rows2d(a):
    return a.reshape(-1, a.shape[-1])


def _gather_weights(shards):
    own = [_rows2d(shards[n].astype(BF16)) for n, _ in MATMUL_SHARDED]
    small = _pack([shards[n] for n, _ in SMALL_SHARDED])
    gathered = gather_all(own + [small])
    me = 2 * lax.axis_index("x") + lax.axis_index("y")

    def whole(got, mine, ax):
        return jnp.concatenate([jnp.where(me == j, mine, got[j]) for j in range(N_CHIPS)], axis=ax)

    full = {}
    for (n, ax), got, mine in zip(MATMUL_SHARDED, gathered, own):
        if n in STACKED:
            full[n] = place_own(got, mine)
        else:
            shp = shards[n].shape
            full[n] = whole(got.reshape((N_CHIPS,) + shp), mine.reshape(shp), ax)
    pieces = _unpack(gathered[-1], [shards[n].shape for n, _ in SMALL_SHARDED], lead=(N_CHIPS,))
    for (n, ax), got in zip(SMALL_SHARDED, pieces):
        full[n] = whole(got, shards[n], ax)
    return full


def _split_shards(full, axis):
    return jnp.stack(jnp.split(full, N_CHIPS, axis=axis))


def _ffn_fwd(x, gain, wg4, wu4, wd4, layer):
    h = rmsnorm_fwd(x, gain, name="ffn_norm_fwd")
    a4, b4, hid4 = ffn_up(h, wg4, wu4, layer)
    xo = ffn_down(hid4, wd4, layer, x)
    return xo, (x, h, a4, b4, hid4)


def _ffn_bwd(dout, saved, gain, wg4, wu4, wd4, layer, gbufs):
    x, h, a4, b4, hid4 = saved
    da4, db4 = ffn_bwd_hidden(dout, wd4, layer, a4, b4)
    dwd4 = ffn_wgrad_out(hid4, dout, layer, None if gbufs is None else gbufs[2])
    dwg4, dwu4 = ffn_wgrad_in(h, da4, db4, layer, None if gbufs is None else gbufs[:2])
    dh = ffn_dh(da4, db4, wg4, wu4, layer)
    dx, dgain = rmsnorm_bwd(x, gain, dh, dout, name="ffn_norm_bwd")
    return dx, dgain, (dwg4, dwu4, dwd4)


def _pool_layer_fwd(x, gain, w_in, wgrp, scale):
    h = rmsnorm_fwd(x, gain, name="pool_norm_fwd")
    u = linear([(h, w_in, "nn")], name="pool_in")
    xo, p = pool_fwd(u, wgrp, scale, x)
    return xo, (x, h, p)


def _pool_layer_bwd(dout, saved, gain, w_in, wgrp, scale):
    x, h, p = saved
    dp, dwgrp, dscale = pool_bwd_group(dout, p, wgrp, scale)
    du = pool_bwd_window(dp)
    (dw_in,) = wgrad(h, [du], name="pool_dwin")
    dh = linear([(du, w_in, "nt")], name="pool_dh")
    dx, dgain = rmsnorm_bwd(x, gain, dh, dout, name="pool_norm_bwd")
    return dx, dgain, dw_in, dwgrp, dscale


def _ssd_layer_fwd(x, gain, w_z, w_xbc, w_dt, conv_w, conv_b, dt_bias, a_log, d_full, out_norm, w_out):
    h = rmsnorm_fwd(x, gain, name="ssd_norm_fwd")
    z = linear([(h, w_z, "nn")], name="ssd_in_z")
    xbc = linear([(h, w_xbc, "nn")], tn=2048, name="ssd_in_xbc")
    dt_raw = linear([(h, w_dt, "nn")], name="ssd_in_dt")
    act = conv_fwd(xbc, conv_w, conv_b)
    y, states = ssd_scan_fwd(act, dt_raw, dt_bias, a_log, d_full)
    gn = gate_norm_fwd(y, z, out_norm)
    xo = linear([(gn, w_out, "nn")], res=x, name="ssd_out")
    return xo, (x, h, z, xbc, dt_raw, act, y, states, gn)


def _ssd_layer_bwd(dout, saved, gain, w_z, w_xbc, w_dt, conv_w, conv_b, dt_bias, a_log, d_full, out_norm, w_out):
    x, h, z, xbc, dt_raw, act, y, states, gn = saved
    dgn = linear([(dout, w_out, "nt")], name="ssd_dgn")
    (dw_out,) = wgrad(gn, [dout], name="ssd_dwout")
    dy, dz, dout_norm = gate_norm_bwd(dgn, y, z, out_norm)
    dact, ddt_raw, dbias, dalog, dd_full = ssd_scan_bwd(dy, act, dt_raw, dt_bias, a_log, d_full, states)
    dpre, dconv_w8, dconv_b = conv_bwd_pre(dact, xbc, conv_w, conv_b)
    dxbc = conv_bwd_input(dpre, conv_w)
    ddt_b = ddt_raw.astype(BF16)
    (dw_z,) = wgrad(h, [dz], name="ssd_dwz")
    (dw_xbc,) = wgrad(h, [dxbc], tn=2048, name="ssd_dwxbc")
    (dw_dt,) = wgrad(h, [ddt_b], name="ssd_dwdt")
    dh = linear([(dz, w_z, "nt"), (dxbc, w_xbc, "nt"), (ddt_b, w_dt, "nt")], tm=256, name="ssd_dh")
    dx, dgain = rmsnorm_bwd(x, gain, dh, dout, name="ssd_norm_bwd")
    dw_in = jnp.concatenate([dw_z, dw_xbc, dw_dt], axis=1)
    dd = dd_full.reshape(NH, HP).sum(axis=1).reshape(1, NH)
    return dx, dgain, dw_in, dconv_w8[:4], dconv_b, dbias, dalog, dd, dout_norm, dw_out


def _sb_layer_fwd(x, gain, w_qkv, qg, kg, w_out):
    h = rmsnorm_fwd(x, gain, name="sb_norm_fwd")
    qkv = linear([(h, w_qkv, "nn")], tn=1024, name="sb_qkv")
    qs, kn, v = sb_prep_fwd(qkv, qg, kg)
    o, rsave = sb_fwd(qs, kn, v)
    xo = linear([(o, w_out, "nn")], res=x, name="sb_out")
    return xo, (x, h, qkv, qs, kn, v, o, rsave)


def _sb_layer_bwd(dout, saved, gain, w_qkv, qg, kg, w_out):
    x, h, qkv, qs, kn, v, o, rsave = saved
    do = linear([(dout, w_out, "nt")], out_dtype=BF16, name="sb_do")
    (dw_out,) = wgrad(o, [dout], name="sb_dwout")
    dqs, dkn, dv = sb_bwd(qs, kn, v, do, rsave)
    dqkv, dqg, dkg = sb_prep_bwd(dqs, dkn, dv, qkv, qg, kg)
    (dw_qkv,) = wgrad(h, [dqkv], tn=1024, name="sb_dwqkv")
    dh = linear([(dqkv, w_qkv, "nt")], name="sb_dh")
    dx, dgain = rmsnorm_bwd(x, gain, dh, dout, name="sb_norm_bwd")
    dqg = dqg.reshape(SBH, SBD).sum(axis=0).reshape(1, SBD)
    dkg = dkg.reshape(SBH, SBD).sum(axis=0).reshape(1, SBD)
    return dx, dgain, dw_qkv, dqg, dkg, dw_out


def _local_step(x, target, full, rep):
    S = x.shape[0]
    d_full = jnp.repeat(rep["ssd_d"][0], HP).reshape(1, DI)
    qg = jnp.tile(rep["sb_q_norm"][0], 2).reshape(1, LANES)
    kg = jnp.tile(rep["sb_k_norm"][0], 2).reshape(1, LANES)
    ssd_in = full["ssd_in"][0]
    w_z, w_xbc, w_dt = ssd_in[:, :DI], ssd_in[:, DI:DI + CONV_CH], ssd_in[:, DI + CONV_CH:]
    conv_w = full["ssd_conv_w"][0]
    conv_b = rep["ssd_conv_b"]
    pool_scale = full["pool_scale"]

    def mixer_args(i):
        kind, j = i % 3, i // 3
        if kind == 0:
            return (full["pool_in"][j], full["pool_group"][j], pool_scale[j:j + 1])
        if kind == 1:
            return (w_z, w_xbc, w_dt, conv_w, conv_b, rep["ssd_dt_bias"], rep["ssd_a_log"], d_full,
                    rep["ssd_out_norm"], full["ssd_out"][0])
        return (full["sb_qkv"][0], qg, kg, full["sb_out"][0])

    fwd = (_pool_layer_fwd, _ssd_layer_fwd, _sb_layer_fwd)
    bwd = (_pool_layer_bwd, _ssd_layer_bwd, _sb_layer_bwd)
    saved = []
    for i in range(DEPTH):
        x, sm = fwd[i % 3](x, rep["mix_norm"][i], *mixer_args(i))
        x, sf = _ffn_fwd(x, rep["ffn_norm"][i], full["ffn_gate"], full["ffn_up"], full["ffn_down"], i)
        saved.append((sm, sf))

    colsq, dx = loss_head(x, target)
    loss = 0.5 * jnp.sum(colsq) / D

    g = {n: [None] * DEPTH for n in ("mix_norm", "ffn_norm")}
    ffn_g = None
    g["pool_in"], g["pool_group"], g["pool_scale"] = [None] * 2, [None] * 2, [None] * 2
    for i in reversed(range(DEPTH)):
        sm, sf = saved[i]
        dx, g["ffn_norm"][i], ffn_g = _ffn_bwd(
            dx, sf, rep["ffn_norm"][i], full["ffn_gate"], full["ffn_up"], full["ffn_down"], i, ffn_g)
        kind, j = i % 3, i // 3
        res = bwd[kind](dx, sm, rep["mix_norm"][i], *mixer_args(i))
        dx, g["mix_norm"][i] = res[0], res[1]
        if kind == 0:
            g["pool_in"][j], g["pool_group"][j], g["pool_scale"][j] = res[2:]
        elif kind == 1:
            dw_in, dconv_w, dconv_b, dbias, dalog, dd, don, dw_out = res[2:]
            g.update(ssd_in=dw_in[None], ssd_conv_w=dconv_w[None], ssd_conv_b=dconv_b, ssd_dt_bias=dbias,
                     ssd_a_log=dalog, ssd_d=dd, ssd_out_norm=don, ssd_out=dw_out[None])
        else:
            dw_qkv, dqg, dkg, dw_out = res[2:]
            g.update(sb_qkv=dw_qkv[None], sb_q_norm=dqg, sb_k_norm=dkg, sb_out=dw_out[None])
    for n in ("mix_norm", "ffn_norm", "pool_scale"):
        g[n] = jnp.concatenate(g[n], axis=0)
    for n in ("pool_in", "pool_group"):
        g[n] = jnp.stack(g[n])
    g["ffn_gate"], g["ffn_up"], g["ffn_down"] = ffn_g
    return loss, dx, g


def kernel(x, mix_norm, pool_in, pool_group, pool_scale, ssd_in, ssd_conv_w, ssd_conv_b, ssd_dt_bias, ssd_a_log, ssd_d, ssd_out_norm, ssd_out, sb_qkv, sb_q_norm, sb_k_norm, sb_out, ffn_norm, ffn_gate, ffn_up, ffn_down, loss_target, m_mix_norm, m_pool_in, m_pool_group, m_pool_scale, m_ssd_in, m_ssd_conv_w, m_ssd_conv_b, m_ssd_dt_bias, m_ssd_a_log, m_ssd_d, m_ssd_out_norm, m_ssd_out, m_sb_qkv, m_sb_q_norm, m_sb_k_norm, m_sb_out, m_ffn_norm, m_ffn_gate, m_ffn_up, m_ffn_down, v_mix_norm, v_pool_in, v_pool_group, v_pool_scale, v_ssd_in, v_ssd_conv_w, v_ssd_conv_b, v_ssd_dt_bias, v_ssd_a_log, v_ssd_d, v_ssd_out_norm, v_ssd_out, v_sb_qkv, v_sb_q_norm, v_sb_k_norm, v_sb_out, v_ffn_norm, v_ffn_gate, v_ffn_up, v_ffn_down):
    given = dict(locals())
    w = {n: given[n] for n in WEIGHT_ORDER}
    m = {n: given["m_" + n] for n in WEIGHT_ORDER}
    v = {n: given["v_" + n] for n in WEIGHT_ORDER}
    full = _gather_weights(w)
    rep = {n: w[n] for n in REPLICATED}

    loss, dx, g = _local_step(x[0], loss_target[0], full, rep)
    loss = lax.psum(loss, ("x", "y", "c"))
    out = {}

    gstacks = [g[n] if n in STACKED else _split_shards(g[n], ax).reshape((N_CHIPS,) + _rows2d(w[n]).shape)
               for n, ax in MATMUL_SHARDED]
    mine, other = reduce_scatter(gstacks)
    for (n, _), g_mine, g_other in zip(MATMUL_SHARDED, mine, other):
        res = adamw_halves(_rows2d(w[n]), g_mine, g_other, _rows2d(m[n]), _rows2d(v[n]), name="adamw_" + n)
        for key, a in zip("gdmv", res):
            out[key, n] = a.reshape(w[n].shape)

    small = REPLICATED + [n for n, _ in SMALL_SHARDED]
    gfull = _unpack(allreduce_small(_pack([g[n] for n in small])), [g[n].shape for n in small])
    me = 2 * lax.axis_index("x") + lax.axis_index("y")
    gsum = dict(zip(small, gfull))
    for n, ax in SMALL_SHARDED:
        gsum[n] = lax.dynamic_slice_in_dim(gsum[n], me * w[n].shape[ax], w[n].shape[ax], axis=ax)
    res = adamw(*[_pack([t[n] for n in small]) for t in (w, gsum, m, v)], name="adamw_small")
    for key, flat in zip("dmv", res):
        for n, a in zip(small, _unpack(flat, [w[n].shape for n in small])):
            out[key, n] = a
    for n in small:
        out["g", n] = gsum[n]

    return (loss, dx[None], *[out["g", n] for n in WEIGHT_ORDER], *[out["d", n] for n in WEIGHT_ORDER],
            *[out["m", n] for n in WEIGHT_ORDER], *[out["v", n] for n in WEIGHT_ORDER])
```

```python
import math

import jax
import jax.numpy as jnp
from jax import lax
from jax.experimental import pallas as pl
from jax.experimental.pallas import tpu as pltpu

F32 = jnp.float32
BF16 = jnp.bfloat16

D = 1024
DEPTH = 4
EPS = 1e-6
POOL_WINDOWS = (2, 4, 8, 16)
PG = 256
DI = 2048
NH = 32
HP = 64
NG = 8
NS = 128
GW = 256
CH = 256
CONV_CH = 4096
SSD_IN = 6176
SBH = 16
SBD = 64
FH = 2816
N_CHIPS = 4
LANES = 128

ADAM_LR = 0.001
ADAM_B1 = 0.9
ADAM_B2 = 0.999
ADAM_EPS = 1e-08
ADAM_WD = 0.01
ADAM_STEP = 10

VMEM_LIMIT = 56 * 1024 * 1024


def _pcall(body, **kw):
    return pl.pallas_call(body, **kw)


def _cp(*sem):
    return pltpu.CompilerParams(dimension_semantics=sem, vmem_limit_bytes=VMEM_LIMIT)


def _dot(a, b, prec=None):
    return lax.dot_general(a, b, (((1,), (0,)), ((), ())), precision=prec, preferred_element_type=F32)


def _dot_nt(a, b, prec=None):
    return lax.dot_general(a, b, (((1,), (1,)), ((), ())), precision=prec, preferred_element_type=F32)


def _dot_tn(a, b, prec=None):
    return lax.dot_general(a, b, (((0,), (0,)), ((), ())), precision=prec, preferred_element_type=F32)


def _split3(x):
    x1 = x.astype(BF16)
    r = x - x1.astype(F32)
    x2 = r.astype(BF16)
    return x1, x2, (r - x2.astype(F32)).astype(BF16)


def _sel(dot, x, mask, x_first=True):
    mb = mask.astype(BF16)
    p = [dot(xi, mb) if x_first else dot(mb, xi) for xi in _split3(x)]
    return (p[0] + p[1]) + p[2]


def _sigmoid(x):
    return 1.0 / (1.0 + jnp.exp(-x))


def _iota(shape, axis):
    return lax.broadcasted_iota(jnp.int32, shape, axis)


def linear(pairs, res=None, out_dtype=F32, tm=512, tn=None, name="linear"):
    M = pairs[0][0].shape[0]
    N = pairs[0][1].shape[1] if pairs[0][2] == "nn" else pairs[0][1].shape[0]
    tm = min(tm, M)
    tn = N if tn is None else min(tn, N)
    n_pairs = len(pairs)
    modes = [p[2] for p in pairs]

    def body(*refs):
        acc = None
        for k in range(n_pairs):
            a = refs[2 * k][...].astype(BF16)
            w = refs[2 * k + 1][...]
            t = _dot(a, w) if modes[k] == "nn" else _dot_nt(a, w)
            acc = t if acc is None else acc + t
        if res is not None:
            acc = acc + refs[2 * n_pairs][...]
        refs[-1][...] = acc.astype(out_dtype)

    in_specs, args = [], []
    for a, w, mode in pairs:
        K = a.shape[1]
        in_specs.append(pl.BlockSpec((tm, K), lambda j, i: (i, 0)))
        if mode == "nn":
            in_specs.append(pl.BlockSpec((K, tn), lambda j, i: (0, j)))
        else:
            in_specs.append(pl.BlockSpec((tn, K), lambda j, i: (j, 0)))
        args += [a, w]
    if res is not None:
        in_specs.append(pl.BlockSpec((tm, tn), lambda j, i: (i, j)))
        args.append(res)
    return _pcall(
        body, name=name, grid=(N // tn, M // tm), in_specs=in_specs,
        out_specs=pl.BlockSpec((tm, tn), lambda j, i: (i, j)),
        out_shape=jax.ShapeDtypeStruct((M, N), out_dtype),
        compiler_params=_cp("parallel", "arbitrary"))(*args)


def wgrad(a, gs, tk=1024, tn=None, tm=1024, name="wgrad"):
    M, Ka = a.shape
    N = gs[0].shape[1]
    tk, tm = min(tk, Ka), min(tm, M)
    tn = N if tn is None else min(tn, N)
    n_g = len(gs)

    def body(*refs):
        a_ref, g_refs, o_refs = refs[0], refs[1:1 + n_g], refs[1 + n_g:]
        m = pl.program_id(2)
        at = a_ref[...].astype(BF16)
        for g_ref, o_ref in zip(g_refs, o_refs):
            t = _dot_tn(at, g_ref[...].astype(BF16))

            @pl.when(m == 0)
            def _():
                o_ref[...] = t

            @pl.when(m > 0)
            def _():
                o_ref[...] += t

    out = _pcall(
        body, name=name, grid=(Ka // tk, N // tn, M // tm),
        in_specs=[pl.BlockSpec((tm, tk), lambda k, j, m: (m, k))]
        + [pl.BlockSpec((tm, tn), lambda k, j, m: (m, j))] * n_g,
        out_specs=[pl.BlockSpec((tk, tn), lambda k, j, m: (k, j))] * n_g,
        out_shape=[jax.ShapeDtypeStruct((Ka, N), F32)] * n_g,
        compiler_params=_cp("parallel", "parallel", "arbitrary"))(a, *gs)
    return out


def rmsnorm_fwd(x, gain, tm=512, name="rmsnorm_fwd"):
    S, Dm = x.shape
    tm = min(tm, S)

    def body(x_ref, g_ref, o_ref):
        xv = x_ref[...]
        r = lax.rsqrt(jnp.mean(xv * xv, axis=-1, keepdims=True) + EPS)
        o_ref[...] = (xv * r * g_ref[...]).astype(BF16)

    return _pcall(
        body, name=name, grid=(S // tm,),
        in_specs=[pl.BlockSpec((tm, Dm), lambda i: (i, 0)), pl.BlockSpec((1, Dm), lambda i: (0, 0))],
        out_specs=pl.BlockSpec((tm, Dm), lambda i: (i, 0)),
        out_shape=jax.ShapeDtypeStruct((S, Dm), BF16),
        compiler_params=_cp("parallel"))(x, gain.reshape(1, Dm))


def rmsnorm_bwd(x, gain, dh, dres, tm=512, name="rmsnorm_bwd"):
    S, Dm = x.shape
    tm = min(tm, S)

    def body(x_ref, g_ref, dh_ref, dr_ref, dx_ref, dg_ref):
        i = pl.program_id(0)
        xv = x_ref[...]
        r = lax.rsqrt(jnp.mean(xv * xv, axis=-1, keepdims=True) + EPS)
        y = xv * r
        dhv = dh_ref[...]
        dy = dhv * g_ref[...]
        dx_ref[...] = dr_ref[...] + r * (dy - y * jnp.mean(dy * y, axis=-1, keepdims=True))
        part = jnp.sum(dhv * y, axis=0, keepdims=True)

        @pl.when(i == 0)
        def _():
            dg_ref[...] = part

        @pl.when(i > 0)
        def _():
            dg_ref[...] += part

    return _pcall(
        body, name=name, grid=(S // tm,),
        in_specs=[pl.BlockSpec((tm, Dm), lambda i: (i, 0)), pl.BlockSpec((1, Dm), lambda i: (0, 0)),
                  pl.BlockSpec((tm, Dm), lambda i: (i, 0)), pl.BlockSpec((tm, Dm), lambda i: (i, 0))],
        out_specs=[pl.BlockSpec((tm, Dm), lambda i: (i, 0)), pl.BlockSpec((1, Dm), lambda i: (0, 0))],
        out_shape=[jax.ShapeDtypeStruct((S, Dm), F32), jax.ShapeDtypeStruct((1, Dm), F32)],
        compiler_params=_cp("arbitrary"))(x, gain.reshape(1, Dm), dh, dres)


FS = FH // N_CHIPS


def ffn_up(h, wg4, wu4, layer, tm=1024):
    S = h.shape[0]
    tm = min(tm, S)

    def body(h_ref, wg_ref, wu_ref, a_ref, b_ref, hid_ref):
        hv = h_ref[...]
        a = _dot(hv, wg_ref[0])
        b = _dot(hv, wu_ref[0])
        a_ref[0] = a.astype(BF16)
        b_ref[0] = b.astype(BF16)
        hid_ref[0] = (a * _sigmoid(a) * b).astype(BF16)

    wspec = pl.BlockSpec((1, D, FS), lambda j, i: (j, layer, 0))
    aspec = pl.BlockSpec((1, tm, FS), lambda j, i: (j, i, 0))
    return _pcall(
        body, name="ffn_up", grid=(N_CHIPS, S // tm),
        in_specs=[pl.BlockSpec((tm, D), lambda j, i: (i, 0)), wspec, wspec], out_specs=[aspec] * 3,
        out_shape=[jax.ShapeDtypeStruct((N_CHIPS, S, FS), BF16)] * 3,
        compiler_params=_cp("parallel", "arbitrary"))(h, wg4, wu4)


def ffn_down(hid4, wd4, layer, x, tm=1024):
    S = x.shape[0]
    tm = min(tm, S)

    def body(hid_ref, wd_ref, x_ref, o_ref):
        acc = x_ref[...]
        for j in range(N_CHIPS):
            acc = acc + _dot(hid_ref[j], wd_ref[j])
        o_ref[...] = acc

    return _pcall(
        body, name="ffn_down", grid=(S // tm,),
        in_specs=[pl.BlockSpec((N_CHIPS, tm, FS), lambda i: (0, i, 0)),
                  pl.BlockSpec((N_CHIPS, FS, D), lambda i: (0, layer, 0)), pl.BlockSpec((tm, D), lambda i: (i, 0))],
        out_specs=pl.BlockSpec((tm, D), lambda i: (i, 0)),
        out_shape=jax.ShapeDtypeStruct((S, D), F32), compiler_params=_cp("parallel"))(hid4, wd4, x)


def ffn_bwd_hidden(dout, wd4, layer, a4, b4, tm=1024):
    S = dout.shape[0]
    tm = min(tm, S)

    def body(do_ref, wd_ref, a_ref, b_ref, da_ref, db_ref):
        dhid = _dot_nt(do_ref[...].astype(BF16), wd_ref[0])
        av, bv = a_ref[0].astype(F32), b_ref[0].astype(F32)
        s = _sigmoid(av)
        da_ref[0] = (dhid * bv * (s * (1.0 + av * (1.0 - s)))).astype(BF16)
        db_ref[0] = (dhid * (av * s)).astype(BF16)

    aspec = pl.BlockSpec((1, tm, FS), lambda i, j: (j, i, 0))
    return _pcall(
        body, name="ffn_bwd_hidden", grid=(S // tm, N_CHIPS),
        in_specs=[pl.BlockSpec((tm, D), lambda i, j: (i, 0)), pl.BlockSpec((1, FS, D), lambda i, j: (j, layer, 0)),
                  aspec, aspec],
        out_specs=[aspec] * 2, out_shape=[jax.ShapeDtypeStruct((N_CHIPS, S, FS), BF16)] * 2,
        compiler_params=_cp("parallel", "arbitrary"))(dout, wd4, a4, b4)


def ffn_wgrad_in(h, da4, db4, layer, bufs, tm=2048):
    S = h.shape[0]
    tm = min(tm, S)

    def body(h_ref, da_ref, db_ref, *rest):
        dg_ref, du_ref = rest[-2:]
        m = pl.program_id(1)
        hv = h_ref[...]
        for g_ref, o_ref in ((da_ref, dg_ref), (db_ref, du_ref)):
            t = _dot_tn(hv, g_ref[0])

            @pl.when(m == 0)
            def _():
                o_ref[0] = t

            @pl.when(m > 0)
            def _():
                o_ref[0] += t

    aspec = pl.BlockSpec((1, tm, FS), lambda j, m: (j, m, 0))
    ospec = pl.BlockSpec((1, D, FS), lambda j, m: (j, layer, 0))
    kept = {} if bufs is None else dict(input_output_aliases={3: 0, 4: 1})
    return _pcall(
        body, name="ffn_wgrad_in", grid=(N_CHIPS, S // tm),
        in_specs=[pl.BlockSpec((tm, D), lambda j, m: (m, 0)), aspec, aspec] + ([] if bufs is None else [ANY, ANY]),
        out_specs=[ospec] * 2, out_shape=[jax.ShapeDtypeStruct((N_CHIPS, DEPTH * D, FS), F32)] * 2,
        compiler_params=_cp("parallel", "arbitrary"), **kept)(h, da4, db4, *(bufs or ()))


def ffn_wgrad_out(hid4, dout, layer, buf, tm=2048):
    S = dout.shape[0]
    tm = min(tm, S)

    def body(hid_ref, do_ref, *rest):
        o_ref = rest[-1]
        m = pl.program_id(1)
        t = _dot_tn(hid_ref[0], do_ref[...].astype(BF16))

        @pl.when(m == 0)
        def _():
            o_ref[0] = t

        @pl.when(m > 0)
        def _():
            o_ref[0] += t

    kept = {} if buf is None else dict(input_output_aliases={2: 0})
    return _pcall(
        body, name="ffn_wgrad_out", grid=(N_CHIPS, S // tm),
        in_specs=[pl.BlockSpec((1, tm, FS), lambda j, m: (j, m, 0)), pl.BlockSpec((tm, D), lambda j, m: (m, 0))]
        + ([] if buf is None else [ANY]),
        out_specs=pl.BlockSpec((1, FS, D), lambda j, m: (j, layer, 0)),
        out_shape=jax.ShapeDtypeStruct((N_CHIPS, DEPTH * FS, D), F32),
        compiler_params=_cp("parallel", "arbitrary"), **kept)(hid4, dout, *(() if buf is None else (buf,)))


def ffn_dh(da4, db4, wg4, wu4, layer, tm=512):
    S = da4.shape[1]
    tm = min(tm, S)

    def body(da_ref, db_ref, wg_ref, wu_ref, o_ref):
        acc = _dot_nt(da_ref[0], wg_ref[0]) + _dot_nt(db_ref[0], wu_ref[0])
        for j in range(1, N_CHIPS):
            acc = acc + _dot_nt(da_ref[j], wg_ref[j]) + _dot_nt(db_ref[j], wu_ref[j])
        o_ref[...] = acc

    aspec = pl.BlockSpec((N_CHIPS, tm, FS), lambda i: (0, i, 0))
    wspec = pl.BlockSpec((N_CHIPS, D, FS), lambda i: (0, layer, 0))
    return _pcall(
        body, name="ffn_dh", grid=(S // tm,), in_specs=[aspec, aspec, wspec, wspec],
        out_specs=pl.BlockSpec((tm, D), lambda i: (i, 0)),
        out_shape=jax.ShapeDtypeStruct((S, D), F32), compiler_params=_cp("parallel"))(da4, db4, wg4, wu4)


POOL_T = 256
POOL_HALO = 16


def pool_fwd(u, wgrp, scale, x_res):
    S = u.shape[0]
    T, HB = min(POOL_T, S), POOL_HALO
    per = T // HB

    def body(u_ref, tail_ref, wg_ref, sc_ref, x_ref, xo_ref, p_ref):
        i = pl.program_id(0)
        uc = u_ref[...]
        tail = jnp.where(i > 0, tail_ref[...], 0.0)
        d_cur = _iota((T, T), 0) - _iota((T, T), 1)
        d_tail = _iota((T, HB), 0) - _iota((T, HB), 1) + HB
        tg = i * T + _iota((T, 1), 0)
        for g, w in enumerate(POOL_WINDOWS):
            gs = slice(g * PG, (g + 1) * PG)
            band = (d_cur >= 0) & (d_cur < w)
            band_t = (d_tail >= 0) & (d_tail < w)
            ug = uc[:, gs]
            ws = _sel(_dot, ug, band, False) + _sel(_dot, tail[:, gs], band_t, False)
            cnt = jnp.minimum(tg + 1, w).astype(F32)
            pb = (ws / cnt - ug).astype(BF16)
            p_ref[:, gs] = pb
            xo_ref[:, gs] = x_ref[:, gs] + _dot(pb, wg_ref[g]) * sc_ref[:, gs]

    return _pcall(
        body, name="pool_fwd", grid=(S // T,),
        in_specs=[pl.BlockSpec((T, D), lambda i: (i, 0)),
                  pl.BlockSpec((HB, D), lambda i: (jnp.maximum(i * per - 1, 0), 0)),
                  pl.BlockSpec((4, PG, PG), lambda i: (0, 0, 0)), pl.BlockSpec((1, D), lambda i: (0, 0)),
                  pl.BlockSpec((T, D), lambda i: (i, 0))],
        out_specs=[pl.BlockSpec((T, D), lambda i: (i, 0))] * 2,
        out_shape=[jax.ShapeDtypeStruct((S, D), F32), jax.ShapeDtypeStruct((S, D), BF16)],
        compiler_params=_cp("parallel"))(u, u, wgrp, scale, x_res)


def pool_bwd_group(dm, p, wgrp, scale, tm=512):
    S = dm.shape[0]
    tm = min(tm, S)

    def body(dm_ref, p_ref, wg_ref, sc_ref, dp_ref, dwg_ref, dsc_ref):
        i = pl.program_id(0)

        @pl.when(i == 0)
        def _():
            dwg_ref[...] = jnp.zeros_like(dwg_ref)
            dsc_ref[...] = jnp.zeros_like(dsc_ref)

        for g in range(4):
            gs = slice(g * PG, (g + 1) * PG)
            dmg, pg, wg = dm_ref[:, gs], p_ref[:, gs], wg_ref[g]
            dsc_ref[:, gs] += jnp.sum(dmg * _dot(pg, wg), axis=0, keepdims=True)
            dy = (dmg * sc_ref[:, gs]).astype(BF16)
            dp_ref[:, gs] = _dot_nt(dy, wg)
            dwg_ref[g] += _dot_tn(pg, dy)

    return _pcall(
        body, name="pool_bwd_group", grid=(S // tm,),
        in_specs=[pl.BlockSpec((tm, D), lambda i: (i, 0)), pl.BlockSpec((tm, D), lambda i: (i, 0)),
                  pl.BlockSpec((4, PG, PG), lambda i: (0, 0, 0)), pl.BlockSpec((1, D), lambda i: (0, 0))],
        out_specs=[pl.BlockSpec((tm, D), lambda i: (i, 0)), pl.BlockSpec((4, PG, PG), lambda i: (0, 0, 0)),
                   pl.BlockSpec((1, D), lambda i: (0, 0))],
        out_shape=[jax.ShapeDtypeStruct((S, D), F32), jax.ShapeDtypeStruct((4, PG, PG), F32),
                   jax.ShapeDtypeStruct((1, D), F32)],
        compiler_params=_cp("arbitrary"))(dm, p, wgrp, scale)


def pool_bwd_window(dp):
    S = dp.shape[0]
    T, HB = min(POOL_T, S), POOL_HALO
    per = T // HB
    nt = S // T

    def body(dp_ref, nxt_ref, du_ref):
        i = pl.program_id(0)
        dc = dp_ref[...]
        nxt = jnp.where(i < nt - 1, nxt_ref[...], 0.0)
        d_cur = _iota((T, T), 1) - _iota((T, T), 0)
        d_nxt = _iota((T, HB), 1) - _iota((T, HB), 0) + T
        tg = i * T + _iota((T, 1), 0)
        tn_ = (i + 1) * T + _iota((HB, 1), 0)
        for g, w in enumerate(POOL_WINDOWS):
            gs = slice(g * PG, (g + 1) * PG)
            band = (d_cur >= 0) & (d_cur < w)
            band_n = (d_nxt >= 0) & (d_nxt < w)
            dcg = dc[:, gs]
            cur = dcg / jnp.minimum(tg + 1, w).astype(F32)
            nx = nxt[:, gs] / jnp.minimum(tn_ + 1, w).astype(F32)
            du_ref[:, gs] = (_sel(_dot, cur, band, False) + _sel(_dot, nx, band_n, False) - dcg).astype(BF16)

    return _pcall(
        body, name="pool_bwd_window", grid=(nt,),
        in_specs=[pl.BlockSpec((T, D), lambda i: (i, 0)),
                  pl.BlockSpec((HB, D), lambda i: (jnp.minimum((i + 1) * per, S // HB - 1), 0))],
        out_specs=pl.BlockSpec((T, D), lambda i: (i, 0)),
        out_shape=jax.ShapeDtypeStruct((S, D), BF16),
        compiler_params=_cp("parallel"))(dp, dp)


CONV_T = 256


def _shift_down(xc, prev8, j):
    if j == 0:
        return xc
    T = xc.shape[0]
    body = pltpu.roll(xc, j, 0)
    first = jnp.where(_iota((8, 1), 0) < j, pltpu.roll(prev8, j, 0), body[0:8])
    return jnp.concatenate([first, body[8:T]], axis=0)


def _shift_up(dc, next8, j):
    if j == 0:
        return dc
    T = dc.shape[0]
    body = pltpu.roll(dc, T - j, 0)
    last = jnp.where(_iota((8, 1), 0) + j < 8, body[T - 8:T], pltpu.roll(next8, 8 - j, 0))
    return jnp.concatenate([body[0:T - 8], last], axis=0)


def conv_fwd(xbc, conv_w, conv_b):
    S = xbc.shape[0]
    T = min(CONV_T, S)
    CB = 1024

    def body(x_ref, prev_ref, w_ref, b_ref, o_ref):
        i = pl.program_id(1)
        xc = x_ref[...]
        prev8 = jnp.where(i > 0, prev_ref[...], 0.0)
        pre = b_ref[...] + w_ref[3:4, :] * xc
        for j in range(1, 4):
            pre = pre + w_ref[3 - j:4 - j, :] * _shift_down(xc, prev8, j)
        o_ref[...] = pre * _sigmoid(pre)

    return _pcall(
        body, name="conv_fwd", grid=(CONV_CH // CB, S // T),
        in_specs=[pl.BlockSpec((T, CB), lambda c, i: (i, c)),
                  pl.BlockSpec((8, CB), lambda c, i: (jnp.maximum(i * (T // 8) - 1, 0), c)),
                  pl.BlockSpec((4, CB), lambda c, i: (0, c)), pl.BlockSpec((1, CB), lambda c, i: (0, c))],
        out_specs=pl.BlockSpec((T, CB), lambda c, i: (i, c)),
        out_shape=jax.ShapeDtypeStruct((S, CONV_CH), F32),
        compiler_params=_cp("parallel", "parallel"))(xbc, xbc, conv_w, conv_b)


def conv_bwd_pre(dact, xbc, conv_w, conv_b):
    S = xbc.shape[0]
    T = min(CONV_T, S)
    CB = 1024

    def body(da_ref, x_ref, prev_ref, w_ref, b_ref, dpre_ref, dw_ref, db_ref):
        i = pl.program_id(1)
        xc = x_ref[...]
        prev8 = jnp.where(i > 0, prev_ref[...], 0.0)
        sh = [_shift_down(xc, prev8, j) for j in range(4)]
        pre = b_ref[...] + w_ref[3:4, :] * sh[0]
        for j in range(1, 4):
            pre = pre + w_ref[3 - j:4 - j, :] * sh[j]
        s = _sigmoid(pre)
        dpre = da_ref[...] * (s * (1.0 + pre * (1.0 - s)))
        dpre_ref[...] = dpre
        rows = [jnp.sum(dpre * sh[3 - k], axis=0, keepdims=True) for k in range(4)]
        dw = jnp.concatenate(rows + [jnp.zeros((4, CB), F32)], axis=0)
        db = jnp.sum(dpre, axis=0, keepdims=True)

        @pl.when(i == 0)
        def _():
            dw_ref[...] = dw
            db_ref[...] = db

        @pl.when(i > 0)
        def _():
            dw_ref[...] += dw
            db_ref[...] += db

    return _pcall(
        body, name="conv_bwd_pre", grid=(CONV_CH // CB, S // T),
        in_specs=[pl.BlockSpec((T, CB), lambda c, i: (i, c)), pl.BlockSpec((T, CB), lambda c, i: (i, c)),
                  pl.BlockSpec((8, CB), lambda c, i: (jnp.maximum(i * (T // 8) - 1, 0), c)),
                  pl.BlockSpec((4, CB), lambda c, i: (0, c)), pl.BlockSpec((1, CB), lambda c, i: (0, c))],
        out_specs=[pl.BlockSpec((T, CB), lambda c, i: (i, c)), pl.BlockSpec((8, CB), lambda c, i: (0, c)),
                   pl.BlockSpec((1, CB), lambda c, i: (0, c))],
        out_shape=[jax.ShapeDtypeStruct((S, CONV_CH), F32), jax.ShapeDtypeStruct((8, CONV_CH), F32),
                   jax.ShapeDtypeStruct((1, CONV_CH), F32)],
        compiler_params=_cp("parallel", "arbitrary"))(dact, xbc, xbc, conv_w, conv_b)


def conv_bwd_input(dpre, conv_w):
    S = dpre.shape[0]
    T = min(CONV_T, S)
    CB = 1024
    nt = S // T

    def body(d_ref, nxt_ref, w_ref, o_ref):
        i = pl.program_id(1)
        dc = d_ref[...]
        next8 = jnp.where(i < nt - 1, nxt_ref[...], 0.0)
        acc = w_ref[3:4, :] * dc
        for j in range(1, 4):
            acc = acc + w_ref[3 - j:4 - j, :] * _shift_up(dc, next8, j)
        o_ref[...] = acc.astype(BF16)

    return _pcall(
        body, name="conv_bwd_input", grid=(CONV_CH // CB, nt),
        in_specs=[pl.BlockSpec((T, CB), lambda c, i: (i, c)),
                  pl.BlockSpec((8, CB), lambda c, i: (jnp.minimum((i + 1) * (T // 8), S // 8 - 1), c)),
                  pl.BlockSpec((4, CB), lambda c, i: (0, c))],
        out_specs=pl.BlockSpec((T, CB), lambda c, i: (i, c)),
        out_shape=jax.ShapeDtypeStruct((S, CONV_CH), BF16),
        compiler_params=_cp("parallel", "parallel"))(dpre, dpre, conv_w)


def _ssd_chunk_terms(dt_ref, bias_ref, alog_ref):
    L = CH
    dtp = dt_ref[...] + bias_ref[...]
    dt = jnp.maximum(dtp, 0.0) + jnp.log(1.0 + jnp.exp(-jnp.abs(dtp)))
    a = -jnp.exp(alog_ref[...])
    da = dt * a
    tri = _iota((L, L), 0) >= _iota((L, L), 1)
    acum = _sel(_dot, da, tri, False)
    triu = _iota((L, L), 0) <= _iota((L, L), 1)
    acum_row = _sel(_dot_tn, da, triu)
    expand = _iota((NH, DI), 1) // HP == _iota((NH, DI), 0)
    acum_full = _sel(_dot, acum, expand)
    e_full = jnp.exp(acum_full)
    w_full = jnp.exp(acum_full[L - 1:L, :] - acum_full)
    dt_full = _sel(_dot, dt, expand)
    return dtp, dt, a, acum, acum_row, expand, triu, e_full, w_full, dt_full


def ssd_scan_fwd(xbc_act, dt_raw, dt_bias, a_log, d_full):
    S = xbc_act.shape[0]
    L = CH
    nc = S // L

    def body(xs_ref, b_ref, c_ref, dt_ref, bias_ref, alog_ref, d_ref, y_ref, st_ref, state):
        c = pl.program_id(0)

        @pl.when(c == 0)
        def _():
            state[...] = jnp.zeros_like(state)

        st_ref[0] = state[...]
        _, _, _, acum, acum_row, _, _, e_full, w_full, dt_full = _ssd_chunk_terms(dt_ref, bias_ref, alog_ref)
        causal = _iota((L, L), 0) >= _iota((L, L), 1)
        lane_head = _iota((1, GW), 1) // HP
        for g in range(NG):
            gs = slice(g * GW, (g + 1) * GW)
            ns = slice(g * NS, (g + 1) * NS)
            xs_g = xs_ref[:, gs]
            xdt_g = xs_g * dt_full[:, gs]
            cg = c_ref[:, ns].astype(BF16)
            bg = b_ref[:, ns].astype(BF16)
            gmat = _dot_nt(cg, bg)
            yg = jnp.zeros((L, GW), F32)
            for hh in range(4):
                h = 4 * g + hh
                diff = acum[:, h:h + 1] - acum_row[h:h + 1, :]
                dk = jnp.exp(jnp.where(causal, diff, -1e30))
                xm = jnp.where(lane_head == hh, xdt_g, 0.0).astype(BF16)
                yg = yg + _dot((gmat * dk).astype(BF16), xm)
            sg = state[g]
            yoff = _dot(cg, sg.astype(BF16)) * e_full[:, gs]
            y_ref[:, gs] = yg + yoff + d_ref[:, gs] * xs_g
            state[g] = sg * e_full[L - 1:L, gs] + _dot_tn(bg, (w_full[:, gs] * xdt_g).astype(BF16))

    return _pcall(
        body, name="ssd_scan_fwd", grid=(nc,),
        in_specs=[pl.BlockSpec((L, DI), lambda c: (c, 0)), pl.BlockSpec((L, 1024), lambda c: (c, 2)),
                  pl.BlockSpec((L, 1024), lambda c: (c, 3)), pl.BlockSpec((L, NH), lambda c: (c, 0)),
                  pl.BlockSpec((1, NH), lambda c: (0, 0)), pl.BlockSpec((1, NH), lambda c: (0, 0)),
                  pl.BlockSpec((1, DI), lambda c: (0, 0))],
        out_specs=[pl.BlockSpec((L, DI), lambda c: (c, 0)), pl.BlockSpec((1, NG, NS, GW), lambda c: (c, 0, 0, 0))],
        out_shape=[jax.ShapeDtypeStruct((S, DI), F32), jax.ShapeDtypeStruct((nc, NG, NS, GW), F32)],
        scratch_shapes=[pltpu.VMEM((NG, NS, GW), F32)],
        compiler_params=_cp("arbitrary"))(xbc_act, xbc_act, xbc_act, dt_raw, dt_bias, a_log, d_full)


def ssd_scan_bwd(dy, xbc_act, dt_raw, dt_bias, a_log, d_full, states):
    S = xbc_act.shape[0]
    L = CH
    nc = S // L

    def body(dy_ref, xs_ref, b_ref, c_ref, dt_ref, bias_ref, alog_ref, d_ref, st_ref,
             dxbc_ref, ddt_ref, dbias_ref, dalog_ref, dd_ref, dstate):
        c = pl.program_id(0)

        @pl.when(c == 0)
        def _():
            dstate[...] = jnp.zeros_like(dstate)
            dbias_ref[...] = jnp.zeros_like(dbias_ref)
            dalog_ref[...] = jnp.zeros_like(dalog_ref)
            dd_ref[...] = jnp.zeros_like(dd_ref)

        dtp, dt, a, acum, acum_row, expand, triu, e_full, w_full, dt_full = _ssd_chunk_terms(
            dt_ref, bias_ref, alog_ref)
        causal = _iota((L, L), 0) >= _iota((L, L), 1)
        lane_head = _iota((1, GW), 1) // HP
        head_id = _iota((1, NH), 1)
        head_row = _iota((NH, 1), 0)
        dacum = jnp.zeros((L, NH), F32)
        dacum_t = jnp.zeros((NH, L), F32)
        red_parts = []
        dxdt_parts = []
        alast_parts = []
        for g in range(NG):
            gs = slice(g * GW, (g + 1) * GW)
            ns = slice(g * NS, (g + 1) * NS)
            xs_g = xs_ref[:, gs]
            xdt_g = xs_g * dt_full[:, gs]
            dy_g = dy_ref[:, gs]
            cg = c_ref[:, ns].astype(BF16)
            bg = b_ref[:, ns].astype(BF16)
            gmat = _dot_nt(cg, bg)
            sg = st_ref[0, g]
            dsg = dstate[g]
            sgb, dsgb = sg.astype(BF16), dsg.astype(BF16)
            cs = _dot(cg, sgb)
            bds = _dot(bg, dsgb)
            e_g, w_g = e_full[:, gs], w_full[:, gs]
            dxdt = w_g * bds
            dgsum = jnp.zeros((L, L), F32)
            for hh in range(4):
                h = 4 * g + hh
                hm = lane_head == hh
                diff = acum[:, h:h + 1] - acum_row[h:h + 1, :]
                dk = jnp.exp(jnp.where(causal, diff, -1e30))
                m = gmat * dk
                dym = jnp.where(hm, dy_g, 0.0).astype(BF16)
                xm = jnp.where(hm, xdt_g, 0.0).astype(BF16)
                dm = _dot_nt(dym, xm)
                dxdt = dxdt + _dot_tn(m.astype(BF16), dym)
                dgsum = dgsum + dm * dk
                em = dm * m
                dacum = dacum + jnp.sum(em, axis=1, keepdims=True) * (head_id == h).astype(F32)
                dacum_t = dacum_t + (head_row == h).astype(F32) * jnp.sum(em, axis=0, keepdims=True)
            dgb = dgsum.astype(BF16)
            edy = (e_g * dy_g).astype(BF16)
            wx = (w_g * xdt_g).astype(BF16)
            dc_g = _dot(dgb, bg) + _dot_nt(edy, sgb)
            db_g = _dot_tn(dgb, cg) + _dot_nt(wx, dsgb)
            dxbc_ref[:, DI + g * NS:DI + (g + 1) * NS] = db_g
            dxbc_ref[:, DI + 1024 + g * NS:DI + 1024 + (g + 1) * NS] = dc_g
            p2w = bds * xdt_g * w_g
            red_parts.append(dy_g * cs * e_g - p2w)
            alast_parts.append(jnp.sum(p2w, axis=0, keepdims=True)
                               + e_full[L - 1:L, gs] * jnp.sum(dsg * sg, axis=0, keepdims=True))
            dxdt_parts.append(dxdt)
            dstate[g] = e_full[L - 1:L, gs] * dsg + _dot_tn(cg, edy)
            dxbc_ref[:, gs] = dxdt * dt_full[:, gs] + dy_g * d_ref[:, gs]
            dd_ref[:, gs] += jnp.sum(dy_g * xs_g, axis=0, keepdims=True)
        red = jnp.concatenate(red_parts, axis=1)
        dxdt_all = jnp.concatenate(dxdt_parts, axis=1)
        alast = jnp.concatenate(alast_parts, axis=1)
        eye = _iota((NH, NH), 0) == _iota((NH, NH), 1)
        dacum = dacum - _sel(_dot_tn, dacum_t, eye) + _sel(_dot_nt, red, expand)
        dalast = _sel(_dot_nt, jnp.broadcast_to(alast, (8, DI)), expand)[0:1, :]
        dacum = dacum + jnp.where(_iota((L, 1), 0) == L - 1, dalast, 0.0)
        dda = _sel(_dot, dacum, triu, False)
        ddt = _sel(_dot_nt, dxdt_all * xs_ref[...], expand) + dda * a
        dalog_ref[...] += jnp.sum(dda * dt, axis=0, keepdims=True) * a
        ddt_raw = ddt * _sigmoid(dtp)
        ddt_ref[...] = ddt_raw
        dbias_ref[...] += jnp.sum(ddt_raw, axis=0, keepdims=True)

    rev = lambda c: (nc - 1 - c, 0)
    return _pcall(
        body, name="ssd_scan_bwd", grid=(nc,),
        in_specs=[pl.BlockSpec((L, DI), rev), pl.BlockSpec((L, DI), rev),
                  pl.BlockSpec((L, 1024), lambda c: (nc - 1 - c, 2)), pl.BlockSpec((L, 1024), lambda c: (nc - 1 - c, 3)),
                  pl.BlockSpec((L, NH), rev), pl.BlockSpec((1, NH), lambda c: (0, 0)),
                  pl.BlockSpec((1, NH), lambda c: (0, 0)), pl.BlockSpec((1, DI), lambda c: (0, 0)),
                  pl.BlockSpec((1, NG, NS, GW), lambda c: (nc - 1 - c, 0, 0, 0))],
        out_specs=[pl.BlockSpec((L, CONV_CH), rev), pl.BlockSpec((L, NH), rev),
                   pl.BlockSpec((1, NH), lambda c: (0, 0)), pl.BlockSpec((1, NH), lambda c: (0, 0)),
                   pl.BlockSpec((1, DI), lambda c: (0, 0))],
        out_shape=[jax.ShapeDtypeStruct((S, CONV_CH), F32), jax.ShapeDtypeStruct((S, NH), F32),
                   jax.ShapeDtypeStruct((1, NH), F32), jax.ShapeDtypeStruct((1, NH), F32),
                   jax.ShapeDtypeStruct((1, DI), F32)],
        scratch_shapes=[pltpu.VMEM((NG, NS, GW), F32)],
        compiler_params=_cp("arbitrary"))(dy, xbc_act, xbc_act, xbc_act, dt_raw, dt_bias, a_log, d_full, states)


def gate_norm_fwd(y, z, out_norm, tm=256):
    S = y.shape[0]
    tm = min(tm, S)

    def body(y_ref, z_ref, on_ref, o_ref):
        zv = z_ref[...]
        gin = y_ref[...] * (zv * _sigmoid(zv))
        for g in range(NG):
            gs = slice(g * GW, (g + 1) * GW)
            blk = gin[:, gs]
            r = lax.rsqrt(jnp.mean(blk * blk, axis=-1, keepdims=True) + EPS)
            o_ref[:, gs] = (blk * r * on_ref[:, gs]).astype(BF16)

    return _pcall(
        body, name="gate_norm_fwd", grid=(S // tm,),
        in_specs=[pl.BlockSpec((tm, DI), lambda i: (i, 0)), pl.BlockSpec((tm, DI), lambda i: (i, 0)),
                  pl.BlockSpec((1, DI), lambda i: (0, 0))],
        out_specs=pl.BlockSpec((tm, DI), lambda i: (i, 0)),
        out_shape=jax.ShapeDtypeStruct((S, DI), BF16),
        compiler_params=_cp("parallel"))(y, z, out_norm)


def gate_norm_bwd(dgn, y, z, out_norm, tm=256):
    S = y.shape[0]
    tm = min(tm, S)

    def body(dg_ref, y_ref, z_ref, on_ref, dy_ref, dz_ref, don_ref):
        i = pl.program_id(0)

        @pl.when(i == 0)
        def _():
            don_ref[...] = jnp.zeros_like(don_ref)

        zv, yv = z_ref[...], y_ref[...]
        s = _sigmoid(zv)
        sz = zv * s
        gin = yv * sz
        for g in range(NG):
            gs = slice(g * GW, (g + 1) * GW)
            blk = gin[:, gs]
            r = lax.rsqrt(jnp.mean(blk * blk, axis=-1, keepdims=True) + EPS)
            n = blk * r
            dg = dg_ref[:, gs]
            don_ref[:, gs] += jnp.sum(dg * n, axis=0, keepdims=True)
            dn = dg * on_ref[:, gs]
            dgin = r * (dn - n * jnp.mean(dn * n, axis=-1, keepdims=True))
            dy_ref[:, gs] = dgin * sz[:, gs]
            dz_ref[:, gs] = (dgin * yv[:, gs] * (s[:, gs] * (1.0 + zv[:, gs] * (1.0 - s[:, gs])))).astype(BF16)

    return _pcall(
        body, name="gate_norm_bwd", grid=(S // tm,),
        in_specs=[pl.BlockSpec((tm, DI), lambda i: (i, 0))] * 3 + [pl.BlockSpec((1, DI), lambda i: (0, 0))],
        out_specs=[pl.BlockSpec((tm, DI), lambda i: (i, 0)), pl.BlockSpec((tm, DI), lambda i: (i, 0)),
                   pl.BlockSpec((1, DI), lambda i: (0, 0))],
        out_shape=[jax.ShapeDtypeStruct((S, DI), F32), jax.ShapeDtypeStruct((S, DI), BF16),
                   jax.ShapeDtypeStruct((1, DI), F32)],
        compiler_params=_cp("arbitrary"))(dgn, y, z, out_norm)


SB_T = 256
SB_QSCALE = 0.125
SB_DEAD = -110.0
SB_UNSEEN = -1e30


def _head_norm(xv, lo):
    sq = xv * xv
    s0 = jnp.sum(jnp.where(lo, sq, 0.0), axis=-1, keepdims=True)
    s1 = jnp.sum(jnp.where(lo, 0.0, sq), axis=-1, keepdims=True)
    return jnp.where(lo, lax.rsqrt(s0 / SBD + EPS), lax.rsqrt(s1 / SBD + EPS))


def sb_prep_fwd(qkv, qg, kg, tm=256):
    S = qkv.shape[0]
    tm = min(tm, S)

    def body(x_ref, qg_ref, kg_ref, q_ref, k_ref, v_ref):
        lo = _iota((1, LANES), 1) < SBD
        for sl in range(D // LANES):
            cs = slice(sl * LANES, (sl + 1) * LANES)
            xq = x_ref[:, cs]
            q_ref[:, cs] = ((xq * _head_norm(xq, lo) * qg_ref[...]).astype(BF16).astype(F32) * SB_QSCALE).astype(BF16)
            xk = x_ref[:, D + sl * LANES:D + (sl + 1) * LANES]
            k_ref[:, cs] = (xk * _head_norm(xk, lo) * kg_ref[...]).astype(BF16)
        v_ref[...] = x_ref[:, 2 * D:3 * D].astype(BF16)

    return _pcall(
        body, name="sb_prep_fwd", grid=(S // tm,),
        in_specs=[pl.BlockSpec((tm, 3 * D), lambda i: (i, 0)), pl.BlockSpec((1, LANES), lambda i: (0, 0)),
                  pl.BlockSpec((1, LANES), lambda i: (0, 0))],
        out_specs=[pl.BlockSpec((tm, D), lambda i: (i, 0))] * 3,
        out_shape=[jax.ShapeDtypeStruct((S, D), BF16)] * 3,
        compiler_params=_cp("parallel"))(qkv, qg, kg)


def sb_prep_bwd(dqs, dkn, dv, qkv, qg, kg, tm=256):
    S = qkv.shape[0]
    tm = min(tm, S)

    def body(dq_ref, dk_ref, dv_ref, x_ref, qg_ref, kg_ref, dx_ref, dqg_ref, dkg_ref):
        i = pl.program_id(0)

        @pl.when(i == 0)
        def _():
            dqg_ref[...] = jnp.zeros_like(dqg_ref)
            dkg_ref[...] = jnp.zeros_like(dkg_ref)

        lo = _iota((1, LANES), 1) < SBD

        def one(xv, dh, gain):
            r = _head_norm(xv, lo)
            y = xv * r
            dy = dh * gain
            t = dy * y
            m0 = jnp.sum(jnp.where(lo, t, 0.0), axis=-1, keepdims=True)
            m1 = jnp.sum(jnp.where(lo, 0.0, t), axis=-1, keepdims=True)
            dx = r * (dy - y * (jnp.where(lo, m0, m1) / SBD))
            return dx, jnp.sum(dh * y, axis=0, keepdims=True)

        for sl in range(D // LANES):
            cs = slice(sl * LANES, (sl + 1) * LANES)
            dx, dg = one(x_ref[:, cs], dq_ref[:, cs] * SB_QSCALE, qg_ref[...])
            dx_ref[:, cs] = dx.astype(BF16)
            dqg_ref[:, cs] += dg
            ks = slice(D + sl * LANES, D + (sl + 1) * LANES)
            dx, dg = one(x_ref[:, ks], dk_ref[:, cs], kg_ref[...])
            dx_ref[:, ks] = dx.astype(BF16)
            dkg_ref[:, cs] += dg
        dx_ref[:, 2 * D:3 * D] = dv_ref[...].astype(BF16)

    return _pcall(
        body, name="sb_prep_bwd", grid=(S // tm,),
        in_specs=[pl.BlockSpec((tm, D), lambda i: (i, 0))] * 3
        + [pl.BlockSpec((tm, 3 * D), lambda i: (i, 0)), pl.BlockSpec((1, LANES), lambda i: (0, 0)),
           pl.BlockSpec((1, LANES), lambda i: (0, 0))],
        out_specs=[pl.BlockSpec((tm, 3 * D), lambda i: (i, 0)), pl.BlockSpec((1, D), lambda i: (0, 0)),
                   pl.BlockSpec((1, D), lambda i: (0, 0))],
        out_shape=[jax.ShapeDtypeStruct((S, 3 * D), BF16), jax.ShapeDtypeStruct((1, D), F32),
                   jax.ShapeDtypeStruct((1, D), F32)],
        compiler_params=_cp("arbitrary"))(dqs, dkn, dv, qkv, qg, kg)


def _split_dot(x, u):
    hi = x.astype(BF16)
    lo = (x - hi.astype(F32)).astype(BF16)
    return _dot(hi, u) + _dot(lo, u)


def _sb_logits(qh, kb, valid):
    z = _dot_nt(qh, kb)
    e = jnp.exp(-jnp.abs(z))
    lp = jnp.log(1.0 + e)
    lb = jnp.minimum(z, 0.0) - lp
    l1m = lb - z if valid is None else jnp.where(valid, lb - z, 0.0)
    return z, e, lb, l1m


def _keep(valid, x):
    return x if valid is None else jnp.where(valid, x, 0.0)


def sb_fwd(qs, kn, v):
    S = qs.shape[0]
    T = min(SB_T, S)
    nq = S // T

    def body(q_ref, k_ref, v_ref, o_ref, r_ref, oacc, rrun):
        i = pl.program_id(1)
        qb = q_ref[...]
        lo = _iota((1, LANES), 1) < SBD
        masks = (lo, jnp.logical_not(lo))
        qhs = [jnp.where(hm, qb, jnp.zeros_like(qb)) for hm in masks]
        row, col = _iota((T, T), 0), _iota((T, T), 1)
        u = (row > col).astype(BF16)
        lane_blk = _iota((T, LANES), 1)
        oacc[...] = jnp.zeros_like(oacc)
        rrun[...] = jnp.zeros_like(rrun)
        r_ref[...] = jnp.full((2, T, LANES), SB_UNSEEN, F32)

        def live(carry):
            s, rmax = carry
            return jnp.logical_and(s <= i, rmax > SB_DEAD)

        def block(j, valid):
            off = pl.multiple_of(j * T, T)
            kb = k_ref[pl.ds(off, T), :]
            vb = v_ref[pl.ds(off, T), :]
            acc, rmax = None, None
            for hh in range(2):
                _, _, lb, l1m = _sb_logits(qhs[hh], kb, valid)
                r = rrun[hh]
                aft = _split_dot(l1m, u) + r
                a = _keep(valid, jnp.exp(lb + aft))
                t = _dot(a.astype(BF16), jnp.where(masks[hh], vb, jnp.zeros_like(vb)))
                acc = t if acc is None else acc + t
                r_ref[hh] = jnp.where(lane_blk == j, r, r_ref[hh])
                rnew = r + jnp.sum(l1m, axis=-1, keepdims=True)
                rrun[hh] = rnew
                top = jnp.max(rnew)
                rmax = top if rmax is None else jnp.maximum(rmax, top)
            oacc[...] += acc
            return rmax

        first = block(i, col < row)
        lax.while_loop(live, lambda carry: (carry[0] + 1, block(i - carry[0], None)), (jnp.int32(1), first))
        o_ref[...] = oacc[...].astype(BF16)

    return _pcall(
        body, name="sb_fwd", grid=(D // LANES, nq),
        in_specs=[pl.BlockSpec((T, LANES), lambda h, i: (i, h)), pl.BlockSpec((S, LANES), lambda h, i: (0, h)),
                  pl.BlockSpec((S, LANES), lambda h, i: (0, h))],
        out_specs=[pl.BlockSpec((T, LANES), lambda h, i: (i, h)), pl.BlockSpec((2, T, LANES), lambda h, i: (h, i, 0))],
        out_shape=[jax.ShapeDtypeStruct((S, D), BF16), jax.ShapeDtypeStruct((SBH, S, LANES), F32)],
        scratch_shapes=[pltpu.VMEM((T, LANES), F32), pltpu.VMEM((2, T, 1), F32)],
        compiler_params=_cp("parallel", "arbitrary"))(qs, kn, v)


def sb_bwd(qs, kn, v, do, rsave):
    S = qs.shape[0]
    T = min(SB_T, S)
    nq = S // T

    def body(q_ref, k_ref, v_ref, do_ref, r_ref, dq_ref, dk_ref, dv_ref, crun):
        i = pl.program_id(1)

        @pl.when(i == 0)
        def _():
            dk_ref[...] = jnp.zeros_like(dk_ref)
            dv_ref[...] = jnp.zeros_like(dv_ref)

        qb, dob = q_ref[...], do_ref[...]
        lo = _iota((1, LANES), 1) < SBD
        masks = (lo, jnp.logical_not(lo))
        qhs = [jnp.where(hm, qb, jnp.zeros_like(qb)) for hm in masks]
        dohs = [jnp.where(hm, dob, jnp.zeros_like(dob)) for hm in masks]
        row, col = _iota((T, T), 0), _iota((T, T), 1)
        u = (row > col).astype(BF16)
        u2 = (row < col).astype(BF16)
        lane_blk = _iota((T, LANES), 1)
        dq_ref[...] = jnp.zeros_like(dq_ref)
        crun[...] = jnp.zeros_like(crun)

        def block(j, valid):
            off = pl.multiple_of(j * T, T)
            kb = k_ref[pl.ds(off, T), :]
            vb = v_ref[pl.ds(off, T), :]
            dq_t, dk_t, dv_t = None, None, None
            for hh in range(2):
                hm, qh, doh = masks[hh], qhs[hh], dohs[hh]
                z, e, lb, l1m = _sb_logits(qh, kb, valid)
                r = jnp.sum(jnp.where(lane_blk == j, r_ref[hh], 0.0), axis=-1, keepdims=True)
                aft = _split_dot(l1m, u) + r
                a = _keep(valid, jnp.exp(lb + aft))
                w = a * _dot_nt(doh, jnp.where(hm, vb, jnp.zeros_like(vb)))
                cprev = crun[hh]
                cw = _split_dot(w, u2) + cprev
                inv = 1.0 / (1.0 + e)
                pos = z >= 0.0
                beta = jnp.where(pos, 1.0, e) * inv
                onem = jnp.where(pos, e, 1.0) * inv
                dz = _keep(valid, w * onem - beta * cw).astype(BF16)
                tq = _dot(dz, jnp.where(hm, kb, jnp.zeros_like(kb)))
                tk = _dot_tn(dz, qh)
                tv = _dot_tn(a.astype(BF16), doh)
                dq_t, dk_t, dv_t = (tq, tk, tv) if dq_t is None else (dq_t + tq, dk_t + tk, dv_t + tv)
                crun[hh] = cprev + jnp.sum(w, axis=-1, keepdims=True)
            dq_ref[...] += dq_t
            dk_ref[pl.ds(off, T), :] += dk_t
            dv_ref[pl.ds(off, T), :] += dv_t

        n_live = None
        for hh in range(2):
            col_max = jnp.max(r_ref[hh], axis=0, keepdims=True)
            seen = jnp.logical_and(col_max > SB_DEAD, _iota((1, LANES), 1) <= i)
            n = jnp.sum(seen.astype(jnp.int32))
            n_live = n if n_live is None else jnp.maximum(n_live, n)
        lax.fori_loop(i + 1 - n_live, i, lambda j, carry: (block(j, None), carry)[1], 0)
        block(i, col < row)

    return _pcall(
        body, name="sb_bwd", grid=(D // LANES, nq),
        in_specs=[pl.BlockSpec((T, LANES), lambda h, i: (i, h)), pl.BlockSpec((S, LANES), lambda h, i: (0, h)),
                  pl.BlockSpec((S, LANES), lambda h, i: (0, h)), pl.BlockSpec((T, LANES), lambda h, i: (i, h)),
                  pl.BlockSpec((2, T, LANES), lambda h, i: (h, i, 0))],
        out_specs=[pl.BlockSpec((T, LANES), lambda h, i: (i, h)), pl.BlockSpec((S, LANES), lambda h, i: (0, h)),
                   pl.BlockSpec((S, LANES), lambda h, i: (0, h))],
        out_shape=[jax.ShapeDtypeStruct((S, D), F32)] * 3,
        scratch_shapes=[pltpu.VMEM((2, T, 1), F32)],
        compiler_params=_cp("parallel", "arbitrary"))(qs, kn, v, do, rsave)


def loss_head(y, target, tm=512):
    S = y.shape[0]
    tm = min(tm, S)

    def body(y_ref, t_ref, ls_ref, dy_ref):
        i = pl.program_id(0)
        err = y_ref[...] - t_ref[...]
        dy_ref[...] = err * (1.0 / D)
        part = jnp.sum(err * err, axis=0, keepdims=True)

        @pl.when(i == 0)
        def _():
            ls_ref[...] = part

        @pl.when(i > 0)
        def _():
            ls_ref[...] += part

    return _pcall(
        body, name="loss_head", grid=(S // tm,),
        in_specs=[pl.BlockSpec((tm, D), lambda i: (i, 0))] * 2,
        out_specs=[pl.BlockSpec((1, D), lambda i: (0, 0)), pl.BlockSpec((tm, D), lambda i: (i, 0))],
        out_shape=[jax.ShapeDtypeStruct((1, D), F32), jax.ShapeDtypeStruct((S, D), F32)],
        compiler_params=_cp("arbitrary"))(y, target)


def _row_tile(rows, cols):
    cap = max(8, (1 << 20) // (4 * cols))
    return max(t for t in range(8, min(rows, cap) + 1, 8) if rows % t == 0)


def _adamw_update(w, g, m, v):
    c1 = 1.0 / (1.0 - ADAM_B1 ** ADAM_STEP)
    c2 = 1.0 / (1.0 - ADAM_B2 ** ADAM_STEP)
    mn = ADAM_B1 * m + (1.0 - ADAM_B1) * g
    vn = ADAM_B2 * v + (1.0 - ADAM_B2) * (g * g)
    return -ADAM_LR * ((mn * c1) / (jnp.sqrt(vn * c2) + ADAM_EPS) + ADAM_WD * w), mn, vn


def adamw(w, g, m, v, name="adamw"):
    R, C = w.shape
    tr = _row_tile(R, C)

    def body(w_ref, g_ref, m_ref, v_ref, d_ref, mo_ref, vo_ref):
        d_ref[...], mo_ref[...], vo_ref[...] = _adamw_update(w_ref[...], g_ref[...], m_ref[...], v_ref[...])

    spec = pl.BlockSpec((tr, C), lambda i: (i, 0))
    return _pcall(
        body, name=name, grid=(R // tr,), in_specs=[spec] * 4, out_specs=[spec] * 3,
        out_shape=[jax.ShapeDtypeStruct((R, C), F32)] * 3,
        compiler_params=_cp("parallel"))(w, g, m, v)


def adamw_halves(w, g_mine, g_other, m, v, name="adamw_halves"):
    R, C = w.shape
    H = R // 2
    tr = _row_tile(H, C)
    n_i = H // tr
    where = lax.axis_index("c").astype(jnp.int32).reshape(1)

    def body(s_ref, w_ref, gm_ref, go_ref, m_ref, v_ref, g_ref, d_ref, mo_ref, vo_ref):
        g = jnp.where(pl.program_id(0) == s_ref[0], gm_ref[...], go_ref[...])
        g_ref[...] = g
        d_ref[...], mo_ref[...], vo_ref[...] = _adamw_update(w_ref[...], g, m_ref[...], v_ref[...])

    full = pl.BlockSpec((tr, C), lambda h, i, s: (h * n_i + i, 0))
    half = pl.BlockSpec((tr, C), lambda h, i, s: (i, 0))
    return _pcall(
        body, name=name,
        grid_spec=pltpu.PrefetchScalarGridSpec(
            num_scalar_prefetch=1, grid=(2, n_i), in_specs=[full, half, half, full, full], out_specs=[full] * 4),
        out_shape=[jax.ShapeDtypeStruct((R, C), F32)] * 4,
        compiler_params=_cp("parallel", "parallel"))(where, w, g_mine, g_other, m, v)


def pair_sum(gstacks, halves):
    c = lax.axis_index("c")
    me = 2 * lax.axis_index("x") + lax.axis_index("y")
    where = jnp.stack([c, me]).astype(jnp.int32)
    outs = []
    for g, xh in zip(gstacks, halves):
        _, H, C = xh.shape
        t = _row_tile(H, C)
        n_i = H // t

        def body(s_ref, g_ref, x_ref, qb_ref, own_ref):
            j = pl.program_id(1)
            q = g_ref[0] + x_ref[0]
            qb_ref[0] = q.astype(BF16)

            @pl.when(j == s_ref[1])
            def _():
                own_ref[...] = q

        outs.append(_pcall(
            body, name="pair_sum",
            grid_spec=pltpu.PrefetchScalarGridSpec(
                num_scalar_prefetch=1, grid=(n_i, N_CHIPS),
                in_specs=[pl.BlockSpec((1, t, C), lambda i, j, s, n_i=n_i: (j, s[0] * n_i + i, 0)),
                          pl.BlockSpec((1, t, C), lambda i, j, s: (j, i, 0))],
                out_specs=[pl.BlockSpec((1, t, C), lambda i, j, s: (j, i, 0)),
                           pl.BlockSpec((t, C), lambda i, j, s: (i, 0))]),
            out_shape=[jax.ShapeDtypeStruct((N_CHIPS, H, C), BF16), jax.ShapeDtypeStruct((H, C), F32)],
            compiler_params=_cp("parallel", "arbitrary"))(where, g, xh))
    return [o[0] for o in outs], [o[1] for o in outs]


def chip_sum(owns, recvs):
    outs = []
    for own, rc in zip(owns, recvs):
        H, C = own.shape
        t = _row_tile(H, C)

        def body(o_ref, r_ref, t_ref):
            t_ref[...] = ((o_ref[...] + r_ref[0].astype(F32)) + r_ref[1].astype(F32)) + r_ref[2].astype(F32)

        outs.append(_pcall(
            body, name="chip_sum", grid=(H // t,),
            in_specs=[pl.BlockSpec((t, C), lambda i: (i, 0)), pl.BlockSpec((3, t, C), lambda i: (0, i, 0))],
            out_specs=pl.BlockSpec((t, C), lambda i: (i, 0)),
            out_shape=jax.ShapeDtypeStruct((H, C), F32), compiler_params=_cp("parallel"))(own, rc))
    return outs


MESH = pl.DeviceIdType.MESH
ANY = pl.BlockSpec(memory_space=pl.ANY)
SPLIT_MIN_BYTES = 1 << 20


def _other_chips(x, y):
    return [(1 - x, y), (x, 1 - y), (1 - x, 1 - y)]


def _half_rows(rows, who):
    half = rows // 2
    return pl.ds(pl.multiple_of(who * half, 16), half)


def gather_all(shards):
    n = len(shards)
    rows = [s.shape[0] for s in shards]
    split = [r % 32 == 0 and s.size * s.dtype.itemsize >= SPLIT_MIN_BYTES for r, s in zip(rows, shards)]

    def body(*refs):
        ins, outs = refs[:n], refs[n:2 * n]
        ici_send, ici_recv, d2d_send, d2d_recv = refs[2 * n:]
        x, y, c = lax.axis_index("x"), lax.axis_index("y"), lax.axis_index("c")
        me, sib, chips = 2 * x + y, (x, y, 1 - c), _other_chips(x, y)

        def part(k, who):
            return _half_rows(rows[k], who) if split[k] else pl.ds(0, rows[k])

        def ici(k, r, block):
            px, py = chips[r]
            return pltpu.make_async_remote_copy(
                src_ref=ins[k].at[part(k, c)], dst_ref=outs[k].at[block, part(k, c)],
                send_sem=ici_send.at[3 * k + r], recv_sem=ici_recv.at[3 * k + r],
                device_id=(px, py, c), device_id_type=MESH)

        def d2d(k, r, who):
            px, py = chips[r]
            blk = outs[k].at[2 * px + py, part(k, who)]
            return pltpu.make_async_remote_copy(
                src_ref=blk, dst_ref=blk, send_sem=d2d_send.at[3 * k + r], recv_sem=d2d_recv.at[3 * k + r],
                device_id=sib, device_id_type=MESH)

        sends = [ici(k, r, me) for k in range(n) for r in range(3)]
        for cp in sends:
            cp.start()
        for k in range(n):
            for r in range(3):
                px, py = chips[r]
                ici(k, r, 2 * px + py).wait_recv()
                if split[k]:
                    fwd = d2d(k, r, c)
                    fwd.start()
                    sends.append(fwd)
        for k in range(n):
            for r in range(3):
                if split[k]:
                    d2d(k, r, 1 - c).wait_recv()
        for cp in sends:
            cp.wait_send()

    return _pcall(
        body, name="gather_all", in_specs=[ANY] * n, out_specs=[ANY] * n,
        out_shape=[jax.ShapeDtypeStruct((N_CHIPS,) + s.shape, s.dtype) for s in shards],
        scratch_shapes=[pltpu.SemaphoreType.DMA((3 * n,))] * 4)(*shards)


def swap_halves(gstacks):
    n = len(gstacks)

    def body(*refs):
        ins, outs, send_sems, recv_sems = refs[:n], refs[n:2 * n], refs[2 * n], refs[2 * n + 1]
        x, y, c = lax.axis_index("x"), lax.axis_index("y"), lax.axis_index("c")
        copies = [pltpu.make_async_remote_copy(
            src_ref=ins[k].at[:, _half_rows(ins[k].shape[1], 1 - c)], dst_ref=outs[k],
            send_sem=send_sems.at[k], recv_sem=recv_sems.at[k], device_id=(x, y, 1 - c), device_id_type=MESH)
            for k in range(n)]
        for cp in copies:
            cp.start()
        for cp in copies:
            cp.wait()

    return _pcall(
        body, name="swap_halves", in_specs=[ANY] * n, out_specs=[ANY] * n,
        out_shape=[jax.ShapeDtypeStruct((g.shape[0], g.shape[1] // 2, g.shape[2]), g.dtype) for g in gstacks],
        scratch_shapes=[pltpu.SemaphoreType.DMA((n,)), pltpu.SemaphoreType.DMA((n,))])(*gstacks)


def scatter_chips(stacks):
    n = len(stacks)

    def body(*refs):
        ins, outs, send_sems, recv_sems = refs[:n], refs[n:2 * n], refs[2 * n], refs[2 * n + 1]
        x, y, c = lax.axis_index("x"), lax.axis_index("y"), lax.axis_index("c")
        copies = [pltpu.make_async_remote_copy(
            src_ref=ins[k].at[2 * px + py], dst_ref=outs[k].at[r], send_sem=send_sems.at[3 * k + r],
            recv_sem=recv_sems.at[3 * k + r], device_id=(px, py, c), device_id_type=MESH)
            for k in range(n) for r, (px, py) in enumerate(_other_chips(x, y))]
        for cp in copies:
            cp.start()
        for cp in copies:
            cp.wait()

    return _pcall(
        body, name="scatter_chips", in_specs=[ANY] * n, out_specs=[ANY] * n,
        out_shape=[jax.ShapeDtypeStruct((3,) + s.shape[1:], s.dtype) for s in stacks],
        scratch_shapes=[pltpu.SemaphoreType.DMA((3 * n,)), pltpu.SemaphoreType.DMA((3 * n,))])(*stacks)


def swap_totals(totals):
    n = len(totals)

    def body(*refs):
        ins, outs, send_sems, recv_sems = refs[:n], refs[n:2 * n], refs[2 * n], refs[2 * n + 1]
        x, y, c = lax.axis_index("x"), lax.axis_index("y"), lax.axis_index("c")
        copies = [pltpu.make_async_remote_copy(
            src_ref=ins[k], dst_ref=outs[k], send_sem=send_sems.at[k], recv_sem=recv_sems.at[k],
            device_id=(x, y, 1 - c), device_id_type=MESH) for k in range(n)]
        for cp in copies:
            cp.start()
        for cp in copies:
            cp.wait()

    return _pcall(
        body, name="swap_totals", in_specs=[ANY] * n, out_specs=[ANY] * n,
        out_shape=[jax.ShapeDtypeStruct(t.shape, t.dtype) for t in totals],
        scratch_shapes=[pltpu.SemaphoreType.DMA((n,)), pltpu.SemaphoreType.DMA((n,))])(*totals)


def place_own(gathered, own):
    R, C = own.shape
    t = _row_tile(R, C)
    where = (2 * lax.axis_index("x") + lax.axis_index("y")).astype(jnp.int32).reshape(1)

    def body(s_ref, own_ref, g_ref, o_ref):
        o_ref[0] = own_ref[...]

    return _pcall(
        body, name="place_own",
        grid_spec=pltpu.PrefetchScalarGridSpec(
            num_scalar_prefetch=1, grid=(R // t,), in_specs=[pl.BlockSpec((t, C), lambda i, s: (i, 0)), ANY],
            out_specs=pl.BlockSpec((1, t, C), lambda i, s: (s[0], i, 0))),
        out_shape=jax.ShapeDtypeStruct(gathered.shape, gathered.dtype), input_output_aliases={2: 0},
        compiler_params=_cp("parallel"))(where, own, gathered)


def reduce_scatter(gstacks):
    halves = swap_halves(gstacks)
    payload, own = pair_sum(gstacks, halves)
    recv = scatter_chips(payload)
    mine = chip_sum(own, recv)
    return mine, swap_totals(mine)


def allreduce_small(vec):
    R = vec.shape[0]

    def body(in_ref, out_ref, buf, send_sems, recv_sems):
        x, y, c = lax.axis_index("x"), lax.axis_index("y"), lax.axis_index("c")
        me = 4 * x + 2 * y + c
        buf[me] = in_ref[...]
        copies = []
        for k in range(1, 8):
            peer = (x ^ (k >> 2), y ^ ((k >> 1) & 1), c ^ (k & 1))
            copies.append(pltpu.make_async_remote_copy(
                src_ref=in_ref, dst_ref=buf.at[me], send_sem=send_sems.at[k - 1], recv_sem=recv_sems.at[k - 1],
                device_id=peer, device_id_type=MESH))
        for cp in copies:
            cp.start()
        for cp in copies:
            cp.wait()
        acc = buf[0]
        for d in range(1, 8):
            acc = acc + buf[d]
        out_ref[...] = acc

    vm = pl.BlockSpec(memory_space=pltpu.VMEM)
    return _pcall(
        body, name="allreduce_small", in_specs=[vm], out_specs=vm,
        out_shape=jax.ShapeDtypeStruct((R, LANES), F32),
        scratch_shapes=[pltpu.VMEM((8, R, LANES), F32), pltpu.SemaphoreType.DMA((7,)), pltpu.SemaphoreType.DMA((7,))])(vec)


MATMUL_SHARDED = [("pool_in", 1), ("pool_group", 2), ("ssd_in", 2), ("ssd_out", 1), ("sb_qkv", 2), ("sb_out", 1),
                  ("ffn_gate", 2), ("ffn_up", 2), ("ffn_down", 1)]
STACKED = ["ffn_gate", "ffn_up", "ffn_down"]
SMALL_SHARDED = [("pool_scale", 1), ("ssd_conv_w", 2)]
REPLICATED = ["mix_norm", "ssd_conv_b", "ssd_dt_bias", "ssd_a_log", "ssd_d", "ssd_out_norm", "sb_q_norm",
              "sb_k_norm", "ffn_norm"]
WEIGHT_ORDER = ["mix_norm", "pool_in", "pool_group", "pool_scale", "ssd_in", "ssd_conv_w", "ssd_conv_b",
                "ssd_dt_bias", "ssd_a_log", "ssd_d", "ssd_out_norm", "ssd_out", "sb_qkv", "sb_q_norm", "sb_k_norm",
                "sb_out", "ffn_norm", "ffn_gate", "ffn_up", "ffn_down"]


def _piece_rows(n, mult):
    rows = -(-n // LANES)
    return -(-rows // mult) * mult


def _as_rows(a, mult):
    flat = a.reshape(-1)
    rows = _piece_rows(flat.shape[0], mult)
    if rows * LANES != flat.shape[0]:
        flat = jnp.pad(flat, (0, rows * LANES - flat.shape[0]))
    return flat.reshape(rows, LANES)


def _pack(arrs, mult=8, row_pad=8):
    parts = [_as_rows(a, mult) for a in arrs]
    rows = sum(p.shape[0] for p in parts)
    pad = -rows % row_pad
    if pad:
        parts.append(jnp.zeros((pad, LANES), parts[0].dtype))
    return jnp.concatenate(parts, axis=0)


def _unpack(packed, shapes, mult=8, lead=()):
    out, off = [], 0
    for s in shapes:
        n = math.prod(s)
        rows = _piece_rows(n, mult)
        piece = packed[..., off:off + rows, :].reshape(lead + (rows * LANES,))
        out.append(piece[..., :n].reshape(lead + tuple(s)))
        off += rows
    return out


def _rows2d(a):
    return a.reshape(-1, a.shape[-1])


def _gather_weights(shards):
    own = [_rows2d(shards[n].astype(BF16)) for n, _ in MATMUL_SHARDED]
    small = _pack([shards[n] for n, _ in SMALL_SHARDED])
    gathered = gather_all(own + [small])
    me = 2 * lax.axis_index("x") + lax.axis_index("y")

    def whole(got, mine, ax):
        return jnp.concatenate([jnp.where(me == j, mine, got[j]) for j in range(N_CHIPS)], axis=ax)

    full = {}
    for (n, ax), got, mine in zip(MATMUL_SHARDED, gathered, own):
        if n in STACKED:
            full[n] = place_own(got, mine)
        else:
            shp = shards[n].shape
            full[n] = whole(got.reshape((N_CHIPS,) + shp), mine.reshape(shp), ax)
    pieces = _unpack(gathered[-1], [shards[n].shape for n, _ in SMALL_SHARDED], lead=(N_CHIPS,))
    for (n, ax), got in zip(SMALL_SHARDED, pieces):
        full[n] = whole(got, shards[n], ax)
    return full


def _split_shards(full, axis):
    return jnp.stack(jnp.split(full, N_CHIPS, axis=axis))


def _ffn_fwd(x, gain, wg4, wu4, wd4, layer):
    h = rmsnorm_fwd(x, gain, name="ffn_norm_fwd")
    a4, b4, hid4 = ffn_up(h, wg4, wu4, layer)
    xo = ffn_down(hid4, wd4, layer, x)
    return xo, (x, h, a4, b4, hid4)


def _ffn_bwd(dout, saved, gain, wg4, wu4, wd4, layer, gbufs):
    x, h, a4, b4, hid4 = saved
    da4, db4 = ffn_bwd_hidden(dout, wd4, layer, a4, b4)
    dwd4 = ffn_wgrad_out(hid4, dout, layer, None if gbufs is None else gbufs[2])
    dwg4, dwu4 = ffn_wgrad_in(h, da4, db4, layer, None if gbufs is None else gbufs[:2])
    dh = ffn_dh(da4, db4, wg4, wu4, layer)
    dx, dgain = rmsnorm_bwd(x, gain, dh, dout, name="ffn_norm_bwd")
    return dx, dgain, (dwg4, dwu4, dwd4)


def _pool_layer_fwd(x, gain, w_in, wgrp, scale):
    h = rmsnorm_fwd(x, gain, name="pool_norm_fwd")
    u = linear([(h, w_in, "nn")], name="pool_in")
    xo, p = pool_fwd(u, wgrp, scale, x)
    return xo, (x, h, p)


def _pool_layer_bwd(dout, saved, gain, w_in, wgrp, scale):
    x, h, p = saved
    dp, dwgrp, dscale = pool_bwd_group(dout, p, wgrp, scale)
    du = pool_bwd_window(dp)
    (dw_in,) = wgrad(h, [du], name="pool_dwin")
    dh = linear([(du, w_in, "nt")], name="pool_dh")
    dx, dgain = rmsnorm_bwd(x, gain, dh, dout, name="pool_norm_bwd")
    return dx, dgain, dw_in, dwgrp, dscale


def _ssd_layer_fwd(x, gain, w_z, w_xbc, w_dt, conv_w, conv_b, dt_bias, a_log, d_full, out_norm, w_out):
    h = rmsnorm_fwd(x, gain, name="ssd_norm_fwd")
    z = linear([(h, w_z, "nn")], name="ssd_in_z")
    xbc = linear([(h, w_xbc, "nn")], tn=2048, name="ssd_in_xbc")
    dt_raw = linear([(h, w_dt, "nn")], name="ssd_in_dt")
    act = conv_fwd(xbc, conv_w, conv_b)
    y, states = ssd_scan_fwd(act, dt_raw, dt_bias, a_log, d_full)
    gn = gate_norm_fwd(y, z, out_norm)
    xo = linear([(gn, w_out, "nn")], res=x, name="ssd_out")
    return xo, (x, h, z, xbc, dt_raw, act, y, states, gn)


def _ssd_layer_bwd(dout, saved, gain, w_z, w_xbc, w_dt, conv_w, conv_b, dt_bias, a_log, d_full, out_norm, w_out):
    x, h, z, xbc, dt_raw, act, y, states, gn = saved
    dgn = linear([(dout, w_out, "nt")], name="ssd_dgn")
    (dw_out,) = wgrad(gn, [dout], name="ssd_dwout")
    dy, dz, dout_norm = gate_norm_bwd(dgn, y, z, out_norm)
    dact, ddt_raw, dbias, dalog, dd_full = ssd_scan_bwd(dy, act, dt_raw, dt_bias, a_log, d_full, states)
    dpre, dconv_w8, dconv_b = conv_bwd_pre(dact, xbc, conv_w, conv_b)
    dxbc = conv_bwd_input(dpre, conv_w)
    ddt_b = ddt_raw.astype(BF16)
    (dw_z,) = wgrad(h, [dz], name="ssd_dwz")
    (dw_xbc,) = wgrad(h, [dxbc], tn=2048, name="ssd_dwxbc")
    (dw_dt,) = wgrad(h, [ddt_b], name="ssd_dwdt")
    dh = linear([(dz, w_z, "nt"), (dxbc, w_xbc, "nt"), (ddt_b, w_dt, "nt")], tm=256, name="ssd_dh")
    dx, dgain = rmsnorm_bwd(x, gain, dh, dout, name="ssd_norm_bwd")
    dw_in = jnp.concatenate([dw_z, dw_xbc, dw_dt], axis=1)
    dd = dd_full.reshape(NH, HP).sum(axis=1).reshape(1, NH)
    return dx, dgain, dw_in, dconv_w8[:4], dconv_b, dbias, dalog, dd, dout_norm, dw_out


def _sb_layer_fwd(x, gain, w_qkv, qg, kg, w_out):
    h = rmsnorm_fwd(x, gain, name="sb_norm_fwd")
    qkv = linear([(h, w_qkv, "nn")], tn=1024, name="sb_qkv")
    qs, kn, v = sb_prep_fwd(qkv, qg, kg)
    o, rsave = sb_fwd(qs, kn, v)
    xo = linear([(o, w_out, "nn")], res=x, name="sb_out")
    return xo, (x, h, qkv, qs, kn, v, o, rsave)


def _sb_layer_bwd(dout, saved, gain, w_qkv, qg, kg, w_out):
    x, h, qkv, qs, kn, v, o, rsave = saved
    do = linear([(dout, w_out, "nt")], out_dtype=BF16, name="sb_do")
    (dw_out,) = wgrad(o, [dout], name="sb_dwout")
    dqs, dkn, dv = sb_bwd(qs, kn, v, do, rsave)
    dqkv, dqg, dkg = sb_prep_bwd(dqs, dkn, dv, qkv, qg, kg)
    (dw_qkv,) = wgrad(h, [dqkv], tn=1024, name="sb_dwqkv")
    dh = linear([(dqkv, w_qkv, "nt")], name="sb_dh")
    dx, dgain = rmsnorm_bwd(x, gain, dh, dout, name="sb_norm_bwd")
    dqg = dqg.reshape(SBH, SBD).sum(axis=0).reshape(1, SBD)
    dkg = dkg.reshape(SBH, SBD).sum(axis=0).reshape(1, SBD)
    return dx, dgain, dw_qkv, dqg, dkg, dw_out


def _local_step(x, target, full, rep):
    S = x.shape[0]
    d_full = jnp.repeat(rep["ssd_d"][0], HP).reshape(1, DI)
    qg = jnp.tile(rep["sb_q_norm"][0], 2).reshape(1, LANES)
    kg = jnp.tile(rep["sb_k_norm"][0], 2).reshape(1, LANES)
    ssd_in = full["ssd_in"][0]
    w_z, w_xbc, w_dt = ssd_in[:, :DI], ssd_in[:, DI:DI + CONV_CH], ssd_in[:, DI + CONV_CH:]
    conv_w = full["ssd_conv_w"][0]
    conv_b = rep["ssd_conv_b"]
    pool_scale = full["pool_scale"]

    def mixer_args(i):
        kind, j = i % 3, i // 3
        if kind == 0:
            return (full["pool_in"][j], full["pool_group"][j], pool_scale[j:j + 1])
        if kind == 1:
            return (w_z, w_xbc, w_dt, conv_w, conv_b, rep["ssd_dt_bias"], rep["ssd_a_log"], d_full,
                    rep["ssd_out_norm"], full["ssd_out"][0])
        return (full["sb_qkv"][0], qg, kg, full["sb_out"][0])

    fwd = (_pool_layer_fwd, _ssd_layer_fwd, _sb_layer_fwd)
    bwd = (_pool_layer_bwd, _ssd_layer_bwd, _sb_layer_bwd)
    saved = []
    for i in range(DEPTH):
        x, sm = fwd[i % 3](x, rep["mix_norm"][i], *mixer_args(i))
        x, sf = _ffn_fwd(x, rep["ffn_norm"][i], full["ffn_gate"], full["ffn_up"], full["ffn_down"], i)
        saved.append((sm, sf))

    colsq, dx = loss_head(x, target)
    loss = 0.5 * jnp.sum(colsq) / D

    g = {n: [None] * DEPTH for n in ("mix_norm", "ffn_norm")}
    ffn_g = None
    g["pool_in"], g["pool_group"], g["pool_scale"] = [None] * 2, [None] * 2, [None] * 2
    for i in reversed(range(DEPTH)):
        sm, sf = saved[i]
        dx, g["ffn_norm"][i], ffn_g = _ffn_bwd(
            dx, sf, rep["ffn_norm"][i], full["ffn_gate"], full["ffn_up"], full["ffn_down"], i, ffn_g)
        kind, j = i % 3, i // 3
        res = bwd[kind](dx, sm, rep["mix_norm"][i], *mixer_args(i))
        dx, g["mix_norm"][i] = res[0], res[1]
        if kind == 0:
            g["pool_in"][j], g["pool_group"][j], g["pool_scale"][j] = res[2:]
        elif kind == 1:
            dw_in, dconv_w, dconv_b, dbias, dalog, dd, don, dw_out = res[2:]
            g.update(ssd_in=dw_in[None], ssd_conv_w=dconv_w[None], ssd_conv_b=dconv_b, ssd_dt_bias=dbias,
                     ssd_a_log=dalog, ssd_d=dd, ssd_out_norm=don, ssd_out=dw_out[None])
        else:
            dw_qkv, dqg, dkg, dw_out = res[2:]
            g.update(sb_qkv=dw_qkv[None], sb_q_norm=dqg, sb_k_norm=dkg, sb_out=dw_out[None])
    for n in ("mix_norm", "ffn_norm", "pool_scale"):
        g[n] = jnp.concatenate(g[n], axis=0)
    for n in ("pool_in", "pool_group"):
        g[n] = jnp.stack(g[n])
    g["ffn_gate"], g["ffn_up"], g["ffn_down"] = ffn_g
    return loss, dx, g


def kernel(x, mix_norm, pool_in, pool_group, pool_scale, ssd_in, ssd_conv_w, ssd_conv_b, ssd_dt_bias, ssd_a_log, ssd_d, ssd_out_norm, ssd_out, sb_qkv, sb_q_norm, sb_k_norm, sb_out, ffn_norm, ffn_gate, ffn_up, ffn_down, loss_target, m_mix_norm, m_pool_in, m_pool_group, m_pool_scale, m_ssd_in, m_ssd_conv_w, m_ssd_conv_b, m_ssd_dt_bias, m_ssd_a_log, m_ssd_d, m_ssd_out_norm, m_ssd_out, m_sb_qkv, m_sb_q_norm, m_sb_k_norm, m_sb_out, m_ffn_norm, m_ffn_gate, m_ffn_up, m_ffn_down, v_mix_norm, v_pool_in, v_pool_group, v_pool_scale, v_ssd_in, v_ssd_conv_w, v_ssd_conv_b, v_ssd_dt_bias, v_ssd_a_log, v_ssd_d, v_ssd_out_norm, v_ssd_out, v_sb_qkv, v_sb_q_norm, v_sb_k_norm, v_sb_out, v_ffn_norm, v_ffn_gate, v_ffn_up, v_ffn_down):
    given = dict(locals())
    w = {n: given[n] for n in WEIGHT_ORDER}
    m = {n: given["m_" + n] for n in WEIGHT_ORDER}
    v = {n: given["v_" + n] for n in WEIGHT_ORDER}
    full = _gather_weights(w)
    rep = {n: w[n] for n in REPLICATED}

    loss, dx, g = _local_step(x[0], loss_target[0], full, rep)
    loss = lax.psum(loss, ("x", "y", "c"))
    out = {}

    gstacks = [g[n] if n in STACKED else _split_shards(g[n], ax).reshape((N_CHIPS,) + _rows2d(w[n]).shape)
               for n, ax in MATMUL_SHARDED]
    mine, other = reduce_scatter(gstacks)
    for (n, _), g_mine, g_other in zip(MATMUL_SHARDED, mine, other):
        res = adamw_halves(_rows2d(w[n]), g_mine, g_other, _rows2d(m[n]), _rows2d(v[n]), name="adamw_" + n)
        for key, a in zip("gdmv", res):
            out[key, n] = a.reshape(w[n].shape)

    small = REPLICATED + [n for n, _ in SMALL_SHARDED]
    gfull = _unpack(allreduce_small(_pack([g[n] for n in small])), [g[n].shape for n in small])
    me = 2 * lax.axis_index("x") + lax.axis_index("y")
    gsum = dict(zip(small, gfull))
    for n, ax in SMALL_SHARDED:
        gsum[n] = lax.dynamic_slice_in_dim(gsum[n], me * w[n].shape[ax], w[n].shape[ax], axis=ax)
    res = adamw(*[_pack([t[n] for n in small]) for t in (w, gsum, m, v)], name="adamw_small")
    for key, flat in zip("dmv", res):
        for n, a in zip(small, _unpack(flat, [w[n].shape for n in small])):
            out[key, n] = a
    for n in small:
        out["g", n] = gsum[n]

    return (loss, dx[None], *[out["g", n] for n in WEIGHT_ORDER], *[out["d", n] for n in WEIGHT_ORDER],
            *[out["m", n] for n in WEIGHT_ORDER], *[out["v", n] for n in WEIGHT_ORDER])
```
